```python
import math
import jax, jax.numpy as jnp
from jax import lax
import numpy as np

D_MODEL = 1024
BATCH = 8
SEQ = 2048
DEPTH = 4

HEAD_DIM = 64
A_HEADS = 8
A_KV_HEADS = 2
WINDOW = 128
B_HEADS = 8
C_HEADS = 16
C_LATENT = 128
IDX_HEADS = 8
IDX_DIM = 64
TOPK_MAX = 256
NUM_BUCKETS = 32
MAX_EXACT = 16
MAX_DISTANCE = 128
POS_HEADS = C_HEADS
N_EXPERTS = 32
TOP_K = 4
D_FF = 1024
SWIGLU_LIMIT = 7.0
SWIGLU_ALPHA = 1.702
MOE_BLOCK = 128
PLE_DIM = 256

Q_BLOCK = 128
N_EVEN = (DEPTH + 1) // 2
N_ODD = DEPTH // 2
DEEPNORM_ALPHA = (2 * DEPTH) ** 0.25
DEEPNORM_BETA = (8 * DEPTH) ** -0.25
LN_EPS = 1e-5
NEG = -1e30

EVEN_COLS = (A_HEADS * HEAD_DIM, A_KV_HEADS * HEAD_DIM, A_KV_HEADS * HEAD_DIM,
             B_HEADS * HEAD_DIM, B_HEADS * HEAD_DIM, B_HEADS * HEAD_DIM, B_HEADS)
ODD_COLS = (C_HEADS * HEAD_DIM, C_LATENT, IDX_HEADS * IDX_DIM, IDX_DIM, IDX_HEADS)
EVEN_IN = sum(EVEN_COLS)
ODD_IN = sum(ODD_COLS)
EVEN_SPLITS = [int(v) for v in np.cumsum(EVEN_COLS)[:-1]]
ODD_SPLITS = [int(v) for v in np.cumsum(ODD_COLS)[:-1]]

kernel_name = "hybrid_swa_fox_dsa_moe_deepnorm"


def layer_norm(x, g, b):
    xf = x.astype(jnp.float32)
    mu = xf.mean(-1, keepdims=True)
    var = jnp.square(xf - mu).mean(-1, keepdims=True)
    return ((xf - mu) * lax.rsqrt(var + LN_EPS) * g.astype(jnp.float32) + b.astype(jnp.float32)).astype(x.dtype)


def rms_norm(x, g):
    xf = x.astype(jnp.float32)
    y = xf * lax.rsqrt(jnp.square(xf).mean(-1, keepdims=True) + LN_EPS)
    return (y * g.astype(jnp.float32)).astype(x.dtype)


def t5_bucket(dist):
    d = jnp.maximum(dist, 0)
    ratio = jnp.log(jnp.maximum(d, 1).astype(jnp.float32) / MAX_EXACT) / math.log(MAX_DISTANCE / MAX_EXACT)
    large = MAX_EXACT + (ratio * (NUM_BUCKETS - MAX_EXACT)).astype(jnp.int32)
    large = jnp.minimum(large, NUM_BUCKETS - 1)
    return jnp.where(d < MAX_EXACT, d, large)


def split_blocks(a, nb):
    return a.reshape(a.shape[0], nb, Q_BLOCK, *a.shape[2:]).swapaxes(0, 1)


def swa_sink_attention(q, k, v, sinks, pos_bias):
    bsz, t = q.shape[:2]
    nb = t // WINDOW
    grp = A_HEADS // A_KV_HEADS
    qb = q.reshape(bsz, nb, WINDOW, A_KV_HEADS, grp, HEAD_DIM)

    def band(a):
        ab = a.reshape(bsz, nb, WINDOW, A_KV_HEADS, HEAD_DIM)
        prev = jnp.concatenate([jnp.zeros_like(ab[:, :1]), ab[:, :-1]], axis=1)
        return jnp.concatenate([prev, ab], axis=2)

    kb, vb = band(k), band(v)
    s = jnp.einsum('bnqkgd,bnskd->bnkgqs', qb, kb).astype(jnp.float32) * HEAD_DIM ** -0.5
    qi = jnp.arange(WINDOW)[:, None]
    kj = jnp.arange(2 * WINDOW)[None, :]
    dist = qi + WINDOW - kj
    blk = jnp.arange(nb)[:, None, None]
    valid = (dist >= 0) & (dist < WINDOW) & ((blk - 1) * WINDOW + kj >= 0)
    bias = pos_bias[t5_bucket(dist)][..., :A_HEADS]
    bias = bias.transpose(2, 0, 1).reshape(A_KV_HEADS, grp, WINDOW, 2 * WINDOW)
    s = jnp.where(valid[None, :, None, None], s + bias.astype(jnp.float32), NEG)
    sink = sinks.astype(jnp.float32).reshape(A_KV_HEADS, grp, 1, 1)
    m = jnp.maximum(s.max(-1, keepdims=True), sink)
    e = jnp.exp(s - m)
    probs = e / (e.sum(-1, keepdims=True) + jnp.exp(sink - m))
    o = jnp.einsum('bnkgqs,bnskd->bnqkgd', probs.astype(v.dtype), vb)
    return o.reshape(bsz, t, A_HEADS * HEAD_DIM)


def forgetting_attention(q, k, v, f_logit):
    bsz, t = q.shape[:2]
    nb = t // Q_BLOCK
    c = jnp.cumsum(jax.nn.log_sigmoid(f_logit.astype(jnp.float32)), axis=1)
    c_keys = c.transpose(0, 2, 1)
    kpos = jnp.arange(t)

    def block(args):
        qb, cb, n = args
        s = jnp.einsum('bqhd,bkhd->bhqk', qb, k).astype(jnp.float32) * HEAD_DIM ** -0.5
        s = s + cb.transpose(0, 2, 1)[..., None] - c_keys[:, :, None, :]
        qpos = n * Q_BLOCK + jnp.arange(Q_BLOCK)
        s = jnp.where(kpos[None, :] <= qpos[:, None], s, NEG)
        p = jax.nn.softmax(s, axis=-1)
        return jnp.einsum('bhqk,bkhd->bqhd', p.astype(v.dtype), v)

    o = lax.map(block, (split_blocks(q, nb), split_blocks(c, nb), jnp.arange(nb)))
    return o.swapaxes(0, 1).reshape(bsz, t, B_HEADS * HEAD_DIM)


def dsa_latent_attention(q, c_kv, q_idx, k_idx, w_idx, w_uk, w_uv, pos_bias):
    bsz, t = q.shape[:2]
    nb = t // Q_BLOCK
    n_sel = min(TOPK_MAX, t // 4)
    q_lat = jnp.einsum('bthd,rhd->bthr', q, w_uk)
    kpos = jnp.arange(t)

    def block(args):
        ql, qi, wi, n = args
        qpos = n * Q_BLOCK + jnp.arange(Q_BLOCK)
        causal = kpos[None, :] <= qpos[:, None]
        idx_s = jax.nn.relu(jnp.einsum('bqhd,bsd->bqhs', qi, k_idx).astype(jnp.float32) * IDX_DIM ** -0.5)
        score = jnp.einsum('bqhs,bqh->bqs', idx_s, wi.astype(jnp.float32) * IDX_HEADS ** -0.5)
        score = jnp.where(causal[None], score, NEG)
        _, sel = lax.top_k(score, n_sel)
        c_sel = jax.vmap(lambda cb, ib: cb[ib])(c_kv, sel)
        s = jnp.einsum('bqhr,bqkr->bhqk', ql, c_sel).astype(jnp.float32) * HEAD_DIM ** -0.5
        dist = qpos[None, :, None] - sel
        bias = pos_bias[t5_bucket(dist)].transpose(0, 3, 1, 2)
        s = jnp.where((dist >= 0)[:, None], s + bias.astype(jnp.float32), NEG)
        p = jax.nn.softmax(s, axis=-1)
        o_lat = jnp.einsum('bhqk,bqkr->bqhr', p.astype(c_sel.dtype), c_sel)
        return jnp.einsum('bqhr,rhd->bqhd', o_lat, w_uv)

    o = lax.map(block, (split_blocks(q_lat, nb), split_blocks(q_idx, nb), split_blocks(w_idx, nb), jnp.arange(nb)))
    return o.swapaxes(0, 1).reshape(bsz, t, C_HEADS * HEAD_DIM)


def moe_ffn(x, w_router, b_router, w_gate_up, b_gate_up, w_down, b_down):
    bsz, t, d = x.shape
    x2 = x.reshape(-1, d)
    n_tok = x2.shape[0]
    logits = (x2 @ w_router + b_router).astype(jnp.float32)
    top_val, top_idx = lax.top_k(logits, TOP_K)
    gates = jax.nn.softmax(top_val, axis=-1).astype(x.dtype)
    n_assign = n_tok * TOP_K
    flat_e = top_idx.reshape(-1)
    order = jnp.argsort(flat_e)
    sorted_e = flat_e[order]
    counts = jnp.bincount(flat_e, length=N_EXPERTS)
    starts = jnp.cumsum(counts) - counts
    padded = (counts + MOE_BLOCK - 1) // MOE_BLOCK * MOE_BLOCK
    pad_ends = jnp.cumsum(padded)
    pad_starts = pad_ends - padded
    dest_sorted = pad_starts[sorted_e] + jnp.arange(n_assign) - starts[sorted_e]
    dest = jnp.zeros((n_assign,), jnp.int32).at[order].set(dest_sorted.astype(jnp.int32))
    n_blocks = -(-n_assign // MOE_BLOCK) + N_EXPERTS
    x_pad = jnp.zeros((n_blocks * MOE_BLOCK, d), x.dtype).at[dest].set(x2[jnp.arange(n_assign) // TOP_K])
    block_e = jnp.minimum(jnp.searchsorted(pad_ends, jnp.arange(n_blocks) * MOE_BLOCK, side='right'), N_EXPERTS - 1)

    def expert_block(args):
        xb, e = args
        h = xb @ w_gate_up[e] + b_gate_up[e]
        glu = jnp.minimum(h[:, :D_FF], SWIGLU_LIMIT)
        lin = jnp.clip(h[:, D_FF:], -SWIGLU_LIMIT, SWIGLU_LIMIT)
        act = glu * jax.nn.sigmoid(SWIGLU_ALPHA * glu) * (lin + 1.0)
        return act @ w_down[e] + b_down[e]

    y_pad = lax.map(expert_block, (x_pad.reshape(n_blocks, MOE_BLOCK, d), block_e)).reshape(-1, d)
    y = y_pad[dest].reshape(n_tok, TOP_K, d)
    return jnp.einsum('nk,nkd->nd', gates, y).reshape(bsz, t, d)


def setup_inputs(seed: int = 0) -> dict:
    key = jax.random.key(seed)
    ks = jax.random.split(key, 24)
    nrm = lambda k, shape, scale: jax.random.normal(k, shape, jnp.float32) * scale
    return {
        "x": nrm(ks[0], (BATCH, SEQ, D_MODEL), 1.0),
        "p": nrm(ks[1], (DEPTH, BATCH, SEQ, PLE_DIM), 1.0),
        "pos_bias": nrm(ks[2], (NUM_BUCKETS, POS_HEADS), 0.2),
        "w_in_even": nrm(ks[3], (N_EVEN, D_MODEL, EVEN_IN), D_MODEL ** -0.5),
        "b_forget": nrm(ks[4], (N_EVEN, B_HEADS), 0.01),
        "sinks": nrm(ks[5], (N_EVEN, A_HEADS), 0.5),
        "w_o_even": nrm(ks[6], (N_EVEN, (A_HEADS + B_HEADS) * HEAD_DIM, D_MODEL), ((A_HEADS + B_HEADS) * HEAD_DIM) ** -0.5 * DEEPNORM_BETA),
        "w_in_odd": nrm(ks[7], (N_ODD, D_MODEL, ODD_IN), D_MODEL ** -0.5),
        "kv_norm": 1.0 + nrm(ks[8], (N_ODD, C_LATENT), 0.01),
        "w_uk": nrm(ks[9], (N_ODD, C_LATENT, C_HEADS, HEAD_DIM), C_LATENT ** -0.5),
        "w_uv": nrm(ks[10], (N_ODD, C_LATENT, C_HEADS, HEAD_DIM), C_LATENT ** -0.5),
        "w_o_odd": nrm(ks[11], (N_ODD, C_HEADS * HEAD_DIM, D_MODEL), (C_HEADS * HEAD_DIM) ** -0.5 * DEEPNORM_BETA),
        "ln_g": 1.0 + nrm(ks[12], (DEPTH, 2, D_MODEL), 0.01),
        "ln_b": nrm(ks[13], (DEPTH, 2, D_MODEL), 0.01),
        "w_router": nrm(ks[14], (DEPTH, D_MODEL, N_EXPERTS), D_MODEL ** -0.5),
        "b_router": nrm(ks[15], (DEPTH, N_EXPERTS), 0.01),
        "w_gate_up": nrm(ks[16], (DEPTH, N_EXPERTS, D_MODEL, 2 * D_FF), D_MODEL ** -0.5),
        "b_gate_up": nrm(ks[17], (DEPTH, N_EXPERTS, 2 * D_FF), 0.01),
        "w_down": nrm(ks[18], (DEPTH, N_EXPERTS, D_FF, D_MODEL), D_FF ** -0.5 * DEEPNORM_BETA),
        "b_down": nrm(ks[19], (DEPTH, N_EXPERTS, D_MODEL), 0.01),
        "w_ple_proj": nrm(ks[20], (DEPTH, PLE_DIM, D_MODEL), PLE_DIM ** -0.5),
        "w_ple_gate": nrm(ks[21], (DEPTH, D_MODEL, D_MODEL), D_MODEL ** -0.5),
    }


def reference(x, p, pos_bias, w_in_even, b_forget, sinks, w_o_even, w_in_odd, kv_norm, w_uk, w_uv,
              w_o_odd, ln_g, ln_b, w_router, b_router, w_gate_up, b_gate_up, w_down, b_down,
              w_ple_proj, w_ple_gate):
    bsz, t, _ = x.shape
    for i in range(DEPTH):
        j = i // 2
        if i % 2 == 0:
            h = x @ w_in_even[j]
            aq, ak, av, bq, bk, bv, bf = jnp.split(h, EVEN_SPLITS, axis=-1)
            o_a = swa_sink_attention(aq.reshape(bsz, t, A_HEADS, HEAD_DIM),
                                     ak.reshape(bsz, t, A_KV_HEADS, HEAD_DIM),
                                     av.reshape(bsz, t, A_KV_HEADS, HEAD_DIM), sinks[j], pos_bias)
            o_b = forgetting_attention(bq.reshape(bsz, t, B_HEADS, HEAD_DIM),
                                       bk.reshape(bsz, t, B_HEADS, HEAD_DIM),
                                       bv.reshape(bsz, t, B_HEADS, HEAD_DIM), bf + b_forget[j])
            mix = jnp.concatenate([o_a, o_b], axis=-1) @ w_o_even[j]
        else:
            h = x @ w_in_odd[j]
            cq, ckv, qi, ki, wi = jnp.split(h, ODD_SPLITS, axis=-1)
            o_c = dsa_latent_attention(cq.reshape(bsz, t, C_HEADS, HEAD_DIM), rms_norm(ckv, kv_norm[j]),
                                       qi.reshape(bsz, t, IDX_HEADS, IDX_DIM), ki, wi,
                                       w_uk[j], w_uv[j], pos_bias)
            mix = o_c @ w_o_odd[j]
        x = layer_norm(DEEPNORM_ALPHA * x + mix, ln_g[i, 0], ln_b[i, 0])
        ffn = moe_ffn(x, w_router[i], b_router[i], w_gate_up[i], b_gate_up[i], w_down[i], b_down[i])
        x = layer_norm(DEEPNORM_ALPHA * x + ffn, ln_g[i, 1], ln_b[i, 1])
        x = x + jax.nn.sigmoid(x @ w_ple_gate[i]) * (p[i] @ w_ple_proj[i])
    return x
```

```python
import functools
import math

import jax
import jax.numpy as jnp
from jax import lax
from jax.experimental import pallas as pl
from jax.experimental.pallas import tpu as pltpu

D_MODEL = 1024
HEAD_DIM = 64
A_HEADS, A_KV_HEADS, WINDOW = 8, 2, 128
B_HEADS = 8
C_HEADS, C_LATENT = 16, 128
IDX_HEADS, IDX_DIM = 8, 64
TOPK_MAX = 256
NUM_BUCKETS, MAX_EXACT, MAX_DISTANCE = 32, 16, 128
N_EXPERTS, TOP_K, D_FF = 32, 4, 1024
SWIGLU_LIMIT, SWIGLU_ALPHA = 7.0, 1.702
PLE_DIM = 256
Q_BLOCK = 128
LN_EPS = 1e-5
NEG = -1e30

LANE = 128
VMEM_LIMIT = 56 * 1024 * 1024
TM_PROJ = 512
TM_MOE = 256
TM_POST = 256
FOX_TQ = 256
DSA_CK = 256
INT_MIN = -2 ** 31

BF16 = jnp.bfloat16
F32 = jnp.float32


def _cparams(n_axes):
    return pltpu.CompilerParams(dimension_semantics=("arbitrary",) * n_axes,
                                vmem_limit_bytes=VMEM_LIMIT)


def _const_spec(shape):
    nd = len(shape)
    return pl.BlockSpec(shape, lambda *_: (0,) * nd)


def _dot(a, b):
    return jnp.dot(a, b, preferred_element_type=F32)


def _dot_nt(a, b):
    return lax.dot_general(a, b, (((1,), (1,)), ((), ())), preferred_element_type=F32)


def _layer_norm(v, g, b):
    mu = jnp.mean(v, axis=-1, keepdims=True)
    d = v - mu
    var = jnp.mean(d * d, axis=-1, keepdims=True)
    return d * lax.rsqrt(var + LN_EPS) * g + b


def _split3(v):
    hi = v.astype(BF16)
    r1 = v - hi.astype(F32)
    mid = r1.astype(BF16)
    lo = (r1 - mid.astype(F32)).astype(BF16)
    return hi, mid, lo


def _proj_kernel(*refs, n_w, rms_index):
    x_ref = refs[0]
    w_refs = refs[1:1 + n_w]
    g_ref = refs[1 + n_w]
    o_refs = refs[2 + n_w:2 + 2 * n_w]
    wb_refs = refs[2 + 2 * n_w:]

    @pl.when(pl.program_id(0) == 0)
    def _():
        for w_ref, wb_ref in zip(w_refs, wb_refs):
            wb_ref[...] = w_ref[...].astype(BF16)

    xb = x_ref[...].astype(BF16)
    for i in range(n_w):
        h = _dot(xb, wb_refs[i][...])
        if i == rms_index:
            h = h * lax.rsqrt(jnp.mean(h * h, axis=-1, keepdims=True) + LN_EPS) * g_ref[...]
        o_refs[i][...] = h.astype(o_refs[i].dtype)


def _project(x2, weights, out_dtypes, rms_gain=None, rms_index=-1):
    n, d = x2.shape
    n_w = len(weights)
    if rms_gain is None:
        rms_gain = jnp.ones((1, LANE), F32)
    in_specs = [pl.BlockSpec((TM_PROJ, d), lambda i: (i, 0))]
    in_specs += [_const_spec(w.shape) for w in weights]
    in_specs += [_const_spec(rms_gain.shape)]
    out_specs = [pl.BlockSpec((TM_PROJ, w.shape[1]), lambda i: (i, 0)) for w in weights]
    out_shape = [jax.ShapeDtypeStruct((n, w.shape[1]), dt) for w, dt in zip(weights, out_dtypes)]
    scratch = [pltpu.VMEM(w.shape, BF16) for w in weights]
    return pl.pallas_call(
        functools.partial(_proj_kernel, n_w=n_w, rms_index=rms_index),
        grid=(n // TM_PROJ,),
        in_specs=in_specs, out_specs=out_specs, out_shape=out_shape,
        scratch_shapes=scratch, compiler_params=_cparams(1), name="in_proj",
    )(x2, *weights, rms_gain)


def _cumsum_kernel(f_ref, b_ref, o_ref):
    z = f_ref[...] + b_ref[...]
    ls = jnp.minimum(z, 0.0) - jnp.log1p(jnp.exp(-jnp.abs(z)))
    t = ls.shape[1]
    r = lax.broadcasted_iota(jnp.int32, (LANE, LANE), 0)
    c = lax.broadcasted_iota(jnp.int32, (LANE, LANE), 1)
    tri = jnp.where(r <= c, 1.0, 0.0).astype(BF16)
    carry = jnp.zeros((ls.shape[0], 1), F32)
    for blk in range(t // LANE):
        seg = ls[:, blk * LANE:(blk + 1) * LANE]
        hi, mid, lo = _split3(seg)
        pre = _dot(hi, tri) + _dot(mid, tri) + _dot(lo, tri) + carry
        o_ref[:, blk * LANE:(blk + 1) * LANE] = pre
        carry = pre[:, LANE - 1:LANE]


def _forget_cumsum(f_t, b_forget):
    bsz, h, t = f_t.shape
    return pl.pallas_call(
        _cumsum_kernel,
        grid=(bsz,),
        in_specs=[pl.BlockSpec((None, h, t), lambda b: (b, 0, 0)), _const_spec((h, 1))],
        out_specs=pl.BlockSpec((None, h, t), lambda b: (b, 0, 0)),
        out_shape=jax.ShapeDtypeStruct((bsz, h, t), F32),
        compiler_params=_cparams(1), name="forget_cumsum",
    )(f_t, b_forget.reshape(h, 1).astype(F32))


def _fox_kernel(q_ref, k_ref, v_ref, cq_ref, ck_ref, o_ref):
    i = pl.program_id(2)
    tq = q_ref.shape[0]
    tk = tq
    q = q_ref[...]
    lane = lax.broadcasted_iota(jnp.int32, (tq, LANE), 1)
    rel = (lax.broadcasted_iota(jnp.int32, (tq, tk), 0)
           - lax.broadcasted_iota(jnp.int32, (tq, tk), 1))
    outs = []
    for hh in range(2):
        in_half = (lane >= hh * HEAD_DIM) & (lane < (hh + 1) * HEAD_DIM)
        qh = jnp.where(in_half, q, jnp.zeros_like(q)) * jnp.asarray(HEAD_DIM ** -0.5, BF16)
        cq = cq_ref[:, hh:hh + 1]

        def body(j, carry, qh=qh, cq=cq, hh=hh):
            m, l, acc = carry
            start = pl.multiple_of(j * tk, tk)
            ks = k_ref[pl.ds(start, tk), :]
            vs = v_ref[pl.ds(start, tk), :]
            ck = ck_ref[hh, pl.ds(j, 1), :]
            s = _dot_nt(qh, ks) + cq - ck
            s = jnp.where(rel + (i - j) * tq >= 0, s, NEG)
            m_new = jnp.maximum(m, jnp.max(s, axis=-1, keepdims=True))
            alpha = jnp.exp(m - m_new)
            p = jnp.exp(s - m_new)
            l = alpha * l + jnp.sum(p, axis=-1, keepdims=True)
            acc = alpha * acc + _dot(p.astype(BF16), vs)
            return m_new, l, acc

        init = (jnp.full((tq, 1), NEG, F32), jnp.zeros((tq, 1), F32), jnp.zeros((tq, LANE), F32))
        m, l, acc = lax.fori_loop(0, i + 1, body, init)
        outs.append(acc / l)
    o_ref[...] = jnp.where(lane < HEAD_DIM, outs[0], outs[1]).astype(o_ref.dtype)


def _fox_attention(q, k, v, c_col, c_row, bsz, t):
    n, hd = q.shape
    n_pair = hd // LANE
    nq = t // FOX_TQ
    return pl.pallas_call(
        _fox_kernel,
        grid=(bsz, n_pair, nq),
        in_specs=[
            pl.BlockSpec((FOX_TQ, LANE), lambda b, j, i: (b * nq + i, j)),
            pl.BlockSpec((t, LANE), lambda b, j, i: (b, j)),
            pl.BlockSpec((t, LANE), lambda b, j, i: (b, j)),
            pl.BlockSpec((None, None, FOX_TQ, 2), lambda b, j, i: (b, j, i, 0)),
            pl.BlockSpec((None, None, 2, nq, FOX_TQ), lambda b, j, i: (b, j, 0, 0, 0)),
        ],
        out_specs=pl.BlockSpec((FOX_TQ, LANE), lambda b, j, i: (b * nq + i, j)),
        out_shape=jax.ShapeDtypeStruct((n, hd), BF16),
        compiler_params=_cparams(3), name="fox_attention",
    )(q, k, v, c_col, c_row)


def _swa_kernel(sink_ref, q_ref, kp_ref, kc_ref, vp_ref, vc_ref, bias_ref, o_ref):
    nblk = pl.program_id(1)
    w = q_ref.shape[0]
    kk = jnp.concatenate([kp_ref[...], kc_ref[...]], axis=0)
    vv = jnp.concatenate([vp_ref[...], vc_ref[...]], axis=0)
    lane = lax.broadcasted_iota(jnp.int32, (w, LANE), 1)
    col = lax.broadcasted_iota(jnp.int32, (w, 2 * w), 1)
    no_prev = (col < w) & (nblk == 0)
    grp = A_HEADS // A_KV_HEADS
    for g in range(grp):
        qg = q_ref[:, g * LANE:(g + 1) * LANE]
        outs = []
        for hh in range(A_KV_HEADS):
            head = g + grp * hh
            in_half = (lane >= hh * HEAD_DIM) & (lane < (hh + 1) * HEAD_DIM)
            qh = jnp.where(in_half, qg, jnp.zeros_like(qg)) * jnp.asarray(HEAD_DIM ** -0.5, BF16)
            s = _dot_nt(qh, kk) + bias_ref[head]
            s = jnp.where(no_prev, NEG, s)
            sink = sink_ref[head]
            m = jnp.maximum(jnp.max(s, axis=-1, keepdims=True), sink)
            e = jnp.exp(s - m)
            denom = jnp.sum(e, axis=-1, keepdims=True) + jnp.exp(sink - m)
            p = e / denom
            outs.append(_dot(p.astype(BF16), vv))
        o_ref[:, g * LANE:(g + 1) * LANE] = jnp.where(lane < HEAD_DIM, outs[0], outs[1]).astype(o_ref.dtype)


def _swa_attention(q, k, v, sinks, bias, bsz, t):
    n = q.shape[0]
    nb = t // WINDOW
    cur = lambda b, i: (b * nb + i, 0)
    prev = lambda b, i: (b * nb + jnp.maximum(i - 1, 0), 0)
    return pl.pallas_call(
        _swa_kernel,
        grid=(bsz, nb),
        in_specs=[
            pl.BlockSpec(memory_space=pltpu.SMEM),
            pl.BlockSpec((WINDOW, A_HEADS * HEAD_DIM), cur),
            pl.BlockSpec((WINDOW, LANE), prev), pl.BlockSpec((WINDOW, LANE), cur),
            pl.BlockSpec((WINDOW, LANE), prev), pl.BlockSpec((WINDOW, LANE), cur),
            _const_spec(bias.shape),
        ],
        out_specs=pl.BlockSpec((WINDOW, A_HEADS * HEAD_DIM), cur),
        out_shape=jax.ShapeDtypeStruct((n, A_HEADS * HEAD_DIM), BF16),
        compiler_params=_cparams(2), name="swa_attention",
    )(sinks.astype(F32), q, k, k, v, v, bias)


def _float_order_key(s):
    bits = pltpu.bitcast(s, jnp.int32)
    return bits ^ ((bits >> 31) & jnp.int32(0x7FFFFFFF))


def _dsa_kernel(q_ref, ckv_ref, qi_ref, ki_ref, wi_ref, wuk_ref, wuv_ref, tb_ref, o_ref,
                key_s, sel_s, qlat_s, qis_s, s_buf, p_buf, acc_s, m_s, l_s, a_s):
    n = pl.program_id(1)
    qb = Q_BLOCK
    ck = DSA_CK
    n_sel = TOPK_MAX
    n_chunks = n // (ck // qb) + 1
    lane = lax.broadcasted_iota(jnp.int32, (qb, LANE), 1)
    qpos = n * qb + lax.broadcasted_iota(jnp.int32, (qb, ck), 0)
    kcol = lax.broadcasted_iota(jnp.int32, (qb, ck), 1)

    for h in range(IDX_HEADS):
        pair = qi_ref[:, (h // 2) * LANE:(h // 2 + 1) * LANE]
        in_half = (lane >= (h % 2) * IDX_DIM) & (lane < (h % 2 + 1) * IDX_DIM)
        qis_s[h * qb:(h + 1) * qb, :] = jnp.where(in_half, pair, jnp.zeros_like(pair))

    for h in range(C_HEADS):
        pair = q_ref[:, (h // 2) * LANE:(h // 2 + 1) * LANE]
        ql = _dot(pair, wuk_ref[h]) * (HEAD_DIM ** -0.5)
        qlat_s[h * qb:(h + 1) * qb, :] = ql.astype(BF16)

    wi = wi_ref[...] * (IDX_HEADS ** -0.5)

    def score_body(c, _):
        start = pl.multiple_of(c * ck, ck)
        kc = ki_ref[pl.ds(start, ck), :]
        raw = _dot_nt(qis_s[...], kc)
        score = jnp.zeros((qb, ck), F32)
        for h in range(IDX_HEADS):
            idx_s = jnp.maximum(raw[h * qb:(h + 1) * qb, :] * (IDX_DIM ** -0.5), 0.0)
            score = score + idx_s * wi[:, h:h + 1]
        causal = (start + kcol) <= qpos
        key_s[c] = _float_order_key(jnp.where(causal, score, NEG))
        sel_s[c] = jnp.where(causal, 0.0, NEG)
        return 0

    lax.fori_loop(0, n_chunks, score_body, 0)

    @pl.when((n + 1) * qb > n_sel)
    def _():
        def count_where(pred_fn):
            def body(c, acc):
                hit = jnp.where(pred_fn(key_s[c]), 1.0, 0.0)
                return acc + hit[:, :LANE] + hit[:, LANE:]
            acc = lax.fori_loop(0, n_chunks, body, jnp.zeros((qb, LANE), F32))
            return jnp.sum(acc, axis=-1, keepdims=True)

        cnt0 = count_where(lambda kv: kv >= 0)
        prefix0 = jnp.where(cnt0 >= n_sel, jnp.int32(0), jnp.int32(INT_MIN))

        def bit_body(t, prefix):
            cand = prefix | jnp.left_shift(jnp.int32(1), 30 - t)
            cnt = count_where(lambda kv: kv >= cand)
            return jnp.where(cnt >= n_sel, cand, prefix)

        thr = lax.fori_loop(0, 31, bit_body, prefix0)
        need = n_sel - count_where(lambda kv: kv > thr)
        r = lax.broadcasted_iota(jnp.int32, (ck, ck), 0)
        cc = lax.broadcasted_iota(jnp.int32, (ck, ck), 1)
        before = jnp.where(r < cc, 1.0, 0.0).astype(BF16)

        def tie_body(c, seen):
            kv = key_s[c]
            tie = kv == thr
            tie_f = jnp.where(tie, 1.0, 0.0)
            earlier = _dot(tie_f.astype(BF16), before) + seen
            sel = (kv > thr) | (tie & (earlier < need))
            sel_s[c] = jnp.where(sel, sel_s[c], NEG)
            return seen + jnp.sum(tie_f, axis=-1, keepdims=True)

        lax.fori_loop(0, n_chunks, tie_body, jnp.zeros((qb, 1), F32))

    m_s[...] = jnp.full(m_s.shape, NEG, F32)
    l_s[...] = jnp.zeros(l_s.shape, F32)
    acc_s[...] = jnp.zeros(acc_s.shape, F32)

    def attn_body(c, _):
        start = pl.multiple_of(c * ck, ck)
        kc = ckv_ref[pl.ds(start, ck), :]
        s_buf[...] = _dot_nt(qlat_s[...], kc)
        rel_blk = jnp.minimum(n - c * (ck // qb), 3)
        mask_add = sel_s[c]
        for h in range(C_HEADS):
            rows = slice(h * qb, (h + 1) * qb)
            s = s_buf[rows, :] + tb_ref[rel_blk, h] + mask_add
            m_old = m_s[rows, :]
            m_new = jnp.maximum(m_old, jnp.max(s, axis=-1, keepdims=True))
            p = jnp.where(s > 0.5 * NEG, jnp.exp(s - m_new), 0.0)
            alpha = jnp.exp(m_old - m_new)
            l_s[rows, :] = alpha * l_s[rows, :] + jnp.sum(p, axis=-1, keepdims=True)
            m_s[rows, :] = m_new
            a_s[rows, :] = alpha
            p_buf[rows, :] = p.astype(BF16)
        pv = _dot(p_buf[...], kc)
        acc_s[...] = acc_s[...] * a_s[...] + pv
        return 0

    lax.fori_loop(0, n_chunks, attn_body, 0)

    for g in range(C_HEADS // 2):
        out = jnp.zeros((qb, LANE), F32)
        for hh in range(2):
            h = 2 * g + hh
            rows = slice(h * qb, (h + 1) * qb)
            o_lat = (acc_s[rows, :] / l_s[rows, :]).astype(BF16)
            out = out + _dot(o_lat, wuv_ref[h])
        o_ref[:, g * LANE:(g + 1) * LANE] = out.astype(o_ref.dtype)


def _dsa_attention(q, ckv, qi, ki2, wi, wuk_wide, wuv_wide, tb, bsz, t):
    n = q.shape[0]
    nq = t // Q_BLOCK
    nck = t // DSA_CK
    rows = C_HEADS * Q_BLOCK
    blk = lambda b, i: (b * nq + i, 0)
    per_b = lambda b, i: (b, 0)
    return pl.pallas_call(
        _dsa_kernel,
        grid=(bsz, nq),
        in_specs=[
            pl.BlockSpec((Q_BLOCK, C_HEADS * HEAD_DIM), blk),
            pl.BlockSpec((t, C_LATENT), per_b),
            pl.BlockSpec((Q_BLOCK, IDX_HEADS * IDX_DIM), blk),
            pl.BlockSpec((t, LANE), per_b),
            pl.BlockSpec((Q_BLOCK, LANE), blk),
            _const_spec(wuk_wide.shape), _const_spec(wuv_wide.shape), _const_spec(tb.shape),
        ],
        out_specs=pl.BlockSpec((Q_BLOCK, C_HEADS * HEAD_DIM), blk),
        out_shape=jax.ShapeDtypeStruct((n, C_HEADS * HEAD_DIM), BF16),
        scratch_shapes=[
            pltpu.VMEM((nck, Q_BLOCK, DSA_CK), jnp.int32),
            pltpu.VMEM((nck, Q_BLOCK, DSA_CK), F32),
            pltpu.VMEM((rows, C_LATENT), BF16),
            pltpu.VMEM((IDX_HEADS * Q_BLOCK, LANE), BF16),
            pltpu.VMEM((rows, DSA_CK), F32),
            pltpu.VMEM((rows, DSA_CK), BF16),
            pltpu.VMEM((rows, C_LATENT), F32),
            pltpu.VMEM((rows, 1), F32), pltpu.VMEM((rows, 1), F32), pltpu.VMEM((rows, 1), F32),
        ],
        compiler_params=_cparams(2), name="dsa_attention",
    )(q, ckv, qi, ki2, wi, wuk_wide, wuv_wide, tb)


def _post_attn_kernel(*refs, n_mix, alpha):
    x_ref = refs[0]
    mix_refs = refs[1:1 + n_mix]
    w_refs = refs[1 + n_mix:1 + 2 * n_mix]
    g_ref, b_ref, wr_ref, br_ref = refs[1 + 2 * n_mix:5 + 2 * n_mix]
    xo_ref, idx_ref, gate_ref, rank_ref, cnt_ref = refs[5 + 2 * n_mix:10 + 2 * n_mix]
    scr = refs[10 + 2 * n_mix:]
    wb_refs = scr[:n_mix]
    wr3_ref, before_ref, carry_ref = scr[n_mix:]
    tm = x_ref.shape[0]

    @pl.when(pl.program_id(0) == 0)
    def _():
        for w_ref, wb_ref in zip(w_refs, wb_refs):
            wb_ref[...] = w_ref[...].astype(BF16)
        hi, mid, lo = _split3(wr_ref[...])
        wr3_ref[0] = hi
        wr3_ref[1] = mid
        wr3_ref[2] = lo
        r = lax.broadcasted_iota(jnp.int32, (tm, tm), 0)
        c = lax.broadcasted_iota(jnp.int32, (tm, tm), 1)
        before_ref[...] = jnp.where(r < c, 1.0, 0.0).astype(BF16)
        carry_ref[...] = jnp.zeros(carry_ref.shape, F32)

    mix = _dot(mix_refs[0][...], wb_refs[0][...])
    for i in range(1, n_mix):
        mix = mix + _dot(mix_refs[i][...], wb_refs[i][...])
    xn = _layer_norm(alpha * x_ref[...] + mix, g_ref[...], b_ref[...])
    xo_ref[...] = xn

    xh, xm, xl = _split3(xn)
    logits = (_dot_nt(wr3_ref[0], xh) + _dot_nt(wr3_ref[0], xm) + _dot_nt(wr3_ref[1], xh)
              + _dot_nt(wr3_ref[0], xl) + _dot_nt(wr3_ref[1], xm) + _dot_nt(wr3_ref[2], xh)
              + br_ref[...])
    eidx = lax.broadcasted_iota(jnp.int32, logits.shape, 0)
    cur = logits
    vals, idxs = [], []
    for _ in range(TOP_K):
        mx = jnp.max(cur, axis=0, keepdims=True)
        first = jnp.min(jnp.where(cur == mx, eidx, N_EXPERTS), axis=0, keepdims=True)
        vals.append(mx)
        idxs.append(first)
        cur = jnp.where(eidx == first, -jnp.inf, cur)
    es = [jnp.exp(v - vals[0]) for v in vals]
    tot = es[0] + es[1] + es[2] + es[3]
    member = jnp.zeros(logits.shape, F32)
    for k in range(TOP_K):
        idx_ref[k:k + 1, :] = idxs[k]
        gate_ref[k:k + 1, :] = es[k] / tot
        member = member + jnp.where(eidx == idxs[k], 1.0, 0.0)
    earlier = _dot(member.astype(BF16), before_ref[...]) + carry_ref[...]
    for k in range(TOP_K):
        rk = jnp.sum(jnp.where(eidx == idxs[k], earlier, 0.0), axis=0, keepdims=True)
        rank_ref[k:k + 1, :] = rk.astype(jnp.int32)
    carry_ref[...] = carry_ref[...] + jnp.sum(member, axis=1, keepdims=True)
    cnt_ref[...] = carry_ref[...].astype(jnp.int32)


def _post_attn(x2, mixes, w_os, ln_g, ln_b, w_router, b_router, alpha):
    n, d = x2.shape
    n_mix = len(mixes)
    tm = TM_PROJ
    row = lambda i: (i, 0)
    colblk = lambda i: (0, i)
    in_specs = [pl.BlockSpec((tm, d), row)]
    in_specs += [pl.BlockSpec((tm, m.shape[1]), row) for m in mixes]
    in_specs += [_const_spec(w.shape) for w in w_os]
    in_specs += [_const_spec((1, d)), _const_spec((1, d)), _const_spec((N_EXPERTS, d)),
                 _const_spec((N_EXPERTS, 1))]
    out_specs = [pl.BlockSpec((tm, d), row), pl.BlockSpec((TOP_K, tm), colblk),
                 pl.BlockSpec((TOP_K, tm), colblk), pl.BlockSpec((TOP_K, tm), colblk),
                 _const_spec((N_EXPERTS, 1))]
    out_shape = [jax.ShapeDtypeStruct((n, d), F32), jax.ShapeDtypeStruct((TOP_K, n), jnp.int32),
                 jax.ShapeDtypeStruct((TOP_K, n), F32), jax.ShapeDtypeStruct((TOP_K, n), jnp.int32),
                 jax.ShapeDtypeStruct((N_EXPERTS, 1), jnp.int32)]
    scratch = [pltpu.VMEM(w.shape, BF16) for w in w_os]
    scratch += [pltpu.VMEM((3, N_EXPERTS, d), BF16), pltpu.VMEM((tm, tm), BF16),
                pltpu.VMEM((N_EXPERTS, 1), F32)]
    return pl.pallas_call(
        functools.partial(_post_attn_kernel, n_mix=n_mix, alpha=alpha),
        grid=(n // tm,),
        in_specs=in_specs, out_specs=out_specs, out_shape=out_shape, scratch_shapes=scratch,
        compiler_params=_cparams(1), name="post_attn",
    )(x2, *mixes, *w_os, ln_g.reshape(1, d), ln_b.reshape(1, d), w_router.T,
      b_router.reshape(N_EXPERTS, 1))


def _moe_kernel(be_ref, nu_ref, x_ref, wgu_ref, bgu_ref, wd_ref, bd_ref, y_ref, wgu_s, wd_s):
    i = pl.program_id(0)
    prev = be_ref[jnp.maximum(i - 1, 0)]
    live = i < nu_ref[0]

    @pl.when(live & ((i == 0) | (be_ref[i] != prev)))
    def _():
        wgu_s[...] = wgu_ref[...].astype(BF16)
        wd_s[...] = wd_ref[...].astype(BF16)

    @pl.when(live)
    def _():
        xb = x_ref[...]
        ch = 512
        acc = jnp.zeros(y_ref.shape, F32)
        for j in range(D_FF // ch):
            hg = _dot(xb, wgu_s[:, j * ch:(j + 1) * ch]) + bgu_ref[:, j * ch:(j + 1) * ch]
            hl = (_dot(xb, wgu_s[:, D_FF + j * ch:D_FF + (j + 1) * ch])
                  + bgu_ref[:, D_FF + j * ch:D_FF + (j + 1) * ch])
            glu = jnp.minimum(hg, SWIGLU_LIMIT)
            lin = jnp.clip(hl, -SWIGLU_LIMIT, SWIGLU_LIMIT)
            act = glu * jax.nn.sigmoid(SWIGLU_ALPHA * glu) * (lin + 1.0)
            acc = acc + _dot(act.astype(BF16), wd_s[j * ch:(j + 1) * ch, :])
        y_ref[...] = acc + bd_ref[...]

    @pl.when(jnp.logical_not(live))
    def _():
        y_ref[...] = jnp.zeros(y_ref.shape, F32)


def _moe_ffn(x_pad, block_e, n_used, w_gate_up, b_gate_up, w_down, b_down):
    rows, d = x_pad.shape
    n_blocks = rows // TM_MOE
    grid_spec = pltpu.PrefetchScalarGridSpec(
        num_scalar_prefetch=2,
        grid=(n_blocks,),
        in_specs=[
            pl.BlockSpec((TM_MOE, d), lambda i, be, nu: (i, 0)),
            pl.BlockSpec((None, d, 2 * D_FF), lambda i, be, nu: (be[i], 0, 0)),
            pl.BlockSpec((None, 1, 2 * D_FF), lambda i, be, nu: (be[i], 0, 0)),
            pl.BlockSpec((None, D_FF, d), lambda i, be, nu: (be[i], 0, 0)),
            pl.BlockSpec((None, 1, d), lambda i, be, nu: (be[i], 0, 0)),
        ],
        out_specs=pl.BlockSpec((TM_MOE, d), lambda i, be, nu: (i, 0)),
        scratch_shapes=[pltpu.VMEM((d, 2 * D_FF), BF16), pltpu.VMEM((D_FF, d), BF16)],
    )
    return pl.pallas_call(
        _moe_kernel, grid_spec=grid_spec,
        out_shape=jax.ShapeDtypeStruct((rows, d), F32),
        compiler_params=_cparams(1), name="moe_ffn",
    )(block_e, n_used, x_pad, w_gate_up, b_gate_up.reshape(N_EXPERTS, 1, 2 * D_FF), w_down,
      b_down.reshape(N_EXPERTS, 1, d))


def _post_moe_kernel(x_ref, y_ref, gt_ref, g_ref, b_ref, p_ref, wg_ref, wp_ref, o_ref, wg_s, wp_s,
                     *, alpha):
    @pl.when(pl.program_id(0) == 0)
    def _():
        wg_s[...] = wg_ref[...].astype(BF16)
        wp_s[...] = wp_ref[...].astype(BF16)

    d = x_ref.shape[1]
    gates = gt_ref[...]
    ffn = gates[:, 0:1] * y_ref[:, 0:d]
    for k in range(1, TOP_K):
        ffn = ffn + gates[:, k:k + 1] * y_ref[:, k * d:(k + 1) * d]
    xn = _layer_norm(alpha * x_ref[...] + ffn, g_ref[...], b_ref[...])
    gate = jax.nn.sigmoid(_dot(xn.astype(BF16), wg_s[...]))
    emb = _dot(p_ref[...].astype(BF16), wp_s[...])
    o_ref[...] = xn + gate * emb


def _post_moe(x2, y_sel, gates_t, ln_g, ln_b, p2, w_ple_gate, w_ple_proj, alpha):
    n, d = x2.shape
    tm = TM_POST
    row = lambda i: (i, 0)
    return pl.pallas_call(
        functools.partial(_post_moe_kernel, alpha=alpha),
        grid=(n // tm,),
        in_specs=[pl.BlockSpec((tm, d), row), pl.BlockSpec((tm, TOP_K * d), row),
                  pl.BlockSpec((tm, TOP_K), row), _const_spec((1, d)), _const_spec((1, d)),
                  pl.BlockSpec((tm, PLE_DIM), row), _const_spec((d, d)), _const_spec((PLE_DIM, d))],
        out_specs=pl.BlockSpec((tm, d), row),
        out_shape=jax.ShapeDtypeStruct((n, d), F32),
        scratch_shapes=[pltpu.VMEM((d, d), BF16), pltpu.VMEM((PLE_DIM, d), BF16)],
        compiler_params=_cparams(1), name="post_moe",
    )(x2, y_sel, gates_t, ln_g.reshape(1, d), ln_b.reshape(1, d), p2, w_ple_gate, w_ple_proj)


def _t5_bucket(dist):
    d = jnp.maximum(dist, 0)
    ratio = jnp.log(jnp.maximum(d, 1).astype(F32) / MAX_EXACT) / math.log(MAX_DISTANCE / MAX_EXACT)
    large = MAX_EXACT + (ratio * (NUM_BUCKETS - MAX_EXACT)).astype(jnp.int32)
    large = jnp.minimum(large, NUM_BUCKETS - 1)
    return jnp.where(d < MAX_EXACT, d, large)


def _swa_bias_table(pos_bias):
    qi = jnp.arange(WINDOW)[:, None]
    kj = jnp.arange(2 * WINDOW)[None, :]
    dist = qi + WINDOW - kj
    valid = (dist >= 0) & (dist < WINDOW)
    bias = pos_bias[_t5_bucket(dist)][..., :A_HEADS].astype(F32).transpose(2, 0, 1)
    return jnp.where(valid[None], bias, NEG)


def _dsa_bias_tables(pos_bias):
    qi = jnp.arange(Q_BLOCK)[:, None]
    kj = jnp.arange(DSA_CK)[None, :]
    tabs = []
    for r in range(4):
        dist = r * Q_BLOCK + qi - kj
        tabs.append(pos_bias[_t5_bucket(dist)].astype(F32).transpose(2, 0, 1))
    return jnp.stack(tabs)


def _pad_cols(w, width=LANE):
    return jnp.pad(w, ((0, 0), (0, width - w.shape[1])))


def _even_mixer(x2, bsz, t, w_in, b_forget, sinks, w_o, swa_bias):
    hd = HEAD_DIM
    grp = A_HEADS // A_KV_HEADS
    pair_order = [g + grp * hh for g in range(grp) for hh in range(A_KV_HEADS)]
    na, nkv, nb = A_HEADS * hd, A_KV_HEADS * hd, B_HEADS * hd
    offs = [0, na, na + nkv, na + 2 * nkv, na + 2 * nkv + nb, na + 2 * nkv + 2 * nb,
            na + 2 * nkv + 3 * nb]
    w_aq = w_in[:, offs[0]:offs[1]].reshape(D_MODEL, A_HEADS, hd)[:, pair_order].reshape(D_MODEL, na)
    weights = [w_aq, w_in[:, offs[1]:offs[2]], w_in[:, offs[2]:offs[3]], w_in[:, offs[3]:offs[4]],
               w_in[:, offs[4]:offs[5]], w_in[:, offs[5]:offs[6]], _pad_cols(w_in[:, offs[6]:])]
    aq, ak, av, bq, bk, bv, bf = _project(x2, weights, [BF16] * 6 + [F32])
    o_a = _swa_attention(aq, ak, av, sinks, swa_bias, bsz, t)

    f_t = bf[:, :B_HEADS].reshape(bsz, t, B_HEADS).transpose(0, 2, 1)
    c = _forget_cumsum(f_t, b_forget)
    c_row = c.reshape(bsz, B_HEADS // 2, 2, t // FOX_TQ, FOX_TQ)
    c_col = c.reshape(bsz, B_HEADS // 2, 2, t).transpose(0, 1, 3, 2)
    o_b = _fox_attention(bq, bk, bv, c_col, c_row, bsz, t)

    w_oa = w_o[:na].reshape(A_HEADS, hd, D_MODEL)[jnp.asarray(pair_order)].reshape(na, D_MODEL)
    return [o_a, o_b], [w_oa, w_o[na:]]


def _odd_mixer(x2, bsz, t, w_in, kv_norm, w_uk, w_uv, w_o, dsa_tb):
    hd = HEAD_DIM
    nq, nl, ni = C_HEADS * hd, C_LATENT, IDX_HEADS * IDX_DIM
    w_ki = w_in[:, nq + nl + ni:nq + nl + ni + IDX_DIM]
    weights = [w_in[:, :nq], w_in[:, nq:nq + nl], w_in[:, nq + nl:nq + nl + ni],
               jnp.concatenate([w_ki, w_ki], axis=1), _pad_cols(w_in[:, nq + nl + ni + IDX_DIM:])]
    cq, ckv, qi, ki2, wi = _project(x2, weights, [BF16, BF16, BF16, BF16, F32],
                                    rms_gain=kv_norm.reshape(1, nl).astype(F32), rms_index=1)
    uk = w_uk.transpose(1, 2, 0)
    uv = w_uv.transpose(1, 0, 2)
    zk = jnp.zeros_like(uk)
    zv = jnp.zeros_like(uv)
    odd = (jnp.arange(C_HEADS) % 2 == 1)[:, None, None]
    wuk_wide = jnp.where(odd, jnp.concatenate([zk, uk], axis=1), jnp.concatenate([uk, zk], axis=1))
    wuv_wide = jnp.where(odd, jnp.concatenate([zv, uv], axis=2), jnp.concatenate([uv, zv], axis=2))
    o_c = _dsa_attention(cq, ckv, qi, ki2, wi, wuk_wide.astype(BF16), wuv_wide.astype(BF16),
                         dsa_tb, bsz, t)
    return [o_c], [w_o]


def _dispatch_plan(top_idx, rank, counts, n_tok):
    counts = counts.reshape(N_EXPERTS)
    padded = (counts + TM_MOE - 1) // TM_MOE * TM_MOE
    pad_ends = jnp.cumsum(padded)
    pad_starts = pad_ends - padded
    dest = pad_starts[top_idx] + rank
    n_blocks = (n_tok * TOP_K) // TM_MOE + N_EXPERTS
    block_e = jnp.minimum(
        jnp.searchsorted(pad_ends, jnp.arange(n_blocks, dtype=jnp.int32) * TM_MOE, side="right"),
        N_EXPERTS - 1).astype(jnp.int32)
    n_used = (pad_ends[-1] // TM_MOE).astype(jnp.int32).reshape(1)
    return dest, block_e, n_used, n_blocks


def kernel(x, p, pos_bias, w_in_even, b_forget, sinks, w_o_even, w_in_odd, kv_norm, w_uk, w_uv,
           w_o_odd, ln_g, ln_b, w_router, b_router, w_gate_up, b_gate_up, w_down, b_down,
           w_ple_proj, w_ple_gate):
    bsz, t, d = x.shape
    depth = ln_g.shape[0]
    alpha = (2 * depth) ** 0.25
    n_tok = bsz * t
    x2 = x.reshape(n_tok, d)
    swa_bias = _swa_bias_table(pos_bias)
    dsa_tb = _dsa_bias_tables(pos_bias)
    tok_ids = jnp.tile(jnp.arange(n_tok, dtype=jnp.int32), (TOP_K,))
    for i in range(depth):
        j = i // 2
        if i % 2 == 0:
            mixes, w_os = _even_mixer(x2, bsz, t, w_in_even[j], b_forget[j], sinks[j], w_o_even[j],
                                      swa_bias)
        else:
            mixes, w_os = _odd_mixer(x2, bsz, t, w_in_odd[j], kv_norm[j], w_uk[j], w_uv[j],
                                     w_o_odd[j], dsa_tb)
        x1, top_idx, gates, rank, counts = _post_attn(x2, mixes, w_os, ln_g[i, 0], ln_b[i, 0],
                                                      w_router[i], b_router[i], alpha)
        dest, block_e, n_used, n_blocks = _dispatch_plan(top_idx, rank, counts, n_tok)
        src_tok = jnp.zeros((n_blocks * TM_MOE,), jnp.int32).at[dest.reshape(-1)].set(tok_ids)
        x_pad = jnp.take(x1.astype(BF16), src_tok, axis=0)
        y_pad = _moe_ffn(x_pad, block_e, n_used, w_gate_up[i], b_gate_up[i], w_down[i], b_down[i])
        y_sel = jnp.take(y_pad, dest.T.reshape(-1), axis=0).reshape(n_tok, TOP_K * d)
        x2 = _post_moe(x1, y_sel, gates.T, ln_g[i, 1], ln_b[i, 1], p[i].reshape(n_tok, PLE_DIM),
                       w_ple_gate[i], w_ple_proj[i], alpha)
    return x2.reshape(bsz, t, d)
```

```python
import functools
import math

import jax
import jax.numpy as jnp
from jax import lax
from jax.experimental import pallas as pl
from jax.experimental.pallas import tpu as pltpu

D_MODEL = 1024
HEAD_DIM = 64
A_HEADS, A_KV_HEADS, WINDOW = 8, 2, 128
B_HEADS = 8
C_HEADS, C_LATENT = 16, 128
IDX_HEADS, IDX_DIM = 8, 64
TOPK_MAX = 256
NUM_BUCKETS, MAX_EXACT, MAX_DISTANCE = 32, 16, 128
N_EXPERTS, TOP_K, D_FF = 32, 4, 1024
SWIGLU_LIMIT, SWIGLU_ALPHA = 7.0, 1.702
PLE_DIM = 256
Q_BLOCK = 128
LN_EPS = 1e-5
NEG = -1e30

LANE = 128
VMEM_LIMIT = 56 * 1024 * 1024
TM_PROJ = 512
TM_MOE = 256
TM_POST = 256
FOX_TQ = 256
DSA_CK = 256
INT_MIN = -2 ** 31

BF16 = jnp.bfloat16
F32 = jnp.float32


def _cparams(n_axes):
    return pltpu.CompilerParams(dimension_semantics=("arbitrary",) * n_axes,
                                vmem_limit_bytes=VMEM_LIMIT)


def _const_spec(shape):
    nd = len(shape)
    return pl.BlockSpec(shape, lambda *_: (0,) * nd)


def _dot(a, b):
    return jnp.dot(a, b, preferred_element_type=F32)


def _dot_nt(a, b):
    return lax.dot_general(a, b, (((1,), (1,)), ((), ())), preferred_element_type=F32)


def _layer_norm(v, g, b):
    mu = jnp.mean(v, axis=-1, keepdims=True)
    d = v - mu
    var = jnp.mean(d * d, axis=-1, keepdims=True)
    return d * lax.rsqrt(var + LN_EPS) * g + b


def _split3(v):
    hi = v.astype(BF16)
    r1 = v - hi.astype(F32)
    mid = r1.astype(BF16)
    lo = (r1 - mid.astype(F32)).astype(BF16)
    return hi, mid, lo


def _proj_kernel(*refs, n_w, rms_index):
    x_ref = refs[0]
    w_refs = refs[1:1 + n_w]
    g_ref = refs[1 + n_w]
    o_refs = refs[2 + n_w:2 + 2 * n_w]
    wb_refs = refs[2 + 2 * n_w:]

    @pl.when(pl.program_id(0) == 0)
    def _():
        for w_ref, wb_ref in zip(w_refs, wb_refs):
            wb_ref[...] = w_ref[...].astype(BF16)

    xb = x_ref[...].astype(BF16)
    for i in range(n_w):
        h = _dot(xb, wb_refs[i][...])
        if i == rms_index:
            h = h * lax.rsqrt(jnp.mean(h * h, axis=-1, keepdims=True) + LN_EPS) * g_ref[...]
        o_refs[i][...] = h.astype(o_refs[i].dtype)


def _project(x2, weights, out_dtypes, rms_gain=None, rms_index=-1):
    n, d = x2.shape
    n_w = len(weights)
    if rms_gain is None:
        rms_gain = jnp.ones((1, LANE), F32)
    in_specs = [pl.BlockSpec((TM_PROJ, d), lambda i: (i, 0))]
    in_specs += [_const_spec(w.shape) for w in weights]
    in_specs += [_const_spec(rms_gain.shape)]
    out_specs = [pl.BlockSpec((TM_PROJ, w.shape[1]), lambda i: (i, 0)) for w in weights]
    out_shape = [jax.ShapeDtypeStruct((n, w.shape[1]), dt) for w, dt in zip(weights, out_dtypes)]
    scratch = [pltpu.VMEM(w.shape, BF16) for w in weights]
    return pl.pallas_call(
        functools.partial(_proj_kernel, n_w=n_w, rms_index=rms_index),
        grid=(n // TM_PROJ,),
        in_specs=in_specs, out_specs=out_specs, out_shape=out_shape,
        scratch_shapes=scratch, compiler_params=_cparams(1), name="in_proj",
    )(x2, *weights, rms_gain)


def _cumsum_kernel(f_ref, b_ref, o_ref):
    z = f_ref[...] + b_ref[...]
    ls = jnp.minimum(z, 0.0) - jnp.log1p(jnp.exp(-jnp.abs(z)))
    t = ls.shape[1]
    r = lax.broadcasted_iota(jnp.int32, (LANE, LANE), 0)
    c = lax.broadcasted_iota(jnp.int32, (LANE, LANE), 1)
    tri = jnp.where(r <= c, 1.0, 0.0).astype(BF16)
    carry = jnp.zeros((ls.shape[0], 1), F32)
    for blk in range(t // LANE):
        seg = ls[:, blk * LANE:(blk + 1) * LANE]
        hi, mid, lo = _split3(seg)
        pre = _dot(hi, tri) + _dot(mid, tri) + _dot(lo, tri) + carry
        o_ref[:, blk * LANE:(blk + 1) * LANE] = pre
        carry = pre[:, LANE - 1:LANE]


def _forget_cumsum(f_t, b_forget):
    bsz, h, t = f_t.shape
    return pl.pallas_call(
        _cumsum_kernel,
        grid=(bsz,),
        in_specs=[pl.BlockSpec((None, h, t), lambda b: (b, 0, 0)), _const_spec((h, 1))],
        out_specs=pl.BlockSpec((None, h, t), lambda b: (b, 0, 0)),
        out_shape=jax.ShapeDtypeStruct((bsz, h, t), F32),
        compiler_params=_cparams(1), name="forget_cumsum",
    )(f_t, b_forget.reshape(h, 1).astype(F32))


def _fox_kernel(q_ref, k_ref, v_ref, cq_ref, ck_ref, o_ref):
    i = pl.program_id(2)
    tq = q_ref.shape[0]
    tk = tq
    q = q_ref[...]
    lane = lax.broadcasted_iota(jnp.int32, (tq, LANE), 1)
    rel = (lax.broadcasted_iota(jnp.int32, (tq, tk), 0)
           - lax.broadcasted_iota(jnp.int32, (tq, tk), 1))
    outs = []
    for hh in range(2):
        in_half = (lane >= hh * HEAD_DIM) & (lane < (hh + 1) * HEAD_DIM)
        qh = jnp.where(in_half, q, jnp.zeros_like(q)) * jnp.asarray(HEAD_DIM ** -0.5, BF16)
        cq = cq_ref[:, hh:hh + 1]

        def body(j, carry, qh=qh, cq=cq, hh=hh):
            m, l, acc = carry
            start = pl.multiple_of(j * tk, tk)
            ks = k_ref[pl.ds(start, tk), :]
            vs = v_ref[pl.ds(start, tk), :]
            ck = ck_ref[hh, pl.ds(j, 1), :]
            s = _dot_nt(qh, ks) + cq - ck
            s = jnp.where(rel + (i - j) * tq >= 0, s, NEG)
            m_new = jnp.maximum(m, jnp.max(s, axis=-1, keepdims=True))
            alpha = jnp.exp(m - m_new)
            p = jnp.exp(s - m_new)
            l = alpha * l + jnp.sum(p, axis=-1, keepdims=True)
            acc = alpha * acc + _dot(p.astype(BF16), vs)
            return m_new, l, acc

        init = (jnp.full((tq, 1), NEG, F32), jnp.zeros((tq, 1), F32), jnp.zeros((tq, LANE), F32))
        m, l, acc = lax.fori_loop(0, i + 1, body, init)
        outs.append(acc / l)
    o_ref[...] = jnp.where(lane < HEAD_DIM, outs[0], outs[1]).astype(o_ref.dtype)


def _fox_attention(q, k, v, c_col, c_row, bsz, t):
    n, hd = q.shape
    n_pair = hd // LANE
    nq = t // FOX_TQ
    return pl.pallas_call(
        _fox_kernel,
        grid=(bsz, n_pair, nq),
        in_specs=[
            pl.BlockSpec((FOX_TQ, LANE), lambda b, j, i: (b * nq + i, j)),
            pl.BlockSpec((t, LANE), lambda b, j, i: (b, j)),
            pl.BlockSpec((t, LANE), lambda b, j, i: (b, j)),
            pl.BlockSpec((None, None, FOX_TQ, 2), lambda b, j, i: (b, j, i, 0)),
            pl.BlockSpec((None, None, 2, nq, FOX_TQ), lambda b, j, i: (b, j, 0, 0, 0)),
        ],
        out_specs=pl.BlockSpec((FOX_TQ, LANE), lambda b, j, i: (b * nq + i, j)),
        out_shape=jax.ShapeDtypeStruct((n, hd), BF16),
        compiler_params=_cparams(3), name="fox_attention",
    )(q, k, v, c_col, c_row)


def _swa_kernel(sink_ref, q_ref, kp_ref, kc_ref, vp_ref, vc_ref, bias_ref, o_ref):
    nblk = pl.program_id(1)
    w = q_ref.shape[0]
    kk = jnp.concatenate([kp_ref[...], kc_ref[...]], axis=0)
    vv = jnp.concatenate([vp_ref[...], vc_ref[...]], axis=0)
    lane = lax.broadcasted_iota(jnp.int32, (w, LANE), 1)
    col = lax.broadcasted_iota(jnp.int32, (w, 2 * w), 1)
    no_prev = (col < w) & (nblk == 0)
    grp = A_HEADS // A_KV_HEADS
    for g in range(grp):
        qg = q_ref[:, g * LANE:(g + 1) * LANE]
        outs = []
        for hh in range(A_KV_HEADS):
            head = g + grp * hh
            in_half = (lane >= hh * HEAD_DIM) & (lane < (hh + 1) * HEAD_DIM)
            qh = jnp.where(in_half, qg, jnp.zeros_like(qg)) * jnp.asarray(HEAD_DIM ** -0.5, BF16)
            s = _dot_nt(qh, kk) + bias_ref[head]
            s = jnp.where(no_prev, NEG, s)
            sink = sink_ref[head]
            m = jnp.maximum(jnp.max(s, axis=-1, keepdims=True), sink)
            e = jnp.exp(s - m)
            denom = jnp.sum(e, axis=-1, keepdims=True) + jnp.exp(sink - m)
            p = e / denom
            outs.append(_dot(p.astype(BF16), vv))
        o_ref[:, g * LANE:(g + 1) * LANE] = jnp.where(lane < HEAD_DIM, outs[0], outs[1]).astype(o_ref.dtype)


def _swa_attention(q, k, v, sinks, bias, bsz, t):
    n = q.shape[0]
    nb = t // WINDOW
    cur = lambda b, i: (b * nb + i, 0)
    prev = lambda b, i: (b * nb + jnp.maximum(i - 1, 0), 0)
    return pl.pallas_call(
        _swa_kernel,
        grid=(bsz, nb),
        in_specs=[
            pl.BlockSpec(memory_space=pltpu.SMEM),
            pl.BlockSpec((WINDOW, A_HEADS * HEAD_DIM), cur),
            pl.BlockSpec((WINDOW, LANE), prev), pl.BlockSpec((WINDOW, LANE), cur),
            pl.BlockSpec((WINDOW, LANE), prev), pl.BlockSpec((WINDOW, LANE), cur),
            _const_spec(bias.shape),
        ],
        out_specs=pl.BlockSpec((WINDOW, A_HEADS * HEAD_DIM), cur),
        out_shape=jax.ShapeDtypeStruct((n, A_HEADS * HEAD_DIM), BF16),
        compiler_params=_cparams(2), name="swa_attention",
    )(sinks.astype(F32), q, k, k, v, v, bias)


def _float_order_key(s):
    bits = pltpu.bitcast(s, jnp.int32)
    return bits ^ ((bits >> 31) & jnp.int32(0x7FFFFFFF))


def _dsa_kernel(q_ref, ckv_ref, qi_ref, ki_ref, wi_ref, wuk_ref, wuv_ref, tb_ref, o_ref,
                key_s, sel_s, qlat_s, qis_s, s_buf, p_buf, acc_s, m_s, l_s, a_s):
    n = pl.program_id(1)
    qb = Q_BLOCK
    ck = DSA_CK
    n_sel = TOPK_MAX
    n_chunks = n // (ck // qb) + 1
    lane = lax.broadcasted_iota(jnp.int32, (qb, LANE), 1)
    qpos = n * qb + lax.broadcasted_iota(jnp.int32, (qb, ck), 0)
    kcol = lax.broadcasted_iota(jnp.int32, (qb, ck), 1)

    for h in range(IDX_HEADS):
        pair = qi_ref[:, (h // 2) * LANE:(h // 2 + 1) * LANE]
        in_half = (lane >= (h % 2) * IDX_DIM) & (lane < (h % 2 + 1) * IDX_DIM)
        qis_s[h * qb:(h + 1) * qb, :] = jnp.where(in_half, pair, jnp.zeros_like(pair))

    for h in range(C_HEADS):
        pair = q_ref[:, (h // 2) * LANE:(h // 2 + 1) * LANE]
        ql = _dot(pair, wuk_ref[h]) * (HEAD_DIM ** -0.5)
        qlat_s[h * qb:(h + 1) * qb, :] = ql.astype(BF16)

    wi = wi_ref[...] * (IDX_HEADS ** -0.5)

    def score_body(c, _):
        start = pl.multiple_of(c * ck, ck)
        kc = ki_ref[pl.ds(start, ck), :]
        raw = _dot_nt(qis_s[...], kc)
        score = jnp.zeros((qb, ck), F32)
        for h in range(IDX_HEADS):
            idx_s = jnp.maximum(raw[h * qb:(h + 1) * qb, :] * (IDX_DIM ** -0.5), 0.0)
            score = score + idx_s * wi[:, h:h + 1]
        causal = (start + kcol) <= qpos
        key_s[c] = _float_order_key(jnp.where(causal, score, NEG))
        sel_s[c] = jnp.where(causal, 0.0, NEG)
        return 0

    lax.fori_loop(0, n_chunks, score_body, 0)

    @pl.when((n + 1) * qb > n_sel)
    def _():
        def count_where(pred_fn):
            def body(c, acc):
                hit = jnp.where(pred_fn(key_s[c]), 1.0, 0.0)
                return acc + hit[:, :LANE] + hit[:, LANE:]
            acc = lax.fori_loop(0, n_chunks, body, jnp.zeros((qb, LANE), F32))
            return jnp.sum(acc, axis=-1, keepdims=True)

        cnt0 = count_where(lambda kv: kv >= 0)
        prefix0 = jnp.where(cnt0 >= n_sel, jnp.int32(0), jnp.int32(INT_MIN))

        def bit_body(t, prefix):
            cand = prefix | jnp.left_shift(jnp.int32(1), 30 - t)
            cnt = count_where(lambda kv: kv >= cand)
            return jnp.where(cnt >= n_sel, cand, prefix)

        thr = lax.fori_loop(0, 31, bit_body, prefix0)
        need = n_sel - count_where(lambda kv: kv > thr)
        r = lax.broadcasted_iota(jnp.int32, (ck, ck), 0)
        cc = lax.broadcasted_iota(jnp.int32, (ck, ck), 1)
        before = jnp.where(r < cc, 1.0, 0.0).astype(BF16)

        def tie_body(c, seen):
            kv = key_s[c]
            tie = kv == thr
            tie_f = jnp.where(tie, 1.0, 0.0)
            earlier = _dot(tie_f.astype(BF16), before) + seen
            sel = (kv > thr) | (tie & (earlier < need))
            sel_s[c] = jnp.where(sel, sel_s[c], NEG)
            return seen + jnp.sum(tie_f, axis=-1, keepdims=True)

        lax.fori_loop(0, n_chunks, tie_body, jnp.zeros((qb, 1), F32))

    m_s[...] = jnp.full(m_s.shape, NEG, F32)
    l_s[...] = jnp.zeros(l_s.shape, F32)
    acc_s[...] = jnp.zeros(acc_s.shape, F32)

    def attn_body(c, _):
        start = pl.multiple_of(c * ck, ck)
        kc = ckv_ref[pl.ds(start, ck), :]
        s_buf[...] = _dot_nt(qlat_s[...], kc)
        rel_blk = jnp.minimum(n - c * (ck // qb), 3)
        mask_add = sel_s[c]
        for h in range(C_HEADS):
            rows = slice(h * qb, (h + 1) * qb)
            s = s_buf[rows, :] + tb_ref[rel_blk, h] + mask_add
            m_old = m_s[rows, :]
            m_new = jnp.maximum(m_old, jnp.max(s, axis=-1, keepdims=True))
            p = jnp.where(s > 0.5 * NEG, jnp.exp(s - m_new), 0.0)
            alpha = jnp.exp(m_old - m_new)
            l_s[rows, :] = alpha * l_s[rows, :] + jnp.sum(p, axis=-1, keepdims=True)
            m_s[rows, :] = m_new
            a_s[rows, :] = alpha
            p_buf[rows, :] = p.astype(BF16)
        pv = _dot(p_buf[...], kc)
        acc_s[...] = acc_s[...] * a_s[...] + pv
        return 0

    lax.fori_loop(0, n_chunks, attn_body, 0)

    for g in range(C_HEADS // 2):
        out = jnp.zeros((qb, LANE), F32)
        for hh in range(2):
            h = 2 * g + hh
            rows = slice(h * qb, (h + 1) * qb)
            o_lat = (acc_s[rows, :] / l_s[rows, :]).astype(BF16)
            out = out + _dot(o_lat, wuv_ref[h])
        o_ref[:, g * LANE:(g + 1) * LANE] = out.astype(o_ref.dtype)


def _dsa_attention(q, ckv, qi, ki2, wi, wuk_wide, wuv_wide, tb, bsz, t):
    n = q.shape[0]
    nq = t // Q_BLOCK
    nck = t // DSA_CK
    rows = C_HEADS * Q_BLOCK
    blk = lambda b, i: (b * nq + i, 0)
    per_b = lambda b, i: (b, 0)
    return pl.pallas_call(
        _dsa_kernel,
        grid=(bsz, nq),
        in_specs=[
            pl.BlockSpec((Q_BLOCK, C_HEADS * HEAD_DIM), blk),
            pl.BlockSpec((t, C_LATENT), per_b),
            pl.BlockSpec((Q_BLOCK, IDX_HEADS * IDX_DIM), blk),
            pl.BlockSpec((t, LANE), per_b),
            pl.BlockSpec((Q_BLOCK, LANE), blk),
            _const_spec(wuk_wide.shape), _const_spec(wuv_wide.shape), _const_spec(tb.shape),
        ],
        out_specs=pl.BlockSpec((Q_BLOCK, C_HEADS * HEAD_DIM), blk),
        out_shape=jax.ShapeDtypeStruct((n, C_HEADS * HEAD_DIM), BF16),
        scratch_shapes=[
            pltpu.VMEM((nck, Q_BLOCK, DSA_CK), jnp.int32),
            pltpu.VMEM((nck, Q_BLOCK, DSA_CK), F32),
            pltpu.VMEM((rows, C_LATENT), BF16),
            pltpu.VMEM((IDX_HEADS * Q_BLOCK, LANE), BF16),
            pltpu.VMEM((rows, DSA_CK), F32),
            pltpu.VMEM((rows, DSA_CK), BF16),
            pltpu.VMEM((rows, C_LATENT), F32),
            pltpu.VMEM((rows, 1), F32), pltpu.VMEM((rows, 1), F32), pltpu.VMEM((rows, 1), F32),
        ],
        compiler_params=_cparams(2), name="dsa_attention",
    )(q, ckv, qi, ki2, wi, wuk_wide, wuv_wide, tb)


def _post_attn_kernel(*refs, n_mix, alpha):
    x_ref = refs[0]
    mix_refs = refs[1:1 + n_mix]
    w_refs = refs[1 + n_mix:1 + 2 * n_mix]
    g_ref, b_ref, wr_ref, br_ref = refs[1 + 2 * n_mix:5 + 2 * n_mix]
    xo_ref, idx_ref, gate_ref, rank_ref, cnt_ref = refs[5 + 2 * n_mix:10 + 2 * n_mix]
    scr = refs[10 + 2 * n_mix:]
    wb_refs = scr[:n_mix]
    wr3_ref, before_ref, carry_ref = scr[n_mix:]
    tm = x_ref.shape[0]

    @pl.when(pl.program_id(0) == 0)
    def _():
        for w_ref, wb_ref in zip(w_refs, wb_refs):
            wb_ref[...] = w_ref[...].astype(BF16)
        hi, mid, lo = _split3(wr_ref[...])
        wr3_ref[0] = hi
        wr3_ref[1] = mid
        wr3_ref[2] = lo
        r = lax.broadcasted_iota(jnp.int32, (tm, tm), 0)
        c = lax.broadcasted_iota(jnp.int32, (tm, tm), 1)
        before_ref[...] = jnp.where(r < c, 1.0, 0.0).astype(BF16)
        carry_ref[...] = jnp.zeros(carry_ref.shape, F32)

    mix = _dot(mix_refs[0][...], wb_refs[0][...])
    for i in range(1, n_mix):
        mix = mix + _dot(mix_refs[i][...], wb_refs[i][...])
    xn = _layer_norm(alpha * x_ref[...] + mix, g_ref[...], b_ref[...])
    xo_ref[...] = xn

    xh, xm, xl = _split3(xn)
    logits = (_dot_nt(wr3_ref[0], xh) + _dot_nt(wr3_ref[0], xm) + _dot_nt(wr3_ref[1], xh)
              + _dot_nt(wr3_ref[0], xl) + _dot_nt(wr3_ref[1], xm) + _dot_nt(wr3_ref[2], xh)
              + br_ref[...])
    eidx = lax.broadcasted_iota(jnp.int32, logits.shape, 0)
    cur = logits
    vals, idxs = [], []
    for _ in range(TOP_K):
        mx = jnp.max(cur, axis=0, keepdims=True)
        first = jnp.min(jnp.where(cur == mx, eidx, N_EXPERTS), axis=0, keepdims=True)
        vals.append(mx)
        idxs.append(first)
        cur = jnp.where(eidx == first, -jnp.inf, cur)
    es = [jnp.exp(v - vals[0]) for v in vals]
    tot = es[0] + es[1] + es[2] + es[3]
    member = jnp.zeros(logits.shape, F32)
    for k in range(TOP_K):
        idx_ref[k:k + 1, :] = idxs[k]
        gate_ref[k:k + 1, :] = es[k] / tot
        member = member + jnp.where(eidx == idxs[k], 1.0, 0.0)
    earlier = _dot(member.astype(BF16), before_ref[...]) + carry_ref[...]
    for k in range(TOP_K):
        rk = jnp.sum(jnp.where(eidx == idxs[k], earlier, 0.0), axis=0, keepdims=True)
        rank_ref[k:k + 1, :] = rk.astype(jnp.int32)
    carry_ref[...] = carry_ref[...] + jnp.sum(member, axis=1, keepdims=True)
    cnt_ref[...] = carry_ref[...].astype(jnp.int32)


def _post_attn(x2, mixes, w_os, ln_g, ln_b, w_router, b_router, alpha):
    n, d = x2.shape
    n_mix = len(mixes)
    tm = TM_PROJ
    row = lambda i: (i, 0)
    colblk = lambda i: (0, i)
    in_specs = [pl.BlockSpec((tm, d), row)]
    in_specs += [pl.BlockSpec((tm, m.shape[1]), row) for m in mixes]
    in_specs += [_const_spec(w.shape) for w in w_os]
    in_specs += [_const_spec((1, d)), _const_spec((1, d)), _const_spec((N_EXPERTS, d)),
                 _const_spec((N_EXPERTS, 1))]
    out_specs = [pl.BlockSpec((tm, d), row), pl.BlockSpec((TOP_K, tm), colblk),
                 pl.BlockSpec((TOP_K, tm), colblk), pl.BlockSpec((TOP_K, tm), colblk),
                 _const_spec((N_EXPERTS, 1))]
    out_shape = [jax.ShapeDtypeStruct((n, d), F32), jax.ShapeDtypeStruct((TOP_K, n), jnp.int32),
                 jax.ShapeDtypeStruct((TOP_K, n), F32), jax.ShapeDtypeStruct((TOP_K, n), jnp.int32),
                 jax.ShapeDtypeStruct((N_EXPERTS, 1), jnp.int32)]
    scratch = [pltpu.VMEM(w.shape, BF16) for w in w_os]
    scratch += [pltpu.VMEM((3, N_EXPERTS, d), BF16), pltpu.VMEM((tm, tm), BF16),
                pltpu.VMEM((N_EXPERTS, 1), F32)]
    return pl.pallas_call(
        functools.partial(_post_attn_kernel, n_mix=n_mix, alpha=alpha),
        grid=(n // tm,),
        in_specs=in_specs, out_specs=out_specs, out_shape=out_shape, scratch_shapes=scratch,
        compiler_params=_cparams(1), name="post_attn",
    )(x2, *mixes, *w_os, ln_g.reshape(1, d), ln_b.reshape(1, d), w_router.T,
      b_router.reshape(N_EXPERTS, 1))


def _moe_kernel(be_ref, nu_ref, x_ref, wgu_ref, bgu_ref, wd_ref, bd_ref, y_ref, wgu_s, wd_s):
    i = pl.program_id(0)
    prev = be_ref[jnp.maximum(i - 1, 0)]
    live = i < nu_ref[0]

    @pl.when(live & ((i == 0) | (be_ref[i] != prev)))
    def _():
        wgu_s[...] = wgu_ref[...].astype(BF16)
        wd_s[...] = wd_ref[...].astype(BF16)

    @pl.when(live)
    def _():
        xb = x_ref[...]
        ch = 512
        acc = jnp.zeros(y_ref.shape, F32)
        for j in range(D_FF // ch):
            hg = _dot(xb, wgu_s[:, j * ch:(j + 1) * ch]) + bgu_ref[:, j * ch:(j + 1) * ch]
            hl = (_dot(xb, wgu_s[:, D_FF + j * ch:D_FF + (j + 1) * ch])
                  + bgu_ref[:, D_FF + j * ch:D_FF + (j + 1) * ch])
            glu = jnp.minimum(hg, SWIGLU_LIMIT)
            lin = jnp.clip(hl, -SWIGLU_LIMIT, SWIGLU_LIMIT)
            act = glu * jax.nn.sigmoid(SWIGLU_ALPHA * glu) * (lin + 1.0)
            acc = acc + _dot(act.astype(BF16), wd_s[j * ch:(j + 1) * ch, :])
        y_ref[...] = (acc + bd_ref[...]).astype(y_ref.dtype)

    @pl.when(jnp.logical_not(live))
    def _():
        y_ref[...] = jnp.zeros(y_ref.shape, y_ref.dtype)


def _moe_ffn(x_pad, block_e, n_used, layer, w_gate_up, b_gate_up, w_down, b_down):
    rows, d = x_pad.shape
    depth = w_gate_up.shape[0]
    n_blocks = rows // TM_MOE
    expert = lambda i, be, nu: (layer, be[i], 0, 0)
    grid_spec = pltpu.PrefetchScalarGridSpec(
        num_scalar_prefetch=2,
        grid=(n_blocks,),
        in_specs=[
            pl.BlockSpec((TM_MOE, d), lambda i, be, nu: (i, 0)),
            pl.BlockSpec((None, None, d, 2 * D_FF), expert),
            pl.BlockSpec((None, None, 1, 2 * D_FF), expert),
            pl.BlockSpec((None, None, D_FF, d), expert),
            pl.BlockSpec((None, None, 1, d), expert),
        ],
        out_specs=pl.BlockSpec((TM_MOE, d), lambda i, be, nu: (i, 0)),
        scratch_shapes=[pltpu.VMEM((d, 2 * D_FF), BF16), pltpu.VMEM((D_FF, d), BF16)],
    )
    return pl.pallas_call(
        _moe_kernel, grid_spec=grid_spec,
        out_shape=jax.ShapeDtypeStruct((rows, d), BF16),
        compiler_params=_cparams(1), name="moe_ffn",
    )(block_e, n_used, x_pad, w_gate_up, b_gate_up.reshape(depth, N_EXPERTS, 1, 2 * D_FF), w_down,
      b_down.reshape(depth, N_EXPERTS, 1, d))


def _post_moe_kernel(x_ref, y_ref, gt_ref, g_ref, b_ref, p_ref, wg_ref, wp_ref, o_ref, wg_s, wp_s,
                     *, alpha):
    @pl.when(pl.program_id(0) == 0)
    def _():
        wg_s[...] = wg_ref[...].astype(BF16)
        wp_s[...] = wp_ref[...].astype(BF16)

    d = x_ref.shape[1]
    gates = gt_ref[...]
    ffn = gates[:, 0:1] * y_ref[:, 0:d]
    for k in range(1, TOP_K):
        ffn = ffn + gates[:, k:k + 1] * y_ref[:, k * d:(k + 1) * d]
    xn = _layer_norm(alpha * x_ref[...] + ffn, g_ref[...], b_ref[...])
    gate = jax.nn.sigmoid(_dot(xn.astype(BF16), wg_s[...]))
    emb = _dot(p_ref[...].astype(BF16), wp_s[...])
    o_ref[...] = xn + gate * emb


def _post_moe(x2, y_sel, gates_t, ln_g, ln_b, layer, p3, w_ple_gate, w_ple_proj, alpha):
    n, d = x2.shape
    tm = TM_POST
    row = lambda i: (i, 0)
    return pl.pallas_call(
        functools.partial(_post_moe_kernel, alpha=alpha),
        grid=(n // tm,),
        in_specs=[pl.BlockSpec((tm, d), row), pl.BlockSpec((tm, TOP_K * d), row),
                  pl.BlockSpec((tm, TOP_K), row), _const_spec((1, d)), _const_spec((1, d)),
                  pl.BlockSpec((None, tm, PLE_DIM), lambda i: (layer, i, 0)),
                  pl.BlockSpec((None, d, d), lambda i: (layer, 0, 0)),
                  pl.BlockSpec((None, PLE_DIM, d), lambda i: (layer, 0, 0))],
        out_specs=pl.BlockSpec((tm, d), row),
        out_shape=jax.ShapeDtypeStruct((n, d), F32),
        scratch_shapes=[pltpu.VMEM((d, d), BF16), pltpu.VMEM((PLE_DIM, d), BF16)],
        compiler_params=_cparams(1), name="post_moe",
    )(x2, y_sel, gates_t, ln_g.reshape(1, d), ln_b.reshape(1, d), p3, w_ple_gate, w_ple_proj)


def _t5_bucket(dist):
    d = jnp.maximum(dist, 0)
    ratio = jnp.log(jnp.maximum(d, 1).astype(F32) / MAX_EXACT) / math.log(MAX_DISTANCE / MAX_EXACT)
    large = MAX_EXACT + (ratio * (NUM_BUCKETS - MAX_EXACT)).astype(jnp.int32)
    large = jnp.minimum(large, NUM_BUCKETS - 1)
    return jnp.where(d < MAX_EXACT, d, large)


def _swa_bias_table(pos_bias):
    qi = jnp.arange(WINDOW)[:, None]
    kj = jnp.arange(2 * WINDOW)[None, :]
    dist = qi + WINDOW - kj
    valid = (dist >= 0) & (dist < WINDOW)
    bias = pos_bias[_t5_bucket(dist)][..., :A_HEADS].astype(F32).transpose(2, 0, 1)
    return jnp.where(valid[None], bias, NEG)


def _dsa_bias_tables(pos_bias):
    qi = jnp.arange(Q_BLOCK)[:, None]
    kj = jnp.arange(DSA_CK)[None, :]
    tabs = []
    for r in range(4):
        dist = r * Q_BLOCK + qi - kj
        tabs.append(pos_bias[_t5_bucket(dist)].astype(F32).transpose(2, 0, 1))
    return jnp.stack(tabs)


def _pad_cols(w, width=LANE):
    return jnp.pad(w, ((0, 0), (0, width - w.shape[1])))


def _even_mixer(x2, bsz, t, w_in, b_forget, sinks, w_o, swa_bias):
    hd = HEAD_DIM
    grp = A_HEADS // A_KV_HEADS
    pair_order = [g + grp * hh for g in range(grp) for hh in range(A_KV_HEADS)]
    na, nkv, nb = A_HEADS * hd, A_KV_HEADS * hd, B_HEADS * hd
    offs = [0, na, na + nkv, na + 2 * nkv, na + 2 * nkv + nb, na + 2 * nkv + 2 * nb,
            na + 2 * nkv + 3 * nb]
    w_aq = w_in[:, offs[0]:offs[1]].reshape(D_MODEL, A_HEADS, hd)[:, pair_order].reshape(D_MODEL, na)
    weights = [w_aq, w_in[:, offs[1]:offs[2]], w_in[:, offs[2]:offs[3]], w_in[:, offs[3]:offs[4]],
               w_in[:, offs[4]:offs[5]], w_in[:, offs[5]:offs[6]], _pad_cols(w_in[:, offs[6]:])]
    aq, ak, av, bq, bk, bv, bf = _project(x2, weights, [BF16] * 6 + [F32])
    o_a = _swa_attention(aq, ak, av, sinks, swa_bias, bsz, t)

    f_t = bf[:, :B_HEADS].reshape(bsz, t, B_HEADS).transpose(0, 2, 1)
    c = _forget_cumsum(f_t, b_forget)
    c_row = c.reshape(bsz, B_HEADS // 2, 2, t // FOX_TQ, FOX_TQ)
    c_col = c.reshape(bsz, B_HEADS // 2, 2, t).transpose(0, 1, 3, 2)
    o_b = _fox_attention(bq, bk, bv, c_col, c_row, bsz, t)

    w_oa = w_o[:na].reshape(A_HEADS, hd, D_MODEL)[jnp.asarray(pair_order)].reshape(na, D_MODEL)
    return [o_a, o_b], [w_oa, w_o[na:]]


def _odd_mixer(x2, bsz, t, w_in, kv_norm, w_uk, w_uv, w_o, dsa_tb):
    hd = HEAD_DIM
    nq, nl, ni = C_HEADS * hd, C_LATENT, IDX_HEADS * IDX_DIM
    w_ki = w_in[:, nq + nl + ni:nq + nl + ni + IDX_DIM]
    weights = [w_in[:, :nq], w_in[:, nq:nq + nl], w_in[:, nq + nl:nq + nl + ni],
               jnp.concatenate([w_ki, w_ki], axis=1), _pad_cols(w_in[:, nq + nl + ni + IDX_DIM:])]
    cq, ckv, qi, ki2, wi = _project(x2, weights, [BF16, BF16, BF16, BF16, F32],
                                    rms_gain=kv_norm.reshape(1, nl).astype(F32), rms_index=1)
    uk = w_uk.transpose(1, 2, 0)
    uv = w_uv.transpose(1, 0, 2)
    zk = jnp.zeros_like(uk)
    zv = jnp.zeros_like(uv)
    odd = (jnp.arange(C_HEADS) % 2 == 1)[:, None, None]
    wuk_wide = jnp.where(odd, jnp.concatenate([zk, uk], axis=1), jnp.concatenate([uk, zk], axis=1))
    wuv_wide = jnp.where(odd, jnp.concatenate([zv, uv], axis=2), jnp.concatenate([uv, zv], axis=2))
    o_c = _dsa_attention(cq, ckv, qi, ki2, wi, wuk_wide.astype(BF16), wuv_wide.astype(BF16),
                         dsa_tb, bsz, t)
    return [o_c], [w_o]


def _dispatch_plan(top_idx, rank, counts, n_tok):
    counts = counts.reshape(N_EXPERTS)
    padded = (counts + TM_MOE - 1) // TM_MOE * TM_MOE
    pad_ends = jnp.cumsum(padded)
    pad_starts = pad_ends - padded
    dest = pad_starts[top_idx] + rank
    n_blocks = (n_tok * TOP_K) // TM_MOE + N_EXPERTS
    block_start = jnp.arange(n_blocks, dtype=jnp.int32) * TM_MOE
    block_e = jnp.minimum(jnp.sum(pad_ends[None, :] <= block_start[:, None], axis=1),
                          N_EXPERTS - 1).astype(jnp.int32)
    n_used = (pad_ends[-1] // TM_MOE).astype(jnp.int32).reshape(1)
    return dest, block_e, n_used, n_blocks


def kernel(x, p, pos_bias, w_in_even, b_forget, sinks, w_o_even, w_in_odd, kv_norm, w_uk, w_uv,
           w_o_odd, ln_g, ln_b, w_router, b_router, w_gate_up, b_gate_up, w_down, b_down,
           w_ple_proj, w_ple_gate):
    bsz, t, d = x.shape
    depth = ln_g.shape[0]
    alpha = (2 * depth) ** 0.25
    n_tok = bsz * t
    x2 = x.reshape(n_tok, d)
    swa_bias = _swa_bias_table(pos_bias)
    dsa_tb = _dsa_bias_tables(pos_bias)
    p3 = p.reshape(depth, n_tok, PLE_DIM)
    tok_ids = jnp.tile(jnp.arange(n_tok, dtype=jnp.int32), (TOP_K,))
    for i in range(depth):
        j = i // 2
        if i % 2 == 0:
            mixes, w_os = _even_mixer(x2, bsz, t, w_in_even[j], b_forget[j], sinks[j], w_o_even[j],
                                      swa_bias)
        else:
            mixes, w_os = _odd_mixer(x2, bsz, t, w_in_odd[j], kv_norm[j], w_uk[j], w_uv[j],
                                     w_o_odd[j], dsa_tb)
        x1, top_idx, gates, rank, counts = _post_attn(x2, mixes, w_os, ln_g[i, 0], ln_b[i, 0],
                                                      w_router[i], b_router[i], alpha)
        dest, block_e, n_used, n_blocks = _dispatch_plan(top_idx, rank, counts, n_tok)
        src_tok = jnp.zeros((n_blocks * TM_MOE,), jnp.int32).at[dest.reshape(-1)].set(tok_ids)
        x_pad = jnp.take(x1.astype(BF16), src_tok, axis=0)
        y_pad = _moe_ffn(x_pad, block_e, n_used, i, w_gate_up, b_gate_up, w_down, b_down)
        y_sel = jnp.take(y_pad, dest.T.reshape(-1), axis=0).reshape(n_tok, TOP_K * d)
        x2 = _post_moe(x1, y_sel, gates.T, ln_g[i, 1], ln_b[i, 1], i, p3, w_ple_gate, w_ple_proj,
                       alpha)
    return x2.reshape(bsz, t, d)
```

```python
import functools
import math

import jax
import jax.numpy as jnp
from jax import lax
from jax.experimental import pallas as pl
from jax.experimental.pallas import tpu as pltpu

D_MODEL = 1024
HEAD_DIM = 64
A_HEADS, A_KV_HEADS, WINDOW = 8, 2, 128
B_HEADS = 8
C_HEADS, C_LATENT = 16, 128
IDX_HEADS, IDX_DIM = 8, 64
TOPK_MAX = 256
NUM_BUCKETS, MAX_EXACT, MAX_DISTANCE = 32, 16, 128
N_EXPERTS, TOP_K, D_FF = 32, 4, 1024
SWIGLU_LIMIT, SWIGLU_ALPHA = 7.0, 1.702
PLE_DIM = 256
Q_BLOCK = 128
LN_EPS = 1e-5
NEG = -1e30

LANE = 128
VMEM_LIMIT = 56 * 1024 * 1024
TM_PROJ = 512
TM_MOE = 256
TM_POST = 256
FOX_TQ = 256
DSA_CK = 256
INT_MIN = -2 ** 31

BF16 = jnp.bfloat16
F32 = jnp.float32


def _cparams(n_axes):
    return pltpu.CompilerParams(dimension_semantics=("arbitrary",) * n_axes,
                                vmem_limit_bytes=VMEM_LIMIT)


def _const_spec(shape):
    nd = len(shape)
    return pl.BlockSpec(shape, lambda *_: (0,) * nd)


def _dot(a, b):
    return jnp.dot(a, b, preferred_element_type=F32)


def _dot_nt(a, b):
    return lax.dot_general(a, b, (((1,), (1,)), ((), ())), preferred_element_type=F32)


def _layer_norm(v, g, b):
    mu = jnp.mean(v, axis=-1, keepdims=True)
    d = v - mu
    var = jnp.mean(d * d, axis=-1, keepdims=True)
    return d * lax.rsqrt(var + LN_EPS) * g + b


def _split3(v):
    hi = v.astype(BF16)
    r1 = v - hi.astype(F32)
    mid = r1.astype(BF16)
    lo = (r1 - mid.astype(F32)).astype(BF16)
    return hi, mid, lo


def _proj_kernel(*refs, n_w, rms_index):
    x_ref = refs[0]
    w_refs = refs[1:1 + n_w]
    g_ref = refs[1 + n_w]
    o_refs = refs[2 + n_w:2 + 2 * n_w]
    wb_refs = refs[2 + 2 * n_w:]

    @pl.when(pl.program_id(0) == 0)
    def _():
        for w_ref, wb_ref in zip(w_refs, wb_refs):
            wb_ref[...] = w_ref[...].astype(BF16)

    xb = x_ref[...].astype(BF16)
    for i in range(n_w):
        h = _dot(xb, wb_refs[i][...])
        if i == rms_index:
            h = h * lax.rsqrt(jnp.mean(h * h, axis=-1, keepdims=True) + LN_EPS) * g_ref[...]
        o_refs[i][...] = h.astype(o_refs[i].dtype)


def _project(x2, weights, out_dtypes, rms_gain=None, rms_index=-1):
    n, d = x2.shape
    n_w = len(weights)
    if rms_gain is None:
        rms_gain = jnp.ones((1, LANE), F32)
    in_specs = [pl.BlockSpec((TM_PROJ, d), lambda i: (i, 0))]
    in_specs += [_const_spec(w.shape) for w in weights]
    in_specs += [_const_spec(rms_gain.shape)]
    out_specs = [pl.BlockSpec((TM_PROJ, w.shape[1]), lambda i: (i, 0)) for w in weights]
    out_shape = [jax.ShapeDtypeStruct((n, w.shape[1]), dt) for w, dt in zip(weights, out_dtypes)]
    scratch = [pltpu.VMEM(w.shape, BF16) for w in weights]
    return pl.pallas_call(
        functools.partial(_proj_kernel, n_w=n_w, rms_index=rms_index),
        grid=(n // TM_PROJ,),
        in_specs=in_specs, out_specs=out_specs, out_shape=out_shape,
        scratch_shapes=scratch, compiler_params=_cparams(1), name="in_proj",
    )(x2, *weights, rms_gain)


def _cumsum_kernel(f_ref, b_ref, o_ref):
    z = f_ref[...] + b_ref[...]
    ls = jnp.minimum(z, 0.0) - jnp.log1p(jnp.exp(-jnp.abs(z)))
    t = ls.shape[1]
    r = lax.broadcasted_iota(jnp.int32, (LANE, LANE), 0)
    c = lax.broadcasted_iota(jnp.int32, (LANE, LANE), 1)
    tri = jnp.where(r <= c, 1.0, 0.0).astype(BF16)
    carry = jnp.zeros((ls.shape[0], 1), F32)
    for blk in range(t // LANE):
        seg = ls[:, blk * LANE:(blk + 1) * LANE]
        hi, mid, lo = _split3(seg)
        pre = _dot(hi, tri) + _dot(mid, tri) + _dot(lo, tri) + carry
        o_ref[:, blk * LANE:(blk + 1) * LANE] = pre
        carry = pre[:, LANE - 1:LANE]


def _forget_cumsum(f_t, b_forget):
    bsz, h, t = f_t.shape
    return pl.pallas_call(
        _cumsum_kernel,
        grid=(bsz,),
        in_specs=[pl.BlockSpec((None, h, t), lambda b: (b, 0, 0)), _const_spec((h, 1))],
        out_specs=pl.BlockSpec((None, h, t), lambda b: (b, 0, 0)),
        out_shape=jax.ShapeDtypeStruct((bsz, h, t), F32),
        compiler_params=_cparams(1), name="forget_cumsum",
    )(f_t, b_forget.reshape(h, 1).astype(F32))


def _fox_kernel(q_ref, k_ref, v_ref, cq_ref, ck_ref, o_ref):
    i = pl.program_id(2)
    tq = q_ref.shape[0]
    tk = tq
    q = q_ref[...]
    lane = lax.broadcasted_iota(jnp.int32, (tq, LANE), 1)
    rel = (lax.broadcasted_iota(jnp.int32, (tq, tk), 0)
           - lax.broadcasted_iota(jnp.int32, (tq, tk), 1))
    outs = []
    for hh in range(2):
        in_half = (lane >= hh * HEAD_DIM) & (lane < (hh + 1) * HEAD_DIM)
        qh = jnp.where(in_half, q, jnp.zeros_like(q)) * jnp.asarray(HEAD_DIM ** -0.5, BF16)
        cq = cq_ref[:, hh:hh + 1]

        def body(j, carry, qh=qh, cq=cq, hh=hh):
            m, l, acc = carry
            start = pl.multiple_of(j * tk, tk)
            ks = k_ref[pl.ds(start, tk), :]
            vs = v_ref[pl.ds(start, tk), :]
            ck = ck_ref[hh, pl.ds(j, 1), :]
            s = _dot_nt(qh, ks) + cq - ck
            s = jnp.where(rel + (i - j) * tq >= 0, s, NEG)
            m_new = jnp.maximum(m, jnp.max(s, axis=-1, keepdims=True))
            alpha = jnp.exp(m - m_new)
            p = jnp.exp(s - m_new)
            l = alpha * l + jnp.sum(p, axis=-1, keepdims=True)
            acc = alpha * acc + _dot(p.astype(BF16), vs)
            return m_new, l, acc

        init = (jnp.full((tq, 1), NEG, F32), jnp.zeros((tq, 1), F32), jnp.zeros((tq, LANE), F32))
        m, l, acc = lax.fori_loop(0, i + 1, body, init)
        outs.append(acc / l)
    o_ref[...] = jnp.where(lane < HEAD_DIM, outs[0], outs[1]).astype(o_ref.dtype)


def _fox_attention(q, k, v, c_col, c_row, bsz, t):
    n, hd = q.shape
    n_pair = hd // LANE
    nq = t // FOX_TQ
    return pl.pallas_call(
        _fox_kernel,
        grid=(bsz, n_pair, nq),
        in_specs=[
            pl.BlockSpec((FOX_TQ, LANE), lambda b, j, i: (b * nq + i, j)),
            pl.BlockSpec((t, LANE), lambda b, j, i: (b, j)),
            pl.BlockSpec((t, LANE), lambda b, j, i: (b, j)),
            pl.BlockSpec((None, None, FOX_TQ, 2), lambda b, j, i: (b, j, i, 0)),
            pl.BlockSpec((None, None, 2, nq, FOX_TQ), lambda b, j, i: (b, j, 0, 0, 0)),
        ],
        out_specs=pl.BlockSpec((FOX_TQ, LANE), lambda b, j, i: (b * nq + i, j)),
        out_shape=jax.ShapeDtypeStruct((n, hd), BF16),
        compiler_params=_cparams(3), name="fox_attention",
    )(q, k, v, c_col, c_row)


def _swa_kernel(sink_ref, q_ref, kp_ref, kc_ref, vp_ref, vc_ref, bias_ref, o_ref):
    nblk = pl.program_id(1)
    w = q_ref.shape[0]
    kk = jnp.concatenate([kp_ref[...], kc_ref[...]], axis=0)
    vv = jnp.concatenate([vp_ref[...], vc_ref[...]], axis=0)
    lane = lax.broadcasted_iota(jnp.int32, (w, LANE), 1)
    col = lax.broadcasted_iota(jnp.int32, (w, 2 * w), 1)
    no_prev = (col < w) & (nblk == 0)
    grp = A_HEADS // A_KV_HEADS
    for g in range(grp):
        qg = q_ref[:, g * LANE:(g + 1) * LANE]
        outs = []
        for hh in range(A_KV_HEADS):
            head = g + grp * hh
            in_half = (lane >= hh * HEAD_DIM) & (lane < (hh + 1) * HEAD_DIM)
            qh = jnp.where(in_half, qg, jnp.zeros_like(qg)) * jnp.asarray(HEAD_DIM ** -0.5, BF16)
            s = _dot_nt(qh, kk) + bias_ref[head]
            s = jnp.where(no_prev, NEG, s)
            sink = sink_ref[head]
            m = jnp.maximum(jnp.max(s, axis=-1, keepdims=True), sink)
            e = jnp.exp(s - m)
            denom = jnp.sum(e, axis=-1, keepdims=True) + jnp.exp(sink - m)
            p = e / denom
            outs.append(_dot(p.astype(BF16), vv))
        o_ref[:, g * LANE:(g + 1) * LANE] = jnp.where(lane < HEAD_DIM, outs[0], outs[1]).astype(o_ref.dtype)


def _swa_attention(q, k, v, sinks, bias, bsz, t):
    n = q.shape[0]
    nb = t // WINDOW
    cur = lambda b, i: (b * nb + i, 0)
    prev = lambda b, i: (b * nb + jnp.maximum(i - 1, 0), 0)
    return pl.pallas_call(
        _swa_kernel,
        grid=(bsz, nb),
        in_specs=[
            pl.BlockSpec(memory_space=pltpu.SMEM),
            pl.BlockSpec((WINDOW, A_HEADS * HEAD_DIM), cur),
            pl.BlockSpec((WINDOW, LANE), prev), pl.BlockSpec((WINDOW, LANE), cur),
            pl.BlockSpec((WINDOW, LANE), prev), pl.BlockSpec((WINDOW, LANE), cur),
            _const_spec(bias.shape),
        ],
        out_specs=pl.BlockSpec((WINDOW, A_HEADS * HEAD_DIM), cur),
        out_shape=jax.ShapeDtypeStruct((n, A_HEADS * HEAD_DIM), BF16),
        compiler_params=_cparams(2), name="swa_attention",
    )(sinks.astype(F32), q, k, k, v, v, bias)


def _float_order_key(s):
    bits = pltpu.bitcast(s, jnp.int32)
    return bits ^ ((bits >> 31) & jnp.int32(0x7FFFFFFF))


def _dsa_kernel(q_ref, ckv_ref, qi_ref, ki_ref, wi_ref, wuk_ref, wuv_ref, tb_ref, o_ref,
                key_s, sel_s, qlat_s, qis_s, s_buf, p_buf, acc_s, m_s, l_s, a_s):
    n = pl.program_id(1)
    qb = Q_BLOCK
    ck = DSA_CK
    n_sel = TOPK_MAX
    n_chunks = n // (ck // qb) + 1
    lane = lax.broadcasted_iota(jnp.int32, (qb, LANE), 1)
    qpos = n * qb + lax.broadcasted_iota(jnp.int32, (qb, ck), 0)
    kcol = lax.broadcasted_iota(jnp.int32, (qb, ck), 1)

    for h in range(IDX_HEADS):
        pair = qi_ref[:, (h // 2) * LANE:(h // 2 + 1) * LANE]
        in_half = (lane >= (h % 2) * IDX_DIM) & (lane < (h % 2 + 1) * IDX_DIM)
        qis_s[h * qb:(h + 1) * qb, :] = jnp.where(in_half, pair, jnp.zeros_like(pair))

    for h in range(C_HEADS):
        pair = q_ref[:, (h // 2) * LANE:(h // 2 + 1) * LANE]
        ql = _dot(pair, wuk_ref[h]) * (HEAD_DIM ** -0.5)
        qlat_s[h * qb:(h + 1) * qb, :] = ql.astype(BF16)

    wi = wi_ref[...] * (IDX_HEADS ** -0.5)

    def score_body(c, _):
        start = pl.multiple_of(c * ck, ck)
        kc = ki_ref[pl.ds(start, ck), :]
        raw = _dot_nt(qis_s[...], kc)
        score = jnp.zeros((qb, ck), F32)
        for h in range(IDX_HEADS):
            idx_s = jnp.maximum(raw[h * qb:(h + 1) * qb, :] * (IDX_DIM ** -0.5), 0.0)
            score = score + idx_s * wi[:, h:h + 1]
        causal = (start + kcol) <= qpos
        key_s[c] = _float_order_key(jnp.where(causal, score, NEG))
        sel_s[c] = jnp.where(causal, 0.0, NEG)
        return 0

    lax.fori_loop(0, n_chunks, score_body, 0)

    @pl.when((n + 1) * qb > n_sel)
    def _():
        def count_where(pred_fn):
            def body(c, acc):
                hit = jnp.where(pred_fn(key_s[c]), 1.0, 0.0)
                return acc + hit[:, :LANE] + hit[:, LANE:]
            acc = lax.fori_loop(0, n_chunks, body, jnp.zeros((qb, LANE), F32))
            return jnp.sum(acc, axis=-1, keepdims=True)

        cnt0 = count_where(lambda kv: kv >= 0)
        prefix0 = jnp.where(cnt0 >= n_sel, jnp.int32(0), jnp.int32(INT_MIN))

        def bit_body(t, prefix):
            cand = prefix | jnp.left_shift(jnp.int32(1), 30 - t)
            cnt = count_where(lambda kv: kv >= cand)
            return jnp.where(cnt >= n_sel, cand, prefix)

        thr = lax.fori_loop(0, 31, bit_body, prefix0)
        need = n_sel - count_where(lambda kv: kv > thr)
        r = lax.broadcasted_iota(jnp.int32, (ck, ck), 0)
        cc = lax.broadcasted_iota(jnp.int32, (ck, ck), 1)
        before = jnp.where(r < cc, 1.0, 0.0).astype(BF16)

        def tie_body(c, seen):
            kv = key_s[c]
            tie = kv == thr
            tie_f = jnp.where(tie, 1.0, 0.0)
            earlier = _dot(tie_f.astype(BF16), before) + seen
            sel = (kv > thr) | (tie & (earlier < need))
            sel_s[c] = jnp.where(sel, sel_s[c], NEG)
            return seen + jnp.sum(tie_f, axis=-1, keepdims=True)

        lax.fori_loop(0, n_chunks, tie_body, jnp.zeros((qb, 1), F32))

    m_s[...] = jnp.full(m_s.shape, NEG, F32)
    l_s[...] = jnp.zeros(l_s.shape, F32)
    acc_s[...] = jnp.zeros(acc_s.shape, F32)

    def attn_body(c, _):
        start = pl.multiple_of(c * ck, ck)
        kc = ckv_ref[pl.ds(start, ck), :]
        s_buf[...] = _dot_nt(qlat_s[...], kc)
        rel_blk = jnp.minimum(n - c * (ck // qb), 3)
        mask_add = sel_s[c]
        for h in range(C_HEADS):
            rows = slice(h * qb, (h + 1) * qb)
            s = s_buf[rows, :] + tb_ref[rel_blk, h] + mask_add
            m_old = m_s[rows, :]
            m_new = jnp.maximum(m_old, jnp.max(s, axis=-1, keepdims=True))
            p = jnp.where(s > 0.5 * NEG, jnp.exp(s - m_new), 0.0)
            alpha = jnp.exp(m_old - m_new)
            l_s[rows, :] = alpha * l_s[rows, :] + jnp.sum(p, axis=-1, keepdims=True)
            m_s[rows, :] = m_new
            a_s[rows, :] = alpha
            p_buf[rows, :] = p.astype(BF16)
        pv = _dot(p_buf[...], kc)
        acc_s[...] = acc_s[...] * a_s[...] + pv
        return 0

    lax.fori_loop(0, n_chunks, attn_body, 0)

    for g in range(C_HEADS // 2):
        out = jnp.zeros((qb, LANE), F32)
        for hh in range(2):
            h = 2 * g + hh
            rows = slice(h * qb, (h + 1) * qb)
            o_lat = (acc_s[rows, :] / l_s[rows, :]).astype(BF16)
            out = out + _dot(o_lat, wuv_ref[h])
        o_ref[:, g * LANE:(g + 1) * LANE] = out.astype(o_ref.dtype)


def _dsa_attention(q, ckv, qi, ki2, wi, wuk_wide, wuv_wide, tb, bsz, t):
    n = q.shape[0]
    nq = t // Q_BLOCK
    nck = t // DSA_CK
    rows = C_HEADS * Q_BLOCK
    blk = lambda b, i: (b * nq + i, 0)
    per_b = lambda b, i: (b, 0)
    return pl.pallas_call(
        _dsa_kernel,
        grid=(bsz, nq),
        in_specs=[
            pl.BlockSpec((Q_BLOCK, C_HEADS * HEAD_DIM), blk),
            pl.BlockSpec((t, C_LATENT), per_b),
            pl.BlockSpec((Q_BLOCK, IDX_HEADS * IDX_DIM), blk),
            pl.BlockSpec((t, LANE), per_b),
            pl.BlockSpec((Q_BLOCK, LANE), blk),
            _const_spec(wuk_wide.shape), _const_spec(wuv_wide.shape), _const_spec(tb.shape),
        ],
        out_specs=pl.BlockSpec((Q_BLOCK, C_HEADS * HEAD_DIM), blk),
        out_shape=jax.ShapeDtypeStruct((n, C_HEADS * HEAD_DIM), BF16),
        scratch_shapes=[
            pltpu.VMEM((nck, Q_BLOCK, DSA_CK), jnp.int32),
            pltpu.VMEM((nck, Q_BLOCK, DSA_CK), F32),
            pltpu.VMEM((rows, C_LATENT), BF16),
            pltpu.VMEM((IDX_HEADS * Q_BLOCK, LANE), BF16),
            pltpu.VMEM((rows, DSA_CK), F32),
            pltpu.VMEM((rows, DSA_CK), BF16),
            pltpu.VMEM((rows, C_LATENT), F32),
            pltpu.VMEM((rows, 1), F32), pltpu.VMEM((rows, 1), F32), pltpu.VMEM((rows, 1), F32),
        ],
        compiler_params=_cparams(2), name="dsa_attention",
    )(q, ckv, qi, ki2, wi, wuk_wide, wuv_wide, tb)


def _post_attn_kernel(*refs, n_mix, alpha):
    x_ref = refs[0]
    mix_refs = refs[1:1 + n_mix]
    w_refs = refs[1 + n_mix:1 + 2 * n_mix]
    g_ref, b_ref, wr_ref, br_ref = refs[1 + 2 * n_mix:5 + 2 * n_mix]
    xo_ref, idx_ref, gate_ref, rank_ref, cnt_ref = refs[5 + 2 * n_mix:10 + 2 * n_mix]
    scr = refs[10 + 2 * n_mix:]
    wb_refs = scr[:n_mix]
    wr3_ref, before_ref, carry_ref = scr[n_mix:]
    tm = x_ref.shape[0]

    @pl.when(pl.program_id(0) == 0)
    def _():
        for w_ref, wb_ref in zip(w_refs, wb_refs):
            wb_ref[...] = w_ref[...].astype(BF16)
        hi, mid, lo = _split3(wr_ref[...])
        wr3_ref[0] = hi
        wr3_ref[1] = mid
        wr3_ref[2] = lo
        r = lax.broadcasted_iota(jnp.int32, (tm, tm), 0)
        c = lax.broadcasted_iota(jnp.int32, (tm, tm), 1)
        before_ref[...] = jnp.where(r < c, 1.0, 0.0).astype(BF16)
        carry_ref[...] = jnp.zeros(carry_ref.shape, F32)

    mix = _dot(mix_refs[0][...], wb_refs[0][...])
    for i in range(1, n_mix):
        mix = mix + _dot(mix_refs[i][...], wb_refs[i][...])
    xn = _layer_norm(alpha * x_ref[...] + mix, g_ref[...], b_ref[...])
    xo_ref[...] = xn

    xh, xm, xl = _split3(xn)
    logits = (_dot_nt(wr3_ref[0], xh) + _dot_nt(wr3_ref[0], xm) + _dot_nt(wr3_ref[1], xh)
              + _dot_nt(wr3_ref[0], xl) + _dot_nt(wr3_ref[1], xm) + _dot_nt(wr3_ref[2], xh)
              + br_ref[...])
    eidx = lax.broadcasted_iota(jnp.int32, logits.shape, 0)
    cur = logits
    vals, idxs = [], []
    for _ in range(TOP_K):
        mx = jnp.max(cur, axis=0, keepdims=True)
        first = jnp.min(jnp.where(cur == mx, eidx, N_EXPERTS), axis=0, keepdims=True)
        vals.append(mx)
        idxs.append(first)
        cur = jnp.where(eidx == first, -jnp.inf, cur)
    es = [jnp.exp(v - vals[0]) for v in vals]
    tot = es[0] + es[1] + es[2] + es[3]
    member = jnp.zeros(logits.shape, F32)
    for k in range(TOP_K):
        idx_ref[k:k + 1, :] = idxs[k]
        gate_ref[k:k + 1, :] = es[k] / tot
        member = member + jnp.where(eidx == idxs[k], 1.0, 0.0)
    earlier = _dot(member.astype(BF16), before_ref[...]) + carry_ref[...]
    for k in range(TOP_K):
        rk = jnp.sum(jnp.where(eidx == idxs[k], earlier, 0.0), axis=0, keepdims=True)
        rank_ref[k:k + 1, :] = rk.astype(jnp.int32)
    carry_ref[...] = carry_ref[...] + jnp.sum(member, axis=1, keepdims=True)
    cnt_ref[...] = carry_ref[...].astype(jnp.int32)


def _post_attn(x2, mixes, w_os, ln_g, ln_b, w_router, b_router, alpha):
    n, d = x2.shape
    n_mix = len(mixes)
    tm = TM_PROJ
    row = lambda i: (i, 0)
    colblk = lambda i: (0, i)
    in_specs = [pl.BlockSpec((tm, d), row)]
    in_specs += [pl.BlockSpec((tm, m.shape[1]), row) for m in mixes]
    in_specs += [_const_spec(w.shape) for w in w_os]
    in_specs += [_const_spec((1, d)), _const_spec((1, d)), _const_spec((N_EXPERTS, d)),
                 _const_spec((N_EXPERTS, 1))]
    out_specs = [pl.BlockSpec((tm, d), row), pl.BlockSpec((TOP_K, tm), colblk),
                 pl.BlockSpec((TOP_K, tm), colblk), pl.BlockSpec((TOP_K, tm), colblk),
                 _const_spec((N_EXPERTS, 1))]
    out_shape = [jax.ShapeDtypeStruct((n, d), F32), jax.ShapeDtypeStruct((TOP_K, n), jnp.int32),
                 jax.ShapeDtypeStruct((TOP_K, n), F32), jax.ShapeDtypeStruct((TOP_K, n), jnp.int32),
                 jax.ShapeDtypeStruct((N_EXPERTS, 1), jnp.int32)]
    scratch = [pltpu.VMEM(w.shape, BF16) for w in w_os]
    scratch += [pltpu.VMEM((3, N_EXPERTS, d), BF16), pltpu.VMEM((tm, tm), BF16),
                pltpu.VMEM((N_EXPERTS, 1), F32)]
    return pl.pallas_call(
        functools.partial(_post_attn_kernel, n_mix=n_mix, alpha=alpha),
        grid=(n // tm,),
        in_specs=in_specs, out_specs=out_specs, out_shape=out_shape, scratch_shapes=scratch,
        compiler_params=_cparams(1), name="post_attn",
    )(x2, *mixes, *w_os, ln_g.reshape(1, d), ln_b.reshape(1, d), w_router.T,
      b_router.reshape(N_EXPERTS, 1))


def _moe_kernel(be_ref, nu_ref, x_ref, wgu_ref, bgu_ref, wd_ref, bd_ref, y_ref, wgu_s, wd_s):
    i = pl.program_id(0)
    prev = be_ref[jnp.maximum(i - 1, 0)]
    live = i < nu_ref[0]

    @pl.when(live & ((i == 0) | (be_ref[i] != prev)))
    def _():
        wgu_s[...] = wgu_ref[...].astype(BF16)
        wd_s[...] = wd_ref[...].astype(BF16)

    @pl.when(live)
    def _():
        xb = x_ref[...]
        ch = 512
        acc = jnp.zeros(y_ref.shape, F32)
        for j in range(D_FF // ch):
            hg = _dot(xb, wgu_s[:, j * ch:(j + 1) * ch]) + bgu_ref[:, j * ch:(j + 1) * ch]
            hl = (_dot(xb, wgu_s[:, D_FF + j * ch:D_FF + (j + 1) * ch])
                  + bgu_ref[:, D_FF + j * ch:D_FF + (j + 1) * ch])
            glu = jnp.minimum(hg, SWIGLU_LIMIT)
            lin = jnp.clip(hl, -SWIGLU_LIMIT, SWIGLU_LIMIT)
            act = glu * jax.nn.sigmoid(SWIGLU_ALPHA * glu) * (lin + 1.0)
            acc = acc + _dot(act.astype(BF16), wd_s[j * ch:(j + 1) * ch, :])
        y_ref[...] = (acc + bd_ref[...]).astype(y_ref.dtype)

    @pl.when(jnp.logical_not(live))
    def _():
        y_ref[...] = jnp.zeros(y_ref.shape, y_ref.dtype)


def _moe_ffn(x_pad, block_e, n_used, layer, w_gate_up, b_gate_up, w_down, b_down):
    rows, d = x_pad.shape
    depth = w_gate_up.shape[0]
    n_blocks = rows // TM_MOE
    expert = lambda i, be, nu: (layer, be[i], 0, 0)
    grid_spec = pltpu.PrefetchScalarGridSpec(
        num_scalar_prefetch=2,
        grid=(n_blocks,),
        in_specs=[
            pl.BlockSpec((TM_MOE, d), lambda i, be, nu: (i, 0)),
            pl.BlockSpec((None, None, d, 2 * D_FF), expert),
            pl.BlockSpec((None, None, 1, 2 * D_FF), expert),
            pl.BlockSpec((None, None, D_FF, d), expert),
            pl.BlockSpec((None, None, 1, d), expert),
        ],
        out_specs=pl.BlockSpec((TM_MOE, d), lambda i, be, nu: (i, 0)),
        scratch_shapes=[pltpu.VMEM((d, 2 * D_FF), BF16), pltpu.VMEM((D_FF, d), BF16)],
    )
    return pl.pallas_call(
        _moe_kernel, grid_spec=grid_spec,
        out_shape=jax.ShapeDtypeStruct((rows, d), F32),
        compiler_params=_cparams(1), name="moe_ffn",
    )(block_e, n_used, x_pad, w_gate_up, b_gate_up.reshape(depth, N_EXPERTS, 1, 2 * D_FF), w_down,
      b_down.reshape(depth, N_EXPERTS, 1, d))


def _post_moe_kernel(dest_ref, x_ref, y_hbm, gt_ref, g_ref, b_ref, p_ref, wg_ref, wp_ref, o_ref,
                     wg_s, wp_s, ybuf, sem, *, alpha, n_tok):
    i = pl.program_id(0)
    n_steps = pl.num_programs(0)
    tm, d = x_ref.shape

    def fetch_tile(tile, slot):
        def body(j, carry):
            for k in range(TOP_K):
                r = dest_ref[k * n_tok + tile * tm + j]
                pltpu.make_async_copy(y_hbm.at[pl.ds(r, 1), :], ybuf.at[slot, k, pl.ds(j, 1), :],
                                      sem.at[slot]).start()
            return carry
        lax.fori_loop(0, tm, body, 0, unroll=8)

    @pl.when(i == 0)
    def _():
        wg_s[...] = wg_ref[...].astype(BF16)
        wp_s[...] = wp_ref[...].astype(BF16)
        fetch_tile(0, 0)

    @pl.when(i + 1 < n_steps)
    def _():
        fetch_tile(i + 1, (i + 1) % 2)

    slot = i % 2
    pltpu.make_async_copy(ybuf.at[slot], ybuf.at[slot], sem.at[slot]).wait()
    gates = gt_ref[...]
    ffn = gates[:, 0:1] * ybuf[slot, 0]
    for k in range(1, TOP_K):
        ffn = ffn + gates[:, k:k + 1] * ybuf[slot, k]
    xn = _layer_norm(alpha * x_ref[...] + ffn, g_ref[...], b_ref[...])
    gate = jax.nn.sigmoid(_dot(xn.astype(BF16), wg_s[...]))
    emb = _dot(p_ref[...].astype(BF16), wp_s[...])
    o_ref[...] = xn + gate * emb


def _post_moe(x2, y_pad, dest_flat, gates_t, ln_g, ln_b, layer, p3, w_ple_gate, w_ple_proj, alpha):
    n, d = x2.shape
    tm = TM_POST
    row = lambda i, dest: (i, 0)
    const2 = lambda i, dest: (0, 0)
    grid_spec = pltpu.PrefetchScalarGridSpec(
        num_scalar_prefetch=1,
        grid=(n // tm,),
        in_specs=[pl.BlockSpec((tm, d), row), pl.BlockSpec(memory_space=pl.ANY),
                  pl.BlockSpec((tm, TOP_K), row), pl.BlockSpec((1, d), const2),
                  pl.BlockSpec((1, d), const2),
                  pl.BlockSpec((None, tm, PLE_DIM), lambda i, dest: (layer, i, 0)),
                  pl.BlockSpec((None, d, d), lambda i, dest: (layer, 0, 0)),
                  pl.BlockSpec((None, PLE_DIM, d), lambda i, dest: (layer, 0, 0))],
        out_specs=pl.BlockSpec((tm, d), row),
        scratch_shapes=[pltpu.VMEM((d, d), BF16), pltpu.VMEM((PLE_DIM, d), BF16),
                        pltpu.VMEM((2, TOP_K, tm, d), F32), pltpu.SemaphoreType.DMA((2,))],
    )
    return pl.pallas_call(
        functools.partial(_post_moe_kernel, alpha=alpha, n_tok=n),
        grid_spec=grid_spec,
        out_shape=jax.ShapeDtypeStruct((n, d), F32),
        compiler_params=_cparams(1), name="post_moe",
    )(dest_flat, x2, y_pad, gates_t, ln_g.reshape(1, d), ln_b.reshape(1, d), p3, w_ple_gate,
      w_ple_proj)


def _t5_bucket(dist):
    d = jnp.maximum(dist, 0)
    ratio = jnp.log(jnp.maximum(d, 1).astype(F32) / MAX_EXACT) / math.log(MAX_DISTANCE / MAX_EXACT)
    large = MAX_EXACT + (ratio * (NUM_BUCKETS - MAX_EXACT)).astype(jnp.int32)
    large = jnp.minimum(large, NUM_BUCKETS - 1)
    return jnp.where(d < MAX_EXACT, d, large)


def _swa_bias_table(pos_bias):
    qi = jnp.arange(WINDOW)[:, None]
    kj = jnp.arange(2 * WINDOW)[None, :]
    dist = qi + WINDOW - kj
    valid = (dist >= 0) & (dist < WINDOW)
    bias = pos_bias[_t5_bucket(dist)][..., :A_HEADS].astype(F32).transpose(2, 0, 1)
    return jnp.where(valid[None], bias, NEG)


def _dsa_bias_tables(pos_bias):
    qi = jnp.arange(Q_BLOCK)[:, None]
    kj = jnp.arange(DSA_CK)[None, :]
    tabs = []
    for r in range(4):
        dist = r * Q_BLOCK + qi - kj
        tabs.append(pos_bias[_t5_bucket(dist)].astype(F32).transpose(2, 0, 1))
    return jnp.stack(tabs)


def _pad_cols(w, width=LANE):
    return jnp.pad(w, ((0, 0), (0, width - w.shape[1])))


def _even_mixer(x2, bsz, t, w_in, b_forget, sinks, w_o, swa_bias):
    hd = HEAD_DIM
    grp = A_HEADS // A_KV_HEADS
    pair_order = [g + grp * hh for g in range(grp) for hh in range(A_KV_HEADS)]
    na, nkv, nb = A_HEADS * hd, A_KV_HEADS * hd, B_HEADS * hd
    offs = [0, na, na + nkv, na + 2 * nkv, na + 2 * nkv + nb, na + 2 * nkv + 2 * nb,
            na + 2 * nkv + 3 * nb]
    w_aq = w_in[:, offs[0]:offs[1]].reshape(D_MODEL, A_HEADS, hd)[:, pair_order].reshape(D_MODEL, na)
    weights = [w_aq, w_in[:, offs[1]:offs[2]], w_in[:, offs[2]:offs[3]], w_in[:, offs[3]:offs[4]],
               w_in[:, offs[4]:offs[5]], w_in[:, offs[5]:offs[6]], _pad_cols(w_in[:, offs[6]:])]
    aq, ak, av, bq, bk, bv, bf = _project(x2, weights, [BF16] * 6 + [F32])
    o_a = _swa_attention(aq, ak, av, sinks, swa_bias, bsz, t)

    f_t = bf[:, :B_HEADS].reshape(bsz, t, B_HEADS).transpose(0, 2, 1)
    c = _forget_cumsum(f_t, b_forget)
    c_row = c.reshape(bsz, B_HEADS // 2, 2, t // FOX_TQ, FOX_TQ)
    c_col = c.reshape(bsz, B_HEADS // 2, 2, t).transpose(0, 1, 3, 2)
    o_b = _fox_attention(bq, bk, bv, c_col, c_row, bsz, t)

    w_oa = w_o[:na].reshape(A_HEADS, hd, D_MODEL)[jnp.asarray(pair_order)].reshape(na, D_MODEL)
    return [o_a, o_b], [w_oa, w_o[na:]]


def _odd_mixer(x2, bsz, t, w_in, kv_norm, w_uk, w_uv, w_o, dsa_tb):
    hd = HEAD_DIM
    nq, nl, ni = C_HEADS * hd, C_LATENT, IDX_HEADS * IDX_DIM
    w_ki = w_in[:, nq + nl + ni:nq + nl + ni + IDX_DIM]
    weights = [w_in[:, :nq], w_in[:, nq:nq + nl], w_in[:, nq + nl:nq + nl + ni],
               jnp.concatenate([w_ki, w_ki], axis=1), _pad_cols(w_in[:, nq + nl + ni + IDX_DIM:])]
    cq, ckv, qi, ki2, wi = _project(x2, weights, [BF16, BF16, BF16, BF16, F32],
                                    rms_gain=kv_norm.reshape(1, nl).astype(F32), rms_index=1)
    uk = w_uk.transpose(1, 2, 0)
    uv = w_uv.transpose(1, 0, 2)
    zk = jnp.zeros_like(uk)
    zv = jnp.zeros_like(uv)
    odd = (jnp.arange(C_HEADS) % 2 == 1)[:, None, None]
    wuk_wide = jnp.where(odd, jnp.concatenate([zk, uk], axis=1), jnp.concatenate([uk, zk], axis=1))
    wuv_wide = jnp.where(odd, jnp.concatenate([zv, uv], axis=2), jnp.concatenate([uv, zv], axis=2))
    o_c = _dsa_attention(cq, ckv, qi, ki2, wi, wuk_wide.astype(BF16), wuv_wide.astype(BF16),
                         dsa_tb, bsz, t)
    return [o_c], [w_o]


def _dispatch_plan(top_idx, rank, counts, n_tok):
    counts = counts.reshape(N_EXPERTS)
    padded = (counts + TM_MOE - 1) // TM_MOE * TM_MOE
    pad_ends = jnp.cumsum(padded)
    pad_starts = pad_ends - padded
    dest = pad_starts[top_idx] + rank
    n_blocks = (n_tok * TOP_K) // TM_MOE + N_EXPERTS
    block_start = jnp.arange(n_blocks, dtype=jnp.int32) * TM_MOE
    block_e = jnp.minimum(jnp.sum(pad_ends[None, :] <= block_start[:, None], axis=1),
                          N_EXPERTS - 1).astype(jnp.int32)
    n_used = (pad_ends[-1] // TM_MOE).astype(jnp.int32).reshape(1)
    return dest, block_e, n_used, n_blocks


def kernel(x, p, pos_bias, w_in_even, b_forget, sinks, w_o_even, w_in_odd, kv_norm, w_uk, w_uv,
           w_o_odd, ln_g, ln_b, w_router, b_router, w_gate_up, b_gate_up, w_down, b_down,
           w_ple_proj, w_ple_gate):
    bsz, t, d = x.shape
    depth = ln_g.shape[0]
    alpha = (2 * depth) ** 0.25
    n_tok = bsz * t
    x2 = x.reshape(n_tok, d)
    swa_bias = _swa_bias_table(pos_bias)
    dsa_tb = _dsa_bias_tables(pos_bias)
    p3 = p.reshape(depth, n_tok, PLE_DIM)
    tok_ids = jnp.tile(jnp.arange(n_tok, dtype=jnp.int32), (TOP_K,))
    for i in range(depth):
        j = i // 2
        if i % 2 == 0:
            mixes, w_os = _even_mixer(x2, bsz, t, w_in_even[j], b_forget[j], sinks[j], w_o_even[j],
                                      swa_bias)
        else:
            mixes, w_os = _odd_mixer(x2, bsz, t, w_in_odd[j], kv_norm[j], w_uk[j], w_uv[j],
                                     w_o_odd[j], dsa_tb)
        x1, top_idx, gates, rank, counts = _post_attn(x2, mixes, w_os, ln_g[i, 0], ln_b[i, 0],
                                                      w_router[i], b_router[i], alpha)
        dest, block_e, n_used, n_blocks = _dispatch_plan(top_idx, rank, counts, n_tok)
        src_tok = jnp.zeros((n_blocks * TM_MOE,), jnp.int32).at[dest.reshape(-1)].set(tok_ids)
        x_pad = jnp.take(x1.astype(BF16), src_tok, axis=0)
        y_pad = _moe_ffn(x_pad, block_e, n_used, i, w_gate_up, b_gate_up, w_down, b_down)
        x2 = _post_moe(x1, y_pad, dest.reshape(-1), gates.T, ln_g[i, 1], ln_b[i, 1], i, p3,
                       w_ple_gate, w_ple_proj, alpha)
    return x2.reshape(bsz, t, d)
```

```python
import functools
import math

import jax
import jax.numpy as jnp
from jax import lax
from jax.experimental import pallas as pl
from jax.experimental.pallas import tpu as pltpu

D_MODEL = 1024
HEAD_DIM = 64
A_HEADS, A_KV_HEADS, WINDOW = 8, 2, 128
B_HEADS = 8
C_HEADS, C_LATENT = 16, 128
IDX_HEADS, IDX_DIM = 8, 64
TOPK_MAX = 256
NUM_BUCKETS, MAX_EXACT, MAX_DISTANCE = 32, 16, 128
N_EXPERTS, TOP_K, D_FF = 32, 4, 1024
SWIGLU_LIMIT, SWIGLU_ALPHA = 7.0, 1.702
PLE_DIM = 256
Q_BLOCK = 128
LN_EPS = 1e-5
NEG = -1e30

LANE = 128
VMEM_LIMIT = 56 * 1024 * 1024
TM_PROJ = 512
TM_MOE = 256
TM_POST = 256
FOX_TQ = 256
DSA_CK = 256
INT_MIN = -2 ** 31

BF16 = jnp.bfloat16
F32 = jnp.float32


def _cparams(n_axes):
    return pltpu.CompilerParams(dimension_semantics=("arbitrary",) * n_axes,
                                vmem_limit_bytes=VMEM_LIMIT)


def _const_spec(shape):
    nd = len(shape)
    return pl.BlockSpec(shape, lambda *_: (0,) * nd)


def _dot(a, b):
    return jnp.dot(a, b, preferred_element_type=F32)


def _dot_nt(a, b):
    return lax.dot_general(a, b, (((1,), (1,)), ((), ())), preferred_element_type=F32)


def _layer_norm(v, g, b):
    mu = jnp.mean(v, axis=-1, keepdims=True)
    d = v - mu
    var = jnp.mean(d * d, axis=-1, keepdims=True)
    return d * lax.rsqrt(var + LN_EPS) * g + b


def _split3(v):
    hi = v.astype(BF16)
    r1 = v - hi.astype(F32)
    mid = r1.astype(BF16)
    lo = (r1 - mid.astype(F32)).astype(BF16)
    return hi, mid, lo


def _proj_kernel(*refs, n_w, rms_index):
    x_ref = refs[0]
    w_refs = refs[1:1 + n_w]
    g_ref = refs[1 + n_w]
    o_refs = refs[2 + n_w:2 + 2 * n_w]
    wb_refs = refs[2 + 2 * n_w:]

    @pl.when(pl.program_id(0) == 0)
    def _():
        for w_ref, wb_ref in zip(w_refs, wb_refs):
            wb_ref[...] = w_ref[...].astype(BF16)

    xb = x_ref[...].astype(BF16)
    for i in range(n_w):
        h = _dot(xb, wb_refs[i][...])
        if i == rms_index:
            h = h * lax.rsqrt(jnp.mean(h * h, axis=-1, keepdims=True) + LN_EPS) * g_ref[...]
        o_refs[i][...] = h.astype(o_refs[i].dtype)


def _project(x2, weights, out_dtypes, rms_gain=None, rms_index=-1):
    n, d = x2.shape
    n_w = len(weights)
    if rms_gain is None:
        rms_gain = jnp.ones((1, LANE), F32)
    in_specs = [pl.BlockSpec((TM_PROJ, d), lambda i: (i, 0))]
    in_specs += [_const_spec(w.shape) for w in weights]
    in_specs += [_const_spec(rms_gain.shape)]
    out_specs = [pl.BlockSpec((TM_PROJ, w.shape[1]), lambda i: (i, 0)) for w in weights]
    out_shape = [jax.ShapeDtypeStruct((n, w.shape[1]), dt) for w, dt in zip(weights, out_dtypes)]
    scratch = [pltpu.VMEM(w.shape, BF16) for w in weights]
    return pl.pallas_call(
        functools.partial(_proj_kernel, n_w=n_w, rms_index=rms_index),
        grid=(n // TM_PROJ,),
        in_specs=in_specs, out_specs=out_specs, out_shape=out_shape,
        scratch_shapes=scratch, compiler_params=_cparams(1), name="in_proj",
    )(x2, *weights, rms_gain)


def _cumsum_kernel(f_ref, b_ref, o_ref):
    z = f_ref[...] + b_ref[...]
    ls = jnp.minimum(z, 0.0) - jnp.log1p(jnp.exp(-jnp.abs(z)))
    t = ls.shape[1]
    r = lax.broadcasted_iota(jnp.int32, (LANE, LANE), 0)
    c = lax.broadcasted_iota(jnp.int32, (LANE, LANE), 1)
    tri = jnp.where(r <= c, 1.0, 0.0).astype(BF16)
    carry = jnp.zeros((ls.shape[0], 1), F32)
    for blk in range(t // LANE):
        seg = ls[:, blk * LANE:(blk + 1) * LANE]
        hi, mid, lo = _split3(seg)
        pre = _dot(hi, tri) + _dot(mid, tri) + _dot(lo, tri) + carry
        o_ref[:, blk * LANE:(blk + 1) * LANE] = pre
        carry = pre[:, LANE - 1:LANE]


def _forget_cumsum(f_t, b_forget):
    bsz, h, t = f_t.shape
    return pl.pallas_call(
        _cumsum_kernel,
        grid=(bsz,),
        in_specs=[pl.BlockSpec((None, h, t), lambda b: (b, 0, 0)), _const_spec((h, 1))],
        out_specs=pl.BlockSpec((None, h, t), lambda b: (b, 0, 0)),
        out_shape=jax.ShapeDtypeStruct((bsz, h, t), F32),
        compiler_params=_cparams(1), name="forget_cumsum",
    )(f_t, b_forget.reshape(h, 1).astype(F32))


def _fox_kernel(q_ref, k_ref, v_ref, cq_ref, ck_ref, o_ref, s_all, m_s, l_s, acc_s):
    i = pl.program_id(2)
    tq = q_ref.shape[0]
    tk = tq
    q = q_ref[...]
    lane = lax.broadcasted_iota(jnp.int32, (tq, LANE), 1)
    on_or_below_diag = (lax.broadcasted_iota(jnp.int32, (tq, tk), 0)
                        >= lax.broadcasted_iota(jnp.int32, (tq, tk), 1))
    outs = []
    for hh in range(2):
        in_half = (lane >= hh * HEAD_DIM) & (lane < (hh + 1) * HEAD_DIM)
        qh = jnp.where(in_half, q, jnp.zeros_like(q)) * jnp.asarray(HEAD_DIM ** -0.5, BF16)
        cq = cq_ref[:, hh:hh + 1]

        def logits(j, qh=qh, cq=cq, hh=hh):
            start = pl.multiple_of(j * tk, tk)
            return _dot_nt(qh, k_ref[pl.ds(start, tk), :]) + cq - ck_ref[hh, pl.ds(j, 1), :]

        def keep(j, s):
            s_all[j] = s
            m_s[...] = jnp.maximum(m_s[...], jnp.maximum(s[:, :LANE], s[:, LANE:]))

        def logit_body(j, carry):
            keep(j, logits(j))
            return carry

        m_s[...] = jnp.full(m_s.shape, NEG, F32)
        lax.fori_loop(0, i, logit_body, 0)
        keep(i, jnp.where(on_or_below_diag, logits(i), NEG))
        m_row = jnp.broadcast_to(jnp.max(m_s[...], axis=-1, keepdims=True), (tq, LANE))
        l_s[...] = jnp.zeros(l_s.shape, F32)
        acc_s[...] = jnp.zeros(acc_s.shape, F32)

        def prob_body(j, carry, m_row=m_row):
            start = pl.multiple_of(j * tk, tk)
            p_lo = jnp.exp(s_all[j, :, :LANE] - m_row)
            p_hi = jnp.exp(s_all[j, :, LANE:] - m_row)
            l_s[...] = l_s[...] + p_lo + p_hi
            p = jnp.concatenate([p_lo, p_hi], axis=1).astype(BF16)
            acc_s[...] = acc_s[...] + _dot(p, v_ref[pl.ds(start, tk), :])
            return carry

        lax.fori_loop(0, i + 1, prob_body, 0)
        outs.append(acc_s[...] / jnp.sum(l_s[...], axis=-1, keepdims=True))
    o_ref[...] = jnp.where(lane < HEAD_DIM, outs[0], outs[1]).astype(o_ref.dtype)


def _fox_attention(q, k, v, c_col, c_row, bsz, t):
    n, hd = q.shape
    n_pair = hd // LANE
    nq = t // FOX_TQ
    return pl.pallas_call(
        _fox_kernel,
        grid=(bsz, n_pair, nq),
        in_specs=[
            pl.BlockSpec((FOX_TQ, LANE), lambda b, j, i: (b * nq + i, j)),
            pl.BlockSpec((t, LANE), lambda b, j, i: (b, j)),
            pl.BlockSpec((t, LANE), lambda b, j, i: (b, j)),
            pl.BlockSpec((None, None, FOX_TQ, 2), lambda b, j, i: (b, j, i, 0)),
            pl.BlockSpec((None, None, 2, nq, FOX_TQ), lambda b, j, i: (b, j, 0, 0, 0)),
        ],
        out_specs=pl.BlockSpec((FOX_TQ, LANE), lambda b, j, i: (b * nq + i, j)),
        out_shape=jax.ShapeDtypeStruct((n, hd), BF16),
        scratch_shapes=[pltpu.VMEM((nq, FOX_TQ, FOX_TQ), F32), pltpu.VMEM((FOX_TQ, LANE), F32),
                        pltpu.VMEM((FOX_TQ, LANE), F32), pltpu.VMEM((FOX_TQ, LANE), F32)],
        compiler_params=_cparams(3), name="fox_attention",
    )(q, k, v, c_col, c_row)


def _swa_kernel(sink_ref, q_ref, kp_ref, kc_ref, vp_ref, vc_ref, bias_ref, o_ref):
    nblk = pl.program_id(1)
    w = q_ref.shape[0]
    kk = jnp.concatenate([kp_ref[...], kc_ref[...]], axis=0)
    vv = jnp.concatenate([vp_ref[...], vc_ref[...]], axis=0)
    lane = lax.broadcasted_iota(jnp.int32, (w, LANE), 1)
    col = lax.broadcasted_iota(jnp.int32, (w, 2 * w), 1)
    no_prev = (col < w) & (nblk == 0)
    grp = A_HEADS // A_KV_HEADS
    for g in range(grp):
        qg = q_ref[:, g * LANE:(g + 1) * LANE]
        outs = []
        for hh in range(A_KV_HEADS):
            head = g + grp * hh
            in_half = (lane >= hh * HEAD_DIM) & (lane < (hh + 1) * HEAD_DIM)
            qh = jnp.where(in_half, qg, jnp.zeros_like(qg)) * jnp.asarray(HEAD_DIM ** -0.5, BF16)
            s = _dot_nt(qh, kk) + bias_ref[head]
            s = jnp.where(no_prev, NEG, s)
            sink = sink_ref[head]
            m = jnp.maximum(jnp.max(s, axis=-1, keepdims=True), sink)
            e = jnp.exp(s - m)
            denom = jnp.sum(e, axis=-1, keepdims=True) + jnp.exp(sink - m)
            p = e / denom
            outs.append(_dot(p.astype(BF16), vv))
        o_ref[:, g * LANE:(g + 1) * LANE] = jnp.where(lane < HEAD_DIM, outs[0], outs[1]).astype(o_ref.dtype)


def _swa_attention(q, k, v, sinks, bias, bsz, t):
    n = q.shape[0]
    nb = t // WINDOW
    cur = lambda b, i: (b * nb + i, 0)
    prev = lambda b, i: (b * nb + jnp.maximum(i - 1, 0), 0)
    return pl.pallas_call(
        _swa_kernel,
        grid=(bsz, nb),
        in_specs=[
            pl.BlockSpec(memory_space=pltpu.SMEM),
            pl.BlockSpec((WINDOW, A_HEADS * HEAD_DIM), cur),
            pl.BlockSpec((WINDOW, LANE), prev), pl.BlockSpec((WINDOW, LANE), cur),
            pl.BlockSpec((WINDOW, LANE), prev), pl.BlockSpec((WINDOW, LANE), cur),
            _const_spec(bias.shape),
        ],
        out_specs=pl.BlockSpec((WINDOW, A_HEADS * HEAD_DIM), cur),
        out_shape=jax.ShapeDtypeStruct((n, A_HEADS * HEAD_DIM), BF16),
        compiler_params=_cparams(2), name="swa_attention",
    )(sinks.astype(F32), q, k, k, v, v, bias)


def _float_order_key(s):
    bits = pltpu.bitcast(s, jnp.int32)
    return bits ^ ((bits >> 31) & jnp.int32(0x7FFFFFFF))


def _dsa_kernel(q_ref, ckv_ref, qi_ref, ki_ref, wi_ref, wuk_ref, wuv_ref, tb_ref, o_ref,
                key_s, sel_s, qlat_s, qis_s, s_all, p_buf, acc_s, m_s, l_s):
    n = pl.program_id(1)
    qb = Q_BLOCK
    ck = DSA_CK
    n_sel = TOPK_MAX
    n_chunks = n // (ck // qb) + 1
    lane = lax.broadcasted_iota(jnp.int32, (qb, LANE), 1)
    qpos = n * qb + lax.broadcasted_iota(jnp.int32, (qb, ck), 0)
    kcol = lax.broadcasted_iota(jnp.int32, (qb, ck), 1)

    for h in range(IDX_HEADS):
        pair = qi_ref[:, (h // 2) * LANE:(h // 2 + 1) * LANE]
        in_half = (lane >= (h % 2) * IDX_DIM) & (lane < (h % 2 + 1) * IDX_DIM)
        qis_s[h * qb:(h + 1) * qb, :] = jnp.where(in_half, pair, jnp.zeros_like(pair))

    for h in range(C_HEADS):
        pair = q_ref[:, (h // 2) * LANE:(h // 2 + 1) * LANE]
        ql = _dot(pair, wuk_ref[h]) * (HEAD_DIM ** -0.5)
        qlat_s[h * qb:(h + 1) * qb, :] = ql.astype(BF16)

    wi = wi_ref[...] * (IDX_HEADS ** -0.5)

    def score_body(c, _):
        start = pl.multiple_of(c * ck, ck)
        kc = ki_ref[pl.ds(start, ck), :]
        raw = _dot_nt(qis_s[...], kc)
        score = jnp.zeros((qb, ck), F32)
        for h in range(IDX_HEADS):
            idx_s = jnp.maximum(raw[h * qb:(h + 1) * qb, :] * (IDX_DIM ** -0.5), 0.0)
            score = score + idx_s * wi[:, h:h + 1]
        causal = (start + kcol) <= qpos
        key_s[c] = _float_order_key(jnp.where(causal, score, NEG))
        sel_s[c] = jnp.where(causal, 0.0, NEG)
        return 0

    lax.fori_loop(0, n_chunks, score_body, 0)

    @pl.when((n + 1) * qb > n_sel)
    def _():
        def count_where(pred_fn):
            def body(c, acc):
                hit = jnp.where(pred_fn(key_s[c]), 1.0, 0.0)
                return acc + hit[:, :LANE] + hit[:, LANE:]
            acc = lax.fori_loop(0, n_chunks, body, jnp.zeros((qb, LANE), F32))
            return jnp.sum(acc, axis=-1, keepdims=True)

        cnt0 = count_where(lambda kv: kv >= 0)
        prefix0 = jnp.where(cnt0 >= n_sel, jnp.int32(0), jnp.int32(INT_MIN))

        def bit_body(t, prefix):
            cand = prefix | jnp.left_shift(jnp.int32(1), 30 - t)
            cnt = count_where(lambda kv: kv >= cand)
            return jnp.where(cnt >= n_sel, cand, prefix)

        thr = lax.fori_loop(0, 31, bit_body, prefix0)
        need = n_sel - count_where(lambda kv: kv > thr)
        r = lax.broadcasted_iota(jnp.int32, (ck, ck), 0)
        cc = lax.broadcasted_iota(jnp.int32, (ck, ck), 1)
        before = jnp.where(r < cc, 1.0, 0.0).astype(BF16)

        def tie_body(c, seen):
            kv = key_s[c]
            tie = kv == thr
            tie_f = jnp.where(tie, 1.0, 0.0)
            earlier = _dot(tie_f.astype(BF16), before) + seen
            sel = (kv > thr) | (tie & (earlier < need))
            sel_s[c] = jnp.where(sel, sel_s[c], NEG)
            return seen + jnp.sum(tie_f, axis=-1, keepdims=True)

        lax.fori_loop(0, n_chunks, tie_body, jnp.zeros((qb, 1), F32))

    m_s[...] = jnp.full(m_s.shape, NEG, F32)

    def logit_body(c, _):
        start = pl.multiple_of(c * ck, ck)
        raw = _dot_nt(qlat_s[...], ckv_ref[pl.ds(start, ck), :])
        rel_blk = jnp.minimum(n - c * (ck // qb), 3)
        mask_add = sel_s[c]
        for h in range(C_HEADS):
            rows = slice(h * qb, (h + 1) * qb)
            s = raw[rows, :] + tb_ref[rel_blk, h] + mask_add
            s_all[c, rows, :] = s
            m_s[rows, :] = jnp.maximum(m_s[rows, :], jnp.maximum(s[:, :LANE], s[:, LANE:]))
        return 0

    lax.fori_loop(0, n_chunks, logit_body, 0)

    for h in range(C_HEADS):
        rows = slice(h * qb, (h + 1) * qb)
        m_s[rows, :] = jnp.broadcast_to(jnp.max(m_s[rows, :], axis=-1, keepdims=True), (qb, LANE))
    l_s[...] = jnp.zeros(l_s.shape, F32)
    acc_s[...] = jnp.zeros(acc_s.shape, F32)

    def prob_body(c, _):
        start = pl.multiple_of(c * ck, ck)
        for h in range(C_HEADS):
            rows = slice(h * qb, (h + 1) * qb)
            m_row = m_s[rows, :]
            p_lo = jnp.exp(s_all[c, rows, :LANE] - m_row)
            p_hi = jnp.exp(s_all[c, rows, LANE:] - m_row)
            l_s[rows, :] = l_s[rows, :] + p_lo + p_hi
            p_buf[rows, :LANE] = p_lo.astype(BF16)
            p_buf[rows, LANE:] = p_hi.astype(BF16)
        acc_s[...] = acc_s[...] + _dot(p_buf[...], ckv_ref[pl.ds(start, ck), :])
        return 0

    lax.fori_loop(0, n_chunks, prob_body, 0)

    for g in range(C_HEADS // 2):
        out = jnp.zeros((qb, LANE), F32)
        for hh in range(2):
            h = 2 * g + hh
            rows = slice(h * qb, (h + 1) * qb)
            denom = jnp.sum(l_s[rows, :], axis=-1, keepdims=True)
            o_lat = (acc_s[rows, :] / denom).astype(BF16)
            out = out + _dot(o_lat, wuv_ref[h])
        o_ref[:, g * LANE:(g + 1) * LANE] = out.astype(o_ref.dtype)


def _dsa_attention(q, ckv, qi, ki2, wi, wuk_wide, wuv_wide, tb, bsz, t):
    n = q.shape[0]
    nq = t // Q_BLOCK
    nck = t // DSA_CK
    rows = C_HEADS * Q_BLOCK
    blk = lambda b, i: (b * nq + i, 0)
    per_b = lambda b, i: (b, 0)
    return pl.pallas_call(
        _dsa_kernel,
        grid=(bsz, nq),
        in_specs=[
            pl.BlockSpec((Q_BLOCK, C_HEADS * HEAD_DIM), blk),
            pl.BlockSpec((t, C_LATENT), per_b),
            pl.BlockSpec((Q_BLOCK, IDX_HEADS * IDX_DIM), blk),
            pl.BlockSpec((t, LANE), per_b),
            pl.BlockSpec((Q_BLOCK, LANE), blk),
            _const_spec(wuk_wide.shape), _const_spec(wuv_wide.shape), _const_spec(tb.shape),
        ],
        out_specs=pl.BlockSpec((Q_BLOCK, C_HEADS * HEAD_DIM), blk),
        out_shape=jax.ShapeDtypeStruct((n, C_HEADS * HEAD_DIM), BF16),
        scratch_shapes=[
            pltpu.VMEM((nck, Q_BLOCK, DSA_CK), jnp.int32),
            pltpu.VMEM((nck, Q_BLOCK, DSA_CK), F32),
            pltpu.VMEM((rows, C_LATENT), BF16),
            pltpu.VMEM((IDX_HEADS * Q_BLOCK, LANE), BF16),
            pltpu.VMEM((nck, rows, DSA_CK), F32),
            pltpu.VMEM((rows, DSA_CK), BF16),
            pltpu.VMEM((rows, C_LATENT), F32),
            pltpu.VMEM((rows, LANE), F32),
            pltpu.VMEM((rows, LANE), F32),
        ],
        compiler_params=_cparams(2), name="dsa_attention",
    )(q, ckv, qi, ki2, wi, wuk_wide, wuv_wide, tb)


def _post_attn_kernel(*refs, n_mix, alpha):
    x_ref = refs[0]
    mix_refs = refs[1:1 + n_mix]
    w_refs = refs[1 + n_mix:1 + 2 * n_mix]
    g_ref, b_ref, wr_ref, br_ref = refs[1 + 2 * n_mix:5 + 2 * n_mix]
    xo_ref, idx_ref, gate_ref, rank_ref, cnt_ref = refs[5 + 2 * n_mix:10 + 2 * n_mix]
    scr = refs[10 + 2 * n_mix:]
    wb_refs = scr[:n_mix]
    wr3_ref, before_ref, carry_ref = scr[n_mix:]
    tm = x_ref.shape[0]

    @pl.when(pl.program_id(0) == 0)
    def _():
        for w_ref, wb_ref in zip(w_refs, wb_refs):
            wb_ref[...] = w_ref[...].astype(BF16)
        hi, mid, lo = _split3(wr_ref[...])
        wr3_ref[0] = hi
        wr3_ref[1] = mid
        wr3_ref[2] = lo
        r = lax.broadcasted_iota(jnp.int32, (tm, tm), 0)
        c = lax.broadcasted_iota(jnp.int32, (tm, tm), 1)
        before_ref[...] = jnp.where(r < c, 1.0, 0.0).astype(BF16)
        carry_ref[...] = jnp.zeros(carry_ref.shape, F32)

    mix = _dot(mix_refs[0][...], wb_refs[0][...])
    for i in range(1, n_mix):
        mix = mix + _dot(mix_refs[i][...], wb_refs[i][...])
    xn = _layer_norm(alpha * x_ref[...] + mix, g_ref[...], b_ref[...])
    xo_ref[...] = xn

    xh, xm, xl = _split3(xn)
    logits = (_dot_nt(wr3_ref[0], xh) + _dot_nt(wr3_ref[0], xm) + _dot_nt(wr3_ref[1], xh)
              + _dot_nt(wr3_ref[0], xl) + _dot_nt(wr3_ref[1], xm) + _dot_nt(wr3_ref[2], xh)
              + br_ref[...])
    eidx = lax.broadcasted_iota(jnp.int32, logits.shape, 0)
    cur = logits
    vals, idxs = [], []
    for _ in range(TOP_K):
        mx = jnp.max(cur, axis=0, keepdims=True)
        first = jnp.min(jnp.where(cur == mx, eidx, N_EXPERTS), axis=0, keepdims=True)
        vals.append(mx)
        idxs.append(first)
        cur = jnp.where(eidx == first, -jnp.inf, cur)
    es = [jnp.exp(v - vals[0]) for v in vals]
    tot = es[0] + es[1] + es[2] + es[3]
    member = jnp.zeros(logits.shape, F32)
    for k in range(TOP_K):
        idx_ref[k:k + 1, :] = idxs[k]
        gate_ref[k:k + 1, :] = es[k] / tot
        member = member + jnp.where(eidx == idxs[k], 1.0, 0.0)
    earlier = _dot(member.astype(BF16), before_ref[...]) + carry_ref[...]
    for k in range(TOP_K):
        rk = jnp.sum(jnp.where(eidx == idxs[k], earlier, 0.0), axis=0, keepdims=True)
        rank_ref[k:k + 1, :] = rk.astype(jnp.int32)
    carry_ref[...] = carry_ref[...] + jnp.sum(member, axis=1, keepdims=True)
    cnt_ref[...] = carry_ref[...].astype(jnp.int32)


def _post_attn(x2, mixes, w_os, ln_g, ln_b, w_router, b_router, alpha):
    n, d = x2.shape
    n_mix = len(mixes)
    tm = TM_PROJ
    row = lambda i: (i, 0)
    colblk = lambda i: (0, i)
    in_specs = [pl.BlockSpec((tm, d), row)]
    in_specs += [pl.BlockSpec((tm, m.shape[1]), row) for m in mixes]
    in_specs += [_const_spec(w.shape) for w in w_os]
    in_specs += [_const_spec((1, d)), _const_spec((1, d)), _const_spec((N_EXPERTS, d)),
                 _const_spec((N_EXPERTS, 1))]
    out_specs = [pl.BlockSpec((tm, d), row), pl.BlockSpec((TOP_K, tm), colblk),
                 pl.BlockSpec((TOP_K, tm), colblk), pl.BlockSpec((TOP_K, tm), colblk),
                 _const_spec((N_EXPERTS, 1))]
    out_shape = [jax.ShapeDtypeStruct((n, d), F32), jax.ShapeDtypeStruct((TOP_K, n), jnp.int32),
                 jax.ShapeDtypeStruct((TOP_K, n), F32), jax.ShapeDtypeStruct((TOP_K, n), jnp.int32),
                 jax.ShapeDtypeStruct((N_EXPERTS, 1), jnp.int32)]
    scratch = [pltpu.VMEM(w.shape, BF16) for w in w_os]
    scratch += [pltpu.VMEM((3, N_EXPERTS, d), BF16), pltpu.VMEM((tm, tm), BF16),
                pltpu.VMEM((N_EXPERTS, 1), F32)]
    return pl.pallas_call(
        functools.partial(_post_attn_kernel, n_mix=n_mix, alpha=alpha),
        grid=(n // tm,),
        in_specs=in_specs, out_specs=out_specs, out_shape=out_shape, scratch_shapes=scratch,
        compiler_params=_cparams(1), name="post_attn",
    )(x2, *mixes, *w_os, ln_g.reshape(1, d), ln_b.reshape(1, d), w_router.T,
      b_router.reshape(N_EXPERTS, 1))


def _moe_kernel(be_ref, nu_ref, x_ref, wgu_ref, bgu_ref, wd_ref, bd_ref, y_ref, wgu_s, wd_s):
    i = pl.program_id(0)
    prev = be_ref[jnp.maximum(i - 1, 0)]
    live = i < nu_ref[0]

    @pl.when(live & ((i == 0) | (be_ref[i] != prev)))
    def _():
        wgu_s[...] = wgu_ref[...].astype(BF16)
        wd_s[...] = wd_ref[...].astype(BF16)

    @pl.when(live)
    def _():
        xb = x_ref[...]
        ch = 512
        acc = jnp.zeros(y_ref.shape, F32)
        for j in range(D_FF // ch):
            hg = _dot(xb, wgu_s[:, j * ch:(j + 1) * ch]) + bgu_ref[:, j * ch:(j + 1) * ch]
            hl = (_dot(xb, wgu_s[:, D_FF + j * ch:D_FF + (j + 1) * ch])
                  + bgu_ref[:, D_FF + j * ch:D_FF + (j + 1) * ch])
            glu = jnp.minimum(hg, SWIGLU_LIMIT)
            lin = jnp.clip(hl, -SWIGLU_LIMIT, SWIGLU_LIMIT)
            act = glu * jax.nn.sigmoid(SWIGLU_ALPHA * glu) * (lin + 1.0)
            acc = acc + _dot(act.astype(BF16), wd_s[j * ch:(j + 1) * ch, :])
        y_ref[...] = (acc + bd_ref[...]).astype(y_ref.dtype)

    @pl.when(jnp.logical_not(live))
    def _():
        y_ref[...] = jnp.zeros(y_ref.shape, y_ref.dtype)


def _moe_ffn(x_pad, block_e, n_used, layer, w_gate_up, b_gate_up, w_down, b_down):
    rows, d = x_pad.shape
    depth = w_gate_up.shape[0]
    n_blocks = rows // TM_MOE
    expert = lambda i, be, nu: (layer, be[i], 0, 0)
    grid_spec = pltpu.PrefetchScalarGridSpec(
        num_scalar_prefetch=2,
        grid=(n_blocks,),
        in_specs=[
            pl.BlockSpec((TM_MOE, d), lambda i, be, nu: (i, 0)),
            pl.BlockSpec((None, None, d, 2 * D_FF), expert),
            pl.BlockSpec((None, None, 1, 2 * D_FF), expert),
            pl.BlockSpec((None, None, D_FF, d), expert),
            pl.BlockSpec((None, None, 1, d), expert),
        ],
        out_specs=pl.BlockSpec((TM_MOE, d), lambda i, be, nu: (i, 0)),
        scratch_shapes=[pltpu.VMEM((d, 2 * D_FF), BF16), pltpu.VMEM((D_FF, d), BF16)],
    )
    return pl.pallas_call(
        _moe_kernel, grid_spec=grid_spec,
        out_shape=jax.ShapeDtypeStruct((rows, d), F32),
        compiler_params=_cparams(1), name="moe_ffn",
    )(block_e, n_used, x_pad, w_gate_up, b_gate_up.reshape(depth, N_EXPERTS, 1, 2 * D_FF), w_down,
      b_down.reshape(depth, N_EXPERTS, 1, d))


def _post_moe_kernel(dest_ref, x_ref, y_hbm, gt_ref, g_ref, b_ref, p_ref, wg_ref, wp_ref, o_ref,
                     wg_s, wp_s, ybuf, sem, *, alpha, n_tok):
    i = pl.program_id(0)
    n_steps = pl.num_programs(0)
    tm, d = x_ref.shape

    def fetch_tile(tile, slot):
        def body(j, carry):
            for k in range(TOP_K):
                r = dest_ref[k * n_tok + tile * tm + j]
                pltpu.make_async_copy(y_hbm.at[pl.ds(r, 1), :], ybuf.at[slot, k, pl.ds(j, 1), :],
                                      sem.at[slot]).start()
            return carry
        lax.fori_loop(0, tm, body, 0, unroll=8)

    @pl.when(i == 0)
    def _():
        wg_s[...] = wg_ref[...].astype(BF16)
        wp_s[...] = wp_ref[...].astype(BF16)
        fetch_tile(0, 0)

    @pl.when(i + 1 < n_steps)
    def _():
        fetch_tile(i + 1, (i + 1) % 2)

    slot = i % 2
    pltpu.make_async_copy(ybuf.at[slot], ybuf.at[slot], sem.at[slot]).wait()
    gates = gt_ref[...]
    ffn = gates[:, 0:1] * ybuf[slot, 0]
    for k in range(1, TOP_K):
        ffn = ffn + gates[:, k:k + 1] * ybuf[slot, k]
    xn = _layer_norm(alpha * x_ref[...] + ffn, g_ref[...], b_ref[...])
    gate = jax.nn.sigmoid(_dot(xn.astype(BF16), wg_s[...]))
    emb = _dot(p_ref[...].astype(BF16), wp_s[...])
    o_ref[...] = xn + gate * emb


def _post_moe(x2, y_pad, dest_flat, gates_t, ln_g, ln_b, layer, p3, w_ple_gate, w_ple_proj, alpha):
    n, d = x2.shape
    tm = TM_POST
    row = lambda i, dest: (i, 0)
    const2 = lambda i, dest: (0, 0)
    grid_spec = pltpu.PrefetchScalarGridSpec(
        num_scalar_prefetch=1,
        grid=(n // tm,),
        in_specs=[pl.BlockSpec((tm, d), row), pl.BlockSpec(memory_space=pl.ANY),
                  pl.BlockSpec((tm, TOP_K), row), pl.BlockSpec((1, d), const2),
                  pl.BlockSpec((1, d), const2),
                  pl.BlockSpec((None, tm, PLE_DIM), lambda i, dest: (layer, i, 0)),
                  pl.BlockSpec((None, d, d), lambda i, dest: (layer, 0, 0)),
                  pl.BlockSpec((None, PLE_DIM, d), lambda i, dest: (layer, 0, 0))],
        out_specs=pl.BlockSpec((tm, d), row),
        scratch_shapes=[pltpu.VMEM((d, d), BF16), pltpu.VMEM((PLE_DIM, d), BF16),
                        pltpu.VMEM((2, TOP_K, tm, d), F32), pltpu.SemaphoreType.DMA((2,))],
    )
    return pl.pallas_call(
        functools.partial(_post_moe_kernel, alpha=alpha, n_tok=n),
        grid_spec=grid_spec,
        out_shape=jax.ShapeDtypeStruct((n, d), F32),
        compiler_params=_cparams(1), name="post_moe",
    )(dest_flat, x2, y_pad, gates_t, ln_g.reshape(1, d), ln_b.reshape(1, d), p3, w_ple_gate,
      w_ple_proj)


def _t5_bucket(dist):
    d = jnp.maximum(dist, 0)
    ratio = jnp.log(jnp.maximum(d, 1).astype(F32) / MAX_EXACT) / math.log(MAX_DISTANCE / MAX_EXACT)
    large = MAX_EXACT + (ratio * (NUM_BUCKETS - MAX_EXACT)).astype(jnp.int32)
    large = jnp.minimum(large, NUM_BUCKETS - 1)
    return jnp.where(d < MAX_EXACT, d, large)


def _swa_bias_table(pos_bias):
    qi = jnp.arange(WINDOW)[:, None]
    kj = jnp.arange(2 * WINDOW)[None, :]
    dist = qi + WINDOW - kj
    valid = (dist >= 0) & (dist < WINDOW)
    bias = pos_bias[_t5_bucket(dist)][..., :A_HEADS].astype(F32).transpose(2, 0, 1)
    return jnp.where(valid[None], bias, NEG)


def _dsa_bias_tables(pos_bias):
    qi = jnp.arange(Q_BLOCK)[:, None]
    kj = jnp.arange(DSA_CK)[None, :]
    tabs = []
    for r in range(4):
        dist = r * Q_BLOCK + qi - kj
        tabs.append(pos_bias[_t5_bucket(dist)].astype(F32).transpose(2, 0, 1))
    return jnp.stack(tabs)


def _pad_cols(w, width=LANE):
    return jnp.pad(w, ((0, 0), (0, width - w.shape[1])))


def _even_mixer(x2, bsz, t, w_in, b_forget, sinks, w_o, swa_bias):
    hd = HEAD_DIM
    grp = A_HEADS // A_KV_HEADS
    pair_order = [g + grp * hh for g in range(grp) for hh in range(A_KV_HEADS)]
    na, nkv, nb = A_HEADS * hd, A_KV_HEADS * hd, B_HEADS * hd
    offs = [0, na, na + nkv, na + 2 * nkv, na + 2 * nkv + nb, na + 2 * nkv + 2 * nb,
            na + 2 * nkv + 3 * nb]
    w_aq = w_in[:, offs[0]:offs[1]].reshape(D_MODEL, A_HEADS, hd)[:, pair_order].reshape(D_MODEL, na)
    weights = [w_aq, w_in[:, offs[1]:offs[2]], w_in[:, offs[2]:offs[3]], w_in[:, offs[3]:offs[4]],
               w_in[:, offs[4]:offs[5]], w_in[:, offs[5]:offs[6]], _pad_cols(w_in[:, offs[6]:])]
    aq, ak, av, bq, bk, bv, bf = _project(x2, weights, [BF16] * 6 + [F32])
    o_a = _swa_attention(aq, ak, av, sinks, swa_bias, bsz, t)

    f_t = bf[:, :B_HEADS].reshape(bsz, t, B_HEADS).transpose(0, 2, 1)
    c = _forget_cumsum(f_t, b_forget)
    c_row = c.reshape(bsz, B_HEADS // 2, 2, t // FOX_TQ, FOX_TQ)
    c_col = c.reshape(bsz, B_HEADS // 2, 2, t).transpose(0, 1, 3, 2)
    o_b = _fox_attention(bq, bk, bv, c_col, c_row, bsz, t)

    w_oa = w_o[:na].reshape(A_HEADS, hd, D_MODEL)[jnp.asarray(pair_order)].reshape(na, D_MODEL)
    return [o_a, o_b], [w_oa, w_o[na:]]


def _odd_mixer(x2, bsz, t, w_in, kv_norm, w_uk, w_uv, w_o, dsa_tb):
    hd = HEAD_DIM
    nq, nl, ni = C_HEADS * hd, C_LATENT, IDX_HEADS * IDX_DIM
    w_ki = w_in[:, nq + nl + ni:nq + nl + ni + IDX_DIM]
    weights = [w_in[:, :nq], w_in[:, nq:nq + nl], w_in[:, nq + nl:nq + nl + ni],
               jnp.concatenate([w_ki, w_ki], axis=1), _pad_cols(w_in[:, nq + nl + ni + IDX_DIM:])]
    cq, ckv, qi, ki2, wi = _project(x2, weights, [BF16, BF16, BF16, BF16, F32],
                                    rms_gain=kv_norm.reshape(1, nl).astype(F32), rms_index=1)
    uk = w_uk.transpose(1, 2, 0)
    uv = w_uv.transpose(1, 0, 2)
    zk = jnp.zeros_like(uk)
    zv = jnp.zeros_like(uv)
    odd = (jnp.arange(C_HEADS) % 2 == 1)[:, None, None]
    wuk_wide = jnp.where(odd, jnp.concatenate([zk, uk], axis=1), jnp.concatenate([uk, zk], axis=1))
    wuv_wide = jnp.where(odd, jnp.concatenate([zv, uv], axis=2), jnp.concatenate([uv, zv], axis=2))
    o_c = _dsa_attention(cq, ckv, qi, ki2, wi, wuk_wide.astype(BF16), wuv_wide.astype(BF16),
                         dsa_tb, bsz, t)
    return [o_c], [w_o]


def _dispatch_plan(top_idx, rank, counts, n_tok):
    counts = counts.reshape(N_EXPERTS)
    padded = (counts + TM_MOE - 1) // TM_MOE * TM_MOE
    pad_ends = jnp.cumsum(padded)
    pad_starts = pad_ends - padded
    is_e = top_idx[..., None] == jnp.arange(N_EXPERTS, dtype=jnp.int32)
    dest = jnp.sum(jnp.where(is_e, pad_starts.astype(jnp.int32), 0), axis=-1) + rank
    n_blocks = (n_tok * TOP_K) // TM_MOE + N_EXPERTS
    block_start = jnp.arange(n_blocks, dtype=jnp.int32) * TM_MOE
    block_e = jnp.minimum(jnp.sum(pad_ends[None, :] <= block_start[:, None], axis=1),
                          N_EXPERTS - 1).astype(jnp.int32)
    n_used = (pad_ends[-1] // TM_MOE).astype(jnp.int32).reshape(1)
    return dest, block_e, n_used, n_blocks


def kernel(x, p, pos_bias, w_in_even, b_forget, sinks, w_o_even, w_in_odd, kv_norm, w_uk, w_uv,
           w_o_odd, ln_g, ln_b, w_router, b_router, w_gate_up, b_gate_up, w_down, b_down,
           w_ple_proj, w_ple_gate):
    bsz, t, d = x.shape
    depth = ln_g.shape[0]
    alpha = (2 * depth) ** 0.25
    n_tok = bsz * t
    x2 = x.reshape(n_tok, d)
    swa_bias = _swa_bias_table(pos_bias)
    dsa_tb = _dsa_bias_tables(pos_bias)
    p3 = p.reshape(depth, n_tok, PLE_DIM)
    tok_ids = jnp.tile(jnp.arange(n_tok, dtype=jnp.int32), (TOP_K,))
    for i in range(depth):
        j = i // 2
        if i % 2 == 0:
            mixes, w_os = _even_mixer(x2, bsz, t, w_in_even[j], b_forget[j], sinks[j], w_o_even[j],
                                      swa_bias)
        else:
            mixes, w_os = _odd_mixer(x2, bsz, t, w_in_odd[j], kv_norm[j], w_uk[j], w_uv[j],
                                     w_o_odd[j], dsa_tb)
        x1, top_idx, gates, rank, counts = _post_attn(x2, mixes, w_os, ln_g[i, 0], ln_b[i, 0],
                                                      w_router[i], b_router[i], alpha)
        dest, block_e, n_used, n_blocks = _dispatch_plan(top_idx, rank, counts, n_tok)
        src_tok = jnp.zeros((n_blocks * TM_MOE,), jnp.int32).at[dest.reshape(-1)].set(tok_ids)
        x_pad = jnp.take(x1.astype(BF16), src_tok, axis=0)
        y_pad = _moe_ffn(x_pad, block_e, n_used, i, w_gate_up, b_gate_up, w_down, b_down)
        x2 = _post_moe(x1, y_pad, dest.reshape(-1), gates.T, ln_g[i, 1], ln_b[i, 1], i, p3,
                       w_ple_gate, w_ple_proj, alpha)
    return x2.reshape(bsz, t, d)
```

```python
import functools
import math

import jax
import jax.numpy as jnp
from jax import lax
from jax.experimental import pallas as pl
from jax.experimental.pallas import tpu as pltpu

D_MODEL = 1024
HEAD_DIM = 64
A_HEADS, A_KV_HEADS, WINDOW = 8, 2, 128
B_HEADS = 8
C_HEADS, C_LATENT = 16, 128
IDX_HEADS, IDX_DIM = 8, 64
TOPK_MAX = 256
NUM_BUCKETS, MAX_EXACT, MAX_DISTANCE = 32, 16, 128
N_EXPERTS, TOP_K, D_FF = 32, 4, 1024
SWIGLU_LIMIT, SWIGLU_ALPHA = 7.0, 1.702
PLE_DIM = 256
Q_BLOCK = 128
LN_EPS = 1e-5
NEG = -1e30

LANE = 128
VMEM_LIMIT = 56 * 1024 * 1024
TM_PROJ = 512
TM_MOE = 256
TM_POST = 256
FOX_TQ = 256
DSA_CK = 256
INT_MIN = -2 ** 31

BF16 = jnp.bfloat16
F32 = jnp.float32


def _cparams(n_axes):
    return pltpu.CompilerParams(dimension_semantics=("arbitrary",) * n_axes,
                                vmem_limit_bytes=VMEM_LIMIT)


def _const_spec(shape):
    nd = len(shape)
    return pl.BlockSpec(shape, lambda *_: (0,) * nd)


def _dot(a, b):
    return jnp.dot(a, b, preferred_element_type=F32)


def _dot_nt(a, b):
    return lax.dot_general(a, b, (((1,), (1,)), ((), ())), preferred_element_type=F32)


def _layer_norm(v, g, b):
    mu = jnp.mean(v, axis=-1, keepdims=True)
    d = v - mu
    var = jnp.mean(d * d, axis=-1, keepdims=True)
    return d * lax.rsqrt(var + LN_EPS) * g + b


def _split3(v):
    hi = v.astype(BF16)
    r1 = v - hi.astype(F32)
    mid = r1.astype(BF16)
    lo = (r1 - mid.astype(F32)).astype(BF16)
    return hi, mid, lo


def _proj_kernel(*refs, n_w, rms_index):
    x_ref = refs[0]
    w_refs = refs[1:1 + n_w]
    g_ref = refs[1 + n_w]
    o_refs = refs[2 + n_w:2 + 2 * n_w]
    wb_refs = refs[2 + 2 * n_w:]

    @pl.when(pl.program_id(0) == 0)
    def _():
        for w_ref, wb_ref in zip(w_refs, wb_refs):
            wb_ref[...] = w_ref[...].astype(BF16)

    xb = x_ref[...].astype(BF16)
    for i in range(n_w):
        h = _dot(xb, wb_refs[i][...])
        if i == rms_index:
            h = h * lax.rsqrt(jnp.mean(h * h, axis=-1, keepdims=True) + LN_EPS) * g_ref[...]
        o_refs[i][...] = h.astype(o_refs[i].dtype)


def _project(x2, weights, out_dtypes, rms_gain=None, rms_index=-1):
    n, d = x2.shape
    n_w = len(weights)
    if rms_gain is None:
        rms_gain = jnp.ones((1, LANE), F32)
    in_specs = [pl.BlockSpec((TM_PROJ, d), lambda i: (i, 0))]
    in_specs += [_const_spec(w.shape) for w in weights]
    in_specs += [_const_spec(rms_gain.shape)]
    out_specs = [pl.BlockSpec((TM_PROJ, w.shape[1]), lambda i: (i, 0)) for w in weights]
    out_shape = [jax.ShapeDtypeStruct((n, w.shape[1]), dt) for w, dt in zip(weights, out_dtypes)]
    scratch = [pltpu.VMEM(w.shape, BF16) for w in weights]
    return pl.pallas_call(
        functools.partial(_proj_kernel, n_w=n_w, rms_index=rms_index),
        grid=(n // TM_PROJ,),
        in_specs=in_specs, out_specs=out_specs, out_shape=out_shape,
        scratch_shapes=scratch, compiler_params=_cparams(1), name="in_proj",
    )(x2, *weights, rms_gain)


def _cumsum_kernel(f_ref, b_ref, o_ref):
    z = f_ref[...] + b_ref[...]
    ls = jnp.minimum(z, 0.0) - jnp.log1p(jnp.exp(-jnp.abs(z)))
    t = ls.shape[1]
    r = lax.broadcasted_iota(jnp.int32, (LANE, LANE), 0)
    c = lax.broadcasted_iota(jnp.int32, (LANE, LANE), 1)
    tri = jnp.where(r <= c, 1.0, 0.0).astype(BF16)
    carry = jnp.zeros((ls.shape[0], 1), F32)
    for blk in range(t // LANE):
        seg = ls[:, blk * LANE:(blk + 1) * LANE]
        hi, mid, lo = _split3(seg)
        pre = _dot(hi, tri) + _dot(mid, tri) + _dot(lo, tri) + carry
        o_ref[:, blk * LANE:(blk + 1) * LANE] = pre
        carry = pre[:, LANE - 1:LANE]


def _forget_cumsum(f_t, b_forget):
    bsz, h, t = f_t.shape
    return pl.pallas_call(
        _cumsum_kernel,
        grid=(bsz,),
        in_specs=[pl.BlockSpec((None, h, t), lambda b: (b, 0, 0)), _const_spec((h, 1))],
        out_specs=pl.BlockSpec((None, h, t), lambda b: (b, 0, 0)),
        out_shape=jax.ShapeDtypeStruct((bsz, h, t), F32),
        compiler_params=_cparams(1), name="forget_cumsum",
    )(f_t, b_forget.reshape(h, 1).astype(F32))


def _fox_kernel(q_ref, k_ref, v_ref, cq_ref, ck_ref, o_ref, s_all, m_s, l_s, acc_s):
    i = pl.program_id(2)
    tq = q_ref.shape[0]
    tk = tq
    q = q_ref[...]
    lane = lax.broadcasted_iota(jnp.int32, (tq, LANE), 1)
    on_or_below_diag = (lax.broadcasted_iota(jnp.int32, (tq, tk), 0)
                        >= lax.broadcasted_iota(jnp.int32, (tq, tk), 1))
    outs = []
    for hh in range(2):
        in_half = (lane >= hh * HEAD_DIM) & (lane < (hh + 1) * HEAD_DIM)
        qh = jnp.where(in_half, q, jnp.zeros_like(q)) * jnp.asarray(HEAD_DIM ** -0.5, BF16)
        cq = cq_ref[:, hh:hh + 1]

        def logits(j, qh=qh, cq=cq, hh=hh):
            start = pl.multiple_of(j * tk, tk)
            return _dot_nt(qh, k_ref[pl.ds(start, tk), :]) + cq - ck_ref[hh, pl.ds(j, 1), :]

        def keep(j, s):
            s_all[j] = s
            m_s[...] = jnp.maximum(m_s[...], jnp.maximum(s[:, :LANE], s[:, LANE:]))

        def logit_body(j, carry):
            keep(j, logits(j))
            return carry

        m_s[...] = jnp.full(m_s.shape, NEG, F32)
        lax.fori_loop(0, i, logit_body, 0)
        keep(i, jnp.where(on_or_below_diag, logits(i), NEG))
        m_row = jnp.broadcast_to(jnp.max(m_s[...], axis=-1, keepdims=True), (tq, LANE))
        l_s[...] = jnp.zeros(l_s.shape, F32)
        acc_s[...] = jnp.zeros(acc_s.shape, F32)

        def prob_body(j, carry, m_row=m_row):
            start = pl.multiple_of(j * tk, tk)
            p_lo = jnp.exp(s_all[j, :, :LANE] - m_row)
            p_hi = jnp.exp(s_all[j, :, LANE:] - m_row)
            l_s[...] = l_s[...] + p_lo + p_hi
            p = jnp.concatenate([p_lo, p_hi], axis=1).astype(BF16)
            acc_s[...] = acc_s[...] + _dot(p, v_ref[pl.ds(start, tk), :])
            return carry

        lax.fori_loop(0, i + 1, prob_body, 0)
        outs.append(acc_s[...] / jnp.sum(l_s[...], axis=-1, keepdims=True))
    o_ref[...] = jnp.where(lane < HEAD_DIM, outs[0], outs[1]).astype(o_ref.dtype)


def _fox_attention(q, k, v, c_col, c_row, bsz, t):
    n, hd = q.shape
    n_pair = hd // LANE
    nq = t // FOX_TQ
    return pl.pallas_call(
        _fox_kernel,
        grid=(bsz, n_pair, nq),
        in_specs=[
            pl.BlockSpec((FOX_TQ, LANE), lambda b, j, i: (b * nq + i, j)),
            pl.BlockSpec((t, LANE), lambda b, j, i: (b, j)),
            pl.BlockSpec((t, LANE), lambda b, j, i: (b, j)),
            pl.BlockSpec((None, None, FOX_TQ, 2), lambda b, j, i: (b, j, i, 0)),
            pl.BlockSpec((None, None, 2, nq, FOX_TQ), lambda b, j, i: (b, j, 0, 0, 0)),
        ],
        out_specs=pl.BlockSpec((FOX_TQ, LANE), lambda b, j, i: (b * nq + i, j)),
        out_shape=jax.ShapeDtypeStruct((n, hd), BF16),
        scratch_shapes=[pltpu.VMEM((nq, FOX_TQ, FOX_TQ), F32), pltpu.VMEM((FOX_TQ, LANE), F32),
                        pltpu.VMEM((FOX_TQ, LANE), F32), pltpu.VMEM((FOX_TQ, LANE), F32)],
        compiler_params=_cparams(3), name="fox_attention",
    )(q, k, v, c_col, c_row)


def _swa_kernel(sink_ref, q_ref, kp_ref, kc_ref, vp_ref, vc_ref, bias_ref, o_ref):
    nblk = pl.program_id(1)
    w = q_ref.shape[0]
    kk = jnp.concatenate([kp_ref[...], kc_ref[...]], axis=0)
    vv = jnp.concatenate([vp_ref[...], vc_ref[...]], axis=0)
    lane = lax.broadcasted_iota(jnp.int32, (w, LANE), 1)
    col = lax.broadcasted_iota(jnp.int32, (w, 2 * w), 1)
    no_prev = (col < w) & (nblk == 0)
    grp = A_HEADS // A_KV_HEADS
    for g in range(grp):
        qg = q_ref[:, g * LANE:(g + 1) * LANE]
        outs = []
        for hh in range(A_KV_HEADS):
            head = g + grp * hh
            in_half = (lane >= hh * HEAD_DIM) & (lane < (hh + 1) * HEAD_DIM)
            qh = jnp.where(in_half, qg, jnp.zeros_like(qg)) * jnp.asarray(HEAD_DIM ** -0.5, BF16)
            s = _dot_nt(qh, kk) + bias_ref[head]
            s = jnp.where(no_prev, NEG, s)
            sink = sink_ref[head]
            m = jnp.maximum(jnp.max(s, axis=-1, keepdims=True), sink)
            e = jnp.exp(s - m)
            denom = jnp.sum(e, axis=-1, keepdims=True) + jnp.exp(sink - m)
            p = e / denom
            outs.append(_dot(p.astype(BF16), vv))
        o_ref[:, g * LANE:(g + 1) * LANE] = jnp.where(lane < HEAD_DIM, outs[0], outs[1]).astype(o_ref.dtype)


def _swa_attention(q, k, v, sinks, bias, bsz, t):
    n = q.shape[0]
    nb = t // WINDOW
    cur = lambda b, i: (b * nb + i, 0)
    prev = lambda b, i: (b * nb + jnp.maximum(i - 1, 0), 0)
    return pl.pallas_call(
        _swa_kernel,
        grid=(bsz, nb),
        in_specs=[
            pl.BlockSpec(memory_space=pltpu.SMEM),
            pl.BlockSpec((WINDOW, A_HEADS * HEAD_DIM), cur),
            pl.BlockSpec((WINDOW, LANE), prev), pl.BlockSpec((WINDOW, LANE), cur),
            pl.BlockSpec((WINDOW, LANE), prev), pl.BlockSpec((WINDOW, LANE), cur),
            _const_spec(bias.shape),
        ],
        out_specs=pl.BlockSpec((WINDOW, A_HEADS * HEAD_DIM), cur),
        out_shape=jax.ShapeDtypeStruct((n, A_HEADS * HEAD_DIM), BF16),
        compiler_params=_cparams(2), name="swa_attention",
    )(sinks.astype(F32), q, k, k, v, v, bias)


def _float_order_key(s):
    bits = pltpu.bitcast(s, jnp.int32)
    return bits ^ ((bits >> 31) & jnp.int32(0x7FFFFFFF))


def _dsa_kernel(q_ref, ckv_ref, qi_ref, ki_ref, wi_ref, wuk_ref, wuv_ref, tb_ref, o_ref,
                key_s, sel_s, qlat_s, qis_s, s_all, p_buf, acc_s, m_s, l_s):
    n = pl.program_id(1)
    qb = Q_BLOCK
    ck = DSA_CK
    n_sel = TOPK_MAX
    n_chunks = n // (ck // qb) + 1
    lane = lax.broadcasted_iota(jnp.int32, (qb, LANE), 1)
    qpos = n * qb + lax.broadcasted_iota(jnp.int32, (qb, ck), 0)
    kcol = lax.broadcasted_iota(jnp.int32, (qb, ck), 1)

    for h in range(IDX_HEADS):
        pair = qi_ref[:, (h // 2) * LANE:(h // 2 + 1) * LANE]
        in_half = (lane >= (h % 2) * IDX_DIM) & (lane < (h % 2 + 1) * IDX_DIM)
        qis_s[h * qb:(h + 1) * qb, :] = jnp.where(in_half, pair, jnp.zeros_like(pair))

    for h in range(C_HEADS):
        pair = q_ref[:, (h // 2) * LANE:(h // 2 + 1) * LANE]
        ql = _dot(pair, wuk_ref[h]) * (HEAD_DIM ** -0.5)
        qlat_s[h * qb:(h + 1) * qb, :] = ql.astype(BF16)

    wi = wi_ref[...] * (IDX_HEADS ** -0.5)

    def score_body(c, _):
        start = pl.multiple_of(c * ck, ck)
        kc = ki_ref[pl.ds(start, ck), :]
        raw = _dot_nt(qis_s[...], kc)
        score = jnp.zeros((qb, ck), F32)
        for h in range(IDX_HEADS):
            idx_s = jnp.maximum(raw[h * qb:(h + 1) * qb, :] * (IDX_DIM ** -0.5), 0.0)
            score = score + idx_s * wi[:, h:h + 1]
        causal = (start + kcol) <= qpos
        key_s[c] = _float_order_key(jnp.where(causal, score, NEG))
        sel_s[c] = jnp.where(causal, 0.0, NEG)
        return 0

    lax.fori_loop(0, n_chunks, score_body, 0)

    @pl.when((n + 1) * qb > n_sel)
    def _():
        def count_where(pred_fn):
            def body(c, acc):
                hit = jnp.where(pred_fn(key_s[c]), 1.0, 0.0)
                return acc + hit[:, :LANE] + hit[:, LANE:]
            acc = lax.fori_loop(0, n_chunks, body, jnp.zeros((qb, LANE), F32))
            return jnp.sum(acc, axis=-1, keepdims=True)

        cnt0 = count_where(lambda kv: kv >= 0)
        prefix0 = jnp.where(cnt0 >= n_sel, jnp.int32(0), jnp.int32(INT_MIN))

        def bit_body(t, prefix):
            cand = prefix | jnp.left_shift(jnp.int32(1), 30 - t)
            cnt = count_where(lambda kv: kv >= cand)
            return jnp.where(cnt >= n_sel, cand, prefix)

        thr = lax.fori_loop(0, 31, bit_body, prefix0)
        need = n_sel - count_where(lambda kv: kv > thr)
        r = lax.broadcasted_iota(jnp.int32, (ck, ck), 0)
        cc = lax.broadcasted_iota(jnp.int32, (ck, ck), 1)
        before = jnp.where(r < cc, 1.0, 0.0).astype(BF16)

        def tie_body(c, seen):
            kv = key_s[c]
            tie = kv == thr
            tie_f = jnp.where(tie, 1.0, 0.0)
            earlier = _dot(tie_f.astype(BF16), before) + seen
            sel = (kv > thr) | (tie & (earlier < need))
            sel_s[c] = jnp.where(sel, sel_s[c], NEG)
            return seen + jnp.sum(tie_f, axis=-1, keepdims=True)

        lax.fori_loop(0, n_chunks, tie_body, jnp.zeros((qb, 1), F32))

    m_s[...] = jnp.full(m_s.shape, NEG, F32)

    def logit_body(c, _):
        start = pl.multiple_of(c * ck, ck)
        raw = _dot_nt(qlat_s[...], ckv_ref[pl.ds(start, ck), :])
        rel_blk = jnp.minimum(n - c * (ck // qb), 3)
        mask_add = sel_s[c]
        for h in range(C_HEADS):
            rows = slice(h * qb, (h + 1) * qb)
            s = raw[rows, :] + tb_ref[rel_blk, h] + mask_add
            s_all[c, rows, :] = s
            m_s[rows, :] = jnp.maximum(m_s[rows, :], jnp.maximum(s[:, :LANE], s[:, LANE:]))
        return 0

    lax.fori_loop(0, n_chunks, logit_body, 0)

    for h in range(C_HEADS):
        rows = slice(h * qb, (h + 1) * qb)
        m_s[rows, :] = jnp.broadcast_to(jnp.max(m_s[rows, :], axis=-1, keepdims=True), (qb, LANE))
    l_s[...] = jnp.zeros(l_s.shape, F32)
    acc_s[...] = jnp.zeros(acc_s.shape, F32)

    def prob_body(c, _):
        start = pl.multiple_of(c * ck, ck)
        for h in range(C_HEADS):
            rows = slice(h * qb, (h + 1) * qb)
            m_row = m_s[rows, :]
            p_lo = jnp.exp(s_all[c, rows, :LANE] - m_row)
            p_hi = jnp.exp(s_all[c, rows, LANE:] - m_row)
            l_s[rows, :] = l_s[rows, :] + p_lo + p_hi
            p_buf[rows, :LANE] = p_lo.astype(BF16)
            p_buf[rows, LANE:] = p_hi.astype(BF16)
        acc_s[...] = acc_s[...] + _dot(p_buf[...], ckv_ref[pl.ds(start, ck), :])
        return 0

    lax.fori_loop(0, n_chunks, prob_body, 0)

    for g in range(C_HEADS // 2):
        out = jnp.zeros((qb, LANE), F32)
        for hh in range(2):
            h = 2 * g + hh
            rows = slice(h * qb, (h + 1) * qb)
            denom = jnp.sum(l_s[rows, :], axis=-1, keepdims=True)
            o_lat = (acc_s[rows, :] / denom).astype(BF16)
            out = out + _dot(o_lat, wuv_ref[h])
        o_ref[:, g * LANE:(g + 1) * LANE] = out.astype(o_ref.dtype)


def _dsa_attention(q, ckv, qi, ki2, wi, wuk_wide, wuv_wide, tb, bsz, t):
    n = q.shape[0]
    nq = t // Q_BLOCK
    nck = t // DSA_CK
    rows = C_HEADS * Q_BLOCK
    blk = lambda b, i: (b * nq + i, 0)
    per_b = lambda b, i: (b, 0)
    return pl.pallas_call(
        _dsa_kernel,
        grid=(bsz, nq),
        in_specs=[
            pl.BlockSpec((Q_BLOCK, C_HEADS * HEAD_DIM), blk),
            pl.BlockSpec((t, C_LATENT), per_b),
            pl.BlockSpec((Q_BLOCK, IDX_HEADS * IDX_DIM), blk),
            pl.BlockSpec((t, LANE), per_b),
            pl.BlockSpec((Q_BLOCK, LANE), blk),
            _const_spec(wuk_wide.shape), _const_spec(wuv_wide.shape), _const_spec(tb.shape),
        ],
        out_specs=pl.BlockSpec((Q_BLOCK, C_HEADS * HEAD_DIM), blk),
        out_shape=jax.ShapeDtypeStruct((n, C_HEADS * HEAD_DIM), BF16),
        scratch_shapes=[
            pltpu.VMEM((nck, Q_BLOCK, DSA_CK), jnp.int32),
            pltpu.VMEM((nck, Q_BLOCK, DSA_CK), F32),
            pltpu.VMEM((rows, C_LATENT), BF16),
            pltpu.VMEM((IDX_HEADS * Q_BLOCK, LANE), BF16),
            pltpu.VMEM((nck, rows, DSA_CK), F32),
            pltpu.VMEM((rows, DSA_CK), BF16),
            pltpu.VMEM((rows, C_LATENT), F32),
            pltpu.VMEM((rows, LANE), F32),
            pltpu.VMEM((rows, LANE), F32),
        ],
        compiler_params=_cparams(2), name="dsa_attention",
    )(q, ckv, qi, ki2, wi, wuk_wide, wuv_wide, tb)


def _post_attn_kernel(*refs, n_mix, alpha):
    x_ref = refs[0]
    mix_refs = refs[1:1 + n_mix]
    w_refs = refs[1 + n_mix:1 + 2 * n_mix]
    g_ref, b_ref, wr_ref, br_ref = refs[1 + 2 * n_mix:5 + 2 * n_mix]
    xo_ref, idx_ref, gate_ref, rank_ref, cnt_ref = refs[5 + 2 * n_mix:10 + 2 * n_mix]
    scr = refs[10 + 2 * n_mix:]
    wb_refs = scr[:n_mix]
    wr3_ref, before_ref, carry_ref = scr[n_mix:]
    tm = x_ref.shape[0]

    @pl.when(pl.program_id(0) == 0)
    def _():
        for w_ref, wb_ref in zip(w_refs, wb_refs):
            wb_ref[...] = w_ref[...].astype(BF16)
        hi, mid, lo = _split3(wr_ref[...])
        wr3_ref[0] = hi
        wr3_ref[1] = mid
        wr3_ref[2] = lo
        r = lax.broadcasted_iota(jnp.int32, (tm, tm), 0)
        c = lax.broadcasted_iota(jnp.int32, (tm, tm), 1)
        before_ref[...] = jnp.where(r < c, 1.0, 0.0).astype(BF16)
        carry_ref[...] = jnp.zeros(carry_ref.shape, F32)

    mix = _dot(mix_refs[0][...], wb_refs[0][...])
    for i in range(1, n_mix):
        mix = mix + _dot(mix_refs[i][...], wb_refs[i][...])
    xn = _layer_norm(alpha * x_ref[...] + mix, g_ref[...], b_ref[...])
    xo_ref[...] = xn

    xh, xm, xl = _split3(xn)
    logits = (_dot_nt(wr3_ref[0], xh) + _dot_nt(wr3_ref[0], xm) + _dot_nt(wr3_ref[1], xh)
              + _dot_nt(wr3_ref[0], xl) + _dot_nt(wr3_ref[1], xm) + _dot_nt(wr3_ref[2], xh)
              + br_ref[...])
    eidx = lax.broadcasted_iota(jnp.int32, logits.shape, 0)
    cur = logits
    vals, idxs = [], []
    for _ in range(TOP_K):
        mx = jnp.max(cur, axis=0, keepdims=True)
        first = jnp.min(jnp.where(cur == mx, eidx, N_EXPERTS), axis=0, keepdims=True)
        vals.append(mx)
        idxs.append(first)
        cur = jnp.where(eidx == first, -jnp.inf, cur)
    es = [jnp.exp(v - vals[0]) for v in vals]
    tot = es[0] + es[1] + es[2] + es[3]
    member = jnp.zeros(logits.shape, F32)
    for k in range(TOP_K):
        idx_ref[k:k + 1, :] = idxs[k]
        gate_ref[k:k + 1, :] = es[k] / tot
        member = member + jnp.where(eidx == idxs[k], 1.0, 0.0)
    earlier = _dot(member.astype(BF16), before_ref[...]) + carry_ref[...]
    for k in range(TOP_K):
        rk = jnp.sum(jnp.where(eidx == idxs[k], earlier, 0.0), axis=0, keepdims=True)
        rank_ref[k:k + 1, :] = rk.astype(jnp.int32)
    carry_ref[...] = carry_ref[...] + jnp.sum(member, axis=1, keepdims=True)
    cnt_ref[...] = carry_ref[...].astype(jnp.int32)


def _post_attn(x2, mixes, w_os, ln_g, ln_b, w_router, b_router, alpha):
    n, d = x2.shape
    n_mix = len(mixes)
    tm = TM_PROJ
    row = lambda i: (i, 0)
    colblk = lambda i: (0, i)
    in_specs = [pl.BlockSpec((tm, d), row)]
    in_specs += [pl.BlockSpec((tm, m.shape[1]), row) for m in mixes]
    in_specs += [_const_spec(w.shape) for w in w_os]
    in_specs += [_const_spec((1, d)), _const_spec((1, d)), _const_spec((N_EXPERTS, d)),
                 _const_spec((N_EXPERTS, 1))]
    out_specs = [pl.BlockSpec((tm, d), row), pl.BlockSpec((TOP_K, tm), colblk),
                 pl.BlockSpec((TOP_K, tm), colblk), pl.BlockSpec((TOP_K, tm), colblk),
                 _const_spec((N_EXPERTS, 1))]
    out_shape = [jax.ShapeDtypeStruct((n, d), F32), jax.ShapeDtypeStruct((TOP_K, n), jnp.int32),
                 jax.ShapeDtypeStruct((TOP_K, n), F32), jax.ShapeDtypeStruct((TOP_K, n), jnp.int32),
                 jax.ShapeDtypeStruct((N_EXPERTS, 1), jnp.int32)]
    scratch = [pltpu.VMEM(w.shape, BF16) for w in w_os]
    scratch += [pltpu.VMEM((3, N_EXPERTS, d), BF16), pltpu.VMEM((tm, tm), BF16),
                pltpu.VMEM((N_EXPERTS, 1), F32)]
    return pl.pallas_call(
        functools.partial(_post_attn_kernel, n_mix=n_mix, alpha=alpha),
        grid=(n // tm,),
        in_specs=in_specs, out_specs=out_specs, out_shape=out_shape, scratch_shapes=scratch,
        compiler_params=_cparams(1), name="post_attn",
    )(x2, *mixes, *w_os, ln_g.reshape(1, d), ln_b.reshape(1, d), w_router.T,
      b_router.reshape(N_EXPERTS, 1))


def _moe_kernel(be_ref, nu_ref, src_ref, x_hbm, wgu_ref, bgu_ref, wd_ref, bd_ref, y_ref,
                wgu_s, wd_s, xbuf, sem):
    i = pl.program_id(0)
    n_steps = pl.num_programs(0)
    tm = y_ref.shape[0]
    prev = be_ref[jnp.maximum(i - 1, 0)]
    live = i < nu_ref[0]

    def fetch_block(block, slot):
        def body(r, carry):
            tok = src_ref[block * tm + r]
            pltpu.make_async_copy(x_hbm.at[pl.ds(tok, 1), :], xbuf.at[slot, pl.ds(r, 1), :],
                                  sem.at[slot]).start()
            return carry
        lax.fori_loop(0, tm, body, 0, unroll=8)

    @pl.when(i == 0)
    def _():
        fetch_block(0, 0)

    @pl.when(i + 1 < n_steps)
    def _():
        fetch_block(i + 1, (i + 1) % 2)

    slot = i % 2
    pltpu.make_async_copy(xbuf.at[slot], xbuf.at[slot], sem.at[slot]).wait()

    @pl.when(live & ((i == 0) | (be_ref[i] != prev)))
    def _():
        wgu_s[...] = wgu_ref[...].astype(BF16)
        wd_s[...] = wd_ref[...].astype(BF16)

    @pl.when(live)
    def _():
        xb = xbuf[slot].astype(BF16)
        ch = 512
        acc = jnp.zeros(y_ref.shape, F32)
        for j in range(D_FF // ch):
            hg = _dot(xb, wgu_s[:, j * ch:(j + 1) * ch]) + bgu_ref[:, j * ch:(j + 1) * ch]
            hl = (_dot(xb, wgu_s[:, D_FF + j * ch:D_FF + (j + 1) * ch])
                  + bgu_ref[:, D_FF + j * ch:D_FF + (j + 1) * ch])
            glu = jnp.minimum(hg, SWIGLU_LIMIT)
            lin = jnp.clip(hl, -SWIGLU_LIMIT, SWIGLU_LIMIT)
            act = glu * jax.nn.sigmoid(SWIGLU_ALPHA * glu) * (lin + 1.0)
            acc = acc + _dot(act.astype(BF16), wd_s[j * ch:(j + 1) * ch, :])
        y_ref[...] = (acc + bd_ref[...]).astype(y_ref.dtype)

    @pl.when(jnp.logical_not(live))
    def _():
        y_ref[...] = jnp.zeros(y_ref.shape, y_ref.dtype)


def _moe_ffn(x2, src_tok, block_e, n_used, layer, w_gate_up, b_gate_up, w_down, b_down):
    d = x2.shape[1]
    rows = src_tok.shape[0]
    depth = w_gate_up.shape[0]
    n_blocks = rows // TM_MOE
    expert = lambda i, be, nu, src: (layer, be[i], 0, 0)
    grid_spec = pltpu.PrefetchScalarGridSpec(
        num_scalar_prefetch=3,
        grid=(n_blocks,),
        in_specs=[
            pl.BlockSpec(memory_space=pl.ANY),
            pl.BlockSpec((None, None, d, 2 * D_FF), expert),
            pl.BlockSpec((None, None, 1, 2 * D_FF), expert),
            pl.BlockSpec((None, None, D_FF, d), expert),
            pl.BlockSpec((None, None, 1, d), expert),
        ],
        out_specs=pl.BlockSpec((TM_MOE, d), lambda i, be, nu, src: (i, 0)),
        scratch_shapes=[pltpu.VMEM((d, 2 * D_FF), BF16), pltpu.VMEM((D_FF, d), BF16),
                        pltpu.VMEM((2, TM_MOE, d), F32), pltpu.SemaphoreType.DMA((2,))],
    )
    return pl.pallas_call(
        _moe_kernel, grid_spec=grid_spec,
        out_shape=jax.ShapeDtypeStruct((rows, d), F32),
        compiler_params=_cparams(1), name="moe_ffn",
    )(block_e, n_used, src_tok, x2, w_gate_up, b_gate_up.reshape(depth, N_EXPERTS, 1, 2 * D_FF),
      w_down, b_down.reshape(depth, N_EXPERTS, 1, d))


def _post_moe_kernel(dest_ref, x_ref, y_hbm, gt_ref, g_ref, b_ref, p_ref, wg_ref, wp_ref, o_ref,
                     wg_s, wp_s, ybuf, sem, *, alpha, n_tok):
    i = pl.program_id(0)
    n_steps = pl.num_programs(0)
    tm, d = x_ref.shape

    def fetch_tile(tile, slot):
        def body(j, carry):
            for k in range(TOP_K):
                r = dest_ref[k * n_tok + tile * tm + j]
                pltpu.make_async_copy(y_hbm.at[pl.ds(r, 1), :], ybuf.at[slot, k, pl.ds(j, 1), :],
                                      sem.at[slot]).start()
            return carry
        lax.fori_loop(0, tm, body, 0, unroll=8)

    @pl.when(i == 0)
    def _():
        wg_s[...] = wg_ref[...].astype(BF16)
        wp_s[...] = wp_ref[...].astype(BF16)
        fetch_tile(0, 0)

    @pl.when(i + 1 < n_steps)
    def _():
        fetch_tile(i + 1, (i + 1) % 2)

    slot = i % 2
    pltpu.make_async_copy(ybuf.at[slot], ybuf.at[slot], sem.at[slot]).wait()
    gates = gt_ref[...]
    ffn = gates[:, 0:1] * ybuf[slot, 0]
    for k in range(1, TOP_K):
        ffn = ffn + gates[:, k:k + 1] * ybuf[slot, k]
    xn = _layer_norm(alpha * x_ref[...] + ffn, g_ref[...], b_ref[...])
    gate = jax.nn.sigmoid(_dot(xn.astype(BF16), wg_s[...]))
    emb = _dot(p_ref[...].astype(BF16), wp_s[...])
    o_ref[...] = xn + gate * emb


def _post_moe(x2, y_pad, dest_flat, gates_t, ln_g, ln_b, layer, p3, w_ple_gate, w_ple_proj, alpha):
    n, d = x2.shape
    tm = TM_POST
    row = lambda i, dest: (i, 0)
    const2 = lambda i, dest: (0, 0)
    grid_spec = pltpu.PrefetchScalarGridSpec(
        num_scalar_prefetch=1,
        grid=(n // tm,),
        in_specs=[pl.BlockSpec((tm, d), row), pl.BlockSpec(memory_space=pl.ANY),
                  pl.BlockSpec((tm, TOP_K), row), pl.BlockSpec((1, d), const2),
                  pl.BlockSpec((1, d), const2),
                  pl.BlockSpec((None, tm, PLE_DIM), lambda i, dest: (layer, i, 0)),
                  pl.BlockSpec((None, d, d), lambda i, dest: (layer, 0, 0)),
                  pl.BlockSpec((None, PLE_DIM, d), lambda i, dest: (layer, 0, 0))],
        out_specs=pl.BlockSpec((tm, d), row),
        scratch_shapes=[pltpu.VMEM((d, d), BF16), pltpu.VMEM((PLE_DIM, d), BF16),
                        pltpu.VMEM((2, TOP_K, tm, d), F32), pltpu.SemaphoreType.DMA((2,))],
    )
    return pl.pallas_call(
        functools.partial(_post_moe_kernel, alpha=alpha, n_tok=n),
        grid_spec=grid_spec,
        out_shape=jax.ShapeDtypeStruct((n, d), F32),
        compiler_params=_cparams(1), name="post_moe",
    )(dest_flat, x2, y_pad, gates_t, ln_g.reshape(1, d), ln_b.reshape(1, d), p3, w_ple_gate,
      w_ple_proj)


def _t5_bucket(dist):
    d = jnp.maximum(dist, 0)
    ratio = jnp.log(jnp.maximum(d, 1).astype(F32) / MAX_EXACT) / math.log(MAX_DISTANCE / MAX_EXACT)
    large = MAX_EXACT + (ratio * (NUM_BUCKETS - MAX_EXACT)).astype(jnp.int32)
    large = jnp.minimum(large, NUM_BUCKETS - 1)
    return jnp.where(d < MAX_EXACT, d, large)


def _swa_bias_table(pos_bias):
    qi = jnp.arange(WINDOW)[:, None]
    kj = jnp.arange(2 * WINDOW)[None, :]
    dist = qi + WINDOW - kj
    valid = (dist >= 0) & (dist < WINDOW)
    bias = pos_bias[_t5_bucket(dist)][..., :A_HEADS].astype(F32).transpose(2, 0, 1)
    return jnp.where(valid[None], bias, NEG)


def _dsa_bias_tables(pos_bias):
    qi = jnp.arange(Q_BLOCK)[:, None]
    kj = jnp.arange(DSA_CK)[None, :]
    tabs = []
    for r in range(4):
        dist = r * Q_BLOCK + qi - kj
        tabs.append(pos_bias[_t5_bucket(dist)].astype(F32).transpose(2, 0, 1))
    return jnp.stack(tabs)


def _pad_cols(w, width=LANE):
    return jnp.pad(w, ((0, 0), (0, width - w.shape[1])))


def _even_mixer(x2, bsz, t, w_in, b_forget, sinks, w_o, swa_bias):
    hd = HEAD_DIM
    grp = A_HEADS // A_KV_HEADS
    pair_order = [g + grp * hh for g in range(grp) for hh in range(A_KV_HEADS)]
    na, nkv, nb = A_HEADS * hd, A_KV_HEADS * hd, B_HEADS * hd
    offs = [0, na, na + nkv, na + 2 * nkv, na + 2 * nkv + nb, na + 2 * nkv + 2 * nb,
            na + 2 * nkv + 3 * nb]
    w_aq = w_in[:, offs[0]:offs[1]].reshape(D_MODEL, A_HEADS, hd)[:, pair_order].reshape(D_MODEL, na)
    weights = [w_aq, w_in[:, offs[1]:offs[2]], w_in[:, offs[2]:offs[3]], w_in[:, offs[3]:offs[4]],
               w_in[:, offs[4]:offs[5]], w_in[:, offs[5]:offs[6]], _pad_cols(w_in[:, offs[6]:])]
    aq, ak, av, bq, bk, bv, bf = _project(x2, weights, [BF16] * 6 + [F32])
    o_a = _swa_attention(aq, ak, av, sinks, swa_bias, bsz, t)

    f_t = bf[:, :B_HEADS].reshape(bsz, t, B_HEADS).transpose(0, 2, 1)
    c = _forget_cumsum(f_t, b_forget)
    c_row = c.reshape(bsz, B_HEADS // 2, 2, t // FOX_TQ, FOX_TQ)
    c_col = c.reshape(bsz, B_HEADS // 2, 2, t).transpose(0, 1, 3, 2)
    o_b = _fox_attention(bq, bk, bv, c_col, c_row, bsz, t)

    w_oa = w_o[:na].reshape(A_HEADS, hd, D_MODEL)[jnp.asarray(pair_order)].reshape(na, D_MODEL)
    return [o_a, o_b], [w_oa, w_o[na:]]


def _odd_mixer(x2, bsz, t, w_in, kv_norm, w_uk, w_uv, w_o, dsa_tb):
    hd = HEAD_DIM
    nq, nl, ni = C_HEADS * hd, C_LATENT, IDX_HEADS * IDX_DIM
    w_ki = w_in[:, nq + nl + ni:nq + nl + ni + IDX_DIM]
    weights = [w_in[:, :nq], w_in[:, nq:nq + nl], w_in[:, nq + nl:nq + nl + ni],
               jnp.concatenate([w_ki, w_ki], axis=1), _pad_cols(w_in[:, nq + nl + ni + IDX_DIM:])]
    cq, ckv, qi, ki2, wi = _project(x2, weights, [BF16, BF16, BF16, BF16, F32],
                                    rms_gain=kv_norm.reshape(1, nl).astype(F32), rms_index=1)
    uk = w_uk.transpose(1, 2, 0)
    uv = w_uv.transpose(1, 0, 2)
    zk = jnp.zeros_like(uk)
    zv = jnp.zeros_like(uv)
    odd = (jnp.arange(C_HEADS) % 2 == 1)[:, None, None]
    wuk_wide = jnp.where(odd, jnp.concatenate([zk, uk], axis=1), jnp.concatenate([uk, zk], axis=1))
    wuv_wide = jnp.where(odd, jnp.concatenate([zv, uv], axis=2), jnp.concatenate([uv, zv], axis=2))
    o_c = _dsa_attention(cq, ckv, qi, ki2, wi, wuk_wide.astype(BF16), wuv_wide.astype(BF16),
                         dsa_tb, bsz, t)
    return [o_c], [w_o]


def _dispatch_plan(top_idx, rank, counts, n_tok):
    counts = counts.reshape(N_EXPERTS)
    padded = (counts + TM_MOE - 1) // TM_MOE * TM_MOE
    pad_ends = jnp.cumsum(padded)
    pad_starts = pad_ends - padded
    is_e = top_idx[..., None] == jnp.arange(N_EXPERTS, dtype=jnp.int32)
    dest = jnp.sum(jnp.where(is_e, pad_starts.astype(jnp.int32), 0), axis=-1) + rank
    n_blocks = (n_tok * TOP_K) // TM_MOE + N_EXPERTS
    block_start = jnp.arange(n_blocks, dtype=jnp.int32) * TM_MOE
    block_e = jnp.minimum(jnp.sum(pad_ends[None, :] <= block_start[:, None], axis=1),
                          N_EXPERTS - 1).astype(jnp.int32)
    n_used = (pad_ends[-1] // TM_MOE).astype(jnp.int32).reshape(1)
    return dest, block_e, n_used, n_blocks


def kernel(x, p, pos_bias, w_in_even, b_forget, sinks, w_o_even, w_in_odd, kv_norm, w_uk, w_uv,
           w_o_odd, ln_g, ln_b, w_router, b_router, w_gate_up, b_gate_up, w_down, b_down,
           w_ple_proj, w_ple_gate):
    bsz, t, d = x.shape
    depth = ln_g.shape[0]
    alpha = (2 * depth) ** 0.25
    n_tok = bsz * t
    x2 = x.reshape(n_tok, d)
    swa_bias = _swa_bias_table(pos_bias)
    dsa_tb = _dsa_bias_tables(pos_bias)
    p3 = p.reshape(depth, n_tok, PLE_DIM)
    tok_ids = jnp.tile(jnp.arange(n_tok, dtype=jnp.int32), (TOP_K,))
    for i in range(depth):
        j = i // 2
        if i % 2 == 0:
            mixes, w_os = _even_mixer(x2, bsz, t, w_in_even[j], b_forget[j], sinks[j], w_o_even[j],
                                      swa_bias)
        else:
            mixes, w_os = _odd_mixer(x2, bsz, t, w_in_odd[j], kv_norm[j], w_uk[j], w_uv[j],
                                     w_o_odd[j], dsa_tb)
        x1, top_idx, gates, rank, counts = _post_attn(x2, mixes, w_os, ln_g[i, 0], ln_b[i, 0],
                                                      w_router[i], b_router[i], alpha)
        dest, block_e, n_used, n_blocks = _dispatch_plan(top_idx, rank, counts, n_tok)
        src_tok = jnp.zeros((n_blocks * TM_MOE,), jnp.int32).at[dest.reshape(-1)].set(tok_ids)
        y_pad = _moe_ffn(x1, src_tok, block_e, n_used, i, w_gate_up, b_gate_up, w_down, b_down)
        x2 = _post_moe(x1, y_pad, dest.reshape(-1), gates.T, ln_g[i, 1], ln_b[i, 1], i, p3,
                       w_ple_gate, w_ple_proj, alpha)
    return x2.reshape(bsz, t, d)
```

```python
import functools
import math

import jax
import jax.numpy as jnp
from jax import lax
from jax.experimental import pallas as pl
from jax.experimental.pallas import tpu as pltpu

D_MODEL = 1024
HEAD_DIM = 64
A_HEADS, A_KV_HEADS, WINDOW = 8, 2, 128
B_HEADS = 8
C_HEADS, C_LATENT = 16, 128
IDX_HEADS, IDX_DIM = 8, 64
TOPK_MAX = 256
NUM_BUCKETS, MAX_EXACT, MAX_DISTANCE = 32, 16, 128
N_EXPERTS, TOP_K, D_FF = 32, 4, 1024
SWIGLU_LIMIT, SWIGLU_ALPHA = 7.0, 1.702
PLE_DIM = 256
Q_BLOCK = 128
LN_EPS = 1e-5
NEG = -1e30

LANE = 128
VMEM_LIMIT = 56 * 1024 * 1024
TM_PROJ = 512
TM_MOE = 256
TM_POST = 256
FOX_TQ = 512
FOX_TK = 2 * LANE
DSA_CK = 256
INT_MIN = -2 ** 31

BF16 = jnp.bfloat16
F32 = jnp.float32


def _cparams(n_axes):
    return pltpu.CompilerParams(dimension_semantics=("arbitrary",) * n_axes,
                                vmem_limit_bytes=VMEM_LIMIT)


def _const_spec(shape):
    nd = len(shape)
    return pl.BlockSpec(shape, lambda *_: (0,) * nd)


def _dot(a, b):
    return jnp.dot(a, b, preferred_element_type=F32)


def _dot_nt(a, b):
    return lax.dot_general(a, b, (((1,), (1,)), ((), ())), preferred_element_type=F32)


def _layer_norm(v, g, b):
    mu = jnp.mean(v, axis=-1, keepdims=True)
    d = v - mu
    var = jnp.mean(d * d, axis=-1, keepdims=True)
    return d * lax.rsqrt(var + LN_EPS) * g + b


def _split3(v):
    hi = v.astype(BF16)
    r1 = v - hi.astype(F32)
    mid = r1.astype(BF16)
    lo = (r1 - mid.astype(F32)).astype(BF16)
    return hi, mid, lo


def _proj_kernel(*refs, n_w, rms_index):
    x_ref = refs[0]
    w_refs = refs[1:1 + n_w]
    g_ref = refs[1 + n_w]
    o_refs = refs[2 + n_w:2 + 2 * n_w]
    wb_refs = refs[2 + 2 * n_w:]

    @pl.when(pl.program_id(0) == 0)
    def _():
        for w_ref, wb_ref in zip(w_refs, wb_refs):
            wb_ref[...] = w_ref[...].astype(BF16)

    xb = x_ref[...].astype(BF16)
    for i in range(n_w):
        h = _dot(xb, wb_refs[i][...])
        if i == rms_index:
            h = h * lax.rsqrt(jnp.mean(h * h, axis=-1, keepdims=True) + LN_EPS) * g_ref[...]
        o_refs[i][...] = h.astype(o_refs[i].dtype)


def _project(x2, weights, out_dtypes, rms_gain=None, rms_index=-1):
    n, d = x2.shape
    n_w = len(weights)
    if rms_gain is None:
        rms_gain = jnp.ones((1, LANE), F32)
    in_specs = [pl.BlockSpec((TM_PROJ, d), lambda i: (i, 0))]
    in_specs += [_const_spec(w.shape) for w in weights]
    in_specs += [_const_spec(rms_gain.shape)]
    out_specs = [pl.BlockSpec((TM_PROJ, w.shape[1]), lambda i: (i, 0)) for w in weights]
    out_shape = [jax.ShapeDtypeStruct((n, w.shape[1]), dt) for w, dt in zip(weights, out_dtypes)]
    scratch = [pltpu.VMEM(w.shape, BF16) for w in weights]
    return pl.pallas_call(
        functools.partial(_proj_kernel, n_w=n_w, rms_index=rms_index),
        grid=(n // TM_PROJ,),
        in_specs=in_specs, out_specs=out_specs, out_shape=out_shape,
        scratch_shapes=scratch, compiler_params=_cparams(1), name="in_proj",
    )(x2, *weights, rms_gain)


def _cumsum_kernel(f_ref, b_ref, o_ref):
    z = f_ref[...] + b_ref[...]
    ls = jnp.minimum(z, 0.0) - jnp.log1p(jnp.exp(-jnp.abs(z)))
    t = ls.shape[1]
    r = lax.broadcasted_iota(jnp.int32, (LANE, LANE), 0)
    c = lax.broadcasted_iota(jnp.int32, (LANE, LANE), 1)
    tri = jnp.where(r <= c, 1.0, 0.0).astype(BF16)
    carry = jnp.zeros((ls.shape[0], 1), F32)
    for blk in range(t // LANE):
        seg = ls[:, blk * LANE:(blk + 1) * LANE]
        hi, mid, lo = _split3(seg)
        pre = _dot(hi, tri) + _dot(mid, tri) + _dot(lo, tri) + carry
        o_ref[:, blk * LANE:(blk + 1) * LANE] = pre
        carry = pre[:, LANE - 1:LANE]


def _forget_cumsum(f_t, b_forget):
    bsz, h, t = f_t.shape
    return pl.pallas_call(
        _cumsum_kernel,
        grid=(bsz,),
        in_specs=[pl.BlockSpec((None, h, t), lambda b: (b, 0, 0)), _const_spec((h, 1))],
        out_specs=pl.BlockSpec((None, h, t), lambda b: (b, 0, 0)),
        out_shape=jax.ShapeDtypeStruct((bsz, h, t), F32),
        compiler_params=_cparams(1), name="forget_cumsum",
    )(f_t, b_forget.reshape(h, 1).astype(F32))


def _fox_kernel(q_ref, k_ref, v_ref, cq_ref, ck_ref, o_ref, s_all, m_s, l_s, acc_s):
    i = pl.program_id(2)
    tq = q_ref.shape[0]
    tk = FOX_TK
    per_q = tq // tk
    q = q_ref[...]
    lane = lax.broadcasted_iota(jnp.int32, (tq, LANE), 1)
    row_minus_col = (lax.broadcasted_iota(jnp.int32, (tq, tk), 0)
                     - lax.broadcasted_iota(jnp.int32, (tq, tk), 1))
    heads = range(2)
    qh, cq = [], []
    for hh in heads:
        in_half = (lane >= hh * HEAD_DIM) & (lane < (hh + 1) * HEAD_DIM)
        qh.append(jnp.where(in_half, q, jnp.zeros_like(q)) * jnp.asarray(HEAD_DIM ** -0.5, BF16))
        cq.append(cq_ref[:, hh:hh + 1])

    def logits(j, hh):
        start = pl.multiple_of(j * tk, tk)
        return _dot_nt(qh[hh], k_ref[pl.ds(start, tk), :]) + cq[hh] - ck_ref[hh, pl.ds(j, 1), :]

    def keep(j, hh, s):
        s_all[hh, j] = s
        m_s[hh] = jnp.maximum(m_s[hh], jnp.maximum(s[:, :LANE], s[:, LANE:]))

    def logit_body(j, carry):
        for hh in heads:
            keep(j, hh, logits(j, hh))
        return carry

    m_s[...] = jnp.full(m_s.shape, NEG, F32)
    lax.fori_loop(0, i * per_q, logit_body, 0)
    for dd in range(per_q):
        j = i * per_q + dd
        causal = row_minus_col >= dd * tk
        for hh in heads:
            keep(j, hh, jnp.where(causal, logits(j, hh), NEG))
    m_row = [jnp.broadcast_to(jnp.max(m_s[hh], axis=-1, keepdims=True), (tq, LANE)) for hh in heads]
    l_s[...] = jnp.zeros(l_s.shape, F32)
    acc_s[...] = jnp.zeros(acc_s.shape, F32)

    def prob_body(j, carry):
        start = pl.multiple_of(j * tk, tk)
        vs = v_ref[pl.ds(start, tk), :]
        for hh in heads:
            p_lo = jnp.exp(s_all[hh, j, :, :LANE] - m_row[hh])
            p_hi = jnp.exp(s_all[hh, j, :, LANE:] - m_row[hh])
            l_s[hh] = l_s[hh] + p_lo + p_hi
            p = jnp.concatenate([p_lo, p_hi], axis=1).astype(BF16)
            acc_s[hh] = acc_s[hh] + _dot(p, vs)
        return carry

    lax.fori_loop(0, (i + 1) * per_q, prob_body, 0)
    outs = [acc_s[hh] / jnp.sum(l_s[hh], axis=-1, keepdims=True) for hh in heads]
    o_ref[...] = jnp.where(lane < HEAD_DIM, outs[0], outs[1]).astype(o_ref.dtype)


def _fox_attention(q, k, v, c_col, c_row, bsz, t):
    n, hd = q.shape
    n_pair = hd // LANE
    nq = t // FOX_TQ
    nk = t // FOX_TK
    return pl.pallas_call(
        _fox_kernel,
        grid=(bsz, n_pair, nq),
        in_specs=[
            pl.BlockSpec((FOX_TQ, LANE), lambda b, j, i: (b * nq + i, j)),
            pl.BlockSpec((t, LANE), lambda b, j, i: (b, j)),
            pl.BlockSpec((t, LANE), lambda b, j, i: (b, j)),
            pl.BlockSpec((None, None, FOX_TQ, 2), lambda b, j, i: (b, j, i, 0)),
            pl.BlockSpec((None, None, 2, nk, FOX_TK), lambda b, j, i: (b, j, 0, 0, 0)),
        ],
        out_specs=pl.BlockSpec((FOX_TQ, LANE), lambda b, j, i: (b * nq + i, j)),
        out_shape=jax.ShapeDtypeStruct((n, hd), BF16),
        scratch_shapes=[pltpu.VMEM((2, nk, FOX_TQ, FOX_TK), F32), pltpu.VMEM((2, FOX_TQ, LANE), F32),
                        pltpu.VMEM((2, FOX_TQ, LANE), F32), pltpu.VMEM((2, FOX_TQ, LANE), F32)],
        compiler_params=_cparams(3), name="fox_attention",
    )(q, k, v, c_col, c_row)


def _swa_kernel(sink_ref, q_ref, kp_ref, kc_ref, vp_ref, vc_ref, bias_ref, o_ref):
    nblk = pl.program_id(1)
    w = q_ref.shape[0]
    kk = jnp.concatenate([kp_ref[...], kc_ref[...]], axis=0)
    vv = jnp.concatenate([vp_ref[...], vc_ref[...]], axis=0)
    lane = lax.broadcasted_iota(jnp.int32, (w, LANE), 1)
    col = lax.broadcasted_iota(jnp.int32, (w, 2 * w), 1)
    no_prev = (col < w) & (nblk == 0)
    grp = A_HEADS // A_KV_HEADS
    for g in range(grp):
        qg = q_ref[:, g * LANE:(g + 1) * LANE]
        outs = []
        for hh in range(A_KV_HEADS):
            head = g + grp * hh
            in_half = (lane >= hh * HEAD_DIM) & (lane < (hh + 1) * HEAD_DIM)
            qh = jnp.where(in_half, qg, jnp.zeros_like(qg)) * jnp.asarray(HEAD_DIM ** -0.5, BF16)
            s = _dot_nt(qh, kk) + bias_ref[head]
            s = jnp.where(no_prev, NEG, s)
            sink = sink_ref[head]
            m = jnp.maximum(jnp.max(s, axis=-1, keepdims=True), sink)
            e = jnp.exp(s - m)
            denom = jnp.sum(e, axis=-1, keepdims=True) + jnp.exp(sink - m)
            p = e / denom
            outs.append(_dot(p.astype(BF16), vv))
        o_ref[:, g * LANE:(g + 1) * LANE] = jnp.where(lane < HEAD_DIM, outs[0], outs[1]).astype(o_ref.dtype)


def _swa_attention(q, k, v, sinks, bias, bsz, t):
    n = q.shape[0]
    nb = t // WINDOW
    cur = lambda b, i: (b * nb + i, 0)
    prev = lambda b, i: (b * nb + jnp.maximum(i - 1, 0), 0)
    return pl.pallas_call(
        _swa_kernel,
        grid=(bsz, nb),
        in_specs=[
            pl.BlockSpec(memory_space=pltpu.SMEM),
            pl.BlockSpec((WINDOW, A_HEADS * HEAD_DIM), cur),
            pl.BlockSpec((WINDOW, LANE), prev), pl.BlockSpec((WINDOW, LANE), cur),
            pl.BlockSpec((WINDOW, LANE), prev), pl.BlockSpec((WINDOW, LANE), cur),
            _const_spec(bias.shape),
        ],
        out_specs=pl.BlockSpec((WINDOW, A_HEADS * HEAD_DIM), cur),
        out_shape=jax.ShapeDtypeStruct((n, A_HEADS * HEAD_DIM), BF16),
        compiler_params=_cparams(2), name="swa_attention",
    )(sinks.astype(F32), q, k, k, v, v, bias)


def _float_order_key(s):
    bits = pltpu.bitcast(s, jnp.int32)
    return bits ^ ((bits >> 31) & jnp.int32(0x7FFFFFFF))


def _dsa_kernel(q_ref, ckv_ref, qi_ref, ki_ref, wi_ref, wuk_ref, wuv_ref, tb_ref, o_ref,
                key_s, sel_s, qlat_s, qis_s, s_all, p_buf, acc_s, m_s, l_s):
    n = pl.program_id(1)
    qb = Q_BLOCK
    ck = DSA_CK
    n_sel = TOPK_MAX
    n_chunks = n // (ck // qb) + 1
    lane = lax.broadcasted_iota(jnp.int32, (qb, LANE), 1)
    qpos = n * qb + lax.broadcasted_iota(jnp.int32, (qb, ck), 0)
    kcol = lax.broadcasted_iota(jnp.int32, (qb, ck), 1)

    for h in range(IDX_HEADS):
        pair = qi_ref[:, (h // 2) * LANE:(h // 2 + 1) * LANE]
        in_half = (lane >= (h % 2) * IDX_DIM) & (lane < (h % 2 + 1) * IDX_DIM)
        qis_s[h * qb:(h + 1) * qb, :] = jnp.where(in_half, pair, jnp.zeros_like(pair))

    for h in range(C_HEADS):
        pair = q_ref[:, (h // 2) * LANE:(h // 2 + 1) * LANE]
        ql = _dot(pair, wuk_ref[h]) * (HEAD_DIM ** -0.5)
        qlat_s[h * qb:(h + 1) * qb, :] = ql.astype(BF16)

    wi = wi_ref[...] * (IDX_HEADS ** -0.5)

    def score_body(c, _):
        start = pl.multiple_of(c * ck, ck)
        kc = ki_ref[pl.ds(start, ck), :]
        raw = _dot_nt(qis_s[...], kc)
        score = jnp.zeros((qb, ck), F32)
        for h in range(IDX_HEADS):
            idx_s = jnp.maximum(raw[h * qb:(h + 1) * qb, :] * (IDX_DIM ** -0.5), 0.0)
            score = score + idx_s * wi[:, h:h + 1]
        causal = (start + kcol) <= qpos
        key_s[c] = _float_order_key(jnp.where(causal, score, NEG))
        sel_s[c] = jnp.where(causal, 0.0, NEG)
        return 0

    lax.fori_loop(0, n_chunks, score_body, 0)

    @pl.when((n + 1) * qb > n_sel)
    def _():
        def count_where(pred_fn):
            def body(c, acc):
                hit = jnp.where(pred_fn(key_s[c]), 1.0, 0.0)
                return acc + hit[:, :LANE] + hit[:, LANE:]
            acc = lax.fori_loop(0, n_chunks, body, jnp.zeros((qb, LANE), F32))
            return jnp.sum(acc, axis=-1, keepdims=True)

        cnt0 = count_where(lambda kv: kv >= 0)
        prefix0 = jnp.where(cnt0 >= n_sel, jnp.int32(0), jnp.int32(INT_MIN))

        def bit_body(t, prefix):
            cand = prefix | jnp.left_shift(jnp.int32(1), 30 - t)
            cnt = count_where(lambda kv: kv >= cand)
            return jnp.where(cnt >= n_sel, cand, prefix)

        thr = lax.fori_loop(0, 31, bit_body, prefix0)
        need = n_sel - count_where(lambda kv: kv > thr)
        r = lax.broadcasted_iota(jnp.int32, (ck, ck), 0)
        cc = lax.broadcasted_iota(jnp.int32, (ck, ck), 1)
        before = jnp.where(r < cc, 1.0, 0.0).astype(BF16)

        def tie_body(c, seen):
            kv = key_s[c]
            tie = kv == thr
            tie_f = jnp.where(tie, 1.0, 0.0)
            earlier = _dot(tie_f.astype(BF16), before) + seen
            sel = (kv > thr) | (tie & (earlier < need))
            sel_s[c] = jnp.where(sel, sel_s[c], NEG)
            return seen + jnp.sum(tie_f, axis=-1, keepdims=True)

        lax.fori_loop(0, n_chunks, tie_body, jnp.zeros((qb, 1), F32))

    m_s[...] = jnp.full(m_s.shape, NEG, F32)

    def logit_body(c, _):
        start = pl.multiple_of(c * ck, ck)
        raw = _dot_nt(qlat_s[...], ckv_ref[pl.ds(start, ck), :])
        rel_blk = jnp.minimum(n - c * (ck // qb), 3)
        mask_add = sel_s[c]
        for h in range(C_HEADS):
            rows = slice(h * qb, (h + 1) * qb)
            s = raw[rows, :] + tb_ref[rel_blk, h] + mask_add
            s_all[c, rows, :] = s
            m_s[rows, :] = jnp.maximum(m_s[rows, :], jnp.maximum(s[:, :LANE], s[:, LANE:]))
        return 0

    lax.fori_loop(0, n_chunks, logit_body, 0)

    for h in range(C_HEADS):
        rows = slice(h * qb, (h + 1) * qb)
        m_s[rows, :] = jnp.broadcast_to(jnp.max(m_s[rows, :], axis=-1, keepdims=True), (qb, LANE))
    l_s[...] = jnp.zeros(l_s.shape, F32)
    acc_s[...] = jnp.zeros(acc_s.shape, F32)

    def prob_body(c, _):
        start = pl.multiple_of(c * ck, ck)
        for h in range(C_HEADS):
            rows = slice(h * qb, (h + 1) * qb)
            m_row = m_s[rows, :]
            p_lo = jnp.exp(s_all[c, rows, :LANE] - m_row)
            p_hi = jnp.exp(s_all[c, rows, LANE:] - m_row)
            l_s[rows, :] = l_s[rows, :] + p_lo + p_hi
            p_buf[rows, :LANE] = p_lo.astype(BF16)
            p_buf[rows, LANE:] = p_hi.astype(BF16)
        acc_s[...] = acc_s[...] + _dot(p_buf[...], ckv_ref[pl.ds(start, ck), :])
        return 0

    lax.fori_loop(0, n_chunks, prob_body, 0)

    for g in range(C_HEADS // 2):
        out = jnp.zeros((qb, LANE), F32)
        for hh in range(2):
            h = 2 * g + hh
            rows = slice(h * qb, (h + 1) * qb)
            denom = jnp.sum(l_s[rows, :], axis=-1, keepdims=True)
            o_lat = (acc_s[rows, :] / denom).astype(BF16)
            out = out + _dot(o_lat, wuv_ref[h])
        o_ref[:, g * LANE:(g + 1) * LANE] = out.astype(o_ref.dtype)


def _dsa_attention(q, ckv, qi, ki2, wi, wuk_wide, wuv_wide, tb, bsz, t):
    n = q.shape[0]
    nq = t // Q_BLOCK
    nck = t // DSA_CK
    rows = C_HEADS * Q_BLOCK
    blk = lambda b, i: (b * nq + i, 0)
    per_b = lambda b, i: (b, 0)
    return pl.pallas_call(
        _dsa_kernel,
        grid=(bsz, nq),
        in_specs=[
            pl.BlockSpec((Q_BLOCK, C_HEADS * HEAD_DIM), blk),
            pl.BlockSpec((t, C_LATENT), per_b),
            pl.BlockSpec((Q_BLOCK, IDX_HEADS * IDX_DIM), blk),
            pl.BlockSpec((t, LANE), per_b),
            pl.BlockSpec((Q_BLOCK, LANE), blk),
            _const_spec(wuk_wide.shape), _const_spec(wuv_wide.shape), _const_spec(tb.shape),
        ],
        out_specs=pl.BlockSpec((Q_BLOCK, C_HEADS * HEAD_DIM), blk),
        out_shape=jax.ShapeDtypeStruct((n, C_HEADS * HEAD_DIM), BF16),
        scratch_shapes=[
            pltpu.VMEM((nck, Q_BLOCK, DSA_CK), jnp.int32),
            pltpu.VMEM((nck, Q_BLOCK, DSA_CK), F32),
            pltpu.VMEM((rows, C_LATENT), BF16),
            pltpu.VMEM((IDX_HEADS * Q_BLOCK, LANE), BF16),
            pltpu.VMEM((nck, rows, DSA_CK), F32),
            pltpu.VMEM((rows, DSA_CK), BF16),
            pltpu.VMEM((rows, C_LATENT), F32),
            pltpu.VMEM((rows, LANE), F32),
            pltpu.VMEM((rows, LANE), F32),
        ],
        compiler_params=_cparams(2), name="dsa_attention",
    )(q, ckv, qi, ki2, wi, wuk_wide, wuv_wide, tb)


def _post_attn_kernel(*refs, n_mix, alpha):
    x_ref = refs[0]
    mix_refs = refs[1:1 + n_mix]
    w_refs = refs[1 + n_mix:1 + 2 * n_mix]
    g_ref, b_ref, wr_ref, br_ref = refs[1 + 2 * n_mix:5 + 2 * n_mix]
    xo_ref, idx_ref, gate_ref, rank_ref, cnt_ref = refs[5 + 2 * n_mix:10 + 2 * n_mix]
    scr = refs[10 + 2 * n_mix:]
    wb_refs = scr[:n_mix]
    wr3_ref, before_ref, carry_ref = scr[n_mix:]
    tm = x_ref.shape[0]

    @pl.when(pl.program_id(0) == 0)
    def _():
        for w_ref, wb_ref in zip(w_refs, wb_refs):
            wb_ref[...] = w_ref[...].astype(BF16)
        hi, mid, lo = _split3(wr_ref[...])
        wr3_ref[0] = hi
        wr3_ref[1] = mid
        wr3_ref[2] = lo
        r = lax.broadcasted_iota(jnp.int32, (tm, tm), 0)
        c = lax.broadcasted_iota(jnp.int32, (tm, tm), 1)
        before_ref[...] = jnp.where(r < c, 1.0, 0.0).astype(BF16)
        carry_ref[...] = jnp.zeros(carry_ref.shape, F32)

    mix = _dot(mix_refs[0][...], wb_refs[0][...])
    for i in range(1, n_mix):
        mix = mix + _dot(mix_refs[i][...], wb_refs[i][...])
    xn = _layer_norm(alpha * x_ref[...] + mix, g_ref[...], b_ref[...])
    xo_ref[...] = xn

    xh, xm, xl = _split3(xn)
    logits = (_dot_nt(wr3_ref[0], xh) + _dot_nt(wr3_ref[0], xm) + _dot_nt(wr3_ref[1], xh)
              + _dot_nt(wr3_ref[0], xl) + _dot_nt(wr3_ref[1], xm) + _dot_nt(wr3_ref[2], xh)
              + br_ref[...])
    eidx = lax.broadcasted_iota(jnp.int32, logits.shape, 0)
    cur = logits
    vals, idxs = [], []
    for _ in range(TOP_K):
        mx = jnp.max(cur, axis=0, keepdims=True)
        first = jnp.min(jnp.where(cur == mx, eidx, N_EXPERTS), axis=0, keepdims=True)
        vals.append(mx)
        idxs.append(first)
        cur = jnp.where(eidx == first, -jnp.inf, cur)
    es = [jnp.exp(v - vals[0]) for v in vals]
    tot = es[0] + es[1] + es[2] + es[3]
    member = jnp.zeros(logits.shape, F32)
    for k in range(TOP_K):
        idx_ref[k:k + 1, :] = idxs[k]
        gate_ref[k:k + 1, :] = es[k] / tot
        member = member + jnp.where(eidx == idxs[k], 1.0, 0.0)
    earlier = _dot(member.astype(BF16), before_ref[...]) + carry_ref[...]
    for k in range(TOP_K):
        rk = jnp.sum(jnp.where(eidx == idxs[k], earlier, 0.0), axis=0, keepdims=True)
        rank_ref[k:k + 1, :] = rk.astype(jnp.int32)
    carry_ref[...] = carry_ref[...] + jnp.sum(member, axis=1, keepdims=True)
    cnt_ref[...] = carry_ref[...].astype(jnp.int32)


def _post_attn(x2, mixes, w_os, ln_g, ln_b, w_router, b_router, alpha):
    n, d = x2.shape
    n_mix = len(mixes)
    tm = TM_PROJ
    row = lambda i: (i, 0)
    colblk = lambda i: (0, i)
    in_specs = [pl.BlockSpec((tm, d), row)]
    in_specs += [pl.BlockSpec((tm, m.shape[1]), row) for m in mixes]
    in_specs += [_const_spec(w.shape) for w in w_os]
    in_specs += [_const_spec((1, d)), _const_spec((1, d)), _const_spec((N_EXPERTS, d)),
                 _const_spec((N_EXPERTS, 1))]
    out_specs = [pl.BlockSpec((tm, d), row), pl.BlockSpec((TOP_K, tm), colblk),
                 pl.BlockSpec((TOP_K, tm), colblk), pl.BlockSpec((TOP_K, tm), colblk),
                 _const_spec((N_EXPERTS, 1))]
    out_shape = [jax.ShapeDtypeStruct((n, d), F32), jax.ShapeDtypeStruct((TOP_K, n), jnp.int32),
                 jax.ShapeDtypeStruct((TOP_K, n), F32), jax.ShapeDtypeStruct((TOP_K, n), jnp.int32),
                 jax.ShapeDtypeStruct((N_EXPERTS, 1), jnp.int32)]
    scratch = [pltpu.VMEM(w.shape, BF16) for w in w_os]
    scratch += [pltpu.VMEM((3, N_EXPERTS, d), BF16), pltpu.VMEM((tm, tm), BF16),
                pltpu.VMEM((N_EXPERTS, 1), F32)]
    return pl.pallas_call(
        functools.partial(_post_attn_kernel, n_mix=n_mix, alpha=alpha),
        grid=(n // tm,),
        in_specs=in_specs, out_specs=out_specs, out_shape=out_shape, scratch_shapes=scratch,
        compiler_params=_cparams(1), name="post_attn",
    )(x2, *mixes, *w_os, ln_g.reshape(1, d), ln_b.reshape(1, d), w_router.T,
      b_router.reshape(N_EXPERTS, 1))


def _moe_kernel(be_ref, nu_ref, src_ref, x_hbm, wgu_ref, bgu_ref, wd_ref, bd_ref, y_ref,
                wgu_s, wd_s, xbuf, sem):
    i = pl.program_id(0)
    tm = y_ref.shape[0]
    prev = be_ref[jnp.maximum(i - 1, 0)]
    n_used = nu_ref[0]
    live = i < n_used
    slot = i % 2

    def row_copy(block, s, r):
        tok = src_ref[block * tm + r]
        return pltpu.make_async_copy(x_hbm.at[pl.ds(tok, 1), :], xbuf.at[s, pl.ds(r, 1), :],
                                     sem.at[s])

    def wait_rows(s):
        pltpu.make_async_copy(xbuf.at[s], xbuf.at[s], sem.at[s]).wait()

    @pl.when(i == 0)
    def _():
        def body(r, carry):
            row_copy(0, 0, r).start()
            return carry
        lax.fori_loop(0, tm, body, 0, unroll=8)

    @pl.when(live & ((i == 0) | (be_ref[i] != prev)))
    def _():
        wgu_s[...] = wgu_ref[...].astype(BF16)
        wd_s[...] = wd_ref[...].astype(BF16)

    @pl.when(live)
    def _():
        wait_rows(slot)
        xb = xbuf[slot].astype(BF16)
        ch = 512
        n_dots = 3 * (D_FF // ch)
        per = tm // n_dots + 1
        issued = [0]

        def fetch_some():
            lo, hi = issued[0], min(issued[0] + per, tm)
            for r in range(lo, hi):
                row_copy(i + 1, 1 - slot, r).start()
            issued[0] = hi

        acc = jnp.zeros(y_ref.shape, F32)
        for j in range(D_FF // ch):
            fetch_some()
            hg = _dot(xb, wgu_s[:, j * ch:(j + 1) * ch]) + bgu_ref[:, j * ch:(j + 1) * ch]
            fetch_some()
            hl = (_dot(xb, wgu_s[:, D_FF + j * ch:D_FF + (j + 1) * ch])
                  + bgu_ref[:, D_FF + j * ch:D_FF + (j + 1) * ch])
            glu = jnp.minimum(hg, SWIGLU_LIMIT)
            lin = jnp.clip(hl, -SWIGLU_LIMIT, SWIGLU_LIMIT)
            act = glu * jax.nn.sigmoid(SWIGLU_ALPHA * glu) * (lin + 1.0)
            fetch_some()
            acc = acc + _dot(act.astype(BF16), wd_s[j * ch:(j + 1) * ch, :])
        assert issued[0] == tm
        y_ref[...] = (acc + bd_ref[...]).astype(y_ref.dtype)

    @pl.when(i == n_used)
    def _():
        wait_rows(slot)

    @pl.when(jnp.logical_not(live))
    def _():
        y_ref[...] = jnp.zeros(y_ref.shape, y_ref.dtype)


def _moe_ffn(x2, src_tok, block_e, n_used, layer, w_gate_up, b_gate_up, w_down, b_down):
    d = x2.shape[1]
    rows = src_tok.shape[0]
    depth = w_gate_up.shape[0]
    n_blocks = rows // TM_MOE
    expert = lambda i, be, nu, src: (layer, be[i], 0, 0)
    grid_spec = pltpu.PrefetchScalarGridSpec(
        num_scalar_prefetch=3,
        grid=(n_blocks,),
        in_specs=[
            pl.BlockSpec(memory_space=pl.ANY),
            pl.BlockSpec((None, None, d, 2 * D_FF), expert),
            pl.BlockSpec((None, None, 1, 2 * D_FF), expert),
            pl.BlockSpec((None, None, D_FF, d), expert),
            pl.BlockSpec((None, None, 1, d), expert),
        ],
        out_specs=pl.BlockSpec((TM_MOE, d), lambda i, be, nu, src: (i, 0)),
        scratch_shapes=[pltpu.VMEM((d, 2 * D_FF), BF16), pltpu.VMEM((D_FF, d), BF16),
                        pltpu.VMEM((2, TM_MOE, d), F32), pltpu.SemaphoreType.DMA((2,))],
    )
    return pl.pallas_call(
        _moe_kernel, grid_spec=grid_spec,
        out_shape=jax.ShapeDtypeStruct((rows, d), F32),
        compiler_params=_cparams(1), name="moe_ffn",
    )(block_e, n_used, src_tok, x2, w_gate_up, b_gate_up.reshape(depth, N_EXPERTS, 1, 2 * D_FF),
      w_down, b_down.reshape(depth, N_EXPERTS, 1, d))


def _post_moe_kernel(dest_ref, x_ref, y_hbm, gt_ref, g_ref, b_ref, p_ref, wg_ref, wp_ref, o_ref,
                     wg_s, wp_s, ybuf, sem, *, alpha, n_tok):
    i = pl.program_id(0)
    n_steps = pl.num_programs(0)
    tm, d = x_ref.shape

    def row_copy(tile, slot, k, j):
        r = dest_ref[k * n_tok + tile * tm + j]
        return pltpu.make_async_copy(y_hbm.at[pl.ds(r, 1), :], ybuf.at[slot, k, pl.ds(j, 1), :],
                                     sem.at[slot])

    @pl.when(i == 0)
    def _():
        wg_s[...] = wg_ref[...].astype(BF16)
        wp_s[...] = wp_ref[...].astype(BF16)

        def body(j, carry):
            for k in range(TOP_K):
                row_copy(0, 0, k, j).start()
            return carry
        lax.fori_loop(0, tm, body, 0, unroll=8)

    @pl.when(i + 1 < n_steps)
    def _():
        for j in range(tm):
            for k in range(TOP_K):
                row_copy(i + 1, (i + 1) % 2, k, j).start()

    slot = i % 2
    pltpu.make_async_copy(ybuf.at[slot], ybuf.at[slot], sem.at[slot]).wait()
    gates = gt_ref[...]
    ffn = gates[:, 0:1] * ybuf[slot, 0]
    for k in range(1, TOP_K):
        ffn = ffn + gates[:, k:k + 1] * ybuf[slot, k]
    xn = _layer_norm(alpha * x_ref[...] + ffn, g_ref[...], b_ref[...])
    gate = jax.nn.sigmoid(_dot(xn.astype(BF16), wg_s[...]))
    emb = _dot(p_ref[...].astype(BF16), wp_s[...])
    o_ref[...] = xn + gate * emb


def _post_moe(x2, y_pad, dest_flat, gates_t, ln_g, ln_b, layer, p3, w_ple_gate, w_ple_proj, alpha):
    n, d = x2.shape
    tm = TM_POST
    row = lambda i, dest: (i, 0)
    const2 = lambda i, dest: (0, 0)
    grid_spec = pltpu.PrefetchScalarGridSpec(
        num_scalar_prefetch=1,
        grid=(n // tm,),
        in_specs=[pl.BlockSpec((tm, d), row), pl.BlockSpec(memory_space=pl.ANY),
                  pl.BlockSpec((tm, TOP_K), row), pl.BlockSpec((1, d), const2),
                  pl.BlockSpec((1, d), const2),
                  pl.BlockSpec((None, tm, PLE_DIM), lambda i, dest: (layer, i, 0)),
                  pl.BlockSpec((None, d, d), lambda i, dest: (layer, 0, 0)),
                  pl.BlockSpec((None, PLE_DIM, d), lambda i, dest: (layer, 0, 0))],
        out_specs=pl.BlockSpec((tm, d), row),
        scratch_shapes=[pltpu.VMEM((d, d), BF16), pltpu.VMEM((PLE_DIM, d), BF16),
                        pltpu.VMEM((2, TOP_K, tm, d), F32), pltpu.SemaphoreType.DMA((2,))],
    )
    return pl.pallas_call(
        functools.partial(_post_moe_kernel, alpha=alpha, n_tok=n),
        grid_spec=grid_spec,
        out_shape=jax.ShapeDtypeStruct((n, d), F32),
        compiler_params=_cparams(1), name="post_moe",
    )(dest_flat, x2, y_pad, gates_t, ln_g.reshape(1, d), ln_b.reshape(1, d), p3, w_ple_gate,
      w_ple_proj)


def _t5_bucket(dist):
    d = jnp.maximum(dist, 0)
    ratio = jnp.log(jnp.maximum(d, 1).astype(F32) / MAX_EXACT) / math.log(MAX_DISTANCE / MAX_EXACT)
    large = MAX_EXACT + (ratio * (NUM_BUCKETS - MAX_EXACT)).astype(jnp.int32)
    large = jnp.minimum(large, NUM_BUCKETS - 1)
    return jnp.where(d < MAX_EXACT, d, large)


def _swa_bias_table(pos_bias):
    qi = jnp.arange(WINDOW)[:, None]
    kj = jnp.arange(2 * WINDOW)[None, :]
    dist = qi + WINDOW - kj
    valid = (dist >= 0) & (dist < WINDOW)
    bias = pos_bias[_t5_bucket(dist)][..., :A_HEADS].astype(F32).transpose(2, 0, 1)
    return jnp.where(valid[None], bias, NEG)


def _dsa_bias_tables(pos_bias):
    qi = jnp.arange(Q_BLOCK)[:, None]
    kj = jnp.arange(DSA_CK)[None, :]
    tabs = []
    for r in range(4):
        dist = r * Q_BLOCK + qi - kj
        tabs.append(pos_bias[_t5_bucket(dist)].astype(F32).transpose(2, 0, 1))
    return jnp.stack(tabs)


def _pad_cols(w, width=LANE):
    return jnp.pad(w, ((0, 0), (0, width - w.shape[1])))


def _even_mixer(x2, bsz, t, w_in, b_forget, sinks, w_o, swa_bias):
    hd = HEAD_DIM
    grp = A_HEADS // A_KV_HEADS
    pair_order = [g + grp * hh for g in range(grp) for hh in range(A_KV_HEADS)]
    na, nkv, nb = A_HEADS * hd, A_KV_HEADS * hd, B_HEADS * hd
    offs = [0, na, na + nkv, na + 2 * nkv, na + 2 * nkv + nb, na + 2 * nkv + 2 * nb,
            na + 2 * nkv + 3 * nb]
    w_aq = w_in[:, offs[0]:offs[1]].reshape(D_MODEL, A_HEADS, hd)[:, pair_order].reshape(D_MODEL, na)
    weights = [w_aq, w_in[:, offs[1]:offs[2]], w_in[:, offs[2]:offs[3]], w_in[:, offs[3]:offs[4]],
               w_in[:, offs[4]:offs[5]], w_in[:, offs[5]:offs[6]], _pad_cols(w_in[:, offs[6]:])]
    aq, ak, av, bq, bk, bv, bf = _project(x2, weights, [BF16] * 6 + [F32])
    o_a = _swa_attention(aq, ak, av, sinks, swa_bias, bsz, t)

    f_t = bf[:, :B_HEADS].reshape(bsz, t, B_HEADS).transpose(0, 2, 1)
    c = _forget_cumsum(f_t, b_forget)
    c_row = c.reshape(bsz, B_HEADS // 2, 2, t // FOX_TK, FOX_TK)
    c_col = c.reshape(bsz, B_HEADS // 2, 2, t).transpose(0, 1, 3, 2)
    o_b = _fox_attention(bq, bk, bv, c_col, c_row, bsz, t)

    w_oa = w_o[:na].reshape(A_HEADS, hd, D_MODEL)[jnp.asarray(pair_order)].reshape(na, D_MODEL)
    return [o_a, o_b], [w_oa, w_o[na:]]


def _odd_mixer(x2, bsz, t, w_in, kv_norm, w_uk, w_uv, w_o, dsa_tb):
    hd = HEAD_DIM
    nq, nl, ni = C_HEADS * hd, C_LATENT, IDX_HEADS * IDX_DIM
    w_ki = w_in[:, nq + nl + ni:nq + nl + ni + IDX_DIM]
    weights = [w_in[:, :nq], w_in[:, nq:nq + nl], w_in[:, nq + nl:nq + nl + ni],
               jnp.concatenate([w_ki, w_ki], axis=1), _pad_cols(w_in[:, nq + nl + ni + IDX_DIM:])]
    cq, ckv, qi, ki2, wi = _project(x2, weights, [BF16, BF16, BF16, BF16, F32],
                                    rms_gain=kv_norm.reshape(1, nl).astype(F32), rms_index=1)
    uk = w_uk.transpose(1, 2, 0)
    uv = w_uv.transpose(1, 0, 2)
    zk = jnp.zeros_like(uk)
    zv = jnp.zeros_like(uv)
    odd = (jnp.arange(C_HEADS) % 2 == 1)[:, None, None]
    wuk_wide = jnp.where(odd, jnp.concatenate([zk, uk], axis=1), jnp.concatenate([uk, zk], axis=1))
    wuv_wide = jnp.where(odd, jnp.concatenate([zv, uv], axis=2), jnp.concatenate([uv, zv], axis=2))
    o_c = _dsa_attention(cq, ckv, qi, ki2, wi, wuk_wide.astype(BF16), wuv_wide.astype(BF16),
                         dsa_tb, bsz, t)
    return [o_c], [w_o]


def _dispatch_plan(top_idx, rank, counts, n_tok):
    counts = counts.reshape(N_EXPERTS)
    padded = (counts + TM_MOE - 1) // TM_MOE * TM_MOE
    pad_ends = jnp.cumsum(padded)
    pad_starts = pad_ends - padded
    is_e = top_idx[..., None] == jnp.arange(N_EXPERTS, dtype=jnp.int32)
    dest = jnp.sum(jnp.where(is_e, pad_starts.astype(jnp.int32), 0), axis=-1) + rank
    n_blocks = (n_tok * TOP_K) // TM_MOE + N_EXPERTS
    block_start = jnp.arange(n_blocks, dtype=jnp.int32) * TM_MOE
    block_e = jnp.minimum(jnp.sum(pad_ends[None, :] <= block_start[:, None], axis=1),
                          N_EXPERTS - 1).astype(jnp.int32)
    n_used = (pad_ends[-1] // TM_MOE).astype(jnp.int32).reshape(1)
    return dest, block_e, n_used, n_blocks


def kernel(x, p, pos_bias, w_in_even, b_forget, sinks, w_o_even, w_in_odd, kv_norm, w_uk, w_uv,
           w_o_odd, ln_g, ln_b, w_router, b_router, w_gate_up, b_gate_up, w_down, b_down,
           w_ple_proj, w_ple_gate):
    bsz, t, d = x.shape
    depth = ln_g.shape[0]
    alpha = (2 * depth) ** 0.25
    n_tok = bsz * t
    x2 = x.reshape(n_tok, d)
    swa_bias = _swa_bias_table(pos_bias)
    dsa_tb = _dsa_bias_tables(pos_bias)
    p3 = p.reshape(depth, n_tok, PLE_DIM)
    tok_ids = jnp.tile(jnp.arange(n_tok, dtype=jnp.int32), (TOP_K,))
    for i in range(depth):
        j = i // 2
        if i % 2 == 0:
            mixes, w_os = _even_mixer(x2, bsz, t, w_in_even[j], b_forget[j], sinks[j], w_o_even[j],
                                      swa_bias)
        else:
            mixes, w_os = _odd_mixer(x2, bsz, t, w_in_odd[j], kv_norm[j], w_uk[j], w_uv[j],
                                     w_o_odd[j], dsa_tb)
        x1, top_idx, gates, rank, counts = _post_attn(x2, mixes, w_os, ln_g[i, 0], ln_b[i, 0],
                                                      w_router[i], b_router[i], alpha)
        dest, block_e, n_used, n_blocks = _dispatch_plan(top_idx, rank, counts, n_tok)
        src_tok = jnp.zeros((n_blocks * TM_MOE,), jnp.int32).at[dest.reshape(-1)].set(tok_ids)
        y_pad = _moe_ffn(x1, src_tok, block_e, n_used, i, w_gate_up, b_gate_up, w_down, b_down)
        x2 = _post_moe(x1, y_pad, dest.reshape(-1), gates.T, ln_g[i, 1], ln_b[i, 1], i, p3,
                       w_ple_gate, w_ple_proj, alpha)
    return x2.reshape(bsz, t, d)
```

```python
import functools
import math

import jax
import jax.numpy as jnp
from jax import lax
from jax.experimental import pallas as pl
from jax.experimental.pallas import tpu as pltpu

D_MODEL = 1024
HEAD_DIM = 64
A_HEADS, A_KV_HEADS, WINDOW = 8, 2, 128
B_HEADS = 8
C_HEADS, C_LATENT = 16, 128
IDX_HEADS, IDX_DIM = 8, 64
TOPK_MAX = 256
NUM_BUCKETS, MAX_EXACT, MAX_DISTANCE = 32, 16, 128
N_EXPERTS, TOP_K, D_FF = 32, 4, 1024
SWIGLU_LIMIT, SWIGLU_ALPHA = 7.0, 1.702
PLE_DIM = 256
Q_BLOCK = 128
LN_EPS = 1e-5
NEG = -1e30

LANE = 128
VMEM_LIMIT = 56 * 1024 * 1024
TM_PROJ = 512
TM_MOE = 256
MOE_SLOTS = 3
TM_POST = 256
FOX_TQ = 512
FOX_TK = 2 * LANE
DSA_CK = 256
INT_MIN = -2 ** 31

BF16 = jnp.bfloat16
F32 = jnp.float32


def _cparams(n_axes):
    return pltpu.CompilerParams(dimension_semantics=("arbitrary",) * n_axes,
                                vmem_limit_bytes=VMEM_LIMIT)


def _const_spec(shape):
    nd = len(shape)
    return pl.BlockSpec(shape, lambda *_: (0,) * nd)


def _dot(a, b):
    return jnp.dot(a, b, preferred_element_type=F32)


def _dot_nt(a, b):
    return lax.dot_general(a, b, (((1,), (1,)), ((), ())), preferred_element_type=F32)


def _layer_norm(v, g, b):
    mu = jnp.mean(v, axis=-1, keepdims=True)
    d = v - mu
    var = jnp.mean(d * d, axis=-1, keepdims=True)
    return d * lax.rsqrt(var + LN_EPS) * g + b


def _split3(v):
    hi = v.astype(BF16)
    r1 = v - hi.astype(F32)
    mid = r1.astype(BF16)
    lo = (r1 - mid.astype(F32)).astype(BF16)
    return hi, mid, lo


def _proj_kernel(*refs, n_w, rms_index):
    x_ref = refs[0]
    w_refs = refs[1:1 + n_w]
    g_ref = refs[1 + n_w]
    o_refs = refs[2 + n_w:2 + 2 * n_w]
    wb_refs = refs[2 + 2 * n_w:]

    @pl.when(pl.program_id(0) == 0)
    def _():
        for w_ref, wb_ref in zip(w_refs, wb_refs):
            wb_ref[...] = w_ref[...].astype(BF16)

    xb = x_ref[...].astype(BF16)
    for i in range(n_w):
        h = _dot(xb, wb_refs[i][...])
        if i == rms_index:
            h = h * lax.rsqrt(jnp.mean(h * h, axis=-1, keepdims=True) + LN_EPS) * g_ref[...]
        o_refs[i][...] = h.astype(o_refs[i].dtype)


def _project(x2, weights, out_dtypes, rms_gain=None, rms_index=-1):
    n, d = x2.shape
    n_w = len(weights)
    if rms_gain is None:
        rms_gain = jnp.ones((1, LANE), F32)
    in_specs = [pl.BlockSpec((TM_PROJ, d), lambda i: (i, 0))]
    in_specs += [_const_spec(w.shape) for w in weights]
    in_specs += [_const_spec(rms_gain.shape)]
    out_specs = [pl.BlockSpec((TM_PROJ, w.shape[1]), lambda i: (i, 0)) for w in weights]
    out_shape = [jax.ShapeDtypeStruct((n, w.shape[1]), dt) for w, dt in zip(weights, out_dtypes)]
    scratch = [pltpu.VMEM(w.shape, BF16) for w in weights]
    return pl.pallas_call(
        functools.partial(_proj_kernel, n_w=n_w, rms_index=rms_index),
        grid=(n // TM_PROJ,),
        in_specs=in_specs, out_specs=out_specs, out_shape=out_shape,
        scratch_shapes=scratch, compiler_params=_cparams(1), name="in_proj",
    )(x2, *weights, rms_gain)


def _cumsum_kernel(f_ref, b_ref, o_ref):
    z = f_ref[...] + b_ref[...]
    ls = jnp.minimum(z, 0.0) - jnp.log1p(jnp.exp(-jnp.abs(z)))
    t = ls.shape[1]
    r = lax.broadcasted_iota(jnp.int32, (LANE, LANE), 0)
    c = lax.broadcasted_iota(jnp.int32, (LANE, LANE), 1)
    tri = jnp.where(r <= c, 1.0, 0.0).astype(BF16)
    carry = jnp.zeros((ls.shape[0], 1), F32)
    for blk in range(t // LANE):
        seg = ls[:, blk * LANE:(blk + 1) * LANE]
        hi, mid, lo = _split3(seg)
        pre = _dot(hi, tri) + _dot(mid, tri) + _dot(lo, tri) + carry
        o_ref[:, blk * LANE:(blk + 1) * LANE] = pre
        carry = pre[:, LANE - 1:LANE]


def _forget_cumsum(f_t, b_forget):
    bsz, h, t = f_t.shape
    return pl.pallas_call(
        _cumsum_kernel,
        grid=(bsz,),
        in_specs=[pl.BlockSpec((None, h, t), lambda b: (b, 0, 0)), _const_spec((h, 1))],
        out_specs=pl.BlockSpec((None, h, t), lambda b: (b, 0, 0)),
        out_shape=jax.ShapeDtypeStruct((bsz, h, t), F32),
        compiler_params=_cparams(1), name="forget_cumsum",
    )(f_t, b_forget.reshape(h, 1).astype(F32))


def _fox_kernel(q_ref, k_ref, v_ref, cq_ref, ck_ref, o_ref, s_all, m_s, l_s, acc_s):
    i = pl.program_id(2)
    tq = q_ref.shape[0]
    tk = FOX_TK
    per_q = tq // tk
    q = q_ref[...]
    lane = lax.broadcasted_iota(jnp.int32, (tq, LANE), 1)
    row_minus_col = (lax.broadcasted_iota(jnp.int32, (tq, tk), 0)
                     - lax.broadcasted_iota(jnp.int32, (tq, tk), 1))
    heads = range(2)
    qh, cq = [], []
    for hh in heads:
        in_half = (lane >= hh * HEAD_DIM) & (lane < (hh + 1) * HEAD_DIM)
        qh.append(jnp.where(in_half, q, jnp.zeros_like(q)) * jnp.asarray(HEAD_DIM ** -0.5, BF16))
        cq.append(cq_ref[:, hh:hh + 1])

    def logits(j, hh):
        start = pl.multiple_of(j * tk, tk)
        return _dot_nt(qh[hh], k_ref[pl.ds(start, tk), :]) + cq[hh] - ck_ref[hh, pl.ds(j, 1), :]

    def keep(j, hh, s):
        s_all[hh, j] = s
        m_s[hh] = jnp.maximum(m_s[hh], jnp.maximum(s[:, :LANE], s[:, LANE:]))

    def logit_body(j, carry):
        for hh in heads:
            keep(j, hh, logits(j, hh))
        return carry

    m_s[...] = jnp.full(m_s.shape, NEG, F32)
    lax.fori_loop(0, i * per_q, logit_body, 0)
    for dd in range(per_q):
        j = i * per_q + dd
        causal = row_minus_col >= dd * tk
        for hh in heads:
            keep(j, hh, jnp.where(causal, logits(j, hh), NEG))
    m_row = [jnp.broadcast_to(jnp.max(m_s[hh], axis=-1, keepdims=True), (tq, LANE)) for hh in heads]
    l_s[...] = jnp.zeros(l_s.shape, F32)
    acc_s[...] = jnp.zeros(acc_s.shape, F32)

    def prob_body(j, carry):
        start = pl.multiple_of(j * tk, tk)
        vs = v_ref[pl.ds(start, tk), :]
        for hh in heads:
            p_lo = jnp.exp(s_all[hh, j, :, :LANE] - m_row[hh])
            p_hi = jnp.exp(s_all[hh, j, :, LANE:] - m_row[hh])
            l_s[hh] = l_s[hh] + p_lo + p_hi
            p = jnp.concatenate([p_lo, p_hi], axis=1).astype(BF16)
            acc_s[hh] = acc_s[hh] + _dot(p, vs)
        return carry

    lax.fori_loop(0, (i + 1) * per_q, prob_body, 0)
    outs = [acc_s[hh] / jnp.sum(l_s[hh], axis=-1, keepdims=True) for hh in heads]
    o_ref[...] = jnp.where(lane < HEAD_DIM, outs[0], outs[1]).astype(o_ref.dtype)


def _fox_attention(q, k, v, c_col, c_row, bsz, t):
    n, hd = q.shape
    n_pair = hd // LANE
    nq = t // FOX_TQ
    nk = t // FOX_TK
    return pl.pallas_call(
        _fox_kernel,
        grid=(bsz, n_pair, nq),
        in_specs=[
            pl.BlockSpec((FOX_TQ, LANE), lambda b, j, i: (b * nq + i, j)),
            pl.BlockSpec((t, LANE), lambda b, j, i: (b, j)),
            pl.BlockSpec((t, LANE), lambda b, j, i: (b, j)),
            pl.BlockSpec((None, None, FOX_TQ, 2), lambda b, j, i: (b, j, i, 0)),
            pl.BlockSpec((None, None, 2, nk, FOX_TK), lambda b, j, i: (b, j, 0, 0, 0)),
        ],
        out_specs=pl.BlockSpec((FOX_TQ, LANE), lambda b, j, i: (b * nq + i, j)),
        out_shape=jax.ShapeDtypeStruct((n, hd), BF16),
        scratch_shapes=[pltpu.VMEM((2, nk, FOX_TQ, FOX_TK), F32), pltpu.VMEM((2, FOX_TQ, LANE), F32),
                        pltpu.VMEM((2, FOX_TQ, LANE), F32), pltpu.VMEM((2, FOX_TQ, LANE), F32)],
        compiler_params=_cparams(3), name="fox_attention",
    )(q, k, v, c_col, c_row)


def _swa_kernel(sink_ref, q_ref, kp_ref, kc_ref, vp_ref, vc_ref, bias_ref, o_ref):
    nblk = pl.program_id(1)
    w = q_ref.shape[0]
    kk = jnp.concatenate([kp_ref[...], kc_ref[...]], axis=0)
    vv = jnp.concatenate([vp_ref[...], vc_ref[...]], axis=0)
    lane = lax.broadcasted_iota(jnp.int32, (w, LANE), 1)
    col = lax.broadcasted_iota(jnp.int32, (w, 2 * w), 1)
    no_prev = (col < w) & (nblk == 0)
    grp = A_HEADS // A_KV_HEADS
    for g in range(grp):
        qg = q_ref[:, g * LANE:(g + 1) * LANE]
        outs = []
        for hh in range(A_KV_HEADS):
            head = g + grp * hh
            in_half = (lane >= hh * HEAD_DIM) & (lane < (hh + 1) * HEAD_DIM)
            qh = jnp.where(in_half, qg, jnp.zeros_like(qg)) * jnp.asarray(HEAD_DIM ** -0.5, BF16)
            s = _dot_nt(qh, kk) + bias_ref[head]
            s = jnp.where(no_prev, NEG, s)
            sink = sink_ref[head]
            m = jnp.maximum(jnp.max(s, axis=-1, keepdims=True), sink)
            e = jnp.exp(s - m)
            denom = jnp.sum(e, axis=-1, keepdims=True) + jnp.exp(sink - m)
            p = e / denom
            outs.append(_dot(p.astype(BF16), vv))
        o_ref[:, g * LANE:(g + 1) * LANE] = jnp.where(lane < HEAD_DIM, outs[0], outs[1]).astype(o_ref.dtype)


def _swa_attention(q, k, v, sinks, bias, bsz, t):
    n = q.shape[0]
    nb = t // WINDOW
    cur = lambda b, i: (b * nb + i, 0)
    prev = lambda b, i: (b * nb + jnp.maximum(i - 1, 0), 0)
    return pl.pallas_call(
        _swa_kernel,
        grid=(bsz, nb),
        in_specs=[
            pl.BlockSpec(memory_space=pltpu.SMEM),
            pl.BlockSpec((WINDOW, A_HEADS * HEAD_DIM), cur),
            pl.BlockSpec((WINDOW, LANE), prev), pl.BlockSpec((WINDOW, LANE), cur),
            pl.BlockSpec((WINDOW, LANE), prev), pl.BlockSpec((WINDOW, LANE), cur),
            _const_spec(bias.shape),
        ],
        out_specs=pl.BlockSpec((WINDOW, A_HEADS * HEAD_DIM), cur),
        out_shape=jax.ShapeDtypeStruct((n, A_HEADS * HEAD_DIM), BF16),
        compiler_params=_cparams(2), name="swa_attention",
    )(sinks.astype(F32), q, k, k, v, v, bias)


def _float_order_key(s):
    bits = pltpu.bitcast(s, jnp.int32)
    return bits ^ ((bits >> 31) & jnp.int32(0x7FFFFFFF))


def _dsa_kernel(q_ref, ckv_ref, qi_ref, ki_ref, wi_ref, wuk_ref, wuv_ref, tb_ref, o_ref,
                key_s, sel_s, qlat_s, qis_s, s_all, p_buf, acc_s, m_s, l_s):
    n = pl.program_id(1)
    qb = Q_BLOCK
    ck = DSA_CK
    n_sel = TOPK_MAX
    n_chunks = n // (ck // qb) + 1
    lane = lax.broadcasted_iota(jnp.int32, (qb, LANE), 1)
    qpos = n * qb + lax.broadcasted_iota(jnp.int32, (qb, ck), 0)
    kcol = lax.broadcasted_iota(jnp.int32, (qb, ck), 1)

    for h in range(IDX_HEADS):
        pair = qi_ref[:, (h // 2) * LANE:(h // 2 + 1) * LANE]
        in_half = (lane >= (h % 2) * IDX_DIM) & (lane < (h % 2 + 1) * IDX_DIM)
        qis_s[h * qb:(h + 1) * qb, :] = jnp.where(in_half, pair, jnp.zeros_like(pair))

    for h in range(C_HEADS):
        pair = q_ref[:, (h // 2) * LANE:(h // 2 + 1) * LANE]
        ql = _dot(pair, wuk_ref[h]) * (HEAD_DIM ** -0.5)
        qlat_s[h * qb:(h + 1) * qb, :] = ql.astype(BF16)

    wi = wi_ref[...] * (IDX_HEADS ** -0.5)

    def score_body(c, _):
        start = pl.multiple_of(c * ck, ck)
        kc = ki_ref[pl.ds(start, ck), :]
        raw = _dot_nt(qis_s[...], kc)
        score = jnp.zeros((qb, ck), F32)
        for h in range(IDX_HEADS):
            idx_s = jnp.maximum(raw[h * qb:(h + 1) * qb, :] * (IDX_DIM ** -0.5), 0.0)
            score = score + idx_s * wi[:, h:h + 1]
        causal = (start + kcol) <= qpos
        key_s[c] = _float_order_key(jnp.where(causal, score, NEG))
        sel_s[c] = jnp.where(causal, 0.0, NEG)
        return 0

    lax.fori_loop(0, n_chunks, score_body, 0)

    @pl.when((n + 1) * qb > n_sel)
    def _():
        def count_where(pred_fn):
            def body(c, acc):
                hit = jnp.where(pred_fn(key_s[c]), 1.0, 0.0)
                return acc + hit[:, :LANE] + hit[:, LANE:]
            acc = lax.fori_loop(0, n_chunks, body, jnp.zeros((qb, LANE), F32))
            return jnp.sum(acc, axis=-1, keepdims=True)

        cnt0 = count_where(lambda kv: kv >= 0)
        prefix0 = jnp.where(cnt0 >= n_sel, jnp.int32(0), jnp.int32(INT_MIN))

        def bit_body(t, prefix):
            cand = prefix | jnp.left_shift(jnp.int32(1), 30 - t)
            cnt = count_where(lambda kv: kv >= cand)
            return jnp.where(cnt >= n_sel, cand, prefix)

        thr = lax.fori_loop(0, 31, bit_body, prefix0)
        need = n_sel - count_where(lambda kv: kv > thr)
        r = lax.broadcasted_iota(jnp.int32, (ck, ck), 0)
        cc = lax.broadcasted_iota(jnp.int32, (ck, ck), 1)
        before = jnp.where(r < cc, 1.0, 0.0).astype(BF16)

        def tie_body(c, seen):
            kv = key_s[c]
            tie = kv == thr
            tie_f = jnp.where(tie, 1.0, 0.0)
            earlier = _dot(tie_f.astype(BF16), before) + seen
            sel = (kv > thr) | (tie & (earlier < need))
            sel_s[c] = jnp.where(sel, sel_s[c], NEG)
            return seen + jnp.sum(tie_f, axis=-1, keepdims=True)

        lax.fori_loop(0, n_chunks, tie_body, jnp.zeros((qb, 1), F32))

    m_s[...] = jnp.full(m_s.shape, NEG, F32)

    def logit_body(c, _):
        start = pl.multiple_of(c * ck, ck)
        raw = _dot_nt(qlat_s[...], ckv_ref[pl.ds(start, ck), :])
        rel_blk = jnp.minimum(n - c * (ck // qb), 3)
        mask_add = sel_s[c]
        for h in range(C_HEADS):
            rows = slice(h * qb, (h + 1) * qb)
            s = raw[rows, :] + tb_ref[rel_blk, h] + mask_add
            s_all[c, rows, :] = s
            m_s[rows, :] = jnp.maximum(m_s[rows, :], jnp.maximum(s[:, :LANE], s[:, LANE:]))
        return 0

    lax.fori_loop(0, n_chunks, logit_body, 0)

    for h in range(C_HEADS):
        rows = slice(h * qb, (h + 1) * qb)
        m_s[rows, :] = jnp.broadcast_to(jnp.max(m_s[rows, :], axis=-1, keepdims=True), (qb, LANE))
    l_s[...] = jnp.zeros(l_s.shape, F32)
    acc_s[...] = jnp.zeros(acc_s.shape, F32)

    def prob_body(c, _):
        start = pl.multiple_of(c * ck, ck)
        for h in range(C_HEADS):
            rows = slice(h * qb, (h + 1) * qb)
            m_row = m_s[rows, :]
            p_lo = jnp.exp(s_all[c, rows, :LANE] - m_row)
            p_hi = jnp.exp(s_all[c, rows, LANE:] - m_row)
            l_s[rows, :] = l_s[rows, :] + p_lo + p_hi
            p_buf[rows, :LANE] = p_lo.astype(BF16)
            p_buf[rows, LANE:] = p_hi.astype(BF16)
        acc_s[...] = acc_s[...] + _dot(p_buf[...], ckv_ref[pl.ds(start, ck), :])
        return 0

    lax.fori_loop(0, n_chunks, prob_body, 0)

    for g in range(C_HEADS // 2):
        out = jnp.zeros((qb, LANE), F32)
        for hh in range(2):
            h = 2 * g + hh
            rows = slice(h * qb, (h + 1) * qb)
            denom = jnp.sum(l_s[rows, :], axis=-1, keepdims=True)
            o_lat = (acc_s[rows, :] / denom).astype(BF16)
            out = out + _dot(o_lat, wuv_ref[h])
        o_ref[:, g * LANE:(g + 1) * LANE] = out.astype(o_ref.dtype)


def _dsa_attention(q, ckv, qi, ki2, wi, wuk_wide, wuv_wide, tb, bsz, t):
    n = q.shape[0]
    nq = t // Q_BLOCK
    nck = t // DSA_CK
    rows = C_HEADS * Q_BLOCK
    blk = lambda b, i: (b * nq + i, 0)
    per_b = lambda b, i: (b, 0)
    return pl.pallas_call(
        _dsa_kernel,
        grid=(bsz, nq),
        in_specs=[
            pl.BlockSpec((Q_BLOCK, C_HEADS * HEAD_DIM), blk),
            pl.BlockSpec((t, C_LATENT), per_b),
            pl.BlockSpec((Q_BLOCK, IDX_HEADS * IDX_DIM), blk),
            pl.BlockSpec((t, LANE), per_b),
            pl.BlockSpec((Q_BLOCK, LANE), blk),
            _const_spec(wuk_wide.shape), _const_spec(wuv_wide.shape), _const_spec(tb.shape),
        ],
        out_specs=pl.BlockSpec((Q_BLOCK, C_HEADS * HEAD_DIM), blk),
        out_shape=jax.ShapeDtypeStruct((n, C_HEADS * HEAD_DIM), BF16),
        scratch_shapes=[
            pltpu.VMEM((nck, Q_BLOCK, DSA_CK), jnp.int32),
            pltpu.VMEM((nck, Q_BLOCK, DSA_CK), F32),
            pltpu.VMEM((rows, C_LATENT), BF16),
            pltpu.VMEM((IDX_HEADS * Q_BLOCK, LANE), BF16),
            pltpu.VMEM((nck, rows, DSA_CK), F32),
            pltpu.VMEM((rows, DSA_CK), BF16),
            pltpu.VMEM((rows, C_LATENT), F32),
            pltpu.VMEM((rows, LANE), F32),
            pltpu.VMEM((rows, LANE), F32),
        ],
        compiler_params=_cparams(2), name="dsa_attention",
    )(q, ckv, qi, ki2, wi, wuk_wide, wuv_wide, tb)


def _post_attn_kernel(*refs, n_mix, alpha):
    x_ref = refs[0]
    mix_refs = refs[1:1 + n_mix]
    w_refs = refs[1 + n_mix:1 + 2 * n_mix]
    g_ref, b_ref, wr_ref, br_ref = refs[1 + 2 * n_mix:5 + 2 * n_mix]
    xo_ref, idx_ref, gate_ref, rank_ref, cnt_ref = refs[5 + 2 * n_mix:10 + 2 * n_mix]
    scr = refs[10 + 2 * n_mix:]
    wb_refs = scr[:n_mix]
    wr3_ref, before_ref, carry_ref = scr[n_mix:]
    tm = x_ref.shape[0]

    @pl.when(pl.program_id(0) == 0)
    def _():
        for w_ref, wb_ref in zip(w_refs, wb_refs):
            wb_ref[...] = w_ref[...].astype(BF16)
        hi, mid, lo = _split3(wr_ref[...])
        wr3_ref[0] = hi
        wr3_ref[1] = mid
        wr3_ref[2] = lo
        r = lax.broadcasted_iota(jnp.int32, (tm, tm), 0)
        c = lax.broadcasted_iota(jnp.int32, (tm, tm), 1)
        before_ref[...] = jnp.where(r < c, 1.0, 0.0).astype(BF16)
        carry_ref[...] = jnp.zeros(carry_ref.shape, F32)

    mix = _dot(mix_refs[0][...], wb_refs[0][...])
    for i in range(1, n_mix):
        mix = mix + _dot(mix_refs[i][...], wb_refs[i][...])
    xn = _layer_norm(alpha * x_ref[...] + mix, g_ref[...], b_ref[...])
    xo_ref[...] = xn

    xh, xm, xl = _split3(xn)
    logits = (_dot_nt(wr3_ref[0], xh) + _dot_nt(wr3_ref[0], xm) + _dot_nt(wr3_ref[1], xh)
              + _dot_nt(wr3_ref[0], xl) + _dot_nt(wr3_ref[1], xm) + _dot_nt(wr3_ref[2], xh)
              + br_ref[...])
    eidx = lax.broadcasted_iota(jnp.int32, logits.shape, 0)
    cur = logits
    vals, idxs = [], []
    for _ in range(TOP_K):
        mx = jnp.max(cur, axis=0, keepdims=True)
        first = jnp.min(jnp.where(cur == mx, eidx, N_EXPERTS), axis=0, keepdims=True)
        vals.append(mx)
        idxs.append(first)
        cur = jnp.where(eidx == first, -jnp.inf, cur)
    es = [jnp.exp(v - vals[0]) for v in vals]
    tot = es[0] + es[1] + es[2] + es[3]
    member = jnp.zeros(logits.shape, F32)
    for k in range(TOP_K):
        idx_ref[k:k + 1, :] = idxs[k]
        gate_ref[k:k + 1, :] = es[k] / tot
        member = member + jnp.where(eidx == idxs[k], 1.0, 0.0)
    earlier = _dot(member.astype(BF16), before_ref[...]) + carry_ref[...]
    for k in range(TOP_K):
        rk = jnp.sum(jnp.where(eidx == idxs[k], earlier, 0.0), axis=0, keepdims=True)
        rank_ref[k:k + 1, :] = rk.astype(jnp.int32)
    carry_ref[...] = carry_ref[...] + jnp.sum(member, axis=1, keepdims=True)
    cnt_ref[...] = carry_ref[...].astype(jnp.int32)


def _post_attn(x2, mixes, w_os, ln_g, ln_b, w_router, b_router, alpha):
    n, d = x2.shape
    n_mix = len(mixes)
    tm = TM_PROJ
    row = lambda i: (i, 0)
    colblk = lambda i: (0, i)
    in_specs = [pl.BlockSpec((tm, d), row)]
    in_specs += [pl.BlockSpec((tm, m.shape[1]), row) for m in mixes]
    in_specs += [_const_spec(w.shape) for w in w_os]
    in_specs += [_const_spec((1, d)), _const_spec((1, d)), _const_spec((N_EXPERTS, d)),
                 _const_spec((N_EXPERTS, 1))]
    out_specs = [pl.BlockSpec((tm, d), row), pl.BlockSpec((TOP_K, tm), colblk),
                 pl.BlockSpec((TOP_K, tm), colblk), pl.BlockSpec((TOP_K, tm), colblk),
                 _const_spec((N_EXPERTS, 1))]
    out_shape = [jax.ShapeDtypeStruct((n, d), F32), jax.ShapeDtypeStruct((TOP_K, n), jnp.int32),
                 jax.ShapeDtypeStruct((TOP_K, n), F32), jax.ShapeDtypeStruct((TOP_K, n), jnp.int32),
                 jax.ShapeDtypeStruct((N_EXPERTS, 1), jnp.int32)]
    scratch = [pltpu.VMEM(w.shape, BF16) for w in w_os]
    scratch += [pltpu.VMEM((3, N_EXPERTS, d), BF16), pltpu.VMEM((tm, tm), BF16),
                pltpu.VMEM((N_EXPERTS, 1), F32)]
    return pl.pallas_call(
        functools.partial(_post_attn_kernel, n_mix=n_mix, alpha=alpha),
        grid=(n // tm,),
        in_specs=in_specs, out_specs=out_specs, out_shape=out_shape, scratch_shapes=scratch,
        compiler_params=_cparams(1), name="post_attn",
    )(x2, *mixes, *w_os, ln_g.reshape(1, d), ln_b.reshape(1, d), w_router.T,
      b_router.reshape(N_EXPERTS, 1))


def _moe_kernel(be_ref, nu_ref, src_ref, x_hbm, wgu_ref, bgu_ref, wd_ref, bd_ref, y_ref,
                wgu_s, wd_s, xbuf, sem):
    i = pl.program_id(0)
    n_steps = pl.num_programs(0)
    tm = y_ref.shape[0]
    prev = be_ref[jnp.maximum(i - 1, 0)]
    n_used = nu_ref[0]
    live = i < n_used
    n_slots = xbuf.shape[0]
    ahead = n_slots - 1
    slot = i % n_slots

    def row_copy(block, s, r):
        tok = src_ref[block * tm + r]
        return pltpu.make_async_copy(x_hbm.at[pl.ds(tok, 1), :], xbuf.at[s, pl.ds(r, 1), :],
                                     sem.at[s])

    def wait_rows(s):
        pltpu.make_async_copy(xbuf.at[s], xbuf.at[s], sem.at[s]).wait()

    @pl.when(i == 0)
    def _():
        for b in range(ahead):
            def body(r, carry, b=b):
                row_copy(b, b, r).start(priority=1)
                return carry
            lax.fori_loop(0, tm, body, 0, unroll=8)

    @pl.when(live & ((i == 0) | (be_ref[i] != prev)))
    def _():
        wgu_s[...] = wgu_ref[...].astype(BF16)
        wd_s[...] = wd_ref[...].astype(BF16)

    @pl.when(live)
    def _():
        wait_rows(slot)
        xb = xbuf[slot].astype(BF16)
        nxt = jnp.minimum(i + ahead, n_steps - 1)
        for r in range(tm):
            row_copy(nxt, (i + ahead) % n_slots, r).start(priority=1)
        ch = 512
        acc = jnp.zeros(y_ref.shape, F32)
        for j in range(D_FF // ch):
            hg = _dot(xb, wgu_s[:, j * ch:(j + 1) * ch]) + bgu_ref[:, j * ch:(j + 1) * ch]
            hl = (_dot(xb, wgu_s[:, D_FF + j * ch:D_FF + (j + 1) * ch])
                  + bgu_ref[:, D_FF + j * ch:D_FF + (j + 1) * ch])
            glu = jnp.minimum(hg, SWIGLU_LIMIT)
            lin = jnp.clip(hl, -SWIGLU_LIMIT, SWIGLU_LIMIT)
            act = glu * jax.nn.sigmoid(SWIGLU_ALPHA * glu) * (lin + 1.0)
            acc = acc + _dot(act.astype(BF16), wd_s[j * ch:(j + 1) * ch, :])
        y_ref[...] = (acc + bd_ref[...]).astype(y_ref.dtype)

    @pl.when(i == n_used)
    def _():
        for b in range(ahead):
            wait_rows((i + b) % n_slots)

    @pl.when(jnp.logical_not(live))
    def _():
        y_ref[...] = jnp.zeros(y_ref.shape, y_ref.dtype)


def _moe_ffn(x2, src_tok, block_e, n_used, layer, w_gate_up, b_gate_up, w_down, b_down):
    d = x2.shape[1]
    rows = src_tok.shape[0]
    depth = w_gate_up.shape[0]
    n_blocks = rows // TM_MOE
    expert = lambda i, be, nu, src: (layer, be[i], 0, 0)
    grid_spec = pltpu.PrefetchScalarGridSpec(
        num_scalar_prefetch=3,
        grid=(n_blocks,),
        in_specs=[
            pl.BlockSpec(memory_space=pl.ANY),
            pl.BlockSpec((None, None, d, 2 * D_FF), expert),
            pl.BlockSpec((None, None, 1, 2 * D_FF), expert),
            pl.BlockSpec((None, None, D_FF, d), expert),
            pl.BlockSpec((None, None, 1, d), expert),
        ],
        out_specs=pl.BlockSpec((TM_MOE, d), lambda i, be, nu, src: (i, 0)),
        scratch_shapes=[pltpu.VMEM((d, 2 * D_FF), BF16), pltpu.VMEM((D_FF, d), BF16),
                        pltpu.VMEM((MOE_SLOTS, TM_MOE, d), F32),
                        pltpu.SemaphoreType.DMA((MOE_SLOTS,))],
    )
    return pl.pallas_call(
        _moe_kernel, grid_spec=grid_spec,
        out_shape=jax.ShapeDtypeStruct((rows, d), F32),
        compiler_params=_cparams(1), name="moe_ffn",
    )(block_e, n_used, src_tok, x2, w_gate_up, b_gate_up.reshape(depth, N_EXPERTS, 1, 2 * D_FF),
      w_down, b_down.reshape(depth, N_EXPERTS, 1, d))


def _post_moe_kernel(dest_ref, x_ref, y_hbm, gt_ref, g_ref, b_ref, p_ref, wg_ref, wp_ref, o_ref,
                     wg_s, wp_s, ybuf, sem, *, alpha, n_tok):
    i = pl.program_id(0)
    n_steps = pl.num_programs(0)
    tm, d = x_ref.shape

    def row_copy(tile, slot, k, j):
        r = dest_ref[k * n_tok + tile * tm + j]
        return pltpu.make_async_copy(y_hbm.at[pl.ds(r, 1), :], ybuf.at[slot, k, pl.ds(j, 1), :],
                                     sem.at[slot])

    @pl.when(i == 0)
    def _():
        wg_s[...] = wg_ref[...].astype(BF16)
        wp_s[...] = wp_ref[...].astype(BF16)

        def body(j, carry):
            for k in range(TOP_K):
                row_copy(0, 0, k, j).start()
            return carry
        lax.fori_loop(0, tm, body, 0, unroll=8)

    @pl.when(i + 1 < n_steps)
    def _():
        for j in range(tm):
            for k in range(TOP_K):
                row_copy(i + 1, (i + 1) % 2, k, j).start(priority=k % 2)

    slot = i % 2
    pltpu.make_async_copy(ybuf.at[slot], ybuf.at[slot], sem.at[slot]).wait()
    gates = gt_ref[...]
    ffn = gates[:, 0:1] * ybuf[slot, 0]
    for k in range(1, TOP_K):
        ffn = ffn + gates[:, k:k + 1] * ybuf[slot, k]
    xn = _layer_norm(alpha * x_ref[...] + ffn, g_ref[...], b_ref[...])
    gate = jax.nn.sigmoid(_dot(xn.astype(BF16), wg_s[...]))
    emb = _dot(p_ref[...].astype(BF16), wp_s[...])
    o_ref[...] = xn + gate * emb


def _post_moe(x2, y_pad, dest_flat, gates_t, ln_g, ln_b, layer, p3, w_ple_gate, w_ple_proj, alpha):
    n, d = x2.shape
    tm = TM_POST
    row = lambda i, dest: (i, 0)
    const2 = lambda i, dest: (0, 0)
    grid_spec = pltpu.PrefetchScalarGridSpec(
        num_scalar_prefetch=1,
        grid=(n // tm,),
        in_specs=[pl.BlockSpec((tm, d), row), pl.BlockSpec(memory_space=pl.ANY),
                  pl.BlockSpec((tm, TOP_K), row), pl.BlockSpec((1, d), const2),
                  pl.BlockSpec((1, d), const2),
                  pl.BlockSpec((None, tm, PLE_DIM), lambda i, dest: (layer, i, 0)),
                  pl.BlockSpec((None, d, d), lambda i, dest: (layer, 0, 0)),
                  pl.BlockSpec((None, PLE_DIM, d), lambda i, dest: (layer, 0, 0))],
        out_specs=pl.BlockSpec((tm, d), row),
        scratch_shapes=[pltpu.VMEM((d, d), BF16), pltpu.VMEM((PLE_DIM, d), BF16),
                        pltpu.VMEM((2, TOP_K, tm, d), F32), pltpu.SemaphoreType.DMA((2,))],
    )
    return pl.pallas_call(
        functools.partial(_post_moe_kernel, alpha=alpha, n_tok=n),
        grid_spec=grid_spec,
        out_shape=jax.ShapeDtypeStruct((n, d), F32),
        compiler_params=_cparams(1), name="post_moe",
    )(dest_flat, x2, y_pad, gates_t, ln_g.reshape(1, d), ln_b.reshape(1, d), p3, w_ple_gate,
      w_ple_proj)


def _t5_bucket(dist):
    d = jnp.maximum(dist, 0)
    ratio = jnp.log(jnp.maximum(d, 1).astype(F32) / MAX_EXACT) / math.log(MAX_DISTANCE / MAX_EXACT)
    large = MAX_EXACT + (ratio * (NUM_BUCKETS - MAX_EXACT)).astype(jnp.int32)
    large = jnp.minimum(large, NUM_BUCKETS - 1)
    return jnp.where(d < MAX_EXACT, d, large)


def _swa_bias_table(pos_bias):
    qi = jnp.arange(WINDOW)[:, None]
    kj = jnp.arange(2 * WINDOW)[None, :]
    dist = qi + WINDOW - kj
    valid = (dist >= 0) & (dist < WINDOW)
    bias = pos_bias[_t5_bucket(dist)][..., :A_HEADS].astype(F32).transpose(2, 0, 1)
    return jnp.where(valid[None], bias, NEG)


def _dsa_bias_tables(pos_bias):
    qi = jnp.arange(Q_BLOCK)[:, None]
    kj = jnp.arange(DSA_CK)[None, :]
    tabs = []
    for r in range(4):
        dist = r * Q_BLOCK + qi - kj
        tabs.append(pos_bias[_t5_bucket(dist)].astype(F32).transpose(2, 0, 1))
    return jnp.stack(tabs)


def _pad_cols(w, width=LANE):
    return jnp.pad(w, ((0, 0), (0, width - w.shape[1])))


def _even_mixer(x2, bsz, t, w_in, b_forget, sinks, w_o, swa_bias):
    hd = HEAD_DIM
    grp = A_HEADS // A_KV_HEADS
    pair_order = [g + grp * hh for g in range(grp) for hh in range(A_KV_HEADS)]
    na, nkv, nb = A_HEADS * hd, A_KV_HEADS * hd, B_HEADS * hd
    offs = [0, na, na + nkv, na + 2 * nkv, na + 2 * nkv + nb, na + 2 * nkv + 2 * nb,
            na + 2 * nkv + 3 * nb]
    w_aq = w_in[:, offs[0]:offs[1]].reshape(D_MODEL, A_HEADS, hd)[:, pair_order].reshape(D_MODEL, na)
    weights = [w_aq, w_in[:, offs[1]:offs[2]], w_in[:, offs[2]:offs[3]], w_in[:, offs[3]:offs[4]],
               w_in[:, offs[4]:offs[5]], w_in[:, offs[5]:offs[6]], _pad_cols(w_in[:, offs[6]:])]
    aq, ak, av, bq, bk, bv, bf = _project(x2, weights, [BF16] * 6 + [F32])
    o_a = _swa_attention(aq, ak, av, sinks, swa_bias, bsz, t)

    f_t = bf[:, :B_HEADS].reshape(bsz, t, B_HEADS).transpose(0, 2, 1)
    c = _forget_cumsum(f_t, b_forget)
    c_row = c.reshape(bsz, B_HEADS // 2, 2, t // FOX_TK, FOX_TK)
    c_col = c.reshape(bsz, B_HEADS // 2, 2, t).transpose(0, 1, 3, 2)
    o_b = _fox_attention(bq, bk, bv, c_col, c_row, bsz, t)

    w_oa = w_o[:na].reshape(A_HEADS, hd, D_MODEL)[jnp.asarray(pair_order)].reshape(na, D_MODEL)
    return [o_a, o_b], [w_oa, w_o[na:]]


def _odd_mixer(x2, bsz, t, w_in, kv_norm, w_uk, w_uv, w_o, dsa_tb):
    hd = HEAD_DIM
    nq, nl, ni = C_HEADS * hd, C_LATENT, IDX_HEADS * IDX_DIM
    w_ki = w_in[:, nq + nl + ni:nq + nl + ni + IDX_DIM]
    weights = [w_in[:, :nq], w_in[:, nq:nq + nl], w_in[:, nq + nl:nq + nl + ni],
               jnp.concatenate([w_ki, w_ki], axis=1), _pad_cols(w_in[:, nq + nl + ni + IDX_DIM:])]
    cq, ckv, qi, ki2, wi = _project(x2, weights, [BF16, BF16, BF16, BF16, F32],
                                    rms_gain=kv_norm.reshape(1, nl).astype(F32), rms_index=1)
    uk = w_uk.transpose(1, 2, 0)
    uv = w_uv.transpose(1, 0, 2)
    zk = jnp.zeros_like(uk)
    zv = jnp.zeros_like(uv)
    odd = (jnp.arange(C_HEADS) % 2 == 1)[:, None, None]
    wuk_wide = jnp.where(odd, jnp.concatenate([zk, uk], axis=1), jnp.concatenate([uk, zk], axis=1))
    wuv_wide = jnp.where(odd, jnp.concatenate([zv, uv], axis=2), jnp.concatenate([uv, zv], axis=2))
    o_c = _dsa_attention(cq, ckv, qi, ki2, wi, wuk_wide.astype(BF16), wuv_wide.astype(BF16),
                         dsa_tb, bsz, t)
    return [o_c], [w_o]


def _dispatch_plan(top_idx, rank, counts, n_tok):
    counts = counts.reshape(N_EXPERTS)
    padded = (counts + TM_MOE - 1) // TM_MOE * TM_MOE
    pad_ends = jnp.cumsum(padded)
    pad_starts = pad_ends - padded
    is_e = top_idx[..., None] == jnp.arange(N_EXPERTS, dtype=jnp.int32)
    dest = jnp.sum(jnp.where(is_e, pad_starts.astype(jnp.int32), 0), axis=-1) + rank
    n_blocks = (n_tok * TOP_K) // TM_MOE + N_EXPERTS
    block_start = jnp.arange(n_blocks, dtype=jnp.int32) * TM_MOE
    block_e = jnp.minimum(jnp.sum(pad_ends[None, :] <= block_start[:, None], axis=1),
                          N_EXPERTS - 1).astype(jnp.int32)
    n_used = (pad_ends[-1] // TM_MOE).astype(jnp.int32).reshape(1)
    return dest, block_e, n_used, n_blocks


def kernel(x, p, pos_bias, w_in_even, b_forget, sinks, w_o_even, w_in_odd, kv_norm, w_uk, w_uv,
           w_o_odd, ln_g, ln_b, w_router, b_router, w_gate_up, b_gate_up, w_down, b_down,
           w_ple_proj, w_ple_gate):
    bsz, t, d = x.shape
    depth = ln_g.shape[0]
    alpha = (2 * depth) ** 0.25
    n_tok = bsz * t
    x2 = x.reshape(n_tok, d)
    swa_bias = _swa_bias_table(pos_bias)
    dsa_tb = _dsa_bias_tables(pos_bias)
    p3 = p.reshape(depth, n_tok, PLE_DIM)
    tok_ids = jnp.tile(jnp.arange(n_tok, dtype=jnp.int32), (TOP_K,))
    for i in range(depth):
        j = i // 2
        if i % 2 == 0:
            mixes, w_os = _even_mixer(x2, bsz, t, w_in_even[j], b_forget[j], sinks[j], w_o_even[j],
                                      swa_bias)
        else:
            mixes, w_os = _odd_mixer(x2, bsz, t, w_in_odd[j], kv_norm[j], w_uk[j], w_uv[j],
                                     w_o_odd[j], dsa_tb)
        x1, top_idx, gates, rank, counts = _post_attn(x2, mixes, w_os, ln_g[i, 0], ln_b[i, 0],
                                                      w_router[i], b_router[i], alpha)
        dest, block_e, n_used, n_blocks = _dispatch_plan(top_idx, rank, counts, n_tok)
        dest_flat = dest.reshape(-1)
        src_tok = jnp.zeros((n_blocks * TM_MOE,), jnp.int32).at[dest_flat].set(tok_ids)
        y_pad = _moe_ffn(x1, src_tok, block_e, n_used, i, w_gate_up, b_gate_up, w_down, b_down)
        x2 = _post_moe(x1, y_pad, dest_flat, gates.T, ln_g[i, 1], ln_b[i, 1], i, p3,
                       w_ple_gate, w_ple_proj, alpha)
    return x2.reshape(bsz, t, d)
```

```python
import functools
import math

import jax
import jax.numpy as jnp
from jax import lax
from jax.experimental import pallas as pl
from jax.experimental.pallas import tpu as pltpu

D_MODEL = 1024
HEAD_DIM = 64
A_HEADS, A_KV_HEADS, WINDOW = 8, 2, 128
B_HEADS = 8
C_HEADS, C_LATENT = 16, 128
IDX_HEADS, IDX_DIM = 8, 64
TOPK_MAX = 256
NUM_BUCKETS, MAX_EXACT, MAX_DISTANCE = 32, 16, 128
N_EXPERTS, TOP_K, D_FF = 32, 4, 1024
SWIGLU_LIMIT, SWIGLU_ALPHA = 7.0, 1.702
PLE_DIM = 256
Q_BLOCK = 128
LN_EPS = 1e-5
NEG = -1e30

LANE = 128
VMEM_LIMIT = 56 * 1024 * 1024
TM_PROJ = 512
TM_MOE = 512
MOE_SLOTS = 3
TM_POST = 256
FOX_TQ = 512
FOX_TK = 2 * LANE
DSA_CK = 256
INT_MIN = -2 ** 31

BF16 = jnp.bfloat16
F32 = jnp.float32


def _cparams(n_axes):
    return pltpu.CompilerParams(dimension_semantics=("arbitrary",) * n_axes,
                                vmem_limit_bytes=VMEM_LIMIT)


def _const_spec(shape):
    nd = len(shape)
    return pl.BlockSpec(shape, lambda *_: (0,) * nd)


def _dot(a, b):
    return jnp.dot(a, b, preferred_element_type=F32)


def _dot_nt(a, b):
    return lax.dot_general(a, b, (((1,), (1,)), ((), ())), preferred_element_type=F32)


def _layer_norm(v, g, b):
    mu = jnp.mean(v, axis=-1, keepdims=True)
    d = v - mu
    var = jnp.mean(d * d, axis=-1, keepdims=True)
    return d * lax.rsqrt(var + LN_EPS) * g + b


def _split3(v):
    hi = v.astype(BF16)
    r1 = v - hi.astype(F32)
    mid = r1.astype(BF16)
    lo = (r1 - mid.astype(F32)).astype(BF16)
    return hi, mid, lo


def _proj_kernel(*refs, n_w, rms_index):
    x_ref = refs[0]
    w_refs = refs[1:1 + n_w]
    g_ref = refs[1 + n_w]
    o_refs = refs[2 + n_w:2 + 2 * n_w]
    wb_refs = refs[2 + 2 * n_w:]

    @pl.when(pl.program_id(0) == 0)
    def _():
        for w_ref, wb_ref in zip(w_refs, wb_refs):
            wb_ref[...] = w_ref[...].astype(BF16)

    xb = x_ref[...].astype(BF16)
    for i in range(n_w):
        h = _dot(xb, wb_refs[i][...])
        if i == rms_index:
            h = h * lax.rsqrt(jnp.mean(h * h, axis=-1, keepdims=True) + LN_EPS) * g_ref[...]
        o_refs[i][...] = h.astype(o_refs[i].dtype)


def _project(x2, weights, out_dtypes, rms_gain=None, rms_index=-1):
    n, d = x2.shape
    n_w = len(weights)
    if rms_gain is None:
        rms_gain = jnp.ones((1, LANE), F32)
    in_specs = [pl.BlockSpec((TM_PROJ, d), lambda i: (i, 0))]
    in_specs += [_const_spec(w.shape) for w in weights]
    in_specs += [_const_spec(rms_gain.shape)]
    out_specs = [pl.BlockSpec((TM_PROJ, w.shape[1]), lambda i: (i, 0)) for w in weights]
    out_shape = [jax.ShapeDtypeStruct((n, w.shape[1]), dt) for w, dt in zip(weights, out_dtypes)]
    scratch = [pltpu.VMEM(w.shape, BF16) for w in weights]
    return pl.pallas_call(
        functools.partial(_proj_kernel, n_w=n_w, rms_index=rms_index),
        grid=(n // TM_PROJ,),
        in_specs=in_specs, out_specs=out_specs, out_shape=out_shape,
        scratch_shapes=scratch, compiler_params=_cparams(1), name="in_proj",
    )(x2, *weights, rms_gain)


def _cumsum_kernel(f_ref, b_ref, o_ref):
    z = f_ref[...] + b_ref[...]
    ls = jnp.minimum(z, 0.0) - jnp.log1p(jnp.exp(-jnp.abs(z)))
    t = ls.shape[1]
    r = lax.broadcasted_iota(jnp.int32, (LANE, LANE), 0)
    c = lax.broadcasted_iota(jnp.int32, (LANE, LANE), 1)
    tri = jnp.where(r <= c, 1.0, 0.0).astype(BF16)
    carry = jnp.zeros((ls.shape[0], 1), F32)
    for blk in range(t // LANE):
        seg = ls[:, blk * LANE:(blk + 1) * LANE]
        hi, mid, lo = _split3(seg)
        pre = _dot(hi, tri) + _dot(mid, tri) + _dot(lo, tri) + carry
        o_ref[:, blk * LANE:(blk + 1) * LANE] = pre
        carry = pre[:, LANE - 1:LANE]


def _forget_cumsum(f_t, b_forget):
    bsz, h, t = f_t.shape
    return pl.pallas_call(
        _cumsum_kernel,
        grid=(bsz,),
        in_specs=[pl.BlockSpec((None, h, t), lambda b: (b, 0, 0)), _const_spec((h, 1))],
        out_specs=pl.BlockSpec((None, h, t), lambda b: (b, 0, 0)),
        out_shape=jax.ShapeDtypeStruct((bsz, h, t), F32),
        compiler_params=_cparams(1), name="forget_cumsum",
    )(f_t, b_forget.reshape(h, 1).astype(F32))


def _fox_kernel(q_ref, k_ref, v_ref, cq_ref, ck_ref, o_ref, s_all, m_s, l_s, acc_s):
    i = pl.program_id(2)
    tq = q_ref.shape[0]
    tk = FOX_TK
    per_q = tq // tk
    q = q_ref[...]
    lane = lax.broadcasted_iota(jnp.int32, (tq, LANE), 1)
    row_minus_col = (lax.broadcasted_iota(jnp.int32, (tq, tk), 0)
                     - lax.broadcasted_iota(jnp.int32, (tq, tk), 1))
    heads = range(2)
    qh, cq = [], []
    for hh in heads:
        in_half = (lane >= hh * HEAD_DIM) & (lane < (hh + 1) * HEAD_DIM)
        qh.append(jnp.where(in_half, q, jnp.zeros_like(q)) * jnp.asarray(HEAD_DIM ** -0.5, BF16))
        cq.append(cq_ref[:, hh:hh + 1])

    def logits(j, hh):
        start = pl.multiple_of(j * tk, tk)
        return _dot_nt(qh[hh], k_ref[pl.ds(start, tk), :]) + cq[hh] - ck_ref[hh, pl.ds(j, 1), :]

    def keep(j, hh, s):
        s_all[hh, j] = s
        m_s[hh] = jnp.maximum(m_s[hh], jnp.maximum(s[:, :LANE], s[:, LANE:]))

    def logit_body(j, carry):
        for hh in heads:
            keep(j, hh, logits(j, hh))
        return carry

    m_s[...] = jnp.full(m_s.shape, NEG, F32)
    lax.fori_loop(0, i * per_q, logit_body, 0)
    for dd in range(per_q):
        j = i * per_q + dd
        causal = row_minus_col >= dd * tk
        for hh in heads:
            keep(j, hh, jnp.where(causal, logits(j, hh), NEG))
    m_row = [jnp.broadcast_to(jnp.max(m_s[hh], axis=-1, keepdims=True), (tq, LANE)) for hh in heads]
    l_s[...] = jnp.zeros(l_s.shape, F32)
    acc_s[...] = jnp.zeros(acc_s.shape, F32)

    def prob_body(j, carry):
        start = pl.multiple_of(j * tk, tk)
        vs = v_ref[pl.ds(start, tk), :]
        for hh in heads:
            p_lo = jnp.exp(s_all[hh, j, :, :LANE] - m_row[hh])
            p_hi = jnp.exp(s_all[hh, j, :, LANE:] - m_row[hh])
            l_s[hh] = l_s[hh] + p_lo + p_hi
            p = jnp.concatenate([p_lo, p_hi], axis=1).astype(BF16)
            acc_s[hh] = acc_s[hh] + _dot(p, vs)
        return carry

    lax.fori_loop(0, (i + 1) * per_q, prob_body, 0)
    outs = [acc_s[hh] / jnp.sum(l_s[hh], axis=-1, keepdims=True) for hh in heads]
    o_ref[...] = jnp.where(lane < HEAD_DIM, outs[0], outs[1]).astype(o_ref.dtype)


def _fox_attention(q, k, v, c_col, c_row, bsz, t):
    n, hd = q.shape
    n_pair = hd // LANE
    nq = t // FOX_TQ
    nk = t // FOX_TK
    return pl.pallas_call(
        _fox_kernel,
        grid=(bsz, n_pair, nq),
        in_specs=[
            pl.BlockSpec((FOX_TQ, LANE), lambda b, j, i: (b * nq + i, j)),
            pl.BlockSpec((t, LANE), lambda b, j, i: (b, j)),
            pl.BlockSpec((t, LANE), lambda b, j, i: (b, j)),
            pl.BlockSpec((None, None, FOX_TQ, 2), lambda b, j, i: (b, j, i, 0)),
            pl.BlockSpec((None, None, 2, nk, FOX_TK), lambda b, j, i: (b, j, 0, 0, 0)),
        ],
        out_specs=pl.BlockSpec((FOX_TQ, LANE), lambda b, j, i: (b * nq + i, j)),
        out_shape=jax.ShapeDtypeStruct((n, hd), BF16),
        scratch_shapes=[pltpu.VMEM((2, nk, FOX_TQ, FOX_TK), F32), pltpu.VMEM((2, FOX_TQ, LANE), F32),
                        pltpu.VMEM((2, FOX_TQ, LANE), F32), pltpu.VMEM((2, FOX_TQ, LANE), F32)],
        compiler_params=_cparams(3), name="fox_attention",
    )(q, k, v, c_col, c_row)


def _swa_kernel(sink_ref, q_ref, kp_ref, kc_ref, vp_ref, vc_ref, bias_ref, o_ref):
    nblk = pl.program_id(1)
    w = q_ref.shape[0]
    kk = jnp.concatenate([kp_ref[...], kc_ref[...]], axis=0)
    vv = jnp.concatenate([vp_ref[...], vc_ref[...]], axis=0)
    lane = lax.broadcasted_iota(jnp.int32, (w, LANE), 1)
    col = lax.broadcasted_iota(jnp.int32, (w, 2 * w), 1)
    no_prev = (col < w) & (nblk == 0)
    grp = A_HEADS // A_KV_HEADS
    for g in range(grp):
        qg = q_ref[:, g * LANE:(g + 1) * LANE]
        outs = []
        for hh in range(A_KV_HEADS):
            head = g + grp * hh
            in_half = (lane >= hh * HEAD_DIM) & (lane < (hh + 1) * HEAD_DIM)
            qh = jnp.where(in_half, qg, jnp.zeros_like(qg)) * jnp.asarray(HEAD_DIM ** -0.5, BF16)
            s = _dot_nt(qh, kk) + bias_ref[head]
            s = jnp.where(no_prev, NEG, s)
            sink = sink_ref[head]
            m = jnp.maximum(jnp.max(s, axis=-1, keepdims=True), sink)
            e = jnp.exp(s - m)
            denom = jnp.sum(e, axis=-1, keepdims=True) + jnp.exp(sink - m)
            p = e / denom
            outs.append(_dot(p.astype(BF16), vv))
        o_ref[:, g * LANE:(g + 1) * LANE] = jnp.where(lane < HEAD_DIM, outs[0], outs[1]).astype(o_ref.dtype)


def _swa_attention(q, k, v, sinks, bias, bsz, t):
    n = q.shape[0]
    nb = t // WINDOW
    cur = lambda b, i: (b * nb + i, 0)
    prev = lambda b, i: (b * nb + jnp.maximum(i - 1, 0), 0)
    return pl.pallas_call(
        _swa_kernel,
        grid=(bsz, nb),
        in_specs=[
            pl.BlockSpec(memory_space=pltpu.SMEM),
            pl.BlockSpec((WINDOW, A_HEADS * HEAD_DIM), cur),
            pl.BlockSpec((WINDOW, LANE), prev), pl.BlockSpec((WINDOW, LANE), cur),
            pl.BlockSpec((WINDOW, LANE), prev), pl.BlockSpec((WINDOW, LANE), cur),
            _const_spec(bias.shape),
        ],
        out_specs=pl.BlockSpec((WINDOW, A_HEADS * HEAD_DIM), cur),
        out_shape=jax.ShapeDtypeStruct((n, A_HEADS * HEAD_DIM), BF16),
        compiler_params=_cparams(2), name="swa_attention",
    )(sinks.astype(F32), q, k, k, v, v, bias)


def _float_order_key(s):
    bits = pltpu.bitcast(s, jnp.int32)
    return bits ^ ((bits >> 31) & jnp.int32(0x7FFFFFFF))


def _dsa_kernel(q_ref, ckv_ref, qi_ref, ki_ref, wi_ref, wuk_ref, wuv_ref, tb_ref, o_ref,
                key_s, sel_s, qlat_s, qis_s, s_all, p_buf, acc_s, m_s, l_s):
    n = pl.program_id(1)
    qb = Q_BLOCK
    ck = DSA_CK
    n_sel = TOPK_MAX
    n_chunks = n // (ck // qb) + 1
    lane = lax.broadcasted_iota(jnp.int32, (qb, LANE), 1)
    qpos = n * qb + lax.broadcasted_iota(jnp.int32, (qb, ck), 0)
    kcol = lax.broadcasted_iota(jnp.int32, (qb, ck), 1)

    for h in range(IDX_HEADS):
        pair = qi_ref[:, (h // 2) * LANE:(h // 2 + 1) * LANE]
        in_half = (lane >= (h % 2) * IDX_DIM) & (lane < (h % 2 + 1) * IDX_DIM)
        qis_s[h * qb:(h + 1) * qb, :] = jnp.where(in_half, pair, jnp.zeros_like(pair))

    for h in range(C_HEADS):
        pair = q_ref[:, (h // 2) * LANE:(h // 2 + 1) * LANE]
        ql = _dot(pair, wuk_ref[h]) * (HEAD_DIM ** -0.5)
        qlat_s[h * qb:(h + 1) * qb, :] = ql.astype(BF16)

    wi = wi_ref[...] * (IDX_HEADS ** -0.5)

    def score_body(c, _):
        start = pl.multiple_of(c * ck, ck)
        kc = ki_ref[pl.ds(start, ck), :]
        raw = _dot_nt(qis_s[...], kc)
        score = jnp.zeros((qb, ck), F32)
        for h in range(IDX_HEADS):
            idx_s = jnp.maximum(raw[h * qb:(h + 1) * qb, :] * (IDX_DIM ** -0.5), 0.0)
            score = score + idx_s * wi[:, h:h + 1]
        causal = (start + kcol) <= qpos
        key_s[c] = _float_order_key(jnp.where(causal, score, NEG))
        sel_s[c] = jnp.where(causal, 0.0, NEG)
        return 0

    lax.fori_loop(0, n_chunks, score_body, 0)

    @pl.when((n + 1) * qb > n_sel)
    def _():
        def count_where(pred_fn):
            def body(c, acc):
                hit = jnp.where(pred_fn(key_s[c]), 1.0, 0.0)
                return acc + hit[:, :LANE] + hit[:, LANE:]
            acc = lax.fori_loop(0, n_chunks, body, jnp.zeros((qb, LANE), F32))
            return jnp.sum(acc, axis=-1, keepdims=True)

        cnt0 = count_where(lambda kv: kv >= 0)
        prefix0 = jnp.where(cnt0 >= n_sel, jnp.int32(0), jnp.int32(INT_MIN))

        def bit_body(t, prefix):
            cand = prefix | jnp.left_shift(jnp.int32(1), 30 - t)
            cnt = count_where(lambda kv: kv >= cand)
            return jnp.where(cnt >= n_sel, cand, prefix)

        thr = lax.fori_loop(0, 31, bit_body, prefix0)
        need = n_sel - count_where(lambda kv: kv > thr)
        r = lax.broadcasted_iota(jnp.int32, (ck, ck), 0)
        cc = lax.broadcasted_iota(jnp.int32, (ck, ck), 1)
        before = jnp.where(r < cc, 1.0, 0.0).astype(BF16)

        def tie_body(c, seen):
            kv = key_s[c]
            tie = kv == thr
            tie_f = jnp.where(tie, 1.0, 0.0)
            earlier = _dot(tie_f.astype(BF16), before) + seen
            sel = (kv > thr) | (tie & (earlier < need))
            sel_s[c] = jnp.where(sel, sel_s[c], NEG)
            return seen + jnp.sum(tie_f, axis=-1, keepdims=True)

        lax.fori_loop(0, n_chunks, tie_body, jnp.zeros((qb, 1), F32))

    m_s[...] = jnp.full(m_s.shape, NEG, F32)

    def logit_body(c, _):
        start = pl.multiple_of(c * ck, ck)
        raw = _dot_nt(qlat_s[...], ckv_ref[pl.ds(start, ck), :])
        rel_blk = jnp.minimum(n - c * (ck // qb), 3)
        mask_add = sel_s[c]
        for h in range(C_HEADS):
            rows = slice(h * qb, (h + 1) * qb)
            s = raw[rows, :] + tb_ref[rel_blk, h] + mask_add
            s_all[c, rows, :] = s
            m_s[rows, :] = jnp.maximum(m_s[rows, :], jnp.maximum(s[:, :LANE], s[:, LANE:]))
        return 0

    lax.fori_loop(0, n_chunks, logit_body, 0)

    for h in range(C_HEADS):
        rows = slice(h * qb, (h + 1) * qb)
        m_s[rows, :] = jnp.broadcast_to(jnp.max(m_s[rows, :], axis=-1, keepdims=True), (qb, LANE))
    l_s[...] = jnp.zeros(l_s.shape, F32)
    acc_s[...] = jnp.zeros(acc_s.shape, F32)

    def prob_body(c, _):
        start = pl.multiple_of(c * ck, ck)
        for h in range(C_HEADS):
            rows = slice(h * qb, (h + 1) * qb)
            m_row = m_s[rows, :]
            p_lo = jnp.exp(s_all[c, rows, :LANE] - m_row)
            p_hi = jnp.exp(s_all[c, rows, LANE:] - m_row)
            l_s[rows, :] = l_s[rows, :] + p_lo + p_hi
            p_buf[rows, :LANE] = p_lo.astype(BF16)
            p_buf[rows, LANE:] = p_hi.astype(BF16)
        acc_s[...] = acc_s[...] + _dot(p_buf[...], ckv_ref[pl.ds(start, ck), :])
        return 0

    lax.fori_loop(0, n_chunks, prob_body, 0)

    for g in range(C_HEADS // 2):
        out = jnp.zeros((qb, LANE), F32)
        for hh in range(2):
            h = 2 * g + hh
            rows = slice(h * qb, (h + 1) * qb)
            denom = jnp.sum(l_s[rows, :], axis=-1, keepdims=True)
            o_lat = (acc_s[rows, :] / denom).astype(BF16)
            out = out + _dot(o_lat, wuv_ref[h])
        o_ref[:, g * LANE:(g + 1) * LANE] = out.astype(o_ref.dtype)


def _dsa_attention(q, ckv, qi, ki2, wi, wuk_wide, wuv_wide, tb, bsz, t):
    n = q.shape[0]
    nq = t // Q_BLOCK
    nck = t // DSA_CK
    rows = C_HEADS * Q_BLOCK
    blk = lambda b, i: (b * nq + i, 0)
    per_b = lambda b, i: (b, 0)
    return pl.pallas_call(
        _dsa_kernel,
        grid=(bsz, nq),
        in_specs=[
            pl.BlockSpec((Q_BLOCK, C_HEADS * HEAD_DIM), blk),
            pl.BlockSpec((t, C_LATENT), per_b),
            pl.BlockSpec((Q_BLOCK, IDX_HEADS * IDX_DIM), blk),
            pl.BlockSpec((t, LANE), per_b),
            pl.BlockSpec((Q_BLOCK, LANE), blk),
            _const_spec(wuk_wide.shape), _const_spec(wuv_wide.shape), _const_spec(tb.shape),
        ],
        out_specs=pl.BlockSpec((Q_BLOCK, C_HEADS * HEAD_DIM), blk),
        out_shape=jax.ShapeDtypeStruct((n, C_HEADS * HEAD_DIM), BF16),
        scratch_shapes=[
            pltpu.VMEM((nck, Q_BLOCK, DSA_CK), jnp.int32),
            pltpu.VMEM((nck, Q_BLOCK, DSA_CK), F32),
            pltpu.VMEM((rows, C_LATENT), BF16),
            pltpu.VMEM((IDX_HEADS * Q_BLOCK, LANE), BF16),
            pltpu.VMEM((nck, rows, DSA_CK), F32),
            pltpu.VMEM((rows, DSA_CK), BF16),
            pltpu.VMEM((rows, C_LATENT), F32),
            pltpu.VMEM((rows, LANE), F32),
            pltpu.VMEM((rows, LANE), F32),
        ],
        compiler_params=_cparams(2), name="dsa_attention",
    )(q, ckv, qi, ki2, wi, wuk_wide, wuv_wide, tb)


def _post_attn_kernel(*refs, n_mix, alpha):
    x_ref = refs[0]
    mix_refs = refs[1:1 + n_mix]
    w_refs = refs[1 + n_mix:1 + 2 * n_mix]
    g_ref, b_ref, wr_ref, br_ref = refs[1 + 2 * n_mix:5 + 2 * n_mix]
    xo_ref, idx_ref, gate_ref, rank_ref, cnt_ref = refs[5 + 2 * n_mix:10 + 2 * n_mix]
    scr = refs[10 + 2 * n_mix:]
    wb_refs = scr[:n_mix]
    wr3_ref, before_ref, carry_ref = scr[n_mix:]
    tm = x_ref.shape[0]

    @pl.when(pl.program_id(0) == 0)
    def _():
        for w_ref, wb_ref in zip(w_refs, wb_refs):
            wb_ref[...] = w_ref[...].astype(BF16)
        hi, mid, lo = _split3(wr_ref[...])
        wr3_ref[0] = hi
        wr3_ref[1] = mid
        wr3_ref[2] = lo
        r = lax.broadcasted_iota(jnp.int32, (tm, tm), 0)
        c = lax.broadcasted_iota(jnp.int32, (tm, tm), 1)
        before_ref[...] = jnp.where(r < c, 1.0, 0.0).astype(BF16)
        carry_ref[...] = jnp.zeros(carry_ref.shape, F32)

    mix = _dot(mix_refs[0][...], wb_refs[0][...])
    for i in range(1, n_mix):
        mix = mix + _dot(mix_refs[i][...], wb_refs[i][...])
    xn = _layer_norm(alpha * x_ref[...] + mix, g_ref[...], b_ref[...])
    xo_ref[...] = xn

    xh, xm, xl = _split3(xn)
    logits = (_dot_nt(wr3_ref[0], xh) + _dot_nt(wr3_ref[0], xm) + _dot_nt(wr3_ref[1], xh)
              + _dot_nt(wr3_ref[0], xl) + _dot_nt(wr3_ref[1], xm) + _dot_nt(wr3_ref[2], xh)
              + br_ref[...])
    eidx = lax.broadcasted_iota(jnp.int32, logits.shape, 0)
    cur = logits
    vals, idxs = [], []
    for _ in range(TOP_K):
        mx = jnp.max(cur, axis=0, keepdims=True)
        first = jnp.min(jnp.where(cur == mx, eidx, N_EXPERTS), axis=0, keepdims=True)
        vals.append(mx)
        idxs.append(first)
        cur = jnp.where(eidx == first, -jnp.inf, cur)
    es = [jnp.exp(v - vals[0]) for v in vals]
    tot = es[0] + es[1] + es[2] + es[3]
    member = jnp.zeros(logits.shape, F32)
    for k in range(TOP_K):
        idx_ref[k:k + 1, :] = idxs[k]
        gate_ref[k:k + 1, :] = es[k] / tot
        member = member + jnp.where(eidx == idxs[k], 1.0, 0.0)
    earlier = _dot(member.astype(BF16), before_ref[...]) + carry_ref[...]
    for k in range(TOP_K):
        rk = jnp.sum(jnp.where(eidx == idxs[k], earlier, 0.0), axis=0, keepdims=True)
        rank_ref[k:k + 1, :] = rk.astype(jnp.int32)
    carry_ref[...] = carry_ref[...] + jnp.sum(member, axis=1, keepdims=True)
    cnt_ref[...] = carry_ref[...].astype(jnp.int32)


def _post_attn(x2, mixes, w_os, ln_g, ln_b, w_router, b_router, alpha):
    n, d = x2.shape
    n_mix = len(mixes)
    tm = TM_PROJ
    row = lambda i: (i, 0)
    colblk = lambda i: (0, i)
    in_specs = [pl.BlockSpec((tm, d), row)]
    in_specs += [pl.BlockSpec((tm, m.shape[1]), row) for m in mixes]
    in_specs += [_const_spec(w.shape) for w in w_os]
    in_specs += [_const_spec((1, d)), _const_spec((1, d)), _const_spec((N_EXPERTS, d)),
                 _const_spec((N_EXPERTS, 1))]
    out_specs = [pl.BlockSpec((tm, d), row), pl.BlockSpec((TOP_K, tm), colblk),
                 pl.BlockSpec((TOP_K, tm), colblk), pl.BlockSpec((TOP_K, tm), colblk),
                 _const_spec((N_EXPERTS, 1))]
    out_shape = [jax.ShapeDtypeStruct((n, d), F32), jax.ShapeDtypeStruct((TOP_K, n), jnp.int32),
                 jax.ShapeDtypeStruct((TOP_K, n), F32), jax.ShapeDtypeStruct((TOP_K, n), jnp.int32),
                 jax.ShapeDtypeStruct((N_EXPERTS, 1), jnp.int32)]
    scratch = [pltpu.VMEM(w.shape, BF16) for w in w_os]
    scratch += [pltpu.VMEM((3, N_EXPERTS, d), BF16), pltpu.VMEM((tm, tm), BF16),
                pltpu.VMEM((N_EXPERTS, 1), F32)]
    return pl.pallas_call(
        functools.partial(_post_attn_kernel, n_mix=n_mix, alpha=alpha),
        grid=(n // tm,),
        in_specs=in_specs, out_specs=out_specs, out_shape=out_shape, scratch_shapes=scratch,
        compiler_params=_cparams(1), name="post_attn",
    )(x2, *mixes, *w_os, ln_g.reshape(1, d), ln_b.reshape(1, d), w_router.T,
      b_router.reshape(N_EXPERTS, 1))


def _moe_kernel(be_ref, nu_ref, src_ref, x_hbm, wgu_ref, bgu_ref, wd_ref, bd_ref, y_ref,
                wgu_s, wd_s, xbuf, sem):
    i = pl.program_id(0)
    n_steps = pl.num_programs(0)
    tm = y_ref.shape[0]
    prev = be_ref[jnp.maximum(i - 1, 0)]
    n_used = nu_ref[0]
    live = i < n_used
    n_slots = xbuf.shape[0]
    ahead = n_slots - 1
    slot = i % n_slots

    def row_copy(block, s, r):
        tok = src_ref[block * tm + r]
        return pltpu.make_async_copy(x_hbm.at[pl.ds(tok, 1), :], xbuf.at[s, pl.ds(r, 1), :],
                                     sem.at[s])

    def wait_rows(s):
        pltpu.make_async_copy(xbuf.at[s], xbuf.at[s], sem.at[s]).wait()

    @pl.when(i == 0)
    def _():
        for b in range(ahead):
            def body(r, carry, b=b):
                row_copy(b, b, r).start(priority=1)
                return carry
            lax.fori_loop(0, tm, body, 0, unroll=8)

    @pl.when(live & ((i == 0) | (be_ref[i] != prev)))
    def _():
        wgu_s[...] = wgu_ref[...].astype(BF16)
        wd_s[...] = wd_ref[...].astype(BF16)

    @pl.when(live)
    def _():
        wait_rows(slot)
        xb = xbuf[slot].astype(BF16)
        nxt = jnp.minimum(i + ahead, n_steps - 1)
        for r in range(tm):
            row_copy(nxt, (i + ahead) % n_slots, r).start(priority=1)
        ch = 512
        acc = jnp.zeros(y_ref.shape, F32)
        for j in range(D_FF // ch):
            hg = _dot(xb, wgu_s[:, j * ch:(j + 1) * ch]) + bgu_ref[:, j * ch:(j + 1) * ch]
            hl = (_dot(xb, wgu_s[:, D_FF + j * ch:D_FF + (j + 1) * ch])
                  + bgu_ref[:, D_FF + j * ch:D_FF + (j + 1) * ch])
            glu = jnp.minimum(hg, SWIGLU_LIMIT)
            lin = jnp.clip(hl, -SWIGLU_LIMIT, SWIGLU_LIMIT)
            act = glu * jax.nn.sigmoid(SWIGLU_ALPHA * glu) * (lin + 1.0)
            acc = acc + _dot(act.astype(BF16), wd_s[j * ch:(j + 1) * ch, :])
        y_ref[...] = (acc + bd_ref[...]).astype(y_ref.dtype)

    @pl.when(i == n_used)
    def _():
        for b in range(ahead):
            wait_rows((i + b) % n_slots)

    @pl.when(jnp.logical_not(live))
    def _():
        y_ref[...] = jnp.zeros(y_ref.shape, y_ref.dtype)


def _moe_ffn(x2, src_tok, block_e, n_used, layer, w_gate_up, b_gate_up, w_down, b_down):
    d = x2.shape[1]
    rows = src_tok.shape[0]
    depth = w_gate_up.shape[0]
    n_blocks = rows // TM_MOE
    expert = lambda i, be, nu, src: (layer, be[i], 0, 0)
    grid_spec = pltpu.PrefetchScalarGridSpec(
        num_scalar_prefetch=3,
        grid=(n_blocks,),
        in_specs=[
            pl.BlockSpec(memory_space=pl.ANY),
            pl.BlockSpec((None, None, d, 2 * D_FF), expert),
            pl.BlockSpec((None, None, 1, 2 * D_FF), expert),
            pl.BlockSpec((None, None, D_FF, d), expert),
            pl.BlockSpec((None, None, 1, d), expert),
        ],
        out_specs=pl.BlockSpec((TM_MOE, d), lambda i, be, nu, src: (i, 0)),
        scratch_shapes=[pltpu.VMEM((d, 2 * D_FF), BF16), pltpu.VMEM((D_FF, d), BF16),
                        pltpu.VMEM((MOE_SLOTS, TM_MOE, d), F32),
                        pltpu.SemaphoreType.DMA((MOE_SLOTS,))],
    )
    return pl.pallas_call(
        _moe_kernel, grid_spec=grid_spec,
        out_shape=jax.ShapeDtypeStruct((rows, d), F32),
        compiler_params=_cparams(1), name="moe_ffn",
    )(block_e, n_used, src_tok, x2, w_gate_up, b_gate_up.reshape(depth, N_EXPERTS, 1, 2 * D_FF),
      w_down, b_down.reshape(depth, N_EXPERTS, 1, d))


def _post_moe_kernel(dest_ref, x_ref, y_hbm, gt_ref, g_ref, b_ref, p_ref, wg_ref, wp_ref, o_ref,
                     wg_s, wp_s, ybuf, sem, *, alpha, n_tok):
    i = pl.program_id(0)
    n_steps = pl.num_programs(0)
    tm, d = x_ref.shape

    def row_copy(tile, slot, k, j):
        r = dest_ref[k * n_tok + tile * tm + j]
        return pltpu.make_async_copy(y_hbm.at[pl.ds(r, 1), :], ybuf.at[slot, k, pl.ds(j, 1), :],
                                     sem.at[slot])

    @pl.when(i == 0)
    def _():
        wg_s[...] = wg_ref[...].astype(BF16)
        wp_s[...] = wp_ref[...].astype(BF16)

        def body(j, carry):
            for k in range(TOP_K):
                row_copy(0, 0, k, j).start()
            return carry
        lax.fori_loop(0, tm, body, 0, unroll=8)

    @pl.when(i + 1 < n_steps)
    def _():
        for j in range(tm):
            for k in range(TOP_K):
                row_copy(i + 1, (i + 1) % 2, k, j).start(priority=k % 2)

    slot = i % 2
    pltpu.make_async_copy(ybuf.at[slot], ybuf.at[slot], sem.at[slot]).wait()
    gates = gt_ref[...]
    ffn = gates[:, 0:1] * ybuf[slot, 0]
    for k in range(1, TOP_K):
        ffn = ffn + gates[:, k:k + 1] * ybuf[slot, k]
    xn = _layer_norm(alpha * x_ref[...] + ffn, g_ref[...], b_ref[...])
    gate = jax.nn.sigmoid(_dot(xn.astype(BF16), wg_s[...]))
    emb = _dot(p_ref[...].astype(BF16), wp_s[...])
    o_ref[...] = xn + gate * emb


def _post_moe(x2, y_pad, dest_flat, gates_t, ln_g, ln_b, layer, p3, w_ple_gate, w_ple_proj, alpha):
    n, d = x2.shape
    tm = TM_POST
    row = lambda i, dest: (i, 0)
    const2 = lambda i, dest: (0, 0)
    grid_spec = pltpu.PrefetchScalarGridSpec(
        num_scalar_prefetch=1,
        grid=(n // tm,),
        in_specs=[pl.BlockSpec((tm, d), row), pl.BlockSpec(memory_space=pl.ANY),
                  pl.BlockSpec((tm, TOP_K), row), pl.BlockSpec((1, d), const2),
                  pl.BlockSpec((1, d), const2),
                  pl.BlockSpec((None, tm, PLE_DIM), lambda i, dest: (layer, i, 0)),
                  pl.BlockSpec((None, d, d), lambda i, dest: (layer, 0, 0)),
                  pl.BlockSpec((None, PLE_DIM, d), lambda i, dest: (layer, 0, 0))],
        out_specs=pl.BlockSpec((tm, d), row),
        scratch_shapes=[pltpu.VMEM((d, d), BF16), pltpu.VMEM((PLE_DIM, d), BF16),
                        pltpu.VMEM((2, TOP_K, tm, d), F32), pltpu.SemaphoreType.DMA((2,))],
    )
    return pl.pallas_call(
        functools.partial(_post_moe_kernel, alpha=alpha, n_tok=n),
        grid_spec=grid_spec,
        out_shape=jax.ShapeDtypeStruct((n, d), F32),
        compiler_params=_cparams(1), name="post_moe",
    )(dest_flat, x2, y_pad, gates_t, ln_g.reshape(1, d), ln_b.reshape(1, d), p3, w_ple_gate,
      w_ple_proj)


def _t5_bucket(dist):
    d = jnp.maximum(dist, 0)
    ratio = jnp.log(jnp.maximum(d, 1).astype(F32) / MAX_EXACT) / math.log(MAX_DISTANCE / MAX_EXACT)
    large = MAX_EXACT + (ratio * (NUM_BUCKETS - MAX_EXACT)).astype(jnp.int32)
    large = jnp.minimum(large, NUM_BUCKETS - 1)
    return jnp.where(d < MAX_EXACT, d, large)


def _swa_bias_table(pos_bias):
    qi = jnp.arange(WINDOW)[:, None]
    kj = jnp.arange(2 * WINDOW)[None, :]
    dist = qi + WINDOW - kj
    valid = (dist >= 0) & (dist < WINDOW)
    bias = pos_bias[_t5_bucket(dist)][..., :A_HEADS].astype(F32).transpose(2, 0, 1)
    return jnp.where(valid[None], bias, NEG)


def _dsa_bias_tables(pos_bias):
    qi = jnp.arange(Q_BLOCK)[:, None]
    kj = jnp.arange(DSA_CK)[None, :]
    tabs = []
    for r in range(4):
        dist = r * Q_BLOCK + qi - kj
        tabs.append(pos_bias[_t5_bucket(dist)].astype(F32).transpose(2, 0, 1))
    return jnp.stack(tabs)


def _pad_cols(w, width=LANE):
    return jnp.pad(w, ((0, 0), (0, width - w.shape[1])))


def _even_mixer(x2, bsz, t, w_in, b_forget, sinks, w_o, swa_bias):
    hd = HEAD_DIM
    grp = A_HEADS // A_KV_HEADS
    pair_order = [g + grp * hh for g in range(grp) for hh in range(A_KV_HEADS)]
    na, nkv, nb = A_HEADS * hd, A_KV_HEADS * hd, B_HEADS * hd
    offs = [0, na, na + nkv, na + 2 * nkv, na + 2 * nkv + nb, na + 2 * nkv + 2 * nb,
            na + 2 * nkv + 3 * nb]
    w_aq = w_in[:, offs[0]:offs[1]].reshape(D_MODEL, A_HEADS, hd)[:, pair_order].reshape(D_MODEL, na)
    weights = [w_aq, w_in[:, offs[1]:offs[2]], w_in[:, offs[2]:offs[3]], w_in[:, offs[3]:offs[4]],
               w_in[:, offs[4]:offs[5]], w_in[:, offs[5]:offs[6]], _pad_cols(w_in[:, offs[6]:])]
    aq, ak, av, bq, bk, bv, bf = _project(x2, weights, [BF16] * 6 + [F32])
    o_a = _swa_attention(aq, ak, av, sinks, swa_bias, bsz, t)

    f_t = bf[:, :B_HEADS].reshape(bsz, t, B_HEADS).transpose(0, 2, 1)
    c = _forget_cumsum(f_t, b_forget)
    c_row = c.reshape(bsz, B_HEADS // 2, 2, t // FOX_TK, FOX_TK)
    c_col = c.reshape(bsz, B_HEADS // 2, 2, t).transpose(0, 1, 3, 2)
    o_b = _fox_attention(bq, bk, bv, c_col, c_row, bsz, t)

    w_oa = w_o[:na].reshape(A_HEADS, hd, D_MODEL)[jnp.asarray(pair_order)].reshape(na, D_MODEL)
    return [o_a, o_b], [w_oa, w_o[na:]]


def _odd_mixer(x2, bsz, t, w_in, kv_norm, w_uk, w_uv, w_o, dsa_tb):
    hd = HEAD_DIM
    nq, nl, ni = C_HEADS * hd, C_LATENT, IDX_HEADS * IDX_DIM
    w_ki = w_in[:, nq + nl + ni:nq + nl + ni + IDX_DIM]
    weights = [w_in[:, :nq], w_in[:, nq:nq + nl], w_in[:, nq + nl:nq + nl + ni],
               jnp.concatenate([w_ki, w_ki], axis=1), _pad_cols(w_in[:, nq + nl + ni + IDX_DIM:])]
    cq, ckv, qi, ki2, wi = _project(x2, weights, [BF16, BF16, BF16, BF16, F32],
                                    rms_gain=kv_norm.reshape(1, nl).astype(F32), rms_index=1)
    uk = w_uk.transpose(1, 2, 0)
    uv = w_uv.transpose(1, 0, 2)
    zk = jnp.zeros_like(uk)
    zv = jnp.zeros_like(uv)
    odd = (jnp.arange(C_HEADS) % 2 == 1)[:, None, None]
    wuk_wide = jnp.where(odd, jnp.concatenate([zk, uk], axis=1), jnp.concatenate([uk, zk], axis=1))
    wuv_wide = jnp.where(odd, jnp.concatenate([zv, uv], axis=2), jnp.concatenate([uv, zv], axis=2))
    o_c = _dsa_attention(cq, ckv, qi, ki2, wi, wuk_wide.astype(BF16), wuv_wide.astype(BF16),
                         dsa_tb, bsz, t)
    return [o_c], [w_o]


def _dispatch_plan(top_idx, rank, counts, n_tok):
    counts = counts.reshape(N_EXPERTS)
    padded = (counts + TM_MOE - 1) // TM_MOE * TM_MOE
    pad_ends = jnp.cumsum(padded)
    pad_starts = pad_ends - padded
    is_e = top_idx[..., None] == jnp.arange(N_EXPERTS, dtype=jnp.int32)
    dest = jnp.sum(jnp.where(is_e, pad_starts.astype(jnp.int32), 0), axis=-1) + rank
    n_blocks = (n_tok * TOP_K) // TM_MOE + N_EXPERTS
    block_start = jnp.arange(n_blocks, dtype=jnp.int32) * TM_MOE
    block_e = jnp.minimum(jnp.sum(pad_ends[None, :] <= block_start[:, None], axis=1),
                          N_EXPERTS - 1).astype(jnp.int32)
    n_used = (pad_ends[-1] // TM_MOE).astype(jnp.int32).reshape(1)
    return dest, block_e, n_used, n_blocks


def kernel(x, p, pos_bias, w_in_even, b_forget, sinks, w_o_even, w_in_odd, kv_norm, w_uk, w_uv,
           w_o_odd, ln_g, ln_b, w_router, b_router, w_gate_up, b_gate_up, w_down, b_down,
           w_ple_proj, w_ple_gate):
    bsz, t, d = x.shape
    depth = ln_g.shape[0]
    alpha = (2 * depth) ** 0.25
    n_tok = bsz * t
    x2 = x.reshape(n_tok, d)
    swa_bias = _swa_bias_table(pos_bias)
    dsa_tb = _dsa_bias_tables(pos_bias)
    p3 = p.reshape(depth, n_tok, PLE_DIM)
    tok_ids = jnp.tile(jnp.arange(n_tok, dtype=jnp.int32), (TOP_K,))
    for i in range(depth):
        j = i // 2
        if i % 2 == 0:
            mixes, w_os = _even_mixer(x2, bsz, t, w_in_even[j], b_forget[j], sinks[j], w_o_even[j],
                                      swa_bias)
        else:
            mixes, w_os = _odd_mixer(x2, bsz, t, w_in_odd[j], kv_norm[j], w_uk[j], w_uv[j],
                                     w_o_odd[j], dsa_tb)
        x1, top_idx, gates, rank, counts = _post_attn(x2, mixes, w_os, ln_g[i, 0], ln_b[i, 0],
                                                      w_router[i], b_router[i], alpha)
        dest, block_e, n_used, n_blocks = _dispatch_plan(top_idx, rank, counts, n_tok)
        dest_flat = dest.reshape(-1)
        src_tok = jnp.zeros((n_blocks * TM_MOE,), jnp.int32).at[dest_flat].set(tok_ids)
        y_pad = _moe_ffn(x1, src_tok, block_e, n_used, i, w_gate_up, b_gate_up, w_down, b_down)
        x2 = _post_moe(x1, y_pad, dest_flat, gates.T, ln_g[i, 1], ln_b[i, 1], i, p3,
                       w_ple_gate, w_ple_proj, alpha)
    return x2.reshape(bsz, t, d)
```

```python
import functools
import math

import jax
import jax.numpy as jnp
from jax import lax
from jax.experimental import pallas as pl
from jax.experimental.pallas import tpu as pltpu

D_MODEL = 1024
HEAD_DIM = 64
A_HEADS, A_KV_HEADS, WINDOW = 8, 2, 128
B_HEADS = 8
C_HEADS, C_LATENT = 16, 128
IDX_HEADS, IDX_DIM = 8, 64
TOPK_MAX = 256
NUM_BUCKETS, MAX_EXACT, MAX_DISTANCE = 32, 16, 128
N_EXPERTS, TOP_K, D_FF = 32, 4, 1024
SWIGLU_LIMIT, SWIGLU_ALPHA = 7.0, 1.702
PLE_DIM = 256
Q_BLOCK = 128
LN_EPS = 1e-5
NEG = -1e30

LANE = 128
VMEM_LIMIT = 56 * 1024 * 1024
TM_PROJ = 512
TM_MOE = 256
MOE_SLOTS = 3
TM_POST = 256
FOX_TQ = 512
FOX_TK = 2 * LANE
DSA_CK = 256
INT_MIN = -2 ** 31

BF16 = jnp.bfloat16
F32 = jnp.float32


def _cparams(n_axes):
    return pltpu.CompilerParams(dimension_semantics=("arbitrary",) * n_axes,
                                vmem_limit_bytes=VMEM_LIMIT)


def _const_spec(shape):
    nd = len(shape)
    return pl.BlockSpec(shape, lambda *_: (0,) * nd)


def _dot(a, b):
    return jnp.dot(a, b, preferred_element_type=F32)


def _dot_nt(a, b):
    return lax.dot_general(a, b, (((1,), (1,)), ((), ())), preferred_element_type=F32)


def _layer_norm(v, g, b):
    mu = jnp.mean(v, axis=-1, keepdims=True)
    d = v - mu
    var = jnp.mean(d * d, axis=-1, keepdims=True)
    return d * lax.rsqrt(var + LN_EPS) * g + b


def _split3(v):
    hi = v.astype(BF16)
    r1 = v - hi.astype(F32)
    mid = r1.astype(BF16)
    lo = (r1 - mid.astype(F32)).astype(BF16)
    return hi, mid, lo


def _proj_kernel(*refs, n_w, rms_index):
    x_ref = refs[0]
    w_refs = refs[1:1 + n_w]
    g_ref = refs[1 + n_w]
    o_refs = refs[2 + n_w:2 + 2 * n_w]
    wb_refs = refs[2 + 2 * n_w:]

    @pl.when(pl.program_id(0) == 0)
    def _():
        for w_ref, wb_ref in zip(w_refs, wb_refs):
            wb_ref[...] = w_ref[...].astype(BF16)

    xb = x_ref[...].astype(BF16)
    for i in range(n_w):
        h = _dot(xb, wb_refs[i][...])
        if i == rms_index:
            h = h * lax.rsqrt(jnp.mean(h * h, axis=-1, keepdims=True) + LN_EPS) * g_ref[...]
        o_refs[i][...] = h.astype(o_refs[i].dtype)


def _project(x2, weights, out_dtypes, rms_gain=None, rms_index=-1):
    n, d = x2.shape
    n_w = len(weights)
    if rms_gain is None:
        rms_gain = jnp.ones((1, LANE), F32)
    in_specs = [pl.BlockSpec((TM_PROJ, d), lambda i: (i, 0))]
    in_specs += [_const_spec(w.shape) for w in weights]
    in_specs += [_const_spec(rms_gain.shape)]
    out_specs = [pl.BlockSpec((TM_PROJ, w.shape[1]), lambda i: (i, 0)) for w in weights]
    out_shape = [jax.ShapeDtypeStruct((n, w.shape[1]), dt) for w, dt in zip(weights, out_dtypes)]
    scratch = [pltpu.VMEM(w.shape, BF16) for w in weights]
    return pl.pallas_call(
        functools.partial(_proj_kernel, n_w=n_w, rms_index=rms_index),
        grid=(n // TM_PROJ,),
        in_specs=in_specs, out_specs=out_specs, out_shape=out_shape,
        scratch_shapes=scratch, compiler_params=_cparams(1), name="in_proj",
    )(x2, *weights, rms_gain)


def _cumsum_kernel(f_ref, b_ref, o_ref):
    z = f_ref[...] + b_ref[...]
    ls = jnp.minimum(z, 0.0) - jnp.log1p(jnp.exp(-jnp.abs(z)))
    t = ls.shape[1]
    r = lax.broadcasted_iota(jnp.int32, (LANE, LANE), 0)
    c = lax.broadcasted_iota(jnp.int32, (LANE, LANE), 1)
    tri = jnp.where(r <= c, 1.0, 0.0).astype(BF16)
    carry = jnp.zeros((ls.shape[0], 1), F32)
    for blk in range(t // LANE):
        seg = ls[:, blk * LANE:(blk + 1) * LANE]
        hi, mid, lo = _split3(seg)
        pre = _dot(hi, tri) + _dot(mid, tri) + _dot(lo, tri) + carry
        o_ref[:, blk * LANE:(blk + 1) * LANE] = pre
        carry = pre[:, LANE - 1:LANE]


def _forget_cumsum(f_t, b_forget):
    bsz, h, t = f_t.shape
    return pl.pallas_call(
        _cumsum_kernel,
        grid=(bsz,),
        in_specs=[pl.BlockSpec((None, h, t), lambda b: (b, 0, 0)), _const_spec((h, 1))],
        out_specs=pl.BlockSpec((None, h, t), lambda b: (b, 0, 0)),
        out_shape=jax.ShapeDtypeStruct((bsz, h, t), F32),
        compiler_params=_cparams(1), name="forget_cumsum",
    )(f_t, b_forget.reshape(h, 1).astype(F32))


def _fox_kernel(q_ref, k_ref, v_ref, cq_ref, ck_ref, o_ref, s_all, m_s, l_s, acc_s):
    i = pl.program_id(2)
    tq = q_ref.shape[0]
    tk = FOX_TK
    per_q = tq // tk
    q = q_ref[...]
    lane = lax.broadcasted_iota(jnp.int32, (tq, LANE), 1)
    row_minus_col = (lax.broadcasted_iota(jnp.int32, (tq, tk), 0)
                     - lax.broadcasted_iota(jnp.int32, (tq, tk), 1))
    heads = range(2)
    qh, cq = [], []
    for hh in heads:
        in_half = (lane >= hh * HEAD_DIM) & (lane < (hh + 1) * HEAD_DIM)
        qh.append(jnp.where(in_half, q, jnp.zeros_like(q)) * jnp.asarray(HEAD_DIM ** -0.5, BF16))
        cq.append(cq_ref[:, hh:hh + 1])

    def logits(j, hh):
        start = pl.multiple_of(j * tk, tk)
        return _dot_nt(qh[hh], k_ref[pl.ds(start, tk), :]) + cq[hh] - ck_ref[hh, pl.ds(j, 1), :]

    def keep(j, hh, s):
        s_all[hh, j] = s
        m_s[hh] = jnp.maximum(m_s[hh], jnp.maximum(s[:, :LANE], s[:, LANE:]))

    def logit_body(j, carry):
        for hh in heads:
            keep(j, hh, logits(j, hh))
        return carry

    m_s[...] = jnp.full(m_s.shape, NEG, F32)
    lax.fori_loop(0, i * per_q, logit_body, 0)
    for dd in range(per_q):
        j = i * per_q + dd
        causal = row_minus_col >= dd * tk
        for hh in heads:
            keep(j, hh, jnp.where(causal, logits(j, hh), NEG))
    m_row = [jnp.broadcast_to(jnp.max(m_s[hh], axis=-1, keepdims=True), (tq, LANE)) for hh in heads]
    l_s[...] = jnp.zeros(l_s.shape, F32)
    acc_s[...] = jnp.zeros(acc_s.shape, F32)

    def prob_body(j, carry):
        start = pl.multiple_of(j * tk, tk)
        vs = v_ref[pl.ds(start, tk), :]
        for hh in heads:
            p_lo = jnp.exp(s_all[hh, j, :, :LANE] - m_row[hh])
            p_hi = jnp.exp(s_all[hh, j, :, LANE:] - m_row[hh])
            l_s[hh] = l_s[hh] + p_lo + p_hi
            p = jnp.concatenate([p_lo, p_hi], axis=1).astype(BF16)
            acc_s[hh] = acc_s[hh] + _dot(p, vs)
        return carry

    lax.fori_loop(0, (i + 1) * per_q, prob_body, 0)
    outs = [acc_s[hh] / jnp.sum(l_s[hh], axis=-1, keepdims=True) for hh in heads]
    o_ref[...] = jnp.where(lane < HEAD_DIM, outs[0], outs[1]).astype(o_ref.dtype)


def _fox_attention(q, k, v, c_col, c_row, bsz, t):
    n, hd = q.shape
    n_pair = hd // LANE
    nq = t // FOX_TQ
    nk = t // FOX_TK
    return pl.pallas_call(
        _fox_kernel,
        grid=(bsz, n_pair, nq),
        in_specs=[
            pl.BlockSpec((FOX_TQ, LANE), lambda b, j, i: (b * nq + i, j)),
            pl.BlockSpec((t, LANE), lambda b, j, i: (b, j)),
            pl.BlockSpec((t, LANE), lambda b, j, i: (b, j)),
            pl.BlockSpec((None, None, FOX_TQ, 2), lambda b, j, i: (b, j, i, 0)),
            pl.BlockSpec((None, None, 2, nk, FOX_TK), lambda b, j, i: (b, j, 0, 0, 0)),
        ],
        out_specs=pl.BlockSpec((FOX_TQ, LANE), lambda b, j, i: (b * nq + i, j)),
        out_shape=jax.ShapeDtypeStruct((n, hd), BF16),
        scratch_shapes=[pltpu.VMEM((2, nk, FOX_TQ, FOX_TK), F32), pltpu.VMEM((2, FOX_TQ, LANE), F32),
                        pltpu.VMEM((2, FOX_TQ, LANE), F32), pltpu.VMEM((2, FOX_TQ, LANE), F32)],
        compiler_params=_cparams(3), name="fox_attention",
    )(q, k, v, c_col, c_row)


def _swa_kernel(sink_ref, q_ref, kp_ref, kc_ref, vp_ref, vc_ref, bias_ref, o_ref):
    nblk = pl.program_id(1)
    w = q_ref.shape[0]
    kk = jnp.concatenate([kp_ref[...], kc_ref[...]], axis=0)
    vv = jnp.concatenate([vp_ref[...], vc_ref[...]], axis=0)
    lane = lax.broadcasted_iota(jnp.int32, (w, LANE), 1)
    col = lax.broadcasted_iota(jnp.int32, (w, 2 * w), 1)
    no_prev = (col < w) & (nblk == 0)
    grp = A_HEADS // A_KV_HEADS
    for g in range(grp):
        qg = q_ref[:, g * LANE:(g + 1) * LANE]
        outs = []
        for hh in range(A_KV_HEADS):
            head = g + grp * hh
            in_half = (lane >= hh * HEAD_DIM) & (lane < (hh + 1) * HEAD_DIM)
            qh = jnp.where(in_half, qg, jnp.zeros_like(qg)) * jnp.asarray(HEAD_DIM ** -0.5, BF16)
            s = _dot_nt(qh, kk) + bias_ref[head]
            s = jnp.where(no_prev, NEG, s)
            sink = sink_ref[head]
            m = jnp.maximum(jnp.max(s, axis=-1, keepdims=True), sink)
            e = jnp.exp(s - m)
            denom = jnp.sum(e, axis=-1, keepdims=True) + jnp.exp(sink - m)
            p = e / denom
            outs.append(_dot(p.astype(BF16), vv))
        o_ref[:, g * LANE:(g + 1) * LANE] = jnp.where(lane < HEAD_DIM, outs[0], outs[1]).astype(o_ref.dtype)


def _swa_attention(q, k, v, sinks, bias, bsz, t):
    n = q.shape[0]
    nb = t // WINDOW
    cur = lambda b, i: (b * nb + i, 0)
    prev = lambda b, i: (b * nb + jnp.maximum(i - 1, 0), 0)
    return pl.pallas_call(
        _swa_kernel,
        grid=(bsz, nb),
        in_specs=[
            pl.BlockSpec(memory_space=pltpu.SMEM),
            pl.BlockSpec((WINDOW, A_HEADS * HEAD_DIM), cur),
            pl.BlockSpec((WINDOW, LANE), prev), pl.BlockSpec((WINDOW, LANE), cur),
            pl.BlockSpec((WINDOW, LANE), prev), pl.BlockSpec((WINDOW, LANE), cur),
            _const_spec(bias.shape),
        ],
        out_specs=pl.BlockSpec((WINDOW, A_HEADS * HEAD_DIM), cur),
        out_shape=jax.ShapeDtypeStruct((n, A_HEADS * HEAD_DIM), BF16),
        compiler_params=_cparams(2), name="swa_attention",
    )(sinks.astype(F32), q, k, k, v, v, bias)


def _float_order_key(s):
    bits = pltpu.bitcast(s, jnp.int32)
    return bits ^ ((bits >> 31) & jnp.int32(0x7FFFFFFF))


def _dsa_kernel(q_ref, ckv_ref, qi_ref, ki_ref, wi_ref, wuk_ref, wuv_ref, tb_ref, o_ref,
                key_s, sel_s, qlat_s, qis_s, s_all, p_buf, acc_s, m_s, l_s):
    n = pl.program_id(1)
    qb = Q_BLOCK
    ck = DSA_CK
    n_sel = TOPK_MAX
    n_chunks = n // (ck // qb) + 1
    lane = lax.broadcasted_iota(jnp.int32, (qb, LANE), 1)
    qpos = n * qb + lax.broadcasted_iota(jnp.int32, (qb, ck), 0)
    kcol = lax.broadcasted_iota(jnp.int32, (qb, ck), 1)

    for h in range(IDX_HEADS):
        pair = qi_ref[:, (h // 2) * LANE:(h // 2 + 1) * LANE]
        in_half = (lane >= (h % 2) * IDX_DIM) & (lane < (h % 2 + 1) * IDX_DIM)
        qis_s[h * qb:(h + 1) * qb, :] = jnp.where(in_half, pair, jnp.zeros_like(pair))

    for h in range(C_HEADS):
        pair = q_ref[:, (h // 2) * LANE:(h // 2 + 1) * LANE]
        ql = _dot(pair, wuk_ref[h]) * (HEAD_DIM ** -0.5)
        qlat_s[h * qb:(h + 1) * qb, :] = ql.astype(BF16)

    wi = wi_ref[...] * (IDX_HEADS ** -0.5)

    def score_body(c, _):
        start = pl.multiple_of(c * ck, ck)
        kc = ki_ref[pl.ds(start, ck), :]
        raw = _dot_nt(qis_s[...], kc)
        score = jnp.zeros((qb, ck), F32)
        for h in range(IDX_HEADS):
            idx_s = jnp.maximum(raw[h * qb:(h + 1) * qb, :] * (IDX_DIM ** -0.5), 0.0)
            score = score + idx_s * wi[:, h:h + 1]
        causal = (start + kcol) <= qpos
        key_s[c] = _float_order_key(jnp.where(causal, score, NEG))
        sel_s[c] = jnp.where(causal, 0.0, NEG)
        return 0

    lax.fori_loop(0, n_chunks, score_body, 0)

    @pl.when((n + 1) * qb > n_sel)
    def _():
        def count_where(pred_fn):
            def body(c, acc):
                hit = jnp.where(pred_fn(key_s[c]), 1.0, 0.0)
                return acc + hit[:, :LANE] + hit[:, LANE:]
            acc = lax.fori_loop(0, n_chunks, body, jnp.zeros((qb, LANE), F32))
            return jnp.sum(acc, axis=-1, keepdims=True)

        cnt0 = count_where(lambda kv: kv >= 0)
        prefix0 = jnp.where(cnt0 >= n_sel, jnp.int32(0), jnp.int32(INT_MIN))

        def bit_body(t, prefix):
            cand = prefix | jnp.left_shift(jnp.int32(1), 30 - t)
            cnt = count_where(lambda kv: kv >= cand)
            return jnp.where(cnt >= n_sel, cand, prefix)

        thr = lax.fori_loop(0, 31, bit_body, prefix0)
        need = n_sel - count_where(lambda kv: kv > thr)
        r = lax.broadcasted_iota(jnp.int32, (ck, ck), 0)
        cc = lax.broadcasted_iota(jnp.int32, (ck, ck), 1)
        before = jnp.where(r < cc, 1.0, 0.0).astype(BF16)

        def tie_body(c, seen):
            kv = key_s[c]
            tie = kv == thr
            tie_f = jnp.where(tie, 1.0, 0.0)
            earlier = _dot(tie_f.astype(BF16), before) + seen
            sel = (kv > thr) | (tie & (earlier < need))
            sel_s[c] = jnp.where(sel, sel_s[c], NEG)
            return seen + jnp.sum(tie_f, axis=-1, keepdims=True)

        lax.fori_loop(0, n_chunks, tie_body, jnp.zeros((qb, 1), F32))

    m_s[...] = jnp.full(m_s.shape, NEG, F32)

    def logit_body(c, _):
        start = pl.multiple_of(c * ck, ck)
        raw = _dot_nt(qlat_s[...], ckv_ref[pl.ds(start, ck), :])
        rel_blk = jnp.minimum(n - c * (ck // qb), 3)
        mask_add = sel_s[c]
        for h in range(C_HEADS):
            rows = slice(h * qb, (h + 1) * qb)
            s = raw[rows, :] + tb_ref[rel_blk, h] + mask_add
            s_all[c, rows, :] = s
            m_s[rows, :] = jnp.maximum(m_s[rows, :], jnp.maximum(s[:, :LANE], s[:, LANE:]))
        return 0

    lax.fori_loop(0, n_chunks, logit_body, 0)

    for h in range(C_HEADS):
        rows = slice(h * qb, (h + 1) * qb)
        m_s[rows, :] = jnp.broadcast_to(jnp.max(m_s[rows, :], axis=-1, keepdims=True), (qb, LANE))
    l_s[...] = jnp.zeros(l_s.shape, F32)
    acc_s[...] = jnp.zeros(acc_s.shape, F32)

    def prob_body(c, _):
        start = pl.multiple_of(c * ck, ck)
        for h in range(C_HEADS):
            rows = slice(h * qb, (h + 1) * qb)
            m_row = m_s[rows, :]
            p_lo = jnp.exp(s_all[c, rows, :LANE] - m_row)
            p_hi = jnp.exp(s_all[c, rows, LANE:] - m_row)
            l_s[rows, :] = l_s[rows, :] + p_lo + p_hi
            p_buf[rows, :LANE] = p_lo.astype(BF16)
            p_buf[rows, LANE:] = p_hi.astype(BF16)
        acc_s[...] = acc_s[...] + _dot(p_buf[...], ckv_ref[pl.ds(start, ck), :])
        return 0

    lax.fori_loop(0, n_chunks, prob_body, 0)

    for g in range(C_HEADS // 2):
        out = jnp.zeros((qb, LANE), F32)
        for hh in range(2):
            h = 2 * g + hh
            rows = slice(h * qb, (h + 1) * qb)
            denom = jnp.sum(l_s[rows, :], axis=-1, keepdims=True)
            o_lat = (acc_s[rows, :] / denom).astype(BF16)
            out = out + _dot(o_lat, wuv_ref[h])
        o_ref[:, g * LANE:(g + 1) * LANE] = out.astype(o_ref.dtype)


def _dsa_attention(q, ckv, qi, ki2, wi, wuk_wide, wuv_wide, tb, bsz, t):
    n = q.shape[0]
    nq = t // Q_BLOCK
    nck = t // DSA_CK
    rows = C_HEADS * Q_BLOCK
    blk = lambda b, i: (b * nq + i, 0)
    per_b = lambda b, i: (b, 0)
    return pl.pallas_call(
        _dsa_kernel,
        grid=(bsz, nq),
        in_specs=[
            pl.BlockSpec((Q_BLOCK, C_HEADS * HEAD_DIM), blk),
            pl.BlockSpec((t, C_LATENT), per_b),
            pl.BlockSpec((Q_BLOCK, IDX_HEADS * IDX_DIM), blk),
            pl.BlockSpec((t, LANE), per_b),
            pl.BlockSpec((Q_BLOCK, LANE), blk),
            _const_spec(wuk_wide.shape), _const_spec(wuv_wide.shape), _const_spec(tb.shape),
        ],
        out_specs=pl.BlockSpec((Q_BLOCK, C_HEADS * HEAD_DIM), blk),
        out_shape=jax.ShapeDtypeStruct((n, C_HEADS * HEAD_DIM), BF16),
        scratch_shapes=[
            pltpu.VMEM((nck, Q_BLOCK, DSA_CK), jnp.int32),
            pltpu.VMEM((nck, Q_BLOCK, DSA_CK), F32),
            pltpu.VMEM((rows, C_LATENT), BF16),
            pltpu.VMEM((IDX_HEADS * Q_BLOCK, LANE), BF16),
            pltpu.VMEM((nck, rows, DSA_CK), F32),
            pltpu.VMEM((rows, DSA_CK), BF16),
            pltpu.VMEM((rows, C_LATENT), F32),
            pltpu.VMEM((rows, LANE), F32),
            pltpu.VMEM((rows, LANE), F32),
        ],
        compiler_params=_cparams(2), name="dsa_attention",
    )(q, ckv, qi, ki2, wi, wuk_wide, wuv_wide, tb)


def _post_attn_kernel(*refs, n_mix, alpha):
    x_ref = refs[0]
    mix_refs = refs[1:1 + n_mix]
    w_refs = refs[1 + n_mix:1 + 2 * n_mix]
    g_ref, b_ref, wr_ref, br_ref = refs[1 + 2 * n_mix:5 + 2 * n_mix]
    xo_ref, idx_ref, gate_ref, rank_ref, cnt_ref = refs[5 + 2 * n_mix:10 + 2 * n_mix]
    scr = refs[10 + 2 * n_mix:]
    wb_refs = scr[:n_mix]
    wr3_ref, before_ref, carry_ref = scr[n_mix:]
    tm = x_ref.shape[0]

    @pl.when(pl.program_id(0) == 0)
    def _():
        for w_ref, wb_ref in zip(w_refs, wb_refs):
            wb_ref[...] = w_ref[...].astype(BF16)
        hi, mid, lo = _split3(wr_ref[...])
        wr3_ref[0] = hi
        wr3_ref[1] = mid
        wr3_ref[2] = lo
        r = lax.broadcasted_iota(jnp.int32, (tm, tm), 0)
        c = lax.broadcasted_iota(jnp.int32, (tm, tm), 1)
        before_ref[...] = jnp.where(r < c, 1.0, 0.0).astype(BF16)
        carry_ref[...] = jnp.zeros(carry_ref.shape, F32)

    mix = _dot(mix_refs[0][...], wb_refs[0][...])
    for i in range(1, n_mix):
        mix = mix + _dot(mix_refs[i][...], wb_refs[i][...])
    xn = _layer_norm(alpha * x_ref[...] + mix, g_ref[...], b_ref[...])
    xo_ref[...] = xn

    xh, xm, xl = _split3(xn)
    logits = (_dot_nt(wr3_ref[0], xh) + _dot_nt(wr3_ref[0], xm) + _dot_nt(wr3_ref[1], xh)
              + _dot_nt(wr3_ref[0], xl) + _dot_nt(wr3_ref[1], xm) + _dot_nt(wr3_ref[2], xh)
              + br_ref[...])
    eidx = lax.broadcasted_iota(jnp.int32, logits.shape, 0)
    cur = logits
    vals, idxs = [], []
    for _ in range(TOP_K):
        mx = jnp.max(cur, axis=0, keepdims=True)
        first = jnp.min(jnp.where(cur == mx, eidx, N_EXPERTS), axis=0, keepdims=True)
        vals.append(mx)
        idxs.append(first)
        cur = jnp.where(eidx == first, -jnp.inf, cur)
    es = [jnp.exp(v - vals[0]) for v in vals]
    tot = es[0] + es[1] + es[2] + es[3]
    member = jnp.zeros(logits.shape, F32)
    for k in range(TOP_K):
        idx_ref[k:k + 1, :] = idxs[k]
        gate_ref[k:k + 1, :] = es[k] / tot
        member = member + jnp.where(eidx == idxs[k], 1.0, 0.0)
    earlier = _dot(member.astype(BF16), before_ref[...]) + carry_ref[...]
    for k in range(TOP_K):
        rk = jnp.sum(jnp.where(eidx == idxs[k], earlier, 0.0), axis=0, keepdims=True)
        rank_ref[k:k + 1, :] = rk.astype(jnp.int32)
    carry_ref[...] = carry_ref[...] + jnp.sum(member, axis=1, keepdims=True)
    cnt_ref[...] = carry_ref[...].astype(jnp.int32)


def _post_attn(x2, mixes, w_os, ln_g, ln_b, w_router, b_router, alpha):
    n, d = x2.shape
    n_mix = len(mixes)
    tm = TM_PROJ
    row = lambda i: (i, 0)
    colblk = lambda i: (0, i)
    in_specs = [pl.BlockSpec((tm, d), row)]
    in_specs += [pl.BlockSpec((tm, m.shape[1]), row) for m in mixes]
    in_specs += [_const_spec(w.shape) for w in w_os]
    in_specs += [_const_spec((1, d)), _const_spec((1, d)), _const_spec((N_EXPERTS, d)),
                 _const_spec((N_EXPERTS, 1))]
    out_specs = [pl.BlockSpec((tm, d), row), pl.BlockSpec((TOP_K, tm), colblk),
                 pl.BlockSpec((TOP_K, tm), colblk), pl.BlockSpec((TOP_K, tm), colblk),
                 _const_spec((N_EXPERTS, 1))]
    out_shape = [jax.ShapeDtypeStruct((n, d), F32), jax.ShapeDtypeStruct((TOP_K, n), jnp.int32),
                 jax.ShapeDtypeStruct((TOP_K, n), F32), jax.ShapeDtypeStruct((TOP_K, n), jnp.int32),
                 jax.ShapeDtypeStruct((N_EXPERTS, 1), jnp.int32)]
    scratch = [pltpu.VMEM(w.shape, BF16) for w in w_os]
    scratch += [pltpu.VMEM((3, N_EXPERTS, d), BF16), pltpu.VMEM((tm, tm), BF16),
                pltpu.VMEM((N_EXPERTS, 1), F32)]
    return pl.pallas_call(
        functools.partial(_post_attn_kernel, n_mix=n_mix, alpha=alpha),
        grid=(n // tm,),
        in_specs=in_specs, out_specs=out_specs, out_shape=out_shape, scratch_shapes=scratch,
        compiler_params=_cparams(1), name="post_attn",
    )(x2, *mixes, *w_os, ln_g.reshape(1, d), ln_b.reshape(1, d), w_router.T,
      b_router.reshape(N_EXPERTS, 1))


def _moe_kernel(be_ref, nu_ref, src_ref, x_hbm, wgu_ref, bgu_ref, wd_ref, bd_ref, y_ref,
                wgu_s, wd_s, xb_s, act_s, xbuf, stage_ref, sem):
    i = pl.program_id(0)
    n_steps = pl.num_programs(0)
    tm = y_ref.shape[0]
    prev = be_ref[jnp.maximum(i - 1, 0)]
    n_used = nu_ref[0]
    live = i < n_used
    n_slots = xbuf.shape[0]
    ahead = n_slots - 1
    slot = i % n_slots

    def row_copy(block, s, r):
        tok = src_ref[block * tm + r]
        return pltpu.make_async_copy(x_hbm.at[pl.ds(tok, 1), :], xbuf.at[s, pl.ds(r, 1), :],
                                     sem.at[s])

    def wait_rows(s):
        pltpu.make_async_copy(xbuf.at[s], xbuf.at[s], sem.at[s]).wait()

    @pl.when(i == 0)
    def _():
        for b in range(ahead):
            def body(r, carry, b=b):
                row_copy(b, b, r).start(priority=1)
                return carry
            lax.fori_loop(0, tm, body, 0, unroll=8)

    @pl.when(live & ((i == 0) | (be_ref[i] != prev)))
    def _():
        wgu_s[...] = wgu_ref[...].astype(BF16)
        wd_s[...] = wd_ref[...].astype(BF16)

    ch = 512
    n_stage = 2 + D_FF // ch
    per_stage = tm // n_stage
    nxt = jnp.minimum(i + ahead, n_steps - 1)

    def fetch_share(stage):
        for r in range(stage * per_stage, (stage + 1) * per_stage):
            row_copy(nxt, (i + ahead) % n_slots, r).start(priority=1)

    @pl.when(live)
    def _():
        wait_rows(slot)
        xb_s[...] = xbuf[slot].astype(BF16)
        fetch_share(0)

    for j in range(D_FF // ch):
        stage_ref[0] = 1 + j

        @pl.when(live)
        def _(j=j):
            xb = xb_s[...]
            hg = _dot(xb, wgu_s[:, j * ch:(j + 1) * ch]) + bgu_ref[:, j * ch:(j + 1) * ch]
            hl = (_dot(xb, wgu_s[:, D_FF + j * ch:D_FF + (j + 1) * ch])
                  + bgu_ref[:, D_FF + j * ch:D_FF + (j + 1) * ch])
            glu = jnp.minimum(hg, SWIGLU_LIMIT)
            lin = jnp.clip(hl, -SWIGLU_LIMIT, SWIGLU_LIMIT)
            act = glu * jax.nn.sigmoid(SWIGLU_ALPHA * glu) * (lin + 1.0)
            act_s[:, j * ch:(j + 1) * ch] = act.astype(BF16)
            fetch_share(1 + j)

    stage_ref[0] = n_stage - 1

    @pl.when(live)
    def _():
        y_ref[...] = (_dot(act_s[...], wd_s[...]) + bd_ref[...]).astype(y_ref.dtype)
        fetch_share(n_stage - 1)

    @pl.when(i == n_used)
    def _():
        for b in range(ahead):
            wait_rows((i + b) % n_slots)

    @pl.when(jnp.logical_not(live))
    def _():
        y_ref[...] = jnp.zeros(y_ref.shape, y_ref.dtype)


def _moe_ffn(x2, src_tok, block_e, n_used, layer, w_gate_up, b_gate_up, w_down, b_down):
    d = x2.shape[1]
    rows = src_tok.shape[0]
    depth = w_gate_up.shape[0]
    n_blocks = rows // TM_MOE
    expert = lambda i, be, nu, src: (layer, be[i], 0, 0)
    grid_spec = pltpu.PrefetchScalarGridSpec(
        num_scalar_prefetch=3,
        grid=(n_blocks,),
        in_specs=[
            pl.BlockSpec(memory_space=pl.ANY),
            pl.BlockSpec((None, None, d, 2 * D_FF), expert),
            pl.BlockSpec((None, None, 1, 2 * D_FF), expert),
            pl.BlockSpec((None, None, D_FF, d), expert),
            pl.BlockSpec((None, None, 1, d), expert),
        ],
        out_specs=pl.BlockSpec((TM_MOE, d), lambda i, be, nu, src: (i, 0)),
        scratch_shapes=[pltpu.VMEM((d, 2 * D_FF), BF16), pltpu.VMEM((D_FF, d), BF16),
                        pltpu.VMEM((TM_MOE, d), BF16), pltpu.VMEM((TM_MOE, D_FF), BF16),
                        pltpu.VMEM((MOE_SLOTS, TM_MOE, d), F32), pltpu.SMEM((1,), jnp.int32),
                        pltpu.SemaphoreType.DMA((MOE_SLOTS,))],
    )
    return pl.pallas_call(
        _moe_kernel, grid_spec=grid_spec,
        out_shape=jax.ShapeDtypeStruct((rows, d), F32),
        compiler_params=_cparams(1), name="moe_ffn",
    )(block_e, n_used, src_tok, x2, w_gate_up, b_gate_up.reshape(depth, N_EXPERTS, 1, 2 * D_FF),
      w_down, b_down.reshape(depth, N_EXPERTS, 1, d))


def _post_moe_kernel(dest_ref, x_ref, y_hbm, gt_ref, g_ref, b_ref, p_ref, wg_ref, wp_ref, o_ref,
                     wg_s, wp_s, ybuf, sem, *, alpha, n_tok):
    i = pl.program_id(0)
    n_steps = pl.num_programs(0)
    tm, d = x_ref.shape

    def row_copy(tile, slot, k, j):
        r = dest_ref[k * n_tok + tile * tm + j]
        return pltpu.make_async_copy(y_hbm.at[pl.ds(r, 1), :], ybuf.at[slot, k, pl.ds(j, 1), :],
                                     sem.at[slot])

    @pl.when(i == 0)
    def _():
        wg_s[...] = wg_ref[...].astype(BF16)
        wp_s[...] = wp_ref[...].astype(BF16)

        def body(j, carry):
            for k in range(TOP_K):
                row_copy(0, 0, k, j).start()
            return carry
        lax.fori_loop(0, tm, body, 0, unroll=8)

    @pl.when(i + 1 < n_steps)
    def _():
        for j in range(tm):
            for k in range(TOP_K):
                row_copy(i + 1, (i + 1) % 2, k, j).start(priority=k % 2)

    slot = i % 2
    pltpu.make_async_copy(ybuf.at[slot], ybuf.at[slot], sem.at[slot]).wait()
    gates = gt_ref[...]
    ffn = gates[:, 0:1] * ybuf[slot, 0]
    for k in range(1, TOP_K):
        ffn = ffn + gates[:, k:k + 1] * ybuf[slot, k]
    xn = _layer_norm(alpha * x_ref[...] + ffn, g_ref[...], b_ref[...])
    gate = jax.nn.sigmoid(_dot(xn.astype(BF16), wg_s[...]))
    emb = _dot(p_ref[...].astype(BF16), wp_s[...])
    o_ref[...] = xn + gate * emb


def _post_moe(x2, y_pad, dest_flat, gates_t, ln_g, ln_b, layer, p3, w_ple_gate, w_ple_proj, alpha):
    n, d = x2.shape
    tm = TM_POST
    row = lambda i, dest: (i, 0)
    const2 = lambda i, dest: (0, 0)
    grid_spec = pltpu.PrefetchScalarGridSpec(
        num_scalar_prefetch=1,
        grid=(n // tm,),
        in_specs=[pl.BlockSpec((tm, d), row), pl.BlockSpec(memory_space=pl.ANY),
                  pl.BlockSpec((tm, TOP_K), row), pl.BlockSpec((1, d), const2),
                  pl.BlockSpec((1, d), const2),
                  pl.BlockSpec((None, tm, PLE_DIM), lambda i, dest: (layer, i, 0)),
                  pl.BlockSpec((None, d, d), lambda i, dest: (layer, 0, 0)),
                  pl.BlockSpec((None, PLE_DIM, d), lambda i, dest: (layer, 0, 0))],
        out_specs=pl.BlockSpec((tm, d), row),
        scratch_shapes=[pltpu.VMEM((d, d), BF16), pltpu.VMEM((PLE_DIM, d), BF16),
                        pltpu.VMEM((2, TOP_K, tm, d), F32), pltpu.SemaphoreType.DMA((2,))],
    )
    return pl.pallas_call(
        functools.partial(_post_moe_kernel, alpha=alpha, n_tok=n),
        grid_spec=grid_spec,
        out_shape=jax.ShapeDtypeStruct((n, d), F32),
        compiler_params=_cparams(1), name="post_moe",
    )(dest_flat, x2, y_pad, gates_t, ln_g.reshape(1, d), ln_b.reshape(1, d), p3, w_ple_gate,
      w_ple_proj)


def _t5_bucket(dist):
    d = jnp.maximum(dist, 0)
    ratio = jnp.log(jnp.maximum(d, 1).astype(F32) / MAX_EXACT) / math.log(MAX_DISTANCE / MAX_EXACT)
    large = MAX_EXACT + (ratio * (NUM_BUCKETS - MAX_EXACT)).astype(jnp.int32)
    large = jnp.minimum(large, NUM_BUCKETS - 1)
    return jnp.where(d < MAX_EXACT, d, large)


def _swa_bias_table(pos_bias):
    qi = jnp.arange(WINDOW)[:, None]
    kj = jnp.arange(2 * WINDOW)[None, :]
    dist = qi + WINDOW - kj
    valid = (dist >= 0) & (dist < WINDOW)
    bias = pos_bias[_t5_bucket(dist)][..., :A_HEADS].astype(F32).transpose(2, 0, 1)
    return jnp.where(valid[None], bias, NEG)


def _dsa_bias_tables(pos_bias):
    qi = jnp.arange(Q_BLOCK)[:, None]
    kj = jnp.arange(DSA_CK)[None, :]
    tabs = []
    for r in range(4):
        dist = r * Q_BLOCK + qi - kj
        tabs.append(pos_bias[_t5_bucket(dist)].astype(F32).transpose(2, 0, 1))
    return jnp.stack(tabs)


def _pad_cols(w, width=LANE):
    return jnp.pad(w, ((0, 0), (0, width - w.shape[1])))


def _even_mixer(x2, bsz, t, w_in, b_forget, sinks, w_o, swa_bias):
    hd = HEAD_DIM
    grp = A_HEADS // A_KV_HEADS
    pair_order = [g + grp * hh for g in range(grp) for hh in range(A_KV_HEADS)]
    na, nkv, nb = A_HEADS * hd, A_KV_HEADS * hd, B_HEADS * hd
    offs = [0, na, na + nkv, na + 2 * nkv, na + 2 * nkv + nb, na + 2 * nkv + 2 * nb,
            na + 2 * nkv + 3 * nb]
    w_aq = w_in[:, offs[0]:offs[1]].reshape(D_MODEL, A_HEADS, hd)[:, pair_order].reshape(D_MODEL, na)
    weights = [w_aq, w_in[:, offs[1]:offs[2]], w_in[:, offs[2]:offs[3]], w_in[:, offs[3]:offs[4]],
               w_in[:, offs[4]:offs[5]], w_in[:, offs[5]:offs[6]], _pad_cols(w_in[:, offs[6]:])]
    aq, ak, av, bq, bk, bv, bf = _project(x2, weights, [BF16] * 6 + [F32])
    o_a = _swa_attention(aq, ak, av, sinks, swa_bias, bsz, t)

    f_t = bf[:, :B_HEADS].reshape(bsz, t, B_HEADS).transpose(0, 2, 1)
    c = _forget_cumsum(f_t, b_forget)
    c_row = c.reshape(bsz, B_HEADS // 2, 2, t // FOX_TK, FOX_TK)
    c_col = c.reshape(bsz, B_HEADS // 2, 2, t).transpose(0, 1, 3, 2)
    o_b = _fox_attention(bq, bk, bv, c_col, c_row, bsz, t)

    w_oa = w_o[:na].reshape(A_HEADS, hd, D_MODEL)[jnp.asarray(pair_order)].reshape(na, D_MODEL)
    return [o_a, o_b], [w_oa, w_o[na:]]


def _odd_mixer(x2, bsz, t, w_in, kv_norm, w_uk, w_uv, w_o, dsa_tb):
    hd = HEAD_DIM
    nq, nl, ni = C_HEADS * hd, C_LATENT, IDX_HEADS * IDX_DIM
    w_ki = w_in[:, nq + nl + ni:nq + nl + ni + IDX_DIM]
    weights = [w_in[:, :nq], w_in[:, nq:nq + nl], w_in[:, nq + nl:nq + nl + ni],
               jnp.concatenate([w_ki, w_ki], axis=1), _pad_cols(w_in[:, nq + nl + ni + IDX_DIM:])]
    cq, ckv, qi, ki2, wi = _project(x2, weights, [BF16, BF16, BF16, BF16, F32],
                                    rms_gain=kv_norm.reshape(1, nl).astype(F32), rms_index=1)
    uk = w_uk.transpose(1, 2, 0)
    uv = w_uv.transpose(1, 0, 2)
    zk = jnp.zeros_like(uk)
    zv = jnp.zeros_like(uv)
    odd = (jnp.arange(C_HEADS) % 2 == 1)[:, None, None]
    wuk_wide = jnp.where(odd, jnp.concatenate([zk, uk], axis=1), jnp.concatenate([uk, zk], axis=1))
    wuv_wide = jnp.where(odd, jnp.concatenate([zv, uv], axis=2), jnp.concatenate([uv, zv], axis=2))
    o_c = _dsa_attention(cq, ckv, qi, ki2, wi, wuk_wide.astype(BF16), wuv_wide.astype(BF16),
                         dsa_tb, bsz, t)
    return [o_c], [w_o]


def _dispatch_plan(top_idx, rank, counts, n_tok):
    counts = counts.reshape(N_EXPERTS)
    padded = (counts + TM_MOE - 1) // TM_MOE * TM_MOE
    pad_ends = jnp.cumsum(padded)
    pad_starts = pad_ends - padded
    is_e = top_idx[..., None] == jnp.arange(N_EXPERTS, dtype=jnp.int32)
    dest = jnp.sum(jnp.where(is_e, pad_starts.astype(jnp.int32), 0), axis=-1) + rank
    n_blocks = (n_tok * TOP_K) // TM_MOE + N_EXPERTS
    block_start = jnp.arange(n_blocks, dtype=jnp.int32) * TM_MOE
    block_e = jnp.minimum(jnp.sum(pad_ends[None, :] <= block_start[:, None], axis=1),
                          N_EXPERTS - 1).astype(jnp.int32)
    n_used = (pad_ends[-1] // TM_MOE).astype(jnp.int32).reshape(1)
    return dest, block_e, n_used, n_blocks


def kernel(x, p, pos_bias, w_in_even, b_forget, sinks, w_o_even, w_in_odd, kv_norm, w_uk, w_uv,
           w_o_odd, ln_g, ln_b, w_router, b_router, w_gate_up, b_gate_up, w_down, b_down,
           w_ple_proj, w_ple_gate):
    bsz, t, d = x.shape
    depth = ln_g.shape[0]
    alpha = (2 * depth) ** 0.25
    n_tok = bsz * t
    x2 = x.reshape(n_tok, d)
    swa_bias = _swa_bias_table(pos_bias)
    dsa_tb = _dsa_bias_tables(pos_bias)
    p3 = p.reshape(depth, n_tok, PLE_DIM)
    tok_ids = jnp.tile(jnp.arange(n_tok, dtype=jnp.int32), (TOP_K,))
    for i in range(depth):
        j = i // 2
        if i % 2 == 0:
            mixes, w_os = _even_mixer(x2, bsz, t, w_in_even[j], b_forget[j], sinks[j], w_o_even[j],
                                      swa_bias)
        else:
            mixes, w_os = _odd_mixer(x2, bsz, t, w_in_odd[j], kv_norm[j], w_uk[j], w_uv[j],
                                     w_o_odd[j], dsa_tb)
        x1, top_idx, gates, rank, counts = _post_attn(x2, mixes, w_os, ln_g[i, 0], ln_b[i, 0],
                                                      w_router[i], b_router[i], alpha)
        dest, block_e, n_used, n_blocks = _dispatch_plan(top_idx, rank, counts, n_tok)
        dest_flat = dest.reshape(-1)
        src_tok = jnp.zeros((n_blocks * TM_MOE,), jnp.int32).at[dest_flat].set(tok_ids)
        y_pad = _moe_ffn(x1, src_tok, block_e, n_used, i, w_gate_up, b_gate_up, w_down, b_down)
        x2 = _post_moe(x1, y_pad, dest_flat, gates.T, ln_g[i, 1], ln_b[i, 1], i, p3,
                       w_ple_gate, w_ple_proj, alpha)
    return x2.reshape(bsz, t, d)
```

```python
import functools
import math

import jax
import jax.numpy as jnp
from jax import lax
from jax.experimental import pallas as pl
from jax.experimental.pallas import tpu as pltpu

D_MODEL = 1024
HEAD_DIM = 64
A_HEADS, A_KV_HEADS, WINDOW = 8, 2, 128
B_HEADS = 8
C_HEADS, C_LATENT = 16, 128
IDX_HEADS, IDX_DIM = 8, 64
TOPK_MAX = 256
NUM_BUCKETS, MAX_EXACT, MAX_DISTANCE = 32, 16, 128
N_EXPERTS, TOP_K, D_FF = 32, 4, 1024
SWIGLU_LIMIT, SWIGLU_ALPHA = 7.0, 1.702
PLE_DIM = 256
Q_BLOCK = 128
LN_EPS = 1e-5
NEG = -1e30

LANE = 128
VMEM_LIMIT = 56 * 1024 * 1024
TM_PROJ = 512
TM_MOE = 256
MOE_SLOTS = 3
TM_POST = 256
FOX_TQ = 512
FOX_TK = 2 * LANE
DSA_CK = 256
INT_MIN = -2 ** 31

BF16 = jnp.bfloat16
F32 = jnp.float32


def _cparams(n_axes):
    return pltpu.CompilerParams(dimension_semantics=("arbitrary",) * n_axes,
                                vmem_limit_bytes=VMEM_LIMIT)


def _const_spec(shape):
    nd = len(shape)
    return pl.BlockSpec(shape, lambda *_: (0,) * nd)


def _dot(a, b):
    return jnp.dot(a, b, preferred_element_type=F32)


def _dot_nt(a, b):
    return lax.dot_general(a, b, (((1,), (1,)), ((), ())), preferred_element_type=F32)


def _layer_norm(v, g, b):
    mu = jnp.mean(v, axis=-1, keepdims=True)
    d = v - mu
    var = jnp.mean(d * d, axis=-1, keepdims=True)
    return d * lax.rsqrt(var + LN_EPS) * g + b


def _split3(v):
    hi = v.astype(BF16)
    r1 = v - hi.astype(F32)
    mid = r1.astype(BF16)
    lo = (r1 - mid.astype(F32)).astype(BF16)
    return hi, mid, lo


def _proj_kernel(*refs, n_w, rms_index):
    x_ref = refs[0]
    w_refs = refs[1:1 + n_w]
    g_ref = refs[1 + n_w]
    o_refs = refs[2 + n_w:2 + 2 * n_w]
    wb_refs = refs[2 + 2 * n_w:]

    @pl.when(pl.program_id(0) == 0)
    def _():
        for w_ref, wb_ref in zip(w_refs, wb_refs):
            wb_ref[...] = w_ref[...].astype(BF16)

    xb = x_ref[...].astype(BF16)
    for i in range(n_w):
        h = _dot(xb, wb_refs[i][...])
        if i == rms_index:
            h = h * lax.rsqrt(jnp.mean(h * h, axis=-1, keepdims=True) + LN_EPS) * g_ref[...]
        o_refs[i][...] = h.astype(o_refs[i].dtype)


def _project(x2, weights, out_dtypes, rms_gain=None, rms_index=-1):
    n, d = x2.shape
    n_w = len(weights)
    if rms_gain is None:
        rms_gain = jnp.ones((1, LANE), F32)
    in_specs = [pl.BlockSpec((TM_PROJ, d), lambda i: (i, 0))]
    in_specs += [_const_spec(w.shape) for w in weights]
    in_specs += [_const_spec(rms_gain.shape)]
    out_specs = [pl.BlockSpec((TM_PROJ, w.shape[1]), lambda i: (i, 0)) for w in weights]
    out_shape = [jax.ShapeDtypeStruct((n, w.shape[1]), dt) for w, dt in zip(weights, out_dtypes)]
    scratch = [pltpu.VMEM(w.shape, BF16) for w in weights]
    return pl.pallas_call(
        functools.partial(_proj_kernel, n_w=n_w, rms_index=rms_index),
        grid=(n // TM_PROJ,),
        in_specs=in_specs, out_specs=out_specs, out_shape=out_shape,
        scratch_shapes=scratch, compiler_params=_cparams(1), name="in_proj",
    )(x2, *weights, rms_gain)


def _cumsum_kernel(f_ref, b_ref, o_ref):
    z = f_ref[...] + b_ref[...]
    ls = jnp.minimum(z, 0.0) - jnp.log1p(jnp.exp(-jnp.abs(z)))
    t = ls.shape[1]
    r = lax.broadcasted_iota(jnp.int32, (LANE, LANE), 0)
    c = lax.broadcasted_iota(jnp.int32, (LANE, LANE), 1)
    tri = jnp.where(r <= c, 1.0, 0.0).astype(BF16)
    carry = jnp.zeros((ls.shape[0], 1), F32)
    for blk in range(t // LANE):
        seg = ls[:, blk * LANE:(blk + 1) * LANE]
        hi, mid, lo = _split3(seg)
        pre = _dot(hi, tri) + _dot(mid, tri) + _dot(lo, tri) + carry
        o_ref[:, blk * LANE:(blk + 1) * LANE] = pre
        carry = pre[:, LANE - 1:LANE]


def _forget_cumsum(f_t, b_forget):
    bsz, h, t = f_t.shape
    return pl.pallas_call(
        _cumsum_kernel,
        grid=(bsz,),
        in_specs=[pl.BlockSpec((None, h, t), lambda b: (b, 0, 0)), _const_spec((h, 1))],
        out_specs=pl.BlockSpec((None, h, t), lambda b: (b, 0, 0)),
        out_shape=jax.ShapeDtypeStruct((bsz, h, t), F32),
        compiler_params=_cparams(1), name="forget_cumsum",
    )(f_t, b_forget.reshape(h, 1).astype(F32))


def _fox_kernel(q_ref, k_ref, v_ref, cq_ref, ck_ref, o_ref, s_all, m_s, l_s, acc_s):
    i = pl.program_id(2)
    tq = q_ref.shape[0]
    tk = FOX_TK
    per_q = tq // tk
    q = q_ref[...]
    lane = lax.broadcasted_iota(jnp.int32, (tq, LANE), 1)
    row_minus_col = (lax.broadcasted_iota(jnp.int32, (tq, tk), 0)
                     - lax.broadcasted_iota(jnp.int32, (tq, tk), 1))
    heads = range(2)
    qh, cq = [], []
    for hh in heads:
        in_half = (lane >= hh * HEAD_DIM) & (lane < (hh + 1) * HEAD_DIM)
        qh.append(jnp.where(in_half, q, jnp.zeros_like(q)) * jnp.asarray(HEAD_DIM ** -0.5, BF16))
        cq.append(cq_ref[:, hh:hh + 1])

    def logits(j, hh):
        start = pl.multiple_of(j * tk, tk)
        return _dot_nt(qh[hh], k_ref[pl.ds(start, tk), :]) + cq[hh] - ck_ref[hh, pl.ds(j, 1), :]

    def keep(j, hh, s):
        s_all[hh, j] = s
        m_s[hh] = jnp.maximum(m_s[hh], jnp.maximum(s[:, :LANE], s[:, LANE:]))

    def logit_body(j, carry):
        for hh in heads:
            keep(j, hh, logits(j, hh))
        return carry

    m_s[...] = jnp.full(m_s.shape, NEG, F32)
    lax.fori_loop(0, i * per_q, logit_body, 0)
    for dd in range(per_q):
        j = i * per_q + dd
        causal = row_minus_col >= dd * tk
        for hh in heads:
            keep(j, hh, jnp.where(causal, logits(j, hh), NEG))
    m_row = [jnp.broadcast_to(jnp.max(m_s[hh], axis=-1, keepdims=True), (tq, LANE)) for hh in heads]
    l_s[...] = jnp.zeros(l_s.shape, F32)
    acc_s[...] = jnp.zeros(acc_s.shape, F32)

    def prob_body(j, carry):
        start = pl.multiple_of(j * tk, tk)
        vs = v_ref[pl.ds(start, tk), :]
        for hh in heads:
            p_lo = jnp.exp(s_all[hh, j, :, :LANE] - m_row[hh])
            p_hi = jnp.exp(s_all[hh, j, :, LANE:] - m_row[hh])
            l_s[hh] = l_s[hh] + p_lo + p_hi
            p = jnp.concatenate([p_lo, p_hi], axis=1).astype(BF16)
            acc_s[hh] = acc_s[hh] + _dot(p, vs)
        return carry

    lax.fori_loop(0, (i + 1) * per_q, prob_body, 0)
    outs = [acc_s[hh] / jnp.sum(l_s[hh], axis=-1, keepdims=True) for hh in heads]
    o_ref[...] = jnp.where(lane < HEAD_DIM, outs[0], outs[1]).astype(o_ref.dtype)


def _fox_attention(q, k, v, c_col, c_row, bsz, t):
    n, hd = q.shape
    n_pair = hd // LANE
    nq = t // FOX_TQ
    nk = t // FOX_TK
    return pl.pallas_call(
        _fox_kernel,
        grid=(bsz, n_pair, nq),
        in_specs=[
            pl.BlockSpec((FOX_TQ, LANE), lambda b, j, i: (b * nq + i, j)),
            pl.BlockSpec((t, LANE), lambda b, j, i: (b, j)),
            pl.BlockSpec((t, LANE), lambda b, j, i: (b, j)),
            pl.BlockSpec((None, None, FOX_TQ, 2), lambda b, j, i: (b, j, i, 0)),
            pl.BlockSpec((None, None, 2, nk, FOX_TK), lambda b, j, i: (b, j, 0, 0, 0)),
        ],
        out_specs=pl.BlockSpec((FOX_TQ, LANE), lambda b, j, i: (b * nq + i, j)),
        out_shape=jax.ShapeDtypeStruct((n, hd), BF16),
        scratch_shapes=[pltpu.VMEM((2, nk, FOX_TQ, FOX_TK), F32), pltpu.VMEM((2, FOX_TQ, LANE), F32),
                        pltpu.VMEM((2, FOX_TQ, LANE), F32), pltpu.VMEM((2, FOX_TQ, LANE), F32)],
        compiler_params=_cparams(3), name="fox_attention",
    )(q, k, v, c_col, c_row)


def _swa_kernel(sink_ref, q_ref, kp_ref, kc_ref, vp_ref, vc_ref, bias_ref, o_ref):
    nblk = pl.program_id(1)
    w = q_ref.shape[0]
    kk = jnp.concatenate([kp_ref[...], kc_ref[...]], axis=0)
    vv = jnp.concatenate([vp_ref[...], vc_ref[...]], axis=0)
    lane = lax.broadcasted_iota(jnp.int32, (w, LANE), 1)
    col = lax.broadcasted_iota(jnp.int32, (w, 2 * w), 1)
    no_prev = (col < w) & (nblk == 0)
    grp = A_HEADS // A_KV_HEADS
    for g in range(grp):
        qg = q_ref[:, g * LANE:(g + 1) * LANE]
        outs = []
        for hh in range(A_KV_HEADS):
            head = g + grp * hh
            in_half = (lane >= hh * HEAD_DIM) & (lane < (hh + 1) * HEAD_DIM)
            qh = jnp.where(in_half, qg, jnp.zeros_like(qg)) * jnp.asarray(HEAD_DIM ** -0.5, BF16)
            s = _dot_nt(qh, kk) + bias_ref[head]
            s = jnp.where(no_prev, NEG, s)
            sink = sink_ref[head]
            m = jnp.maximum(jnp.max(s, axis=-1, keepdims=True), sink)
            e = jnp.exp(s - m)
            denom = jnp.sum(e, axis=-1, keepdims=True) + jnp.exp(sink - m)
            p = e / denom
            outs.append(_dot(p.astype(BF16), vv))
        o_ref[:, g * LANE:(g + 1) * LANE] = jnp.where(lane < HEAD_DIM, outs[0], outs[1]).astype(o_ref.dtype)


def _swa_attention(q, k, v, sinks, bias, bsz, t):
    n = q.shape[0]
    nb = t // WINDOW
    cur = lambda b, i: (b * nb + i, 0)
    prev = lambda b, i: (b * nb + jnp.maximum(i - 1, 0), 0)
    return pl.pallas_call(
        _swa_kernel,
        grid=(bsz, nb),
        in_specs=[
            pl.BlockSpec(memory_space=pltpu.SMEM),
            pl.BlockSpec((WINDOW, A_HEADS * HEAD_DIM), cur),
            pl.BlockSpec((WINDOW, LANE), prev), pl.BlockSpec((WINDOW, LANE), cur),
            pl.BlockSpec((WINDOW, LANE), prev), pl.BlockSpec((WINDOW, LANE), cur),
            _const_spec(bias.shape),
        ],
        out_specs=pl.BlockSpec((WINDOW, A_HEADS * HEAD_DIM), cur),
        out_shape=jax.ShapeDtypeStruct((n, A_HEADS * HEAD_DIM), BF16),
        compiler_params=_cparams(2), name="swa_attention",
    )(sinks.astype(F32), q, k, k, v, v, bias)


def _float_order_key(s):
    bits = pltpu.bitcast(s, jnp.int32)
    return bits ^ ((bits >> 31) & jnp.int32(0x7FFFFFFF))


def _dsa_kernel(q_ref, ckv_ref, qi_ref, ki_ref, wi_ref, wuk_ref, wuv_ref, tb_ref, o_ref,
                key_s, sel_s, qlat_s, qis_s, wib_s, s_all, p_buf, acc_s, m_s, l_s):
    n = pl.program_id(1)
    qb = Q_BLOCK
    ck = DSA_CK
    n_sel = TOPK_MAX
    n_chunks = n // (ck // qb) + 1
    lane = lax.broadcasted_iota(jnp.int32, (qb, LANE), 1)
    qpos = n * qb + lax.broadcasted_iota(jnp.int32, (qb, ck), 0)
    kcol = lax.broadcasted_iota(jnp.int32, (qb, ck), 1)

    for h in range(IDX_HEADS):
        pair = qi_ref[:, (h // 2) * LANE:(h // 2 + 1) * LANE]
        in_half = (lane >= (h % 2) * IDX_DIM) & (lane < (h % 2 + 1) * IDX_DIM)
        qis_s[h * qb:(h + 1) * qb, :] = jnp.where(in_half, pair, jnp.zeros_like(pair))

    for h in range(C_HEADS):
        pair = q_ref[:, (h // 2) * LANE:(h // 2 + 1) * LANE]
        ql = _dot(pair, wuk_ref[h]) * (HEAD_DIM ** -0.5)
        qlat_s[h * qb:(h + 1) * qb, :] = ql.astype(BF16)

    wi = wi_ref[...] * (IDX_HEADS ** -0.5)
    for h in range(IDX_HEADS):
        wib_s[h] = jnp.broadcast_to(wi[:, h:h + 1], (qb, ck))

    def score_body(c, _):
        start = pl.multiple_of(c * ck, ck)
        kc = ki_ref[pl.ds(start, ck), :]
        raw = _dot_nt(qis_s[...], kc)
        score = jnp.zeros((qb, ck), F32)
        for h in range(IDX_HEADS):
            idx_s = jnp.maximum(raw[h * qb:(h + 1) * qb, :] * (IDX_DIM ** -0.5), 0.0)
            score = score + idx_s * wib_s[h]
        causal = (start + kcol) <= qpos
        key_s[c] = _float_order_key(jnp.where(causal, score, NEG)).T
        sel_s[c] = jnp.where(causal, 0.0, NEG)
        return 0

    lax.fori_loop(0, n_chunks, score_body, 0)

    @pl.when((n + 1) * qb > n_sel)
    def _():
        def count_where(pred_fn):
            def body(c, acc):
                hit = jnp.where(pred_fn(key_s[c]), 1.0, 0.0)
                return acc + jnp.sum(hit.reshape(ck // 32, 32, qb), axis=0)
            acc = lax.fori_loop(0, n_chunks, body, jnp.zeros((32, qb), F32))
            return jnp.sum(acc, axis=0, keepdims=True)

        cnt0 = count_where(lambda kv: kv >= 0)
        prefix0 = jnp.where(cnt0 >= n_sel, jnp.int32(0), jnp.int32(INT_MIN))

        def bit_body(t, prefix):
            cand = prefix | jnp.left_shift(jnp.int32(1), 30 - t)
            cnt = count_where(lambda kv: kv >= cand)
            return jnp.where(cnt >= n_sel, cand, prefix)

        thr = lax.fori_loop(0, 31, bit_body, prefix0)
        need = n_sel - count_where(lambda kv: kv > thr)
        r = lax.broadcasted_iota(jnp.int32, (ck, ck), 0)
        cc = lax.broadcasted_iota(jnp.int32, (ck, ck), 1)
        earlier_key = jnp.where(cc < r, 1.0, 0.0).astype(BF16)

        def tie_body(c, seen):
            kv = key_s[c]
            tie = kv == thr
            tie_f = jnp.where(tie, 1.0, 0.0)
            earlier = _dot(earlier_key, tie_f.astype(BF16)) + seen
            sel = (kv > thr) | (tie & (earlier < need))
            sel_s[c] = jnp.minimum(sel_s[c], jnp.where(sel, 0.0, NEG).T)
            return seen + jnp.sum(tie_f, axis=0, keepdims=True)

        lax.fori_loop(0, n_chunks, tie_body, jnp.zeros((1, qb), F32))

    m_s[...] = jnp.full(m_s.shape, NEG, F32)

    def logit_body(c, _):
        start = pl.multiple_of(c * ck, ck)
        raw = _dot_nt(qlat_s[...], ckv_ref[pl.ds(start, ck), :])
        rel_blk = jnp.minimum(n - c * (ck // qb), 3)
        mask_add = sel_s[c]
        for h in range(C_HEADS):
            rows = slice(h * qb, (h + 1) * qb)
            s = raw[rows, :] + tb_ref[rel_blk, h] + mask_add
            s_all[c, rows, :] = s
            m_s[rows, :] = jnp.maximum(m_s[rows, :], jnp.maximum(s[:, :LANE], s[:, LANE:]))
        return 0

    lax.fori_loop(0, n_chunks, logit_body, 0)

    for h in range(C_HEADS):
        rows = slice(h * qb, (h + 1) * qb)
        m_s[rows, :] = jnp.broadcast_to(jnp.max(m_s[rows, :], axis=-1, keepdims=True), (qb, LANE))
    l_s[...] = jnp.zeros(l_s.shape, F32)
    acc_s[...] = jnp.zeros(acc_s.shape, F32)

    def prob_body(c, _):
        start = pl.multiple_of(c * ck, ck)
        for h in range(C_HEADS):
            rows = slice(h * qb, (h + 1) * qb)
            m_row = m_s[rows, :]
            p_lo = jnp.exp(s_all[c, rows, :LANE] - m_row)
            p_hi = jnp.exp(s_all[c, rows, LANE:] - m_row)
            l_s[rows, :] = l_s[rows, :] + p_lo + p_hi
            p_buf[rows, :LANE] = p_lo.astype(BF16)
            p_buf[rows, LANE:] = p_hi.astype(BF16)
        acc_s[...] = acc_s[...] + _dot(p_buf[...], ckv_ref[pl.ds(start, ck), :])
        return 0

    lax.fori_loop(0, n_chunks, prob_body, 0)

    for g in range(C_HEADS // 2):
        out = jnp.zeros((qb, LANE), F32)
        for hh in range(2):
            h = 2 * g + hh
            rows = slice(h * qb, (h + 1) * qb)
            denom = jnp.sum(l_s[rows, :], axis=-1, keepdims=True)
            o_lat = (acc_s[rows, :] / denom).astype(BF16)
            out = out + _dot(o_lat, wuv_ref[h])
        o_ref[:, g * LANE:(g + 1) * LANE] = out.astype(o_ref.dtype)


def _dsa_attention(q, ckv, qi, ki2, wi, wuk_wide, wuv_wide, tb, bsz, t):
    n = q.shape[0]
    nq = t // Q_BLOCK
    nck = t // DSA_CK
    rows = C_HEADS * Q_BLOCK
    blk = lambda b, i: (b * nq + i, 0)
    per_b = lambda b, i: (b, 0)
    return pl.pallas_call(
        _dsa_kernel,
        grid=(bsz, nq),
        in_specs=[
            pl.BlockSpec((Q_BLOCK, C_HEADS * HEAD_DIM), blk),
            pl.BlockSpec((t, C_LATENT), per_b),
            pl.BlockSpec((Q_BLOCK, IDX_HEADS * IDX_DIM), blk),
            pl.BlockSpec((t, LANE), per_b),
            pl.BlockSpec((Q_BLOCK, LANE), blk),
            _const_spec(wuk_wide.shape), _const_spec(wuv_wide.shape), _const_spec(tb.shape),
        ],
        out_specs=pl.BlockSpec((Q_BLOCK, C_HEADS * HEAD_DIM), blk),
        out_shape=jax.ShapeDtypeStruct((n, C_HEADS * HEAD_DIM), BF16),
        scratch_shapes=[
            pltpu.VMEM((nck, DSA_CK, Q_BLOCK), jnp.int32),
            pltpu.VMEM((nck, Q_BLOCK, DSA_CK), F32),
            pltpu.VMEM((rows, C_LATENT), BF16),
            pltpu.VMEM((IDX_HEADS * Q_BLOCK, LANE), BF16),
            pltpu.VMEM((IDX_HEADS, Q_BLOCK, DSA_CK), F32),
            pltpu.VMEM((nck, rows, DSA_CK), F32),
            pltpu.VMEM((rows, DSA_CK), BF16),
            pltpu.VMEM((rows, C_LATENT), F32),
            pltpu.VMEM((rows, LANE), F32),
            pltpu.VMEM((rows, LANE), F32),
        ],
        compiler_params=_cparams(2), name="dsa_attention",
    )(q, ckv, qi, ki2, wi, wuk_wide, wuv_wide, tb)


def _post_attn_kernel(*refs, n_mix, alpha):
    x_ref = refs[0]
    mix_refs = refs[1:1 + n_mix]
    w_refs = refs[1 + n_mix:1 + 2 * n_mix]
    g_ref, b_ref, wr_ref, br_ref = refs[1 + 2 * n_mix:5 + 2 * n_mix]
    xo_ref, idx_ref, gate_ref, rank_ref, cnt_ref = refs[5 + 2 * n_mix:10 + 2 * n_mix]
    scr = refs[10 + 2 * n_mix:]
    wb_refs = scr[:n_mix]
    wr3_ref, before_ref, carry_ref = scr[n_mix:]
    tm = x_ref.shape[0]

    @pl.when(pl.program_id(0) == 0)
    def _():
        for w_ref, wb_ref in zip(w_refs, wb_refs):
            wb_ref[...] = w_ref[...].astype(BF16)
        hi, mid, lo = _split3(wr_ref[...])
        wr3_ref[0] = hi
        wr3_ref[1] = mid
        wr3_ref[2] = lo
        r = lax.broadcasted_iota(jnp.int32, (tm, tm), 0)
        c = lax.broadcasted_iota(jnp.int32, (tm, tm), 1)
        before_ref[...] = jnp.where(r < c, 1.0, 0.0).astype(BF16)
        carry_ref[...] = jnp.zeros(carry_ref.shape, F32)

    mix = _dot(mix_refs[0][...], wb_refs[0][...])
    for i in range(1, n_mix):
        mix = mix + _dot(mix_refs[i][...], wb_refs[i][...])
    xn = _layer_norm(alpha * x_ref[...] + mix, g_ref[...], b_ref[...])
    xo_ref[...] = xn

    xh, xm, xl = _split3(xn)
    logits = (_dot_nt(wr3_ref[0], xh) + _dot_nt(wr3_ref[0], xm) + _dot_nt(wr3_ref[1], xh)
              + _dot_nt(wr3_ref[0], xl) + _dot_nt(wr3_ref[1], xm) + _dot_nt(wr3_ref[2], xh)
              + br_ref[...])
    eidx = lax.broadcasted_iota(jnp.int32, logits.shape, 0)
    cur = logits
    vals, idxs = [], []
    for _ in range(TOP_K):
        mx = jnp.max(cur, axis=0, keepdims=True)
        first = jnp.min(jnp.where(cur == mx, eidx, N_EXPERTS), axis=0, keepdims=True)
        vals.append(mx)
        idxs.append(first)
        cur = jnp.where(eidx == first, -jnp.inf, cur)
    es = [jnp.exp(v - vals[0]) for v in vals]
    tot = es[0] + es[1] + es[2] + es[3]
    member = jnp.zeros(logits.shape, F32)
    for k in range(TOP_K):
        idx_ref[k:k + 1, :] = idxs[k]
        gate_ref[k:k + 1, :] = es[k] / tot
        member = member + jnp.where(eidx == idxs[k], 1.0, 0.0)
    earlier = _dot(member.astype(BF16), before_ref[...]) + carry_ref[...]
    for k in range(TOP_K):
        rk = jnp.sum(jnp.where(eidx == idxs[k], earlier, 0.0), axis=0, keepdims=True)
        rank_ref[k:k + 1, :] = rk.astype(jnp.int32)
    carry_ref[...] = carry_ref[...] + jnp.sum(member, axis=1, keepdims=True)
    cnt_ref[...] = carry_ref[...].astype(jnp.int32)


def _post_attn(x2, mixes, w_os, ln_g, ln_b, w_router, b_router, alpha):
    n, d = x2.shape
    n_mix = len(mixes)
    tm = TM_PROJ
    row = lambda i: (i, 0)
    colblk = lambda i: (0, i)
    in_specs = [pl.BlockSpec((tm, d), row)]
    in_specs += [pl.BlockSpec((tm, m.shape[1]), row) for m in mixes]
    in_specs += [_const_spec(w.shape) for w in w_os]
    in_specs += [_const_spec((1, d)), _const_spec((1, d)), _const_spec((N_EXPERTS, d)),
                 _const_spec((N_EXPERTS, 1))]
    out_specs = [pl.BlockSpec((tm, d), row), pl.BlockSpec((TOP_K, tm), colblk),
                 pl.BlockSpec((TOP_K, tm), colblk), pl.BlockSpec((TOP_K, tm), colblk),
                 _const_spec((N_EXPERTS, 1))]
    out_shape = [jax.ShapeDtypeStruct((n, d), F32), jax.ShapeDtypeStruct((TOP_K, n), jnp.int32),
                 jax.ShapeDtypeStruct((TOP_K, n), F32), jax.ShapeDtypeStruct((TOP_K, n), jnp.int32),
                 jax.ShapeDtypeStruct((N_EXPERTS, 1), jnp.int32)]
    scratch = [pltpu.VMEM(w.shape, BF16) for w in w_os]
    scratch += [pltpu.VMEM((3, N_EXPERTS, d), BF16), pltpu.VMEM((tm, tm), BF16),
                pltpu.VMEM((N_EXPERTS, 1), F32)]
    return pl.pallas_call(
        functools.partial(_post_attn_kernel, n_mix=n_mix, alpha=alpha),
        grid=(n // tm,),
        in_specs=in_specs, out_specs=out_specs, out_shape=out_shape, scratch_shapes=scratch,
        compiler_params=_cparams(1), name="post_attn",
    )(x2, *mixes, *w_os, ln_g.reshape(1, d), ln_b.reshape(1, d), w_router.T,
      b_router.reshape(N_EXPERTS, 1))


def _moe_kernel(be_ref, nu_ref, src_ref, x_hbm, wgu_ref, bgu_ref, wd_ref, bd_ref, y_ref,
                wgu_s, wd_s, xbuf, sem):
    i = pl.program_id(0)
    n_steps = pl.num_programs(0)
    tm = y_ref.shape[0]
    prev = be_ref[jnp.maximum(i - 1, 0)]
    n_used = nu_ref[0]
    live = i < n_used
    n_slots = xbuf.shape[0]
    ahead = n_slots - 1
    slot = i % n_slots

    def row_copy(block, s, r):
        tok = src_ref[block * tm + r]
        return pltpu.make_async_copy(x_hbm.at[pl.ds(tok, 1), :], xbuf.at[s, pl.ds(r, 1), :],
                                     sem.at[s])

    def wait_rows(s):
        pltpu.make_async_copy(xbuf.at[s], xbuf.at[s], sem.at[s]).wait()

    @pl.when(i == 0)
    def _():
        for b in range(ahead):
            def body(r, carry, b=b):
                row_copy(b, b, r).start(priority=1)
                return carry
            lax.fori_loop(0, tm, body, 0, unroll=8)

    @pl.when(live & ((i == 0) | (be_ref[i] != prev)))
    def _():
        wgu_s[...] = wgu_ref[...].astype(BF16)
        wd_s[...] = wd_ref[...].astype(BF16)

    @pl.when(live)
    def _():
        wait_rows(slot)
        xb = xbuf[slot].astype(BF16)
        nxt = jnp.minimum(i + ahead, n_steps - 1)
        for r in range(tm):
            row_copy(nxt, (i + ahead) % n_slots, r).start(priority=1)
        ch = 512
        acc = jnp.zeros(y_ref.shape, F32)
        for j in range(D_FF // ch):
            hg = _dot(xb, wgu_s[:, j * ch:(j + 1) * ch]) + bgu_ref[:, j * ch:(j + 1) * ch]
            hl = (_dot(xb, wgu_s[:, D_FF + j * ch:D_FF + (j + 1) * ch])
                  + bgu_ref[:, D_FF + j * ch:D_FF + (j + 1) * ch])
            glu = jnp.minimum(hg, SWIGLU_LIMIT)
            lin = jnp.clip(hl, -SWIGLU_LIMIT, SWIGLU_LIMIT)
            act = glu * jax.nn.sigmoid(SWIGLU_ALPHA * glu) * (lin + 1.0)
            acc = acc + _dot(act.astype(BF16), wd_s[j * ch:(j + 1) * ch, :])
        y_ref[...] = (acc + bd_ref[...]).astype(y_ref.dtype)

    @pl.when(i == n_used)
    def _():
        for b in range(ahead):
            wait_rows((i + b) % n_slots)

    @pl.when(jnp.logical_not(live))
    def _():
        y_ref[...] = jnp.zeros(y_ref.shape, y_ref.dtype)


def _moe_ffn(x2, src_tok, block_e, n_used, layer, w_gate_up, b_gate_up, w_down, b_down):
    d = x2.shape[1]
    rows = src_tok.shape[0]
    depth = w_gate_up.shape[0]
    n_blocks = rows // TM_MOE
    expert = lambda i, be, nu, src: (layer, be[i], 0, 0)
    grid_spec = pltpu.PrefetchScalarGridSpec(
        num_scalar_prefetch=3,
        grid=(n_blocks,),
        in_specs=[
            pl.BlockSpec(memory_space=pl.ANY),
            pl.BlockSpec((None, None, d, 2 * D_FF), expert),
            pl.BlockSpec((None, None, 1, 2 * D_FF), expert),
            pl.BlockSpec((None, None, D_FF, d), expert),
            pl.BlockSpec((None, None, 1, d), expert),
        ],
        out_specs=pl.BlockSpec((TM_MOE, d), lambda i, be, nu, src: (i, 0)),
        scratch_shapes=[pltpu.VMEM((d, 2 * D_FF), BF16), pltpu.VMEM((D_FF, d), BF16),
                        pltpu.VMEM((MOE_SLOTS, TM_MOE, d), F32),
                        pltpu.SemaphoreType.DMA((MOE_SLOTS,))],
    )
    return pl.pallas_call(
        _moe_kernel, grid_spec=grid_spec,
        out_shape=jax.ShapeDtypeStruct((rows, d), F32),
        compiler_params=_cparams(1), name="moe_ffn",
    )(block_e, n_used, src_tok, x2, w_gate_up, b_gate_up.reshape(depth, N_EXPERTS, 1, 2 * D_FF),
      w_down, b_down.reshape(depth, N_EXPERTS, 1, d))


def _post_moe_kernel(dest_ref, x_ref, y_hbm, gt_ref, g_ref, b_ref, p_ref, wg_ref, wp_ref, o_ref,
                     wg_s, wp_s, ybuf, sem, *, alpha, n_tok):
    i = pl.program_id(0)
    n_steps = pl.num_programs(0)
    tm, d = x_ref.shape

    def row_copy(tile, slot, k, j):
        r = dest_ref[k * n_tok + tile * tm + j]
        return pltpu.make_async_copy(y_hbm.at[pl.ds(r, 1), :], ybuf.at[slot, k, pl.ds(j, 1), :],
                                     sem.at[slot])

    @pl.when(i == 0)
    def _():
        wg_s[...] = wg_ref[...].astype(BF16)
        wp_s[...] = wp_ref[...].astype(BF16)

        def body(j, carry):
            for k in range(TOP_K):
                row_copy(0, 0, k, j).start()
            return carry
        lax.fori_loop(0, tm, body, 0, unroll=8)

    @pl.when(i + 1 < n_steps)
    def _():
        for j in range(tm):
            for k in range(TOP_K):
                row_copy(i + 1, (i + 1) % 2, k, j).start(priority=k % 2)

    slot = i % 2
    pltpu.make_async_copy(ybuf.at[slot], ybuf.at[slot], sem.at[slot]).wait()
    gates = gt_ref[...]
    ffn = gates[:, 0:1] * ybuf[slot, 0]
    for k in range(1, TOP_K):
        ffn = ffn + gates[:, k:k + 1] * ybuf[slot, k]
    xn = _layer_norm(alpha * x_ref[...] + ffn, g_ref[...], b_ref[...])
    gate = jax.nn.sigmoid(_dot(xn.astype(BF16), wg_s[...]))
    emb = _dot(p_ref[...].astype(BF16), wp_s[...])
    o_ref[...] = xn + gate * emb


def _post_moe(x2, y_pad, dest_flat, gates_t, ln_g, ln_b, layer, p3, w_ple_gate, w_ple_proj, alpha):
    n, d = x2.shape
    tm = TM_POST
    row = lambda i, dest: (i, 0)
    const2 = lambda i, dest: (0, 0)
    grid_spec = pltpu.PrefetchScalarGridSpec(
        num_scalar_prefetch=1,
        grid=(n // tm,),
        in_specs=[pl.BlockSpec((tm, d), row), pl.BlockSpec(memory_space=pl.ANY),
                  pl.BlockSpec((tm, TOP_K), row), pl.BlockSpec((1, d), const2),
                  pl.BlockSpec((1, d), const2),
                  pl.BlockSpec((None, tm, PLE_DIM), lambda i, dest: (layer, i, 0)),
                  pl.BlockSpec((None, d, d), lambda i, dest: (layer, 0, 0)),
                  pl.BlockSpec((None, PLE_DIM, d), lambda i, dest: (layer, 0, 0))],
        out_specs=pl.BlockSpec((tm, d), row),
        scratch_shapes=[pltpu.VMEM((d, d), BF16), pltpu.VMEM((PLE_DIM, d), BF16),
                        pltpu.VMEM((2, TOP_K, tm, d), F32), pltpu.SemaphoreType.DMA((2,))],
    )
    return pl.pallas_call(
        functools.partial(_post_moe_kernel, alpha=alpha, n_tok=n),
        grid_spec=grid_spec,
        out_shape=jax.ShapeDtypeStruct((n, d), F32),
        compiler_params=_cparams(1), name="post_moe",
    )(dest_flat, x2, y_pad, gates_t, ln_g.reshape(1, d), ln_b.reshape(1, d), p3, w_ple_gate,
      w_ple_proj)


def _t5_bucket(dist):
    d = jnp.maximum(dist, 0)
    ratio = jnp.log(jnp.maximum(d, 1).astype(F32) / MAX_EXACT) / math.log(MAX_DISTANCE / MAX_EXACT)
    large = MAX_EXACT + (ratio * (NUM_BUCKETS - MAX_EXACT)).astype(jnp.int32)
    large = jnp.minimum(large, NUM_BUCKETS - 1)
    return jnp.where(d < MAX_EXACT, d, large)


def _swa_bias_table(pos_bias):
    qi = jnp.arange(WINDOW)[:, None]
    kj = jnp.arange(2 * WINDOW)[None, :]
    dist = qi + WINDOW - kj
    valid = (dist >= 0) & (dist < WINDOW)
    bias = pos_bias[_t5_bucket(dist)][..., :A_HEADS].astype(F32).transpose(2, 0, 1)
    return jnp.where(valid[None], bias, NEG)


def _dsa_bias_tables(pos_bias):
    qi = jnp.arange(Q_BLOCK)[:, None]
    kj = jnp.arange(DSA_CK)[None, :]
    tabs = []
    for r in range(4):
        dist = r * Q_BLOCK + qi - kj
        tabs.append(pos_bias[_t5_bucket(dist)].astype(F32).transpose(2, 0, 1))
    return jnp.stack(tabs)


def _pad_cols(w, width=LANE):
    return jnp.pad(w, ((0, 0), (0, width - w.shape[1])))


def _even_mixer(x2, bsz, t, w_in, b_forget, sinks, w_o, swa_bias):
    hd = HEAD_DIM
    grp = A_HEADS // A_KV_HEADS
    pair_order = [g + grp * hh for g in range(grp) for hh in range(A_KV_HEADS)]
    na, nkv, nb = A_HEADS * hd, A_KV_HEADS * hd, B_HEADS * hd
    offs = [0, na, na + nkv, na + 2 * nkv, na + 2 * nkv + nb, na + 2 * nkv + 2 * nb,
            na + 2 * nkv + 3 * nb]
    w_aq = w_in[:, offs[0]:offs[1]].reshape(D_MODEL, A_HEADS, hd)[:, pair_order].reshape(D_MODEL, na)
    weights = [w_aq, w_in[:, offs[1]:offs[2]], w_in[:, offs[2]:offs[3]], w_in[:, offs[3]:offs[4]],
               w_in[:, offs[4]:offs[5]], w_in[:, offs[5]:offs[6]], _pad_cols(w_in[:, offs[6]:])]
    aq, ak, av, bq, bk, bv, bf = _project(x2, weights, [BF16] * 6 + [F32])
    o_a = _swa_attention(aq, ak, av, sinks, swa_bias, bsz, t)

    f_t = bf[:, :B_HEADS].reshape(bsz, t, B_HEADS).transpose(0, 2, 1)
    c = _forget_cumsum(f_t, b_forget)
    c_row = c.reshape(bsz, B_HEADS // 2, 2, t // FOX_TK, FOX_TK)
    c_col = c.reshape(bsz, B_HEADS // 2, 2, t).transpose(0, 1, 3, 2)
    o_b = _fox_attention(bq, bk, bv, c_col, c_row, bsz, t)

    w_oa = w_o[:na].reshape(A_HEADS, hd, D_MODEL)[jnp.asarray(pair_order)].reshape(na, D_MODEL)
    return [o_a, o_b], [w_oa, w_o[na:]]


def _odd_mixer(x2, bsz, t, w_in, kv_norm, w_uk, w_uv, w_o, dsa_tb):
    hd = HEAD_DIM
    nq, nl, ni = C_HEADS * hd, C_LATENT, IDX_HEADS * IDX_DIM
    w_ki = w_in[:, nq + nl + ni:nq + nl + ni + IDX_DIM]
    weights = [w_in[:, :nq], w_in[:, nq:nq + nl], w_in[:, nq + nl:nq + nl + ni],
               jnp.concatenate([w_ki, w_ki], axis=1), _pad_cols(w_in[:, nq + nl + ni + IDX_DIM:])]
    cq, ckv, qi, ki2, wi = _project(x2, weights, [BF16, BF16, BF16, BF16, F32],
                                    rms_gain=kv_norm.reshape(1, nl).astype(F32), rms_index=1)
    uk = w_uk.transpose(1, 2, 0)
    uv = w_uv.transpose(1, 0, 2)
    zk = jnp.zeros_like(uk)
    zv = jnp.zeros_like(uv)
    odd = (jnp.arange(C_HEADS) % 2 == 1)[:, None, None]
    wuk_wide = jnp.where(odd, jnp.concatenate([zk, uk], axis=1), jnp.concatenate([uk, zk], axis=1))
    wuv_wide = jnp.where(odd, jnp.concatenate([zv, uv], axis=2), jnp.concatenate([uv, zv], axis=2))
    o_c = _dsa_attention(cq, ckv, qi, ki2, wi, wuk_wide.astype(BF16), wuv_wide.astype(BF16),
                         dsa_tb, bsz, t)
    return [o_c], [w_o]


def _dispatch_plan(top_idx, rank, counts, n_tok):
    counts = counts.reshape(N_EXPERTS)
    padded = (counts + TM_MOE - 1) // TM_MOE * TM_MOE
    pad_ends = jnp.cumsum(padded)
    pad_starts = pad_ends - padded
    is_e = top_idx[..., None] == jnp.arange(N_EXPERTS, dtype=jnp.int32)
    dest = jnp.sum(jnp.where(is_e, pad_starts.astype(jnp.int32), 0), axis=-1) + rank
    n_blocks = (n_tok * TOP_K) // TM_MOE + N_EXPERTS
    block_start = jnp.arange(n_blocks, dtype=jnp.int32) * TM_MOE
    block_e = jnp.minimum(jnp.sum(pad_ends[None, :] <= block_start[:, None], axis=1),
                          N_EXPERTS - 1).astype(jnp.int32)
    n_used = (pad_ends[-1] // TM_MOE).astype(jnp.int32).reshape(1)
    return dest, block_e, n_used, n_blocks


def kernel(x, p, pos_bias, w_in_even, b_forget, sinks, w_o_even, w_in_odd, kv_norm, w_uk, w_uv,
           w_o_odd, ln_g, ln_b, w_router, b_router, w_gate_up, b_gate_up, w_down, b_down,
           w_ple_proj, w_ple_gate):
    bsz, t, d = x.shape
    depth = ln_g.shape[0]
    alpha = (2 * depth) ** 0.25
    n_tok = bsz * t
    x2 = x.reshape(n_tok, d)
    swa_bias = _swa_bias_table(pos_bias)
    dsa_tb = _dsa_bias_tables(pos_bias)
    p3 = p.reshape(depth, n_tok, PLE_DIM)
    tok_ids = jnp.tile(jnp.arange(n_tok, dtype=jnp.int32), (TOP_K,))
    for i in range(depth):
        j = i // 2
        if i % 2 == 0:
            mixes, w_os = _even_mixer(x2, bsz, t, w_in_even[j], b_forget[j], sinks[j], w_o_even[j],
                                      swa_bias)
        else:
            mixes, w_os = _odd_mixer(x2, bsz, t, w_in_odd[j], kv_norm[j], w_uk[j], w_uv[j],
                                     w_o_odd[j], dsa_tb)
        x1, top_idx, gates, rank, counts = _post_attn(x2, mixes, w_os, ln_g[i, 0], ln_b[i, 0],
                                                      w_router[i], b_router[i], alpha)
        dest, block_e, n_used, n_blocks = _dispatch_plan(top_idx, rank, counts, n_tok)
        dest_flat = dest.reshape(-1)
        src_tok = jnp.zeros((n_blocks * TM_MOE,), jnp.int32).at[dest_flat].set(tok_ids)
        y_pad = _moe_ffn(x1, src_tok, block_e, n_used, i, w_gate_up, b_gate_up, w_down, b_down)
        x2 = _post_moe(x1, y_pad, dest_flat, gates.T, ln_g[i, 1], ln_b[i, 1], i, p3,
                       w_ple_gate, w_ple_proj, alpha)
    return x2.reshape(bsz, t, d)
```

```python
import functools
import math

import jax
import jax.numpy as jnp
from jax import lax
from jax.experimental import pallas as pl
from jax.experimental.pallas import tpu as pltpu

D_MODEL = 1024
HEAD_DIM = 64
A_HEADS, A_KV_HEADS, WINDOW = 8, 2, 128
B_HEADS = 8
C_HEADS, C_LATENT = 16, 128
IDX_HEADS, IDX_DIM = 8, 64
TOPK_MAX = 256
NUM_BUCKETS, MAX_EXACT, MAX_DISTANCE = 32, 16, 128
N_EXPERTS, TOP_K, D_FF = 32, 4, 1024
SWIGLU_LIMIT, SWIGLU_ALPHA = 7.0, 1.702
PLE_DIM = 256
Q_BLOCK = 128
LN_EPS = 1e-5
NEG = -1e30

LANE = 128
VMEM_LIMIT = 56 * 1024 * 1024
TM_PROJ = 512
TM_MOE = 256
MOE_SLOTS = 3
TM_POST = 256
FOX_TQ = 512
FOX_TK = 2 * LANE
DSA_CK = 256
INT_MIN = -2 ** 31

BF16 = jnp.bfloat16
F32 = jnp.float32


def _cparams(n_axes):
    return pltpu.CompilerParams(dimension_semantics=("arbitrary",) * n_axes,
                                vmem_limit_bytes=VMEM_LIMIT)


def _const_spec(shape):
    nd = len(shape)
    return pl.BlockSpec(shape, lambda *_: (0,) * nd)


def _dot(a, b):
    return jnp.dot(a, b, preferred_element_type=F32)


def _dot_nt(a, b):
    return lax.dot_general(a, b, (((1,), (1,)), ((), ())), preferred_element_type=F32)


def _layer_norm(v, g, b):
    mu = jnp.mean(v, axis=-1, keepdims=True)
    d = v - mu
    var = jnp.mean(d * d, axis=-1, keepdims=True)
    return d * lax.rsqrt(var + LN_EPS) * g + b


def _split3(v):
    hi = v.astype(BF16)
    r1 = v - hi.astype(F32)
    mid = r1.astype(BF16)
    lo = (r1 - mid.astype(F32)).astype(BF16)
    return hi, mid, lo


def _proj_kernel(*refs, n_w, rms_index):
    x_ref = refs[0]
    w_refs = refs[1:1 + n_w]
    g_ref = refs[1 + n_w]
    o_refs = refs[2 + n_w:2 + 2 * n_w]
    wb_refs = refs[2 + 2 * n_w:]

    @pl.when(pl.program_id(0) == 0)
    def _():
        for w_ref, wb_ref in zip(w_refs, wb_refs):
            wb_ref[...] = w_ref[...].astype(BF16)

    xb = x_ref[...].astype(BF16)
    for i in range(n_w):
        h = _dot(xb, wb_refs[i][...])
        if i == rms_index:
            h = h * lax.rsqrt(jnp.mean(h * h, axis=-1, keepdims=True) + LN_EPS) * g_ref[...]
        o_refs[i][...] = h.astype(o_refs[i].dtype)


def _project(x2, weights, out_dtypes, rms_gain=None, rms_index=-1):
    n, d = x2.shape
    n_w = len(weights)
    if rms_gain is None:
        rms_gain = jnp.ones((1, LANE), F32)
    in_specs = [pl.BlockSpec((TM_PROJ, d), lambda i: (i, 0))]
    in_specs += [_const_spec(w.shape) for w in weights]
    in_specs += [_const_spec(rms_gain.shape)]
    out_specs = [pl.BlockSpec((TM_PROJ, w.shape[1]), lambda i: (i, 0)) for w in weights]
    out_shape = [jax.ShapeDtypeStruct((n, w.shape[1]), dt) for w, dt in zip(weights, out_dtypes)]
    scratch = [pltpu.VMEM(w.shape, BF16) for w in weights]
    return pl.pallas_call(
        functools.partial(_proj_kernel, n_w=n_w, rms_index=rms_index),
        grid=(n // TM_PROJ,),
        in_specs=in_specs, out_specs=out_specs, out_shape=out_shape,
        scratch_shapes=scratch, compiler_params=_cparams(1), name="in_proj",
    )(x2, *weights, rms_gain)


def _cumsum_kernel(f_ref, b_ref, o_ref):
    z = f_ref[...] + b_ref[...]
    ls = jnp.minimum(z, 0.0) - jnp.log1p(jnp.exp(-jnp.abs(z)))
    t = ls.shape[1]
    r = lax.broadcasted_iota(jnp.int32, (LANE, LANE), 0)
    c = lax.broadcasted_iota(jnp.int32, (LANE, LANE), 1)
    tri = jnp.where(r <= c, 1.0, 0.0).astype(BF16)
    carry = jnp.zeros((ls.shape[0], 1), F32)
    for blk in range(t // LANE):
        seg = ls[:, blk * LANE:(blk + 1) * LANE]
        hi, mid, lo = _split3(seg)
        pre = _dot(hi, tri) + _dot(mid, tri) + _dot(lo, tri) + carry
        o_ref[:, blk * LANE:(blk + 1) * LANE] = pre
        carry = pre[:, LANE - 1:LANE]


def _forget_cumsum(f_t, b_forget):
    bsz, h, t = f_t.shape
    return pl.pallas_call(
        _cumsum_kernel,
        grid=(bsz,),
        in_specs=[pl.BlockSpec((None, h, t), lambda b: (b, 0, 0)), _const_spec((h, 1))],
        out_specs=pl.BlockSpec((None, h, t), lambda b: (b, 0, 0)),
        out_shape=jax.ShapeDtypeStruct((bsz, h, t), F32),
        compiler_params=_cparams(1), name="forget_cumsum",
    )(f_t, b_forget.reshape(h, 1).astype(F32))


def _fox_kernel(q_ref, k_ref, v_ref, cq_ref, ck_ref, o_ref, s_all, m_s, l_s, acc_s):
    i = pl.program_id(2)
    tq = q_ref.shape[0]
    tk = FOX_TK
    per_q = tq // tk
    q = q_ref[...]
    lane = lax.broadcasted_iota(jnp.int32, (tq, LANE), 1)
    row_minus_col = (lax.broadcasted_iota(jnp.int32, (tq, tk), 0)
                     - lax.broadcasted_iota(jnp.int32, (tq, tk), 1))
    heads = range(2)
    qh, cq = [], []
    for hh in heads:
        in_half = (lane >= hh * HEAD_DIM) & (lane < (hh + 1) * HEAD_DIM)
        qh.append(jnp.where(in_half, q, jnp.zeros_like(q)) * jnp.asarray(HEAD_DIM ** -0.5, BF16))
        cq.append(cq_ref[:, hh:hh + 1])

    def logits(j, hh):
        start = pl.multiple_of(j * tk, tk)
        return _dot_nt(qh[hh], k_ref[pl.ds(start, tk), :]) + cq[hh] - ck_ref[hh, pl.ds(j, 1), :]

    def keep(j, hh, s):
        s_all[hh, j] = s
        m_s[hh] = jnp.maximum(m_s[hh], jnp.maximum(s[:, :LANE], s[:, LANE:]))

    def logit_body(j, carry):
        for hh in heads:
            keep(j, hh, logits(j, hh))
        return carry

    m_s[...] = jnp.full(m_s.shape, NEG, F32)
    lax.fori_loop(0, i * per_q, logit_body, 0)
    for dd in range(per_q):
        j = i * per_q + dd
        causal = row_minus_col >= dd * tk
        for hh in heads:
            keep(j, hh, jnp.where(causal, logits(j, hh), NEG))
    m_row = [jnp.broadcast_to(jnp.max(m_s[hh], axis=-1, keepdims=True), (tq, LANE)) for hh in heads]
    l_s[...] = jnp.zeros(l_s.shape, F32)
    acc_s[...] = jnp.zeros(acc_s.shape, F32)

    def prob_body(j, carry):
        start = pl.multiple_of(j * tk, tk)
        vs = v_ref[pl.ds(start, tk), :]
        for hh in heads:
            p_lo = jnp.exp(s_all[hh, j, :, :LANE] - m_row[hh])
            p_hi = jnp.exp(s_all[hh, j, :, LANE:] - m_row[hh])
            l_s[hh] = l_s[hh] + p_lo + p_hi
            p = jnp.concatenate([p_lo, p_hi], axis=1).astype(BF16)
            acc_s[hh] = acc_s[hh] + _dot(p, vs)
        return carry

    lax.fori_loop(0, (i + 1) * per_q, prob_body, 0)
    outs = [acc_s[hh] / jnp.sum(l_s[hh], axis=-1, keepdims=True) for hh in heads]
    o_ref[...] = jnp.where(lane < HEAD_DIM, outs[0], outs[1]).astype(o_ref.dtype)


def _fox_attention(q, k, v, c_col, c_row, bsz, t):
    n, hd = q.shape
    n_pair = hd // LANE
    nq = t // FOX_TQ
    nk = t // FOX_TK
    return pl.pallas_call(
        _fox_kernel,
        grid=(bsz, n_pair, nq),
        in_specs=[
            pl.BlockSpec((FOX_TQ, LANE), lambda b, j, i: (b * nq + i, j)),
            pl.BlockSpec((t, LANE), lambda b, j, i: (b, j)),
            pl.BlockSpec((t, LANE), lambda b, j, i: (b, j)),
            pl.BlockSpec((None, None, FOX_TQ, 2), lambda b, j, i: (b, j, i, 0)),
            pl.BlockSpec((None, None, 2, nk, FOX_TK), lambda b, j, i: (b, j, 0, 0, 0)),
        ],
        out_specs=pl.BlockSpec((FOX_TQ, LANE), lambda b, j, i: (b * nq + i, j)),
        out_shape=jax.ShapeDtypeStruct((n, hd), BF16),
        scratch_shapes=[pltpu.VMEM((2, nk, FOX_TQ, FOX_TK), F32), pltpu.VMEM((2, FOX_TQ, LANE), F32),
                        pltpu.VMEM((2, FOX_TQ, LANE), F32), pltpu.VMEM((2, FOX_TQ, LANE), F32)],
        compiler_params=_cparams(3), name="fox_attention",
    )(q, k, v, c_col, c_row)


def _swa_kernel(sink_ref, q_ref, kp_ref, kc_ref, vp_ref, vc_ref, bias_ref, o_ref):
    nblk = pl.program_id(1)
    w = q_ref.shape[0]
    kk = jnp.concatenate([kp_ref[...], kc_ref[...]], axis=0)
    vv = jnp.concatenate([vp_ref[...], vc_ref[...]], axis=0)
    lane = lax.broadcasted_iota(jnp.int32, (w, LANE), 1)
    col = lax.broadcasted_iota(jnp.int32, (w, 2 * w), 1)
    no_prev = (col < w) & (nblk == 0)
    grp = A_HEADS // A_KV_HEADS
    for g in range(grp):
        qg = q_ref[:, g * LANE:(g + 1) * LANE]
        outs = []
        for hh in range(A_KV_HEADS):
            head = g + grp * hh
            in_half = (lane >= hh * HEAD_DIM) & (lane < (hh + 1) * HEAD_DIM)
            qh = jnp.where(in_half, qg, jnp.zeros_like(qg)) * jnp.asarray(HEAD_DIM ** -0.5, BF16)
            s = _dot_nt(qh, kk) + bias_ref[head]
            s = jnp.where(no_prev, NEG, s)
            sink = sink_ref[head]
            m = jnp.maximum(jnp.max(s, axis=-1, keepdims=True), sink)
            e = jnp.exp(s - m)
            denom = jnp.sum(e, axis=-1, keepdims=True) + jnp.exp(sink - m)
            p = e / denom
            outs.append(_dot(p.astype(BF16), vv))
        o_ref[:, g * LANE:(g + 1) * LANE] = jnp.where(lane < HEAD_DIM, outs[0], outs[1]).astype(o_ref.dtype)


def _swa_attention(q, k, v, sinks, bias, bsz, t):
    n = q.shape[0]
    nb = t // WINDOW
    cur = lambda b, i: (b * nb + i, 0)
    prev = lambda b, i: (b * nb + jnp.maximum(i - 1, 0), 0)
    return pl.pallas_call(
        _swa_kernel,
        grid=(bsz, nb),
        in_specs=[
            pl.BlockSpec(memory_space=pltpu.SMEM),
            pl.BlockSpec((WINDOW, A_HEADS * HEAD_DIM), cur),
            pl.BlockSpec((WINDOW, LANE), prev), pl.BlockSpec((WINDOW, LANE), cur),
            pl.BlockSpec((WINDOW, LANE), prev), pl.BlockSpec((WINDOW, LANE), cur),
            _const_spec(bias.shape),
        ],
        out_specs=pl.BlockSpec((WINDOW, A_HEADS * HEAD_DIM), cur),
        out_shape=jax.ShapeDtypeStruct((n, A_HEADS * HEAD_DIM), BF16),
        compiler_params=_cparams(2), name="swa_attention",
    )(sinks.astype(F32), q, k, k, v, v, bias)


def _float_order_key(s):
    bits = pltpu.bitcast(s, jnp.int32)
    return bits ^ ((bits >> 31) & jnp.int32(0x7FFFFFFF))


def _dsa_kernel(q_ref, ckv_ref, qi_ref, ki_ref, wi_ref, wuk_ref, wuv_ref, tb_ref, o_ref,
                key_s, sel_s, qlat_s, qis_s, wib_s, s_all, p_buf, acc_s, m_s, l_s):
    n = pl.program_id(1)
    qb = Q_BLOCK
    ck = DSA_CK
    n_sel = TOPK_MAX
    n_chunks = n // (ck // qb) + 1
    lane = lax.broadcasted_iota(jnp.int32, (qb, LANE), 1)
    qpos = n * qb + lax.broadcasted_iota(jnp.int32, (qb, ck), 0)
    kcol = lax.broadcasted_iota(jnp.int32, (qb, ck), 1)

    for h in range(IDX_HEADS):
        pair = qi_ref[:, (h // 2) * LANE:(h // 2 + 1) * LANE]
        in_half = (lane >= (h % 2) * IDX_DIM) & (lane < (h % 2 + 1) * IDX_DIM)
        qis_s[h * qb:(h + 1) * qb, :] = jnp.where(in_half, pair, jnp.zeros_like(pair))

    for h in range(C_HEADS):
        pair = q_ref[:, (h // 2) * LANE:(h // 2 + 1) * LANE]
        ql = _dot(pair, wuk_ref[h]) * (HEAD_DIM ** -0.5)
        qlat_s[h * qb:(h + 1) * qb, :] = ql.astype(BF16)

    wi = wi_ref[...] * (IDX_HEADS ** -0.5)
    for h in range(IDX_HEADS):
        wib_s[h] = jnp.broadcast_to(wi[:, h:h + 1] * (IDX_DIM ** -0.5), (qb, ck))

    def score_body(c, _):
        start = pl.multiple_of(c * ck, ck)
        kc = ki_ref[pl.ds(start, ck), :]
        raw = _dot_nt(qis_s[...], kc)
        score = jnp.zeros((qb, ck), F32)
        for h in range(IDX_HEADS):
            idx_s = jnp.maximum(raw[h * qb:(h + 1) * qb, :], 0.0)
            score = score + idx_s * wib_s[h]
        causal = (start + kcol) <= qpos
        key_s[c] = _float_order_key(jnp.where(causal, score, NEG)).T
        sel_s[c] = jnp.where(causal, 0.0, NEG)
        return 0

    lax.fori_loop(0, n_chunks, score_body, 0)

    @pl.when((n + 1) * qb > n_sel)
    def _():
        def count_where(pred_fn):
            def body(c, acc):
                hit = jnp.where(pred_fn(key_s[c]), 1.0, 0.0)
                return acc + jnp.sum(hit.reshape(ck // 32, 32, qb), axis=0)
            acc = lax.fori_loop(0, n_chunks, body, jnp.zeros((32, qb), F32))
            return jnp.sum(acc, axis=0, keepdims=True)

        cnt0 = count_where(lambda kv: kv >= 0)
        prefix0 = jnp.where(cnt0 >= n_sel, jnp.int32(0), jnp.int32(INT_MIN))

        def bit_body(t, prefix):
            cand = prefix | jnp.left_shift(jnp.int32(1), 30 - t)
            cnt = count_where(lambda kv: kv >= cand)
            return jnp.where(cnt >= n_sel, cand, prefix)

        thr = lax.fori_loop(0, 31, bit_body, prefix0)
        need = n_sel - count_where(lambda kv: kv > thr)
        r = lax.broadcasted_iota(jnp.int32, (ck, ck), 0)
        cc = lax.broadcasted_iota(jnp.int32, (ck, ck), 1)
        earlier_key = jnp.where(cc < r, 1.0, 0.0).astype(BF16)

        def tie_body(c, seen):
            kv = key_s[c]
            tie = kv == thr
            tie_f = jnp.where(tie, 1.0, 0.0)
            earlier = _dot(earlier_key, tie_f.astype(BF16)) + seen
            sel = (kv > thr) | (tie & (earlier < need))
            sel_s[c] = jnp.minimum(sel_s[c], jnp.where(sel, 0.0, NEG).T)
            return seen + jnp.sum(tie_f, axis=0, keepdims=True)

        lax.fori_loop(0, n_chunks, tie_body, jnp.zeros((1, qb), F32))

    m_s[...] = jnp.full(m_s.shape, NEG, F32)

    def logit_body(c, _):
        start = pl.multiple_of(c * ck, ck)
        raw = _dot_nt(qlat_s[...], ckv_ref[pl.ds(start, ck), :])
        rel_blk = jnp.minimum(n - c * (ck // qb), 3)
        mask_add = sel_s[c]
        for h in range(C_HEADS):
            rows = slice(h * qb, (h + 1) * qb)
            s = raw[rows, :] + tb_ref[rel_blk, h] + mask_add
            s_all[c, rows, :] = s
            m_s[rows, :] = jnp.maximum(m_s[rows, :], jnp.maximum(s[:, :LANE], s[:, LANE:]))
        return 0

    lax.fori_loop(0, n_chunks, logit_body, 0)

    for h in range(C_HEADS):
        rows = slice(h * qb, (h + 1) * qb)
        m_s[rows, :] = jnp.broadcast_to(jnp.max(m_s[rows, :], axis=-1, keepdims=True), (qb, LANE))
    l_s[...] = jnp.zeros(l_s.shape, F32)
    acc_s[...] = jnp.zeros(acc_s.shape, F32)

    def prob_body(c, _):
        start = pl.multiple_of(c * ck, ck)
        for h in range(C_HEADS):
            rows = slice(h * qb, (h + 1) * qb)
            m_row = m_s[rows, :]
            p_lo = jnp.exp(s_all[c, rows, :LANE] - m_row)
            p_hi = jnp.exp(s_all[c, rows, LANE:] - m_row)
            l_s[rows, :] = l_s[rows, :] + p_lo + p_hi
            p_buf[rows, :LANE] = p_lo.astype(BF16)
            p_buf[rows, LANE:] = p_hi.astype(BF16)
        acc_s[...] = acc_s[...] + _dot(p_buf[...], ckv_ref[pl.ds(start, ck), :])
        return 0

    lax.fori_loop(0, n_chunks, prob_body, 0)

    for g in range(C_HEADS // 2):
        out = jnp.zeros((qb, LANE), F32)
        for hh in range(2):
            h = 2 * g + hh
            rows = slice(h * qb, (h + 1) * qb)
            denom = jnp.sum(l_s[rows, :], axis=-1, keepdims=True)
            o_lat = (acc_s[rows, :] / denom).astype(BF16)
            out = out + _dot(o_lat, wuv_ref[h])
        o_ref[:, g * LANE:(g + 1) * LANE] = out.astype(o_ref.dtype)


def _dsa_attention(q, ckv, qi, ki2, wi, wuk_wide, wuv_wide, tb, bsz, t):
    n = q.shape[0]
    nq = t // Q_BLOCK
    nck = t // DSA_CK
    rows = C_HEADS * Q_BLOCK
    blk = lambda b, i: (b * nq + i, 0)
    per_b = lambda b, i: (b, 0)
    return pl.pallas_call(
        _dsa_kernel,
        grid=(bsz, nq),
        in_specs=[
            pl.BlockSpec((Q_BLOCK, C_HEADS * HEAD_DIM), blk),
            pl.BlockSpec((t, C_LATENT), per_b),
            pl.BlockSpec((Q_BLOCK, IDX_HEADS * IDX_DIM), blk),
            pl.BlockSpec((t, LANE), per_b),
            pl.BlockSpec((Q_BLOCK, LANE), blk),
            _const_spec(wuk_wide.shape), _const_spec(wuv_wide.shape), _const_spec(tb.shape),
        ],
        out_specs=pl.BlockSpec((Q_BLOCK, C_HEADS * HEAD_DIM), blk),
        out_shape=jax.ShapeDtypeStruct((n, C_HEADS * HEAD_DIM), BF16),
        scratch_shapes=[
            pltpu.VMEM((nck, DSA_CK, Q_BLOCK), jnp.int32),
            pltpu.VMEM((nck, Q_BLOCK, DSA_CK), F32),
            pltpu.VMEM((rows, C_LATENT), BF16),
            pltpu.VMEM((IDX_HEADS * Q_BLOCK, LANE), BF16),
            pltpu.VMEM((IDX_HEADS, Q_BLOCK, DSA_CK), F32),
            pltpu.VMEM((nck, rows, DSA_CK), F32),
            pltpu.VMEM((rows, DSA_CK), BF16),
            pltpu.VMEM((rows, C_LATENT), F32),
            pltpu.VMEM((rows, LANE), F32),
            pltpu.VMEM((rows, LANE), F32),
        ],
        compiler_params=_cparams(2), name="dsa_attention",
    )(q, ckv, qi, ki2, wi, wuk_wide, wuv_wide, tb)


def _post_attn_kernel(*refs, n_mix, alpha):
    x_ref = refs[0]
    mix_refs = refs[1:1 + n_mix]
    w_refs = refs[1 + n_mix:1 + 2 * n_mix]
    g_ref, b_ref, wr_ref, br_ref = refs[1 + 2 * n_mix:5 + 2 * n_mix]
    xo_ref, idx_ref, gate_ref, rank_ref, cnt_ref = refs[5 + 2 * n_mix:10 + 2 * n_mix]
    scr = refs[10 + 2 * n_mix:]
    wb_refs = scr[:n_mix]
    wr3_ref, before_ref, carry_ref = scr[n_mix:]
    tm = x_ref.shape[0]

    @pl.when(pl.program_id(0) == 0)
    def _():
        for w_ref, wb_ref in zip(w_refs, wb_refs):
            wb_ref[...] = w_ref[...].astype(BF16)
        hi, mid, lo = _split3(wr_ref[...])
        wr3_ref[0] = hi
        wr3_ref[1] = mid
        wr3_ref[2] = lo
        r = lax.broadcasted_iota(jnp.int32, (tm, tm), 0)
        c = lax.broadcasted_iota(jnp.int32, (tm, tm), 1)
        before_ref[...] = jnp.where(r < c, 1.0, 0.0).astype(BF16)
        carry_ref[...] = jnp.zeros(carry_ref.shape, F32)

    mix = _dot(mix_refs[0][...], wb_refs[0][...])
    for i in range(1, n_mix):
        mix = mix + _dot(mix_refs[i][...], wb_refs[i][...])
    xn = _layer_norm(alpha * x_ref[...] + mix, g_ref[...], b_ref[...])
    xo_ref[...] = xn

    xh, xm, xl = _split3(xn)
    logits = (_dot_nt(wr3_ref[0], xh) + _dot_nt(wr3_ref[0], xm) + _dot_nt(wr3_ref[1], xh)
              + _dot_nt(wr3_ref[0], xl) + _dot_nt(wr3_ref[1], xm) + _dot_nt(wr3_ref[2], xh)
              + br_ref[...])
    eidx = lax.broadcasted_iota(jnp.int32, logits.shape, 0)
    cur = logits
    vals, idxs = [], []
    for _ in range(TOP_K):
        mx = jnp.max(cur, axis=0, keepdims=True)
        first = jnp.min(jnp.where(cur == mx, eidx, N_EXPERTS), axis=0, keepdims=True)
        vals.append(mx)
        idxs.append(first)
        cur = jnp.where(eidx == first, -jnp.inf, cur)
    es = [jnp.exp(v - vals[0]) for v in vals]
    tot = es[0] + es[1] + es[2] + es[3]
    member = jnp.zeros(logits.shape, F32)
    for k in range(TOP_K):
        idx_ref[k:k + 1, :] = idxs[k]
        gate_ref[k:k + 1, :] = es[k] / tot
        member = member + jnp.where(eidx == idxs[k], 1.0, 0.0)
    earlier = _dot(member.astype(BF16), before_ref[...]) + carry_ref[...]
    for k in range(TOP_K):
        rk = jnp.sum(jnp.where(eidx == idxs[k], earlier, 0.0), axis=0, keepdims=True)
        rank_ref[k:k + 1, :] = rk.astype(jnp.int32)
    carry_ref[...] = carry_ref[...] + jnp.sum(member, axis=1, keepdims=True)
    cnt_ref[...] = carry_ref[...].astype(jnp.int32)


def _post_attn(x2, mixes, w_os, ln_g, ln_b, w_router, b_router, alpha):
    n, d = x2.shape
    n_mix = len(mixes)
    tm = TM_PROJ
    row = lambda i: (i, 0)
    colblk = lambda i: (0, i)
    in_specs = [pl.BlockSpec((tm, d), row)]
    in_specs += [pl.BlockSpec((tm, m.shape[1]), row) for m in mixes]
    in_specs += [_const_spec(w.shape) for w in w_os]
    in_specs += [_const_spec((1, d)), _const_spec((1, d)), _const_spec((N_EXPERTS, d)),
                 _const_spec((N_EXPERTS, 1))]
    out_specs = [pl.BlockSpec((tm, d), row), pl.BlockSpec((TOP_K, tm), colblk),
                 pl.BlockSpec((TOP_K, tm), colblk), pl.BlockSpec((TOP_K, tm), colblk),
                 _const_spec((N_EXPERTS, 1))]
    out_shape = [jax.ShapeDtypeStruct((n, d), F32), jax.ShapeDtypeStruct((TOP_K, n), jnp.int32),
                 jax.ShapeDtypeStruct((TOP_K, n), F32), jax.ShapeDtypeStruct((TOP_K, n), jnp.int32),
                 jax.ShapeDtypeStruct((N_EXPERTS, 1), jnp.int32)]
    scratch = [pltpu.VMEM(w.shape, BF16) for w in w_os]
    scratch += [pltpu.VMEM((3, N_EXPERTS, d), BF16), pltpu.VMEM((tm, tm), BF16),
                pltpu.VMEM((N_EXPERTS, 1), F32)]
    return pl.pallas_call(
        functools.partial(_post_attn_kernel, n_mix=n_mix, alpha=alpha),
        grid=(n // tm,),
        in_specs=in_specs, out_specs=out_specs, out_shape=out_shape, scratch_shapes=scratch,
        compiler_params=_cparams(1), name="post_attn",
    )(x2, *mixes, *w_os, ln_g.reshape(1, d), ln_b.reshape(1, d), w_router.T,
      b_router.reshape(N_EXPERTS, 1))


def _moe_kernel(be_ref, nu_ref, src_ref, x_hbm, wgu_ref, bgu_ref, wd_ref, bd_ref, y_ref,
                wgu_s, wd_s, xbuf, sem):
    i = pl.program_id(0)
    n_steps = pl.num_programs(0)
    tm = y_ref.shape[0]
    prev = be_ref[jnp.maximum(i - 1, 0)]
    n_used = nu_ref[0]
    live = i < n_used
    n_slots = xbuf.shape[0]
    ahead = n_slots - 1
    slot = i % n_slots

    def row_copy(block, s, r):
        tok = src_ref[block * tm + r]
        return pltpu.make_async_copy(x_hbm.at[pl.ds(tok, 1), :], xbuf.at[s, pl.ds(r, 1), :],
                                     sem.at[s])

    def wait_rows(s):
        pltpu.make_async_copy(xbuf.at[s], xbuf.at[s], sem.at[s]).wait()

    @pl.when(i == 0)
    def _():
        for b in range(ahead):
            def body(r, carry, b=b):
                row_copy(b, b, r).start(priority=1)
                return carry
            lax.fori_loop(0, tm, body, 0, unroll=8)

    @pl.when(live & ((i == 0) | (be_ref[i] != prev)))
    def _():
        wgu_s[...] = wgu_ref[...].astype(BF16)
        wd_s[...] = wd_ref[...].astype(BF16)

    @pl.when(live)
    def _():
        wait_rows(slot)
        xb = xbuf[slot].astype(BF16)
        nxt = jnp.minimum(i + ahead, n_steps - 1)
        for r in range(tm):
            row_copy(nxt, (i + ahead) % n_slots, r).start(priority=1)
        ch = 1024
        acc = jnp.zeros(y_ref.shape, F32)
        for j in range(D_FF // ch):
            hg = _dot(xb, wgu_s[:, j * ch:(j + 1) * ch]) + bgu_ref[:, j * ch:(j + 1) * ch]
            hl = (_dot(xb, wgu_s[:, D_FF + j * ch:D_FF + (j + 1) * ch])
                  + bgu_ref[:, D_FF + j * ch:D_FF + (j + 1) * ch])
            glu = jnp.minimum(hg, SWIGLU_LIMIT)
            lin = jnp.clip(hl, -SWIGLU_LIMIT, SWIGLU_LIMIT)
            act = glu * jax.nn.sigmoid(SWIGLU_ALPHA * glu) * (lin + 1.0)
            acc = acc + _dot(act.astype(BF16), wd_s[j * ch:(j + 1) * ch, :])
        y_ref[...] = (acc + bd_ref[...]).astype(y_ref.dtype)

    @pl.when(i == n_used)
    def _():
        for b in range(ahead):
            wait_rows((i + b) % n_slots)

    @pl.when(jnp.logical_not(live))
    def _():
        y_ref[...] = jnp.zeros(y_ref.shape, y_ref.dtype)


def _moe_ffn(x2, src_tok, block_e, n_used, layer, w_gate_up, b_gate_up, w_down, b_down):
    d = x2.shape[1]
    rows = src_tok.shape[0]
    depth = w_gate_up.shape[0]
    n_blocks = rows // TM_MOE
    expert = lambda i, be, nu, src: (layer, be[i], 0, 0)
    grid_spec = pltpu.PrefetchScalarGridSpec(
        num_scalar_prefetch=3,
        grid=(n_blocks,),
        in_specs=[
            pl.BlockSpec(memory_space=pl.ANY),
            pl.BlockSpec((None, None, d, 2 * D_FF), expert),
            pl.BlockSpec((None, None, 1, 2 * D_FF), expert),
            pl.BlockSpec((None, None, D_FF, d), expert),
            pl.BlockSpec((None, None, 1, d), expert),
        ],
        out_specs=pl.BlockSpec((TM_MOE, d), lambda i, be, nu, src: (i, 0)),
        scratch_shapes=[pltpu.VMEM((d, 2 * D_FF), BF16), pltpu.VMEM((D_FF, d), BF16),
                        pltpu.VMEM((MOE_SLOTS, TM_MOE, d), F32),
                        pltpu.SemaphoreType.DMA((MOE_SLOTS,))],
    )
    return pl.pallas_call(
        _moe_kernel, grid_spec=grid_spec,
        out_shape=jax.ShapeDtypeStruct((rows, d), F32),
        compiler_params=_cparams(1), name="moe_ffn",
    )(block_e, n_used, src_tok, x2, w_gate_up, b_gate_up.reshape(depth, N_EXPERTS, 1, 2 * D_FF),
      w_down, b_down.reshape(depth, N_EXPERTS, 1, d))


def _post_moe_kernel(dest_ref, x_ref, y_hbm, gt_ref, g_ref, b_ref, p_ref, wg_ref, wp_ref, o_ref,
                     wg_s, wp_s, ybuf, sem, *, alpha, n_tok):
    i = pl.program_id(0)
    n_steps = pl.num_programs(0)
    tm, d = x_ref.shape

    def row_copy(tile, slot, k, j):
        r = dest_ref[k * n_tok + tile * tm + j]
        return pltpu.make_async_copy(y_hbm.at[pl.ds(r, 1), :], ybuf.at[slot, k, pl.ds(j, 1), :],
                                     sem.at[slot])

    @pl.when(i == 0)
    def _():
        wg_s[...] = wg_ref[...].astype(BF16)
        wp_s[...] = wp_ref[...].astype(BF16)

        def body(j, carry):
            for k in range(TOP_K):
                row_copy(0, 0, k, j).start()
            return carry
        lax.fori_loop(0, tm, body, 0, unroll=8)

    @pl.when(i + 1 < n_steps)
    def _():
        for j in range(tm):
            for k in range(TOP_K):
                row_copy(i + 1, (i + 1) % 2, k, j).start(priority=k % 2)

    slot = i % 2
    pltpu.make_async_copy(ybuf.at[slot], ybuf.at[slot], sem.at[slot]).wait()
    gates = gt_ref[...]
    ffn = gates[:, 0:1] * ybuf[slot, 0]
    for k in range(1, TOP_K):
        ffn = ffn + gates[:, k:k + 1] * ybuf[slot, k]
    xn = _layer_norm(alpha * x_ref[...] + ffn, g_ref[...], b_ref[...])
    gate = jax.nn.sigmoid(_dot(xn.astype(BF16), wg_s[...]))
    emb = _dot(p_ref[...].astype(BF16), wp_s[...])
    o_ref[...] = xn + gate * emb


def _post_moe(x2, y_pad, dest_flat, gates_t, ln_g, ln_b, layer, p3, w_ple_gate, w_ple_proj, alpha):
    n, d = x2.shape
    tm = TM_POST
    row = lambda i, dest: (i, 0)
    const2 = lambda i, dest: (0, 0)
    grid_spec = pltpu.PrefetchScalarGridSpec(
        num_scalar_prefetch=1,
        grid=(n // tm,),
        in_specs=[pl.BlockSpec((tm, d), row), pl.BlockSpec(memory_space=pl.ANY),
                  pl.BlockSpec((tm, TOP_K), row), pl.BlockSpec((1, d), const2),
                  pl.BlockSpec((1, d), const2),
                  pl.BlockSpec((None, tm, PLE_DIM), lambda i, dest: (layer, i, 0)),
                  pl.BlockSpec((None, d, d), lambda i, dest: (layer, 0, 0)),
                  pl.BlockSpec((None, PLE_DIM, d), lambda i, dest: (layer, 0, 0))],
        out_specs=pl.BlockSpec((tm, d), row),
        scratch_shapes=[pltpu.VMEM((d, d), BF16), pltpu.VMEM((PLE_DIM, d), BF16),
                        pltpu.VMEM((2, TOP_K, tm, d), F32), pltpu.SemaphoreType.DMA((2,))],
    )
    return pl.pallas_call(
        functools.partial(_post_moe_kernel, alpha=alpha, n_tok=n),
        grid_spec=grid_spec,
        out_shape=jax.ShapeDtypeStruct((n, d), F32),
        compiler_params=_cparams(1), name="post_moe",
    )(dest_flat, x2, y_pad, gates_t, ln_g.reshape(1, d), ln_b.reshape(1, d), p3, w_ple_gate,
      w_ple_proj)


def _t5_bucket(dist):
    d = jnp.maximum(dist, 0)
    ratio = jnp.log(jnp.maximum(d, 1).astype(F32) / MAX_EXACT) / math.log(MAX_DISTANCE / MAX_EXACT)
    large = MAX_EXACT + (ratio * (NUM_BUCKETS - MAX_EXACT)).astype(jnp.int32)
    large = jnp.minimum(large, NUM_BUCKETS - 1)
    return jnp.where(d < MAX_EXACT, d, large)


def _swa_bias_table(pos_bias):
    qi = jnp.arange(WINDOW)[:, None]
    kj = jnp.arange(2 * WINDOW)[None, :]
    dist = qi + WINDOW - kj
    valid = (dist >= 0) & (dist < WINDOW)
    bias = pos_bias[_t5_bucket(dist)][..., :A_HEADS].astype(F32).transpose(2, 0, 1)
    return jnp.where(valid[None], bias, NEG)


def _dsa_bias_tables(pos_bias):
    qi = jnp.arange(Q_BLOCK)[:, None]
    kj = jnp.arange(DSA_CK)[None, :]
    tabs = []
    for r in range(4):
        dist = r * Q_BLOCK + qi - kj
        tabs.append(pos_bias[_t5_bucket(dist)].astype(F32).transpose(2, 0, 1))
    return jnp.stack(tabs)


def _pad_cols(w, width=LANE):
    return jnp.pad(w, ((0, 0), (0, width - w.shape[1])))


def _even_mixer(x2, bsz, t, w_in, b_forget, sinks, w_o, swa_bias):
    hd = HEAD_DIM
    grp = A_HEADS // A_KV_HEADS
    pair_order = [g + grp * hh for g in range(grp) for hh in range(A_KV_HEADS)]
    na, nkv, nb = A_HEADS * hd, A_KV_HEADS * hd, B_HEADS * hd
    offs = [0, na, na + nkv, na + 2 * nkv, na + 2 * nkv + nb, na + 2 * nkv + 2 * nb,
            na + 2 * nkv + 3 * nb]
    w_aq = w_in[:, offs[0]:offs[1]].reshape(D_MODEL, A_HEADS, hd)[:, pair_order].reshape(D_MODEL, na)
    weights = [w_aq, w_in[:, offs[1]:offs[2]], w_in[:, offs[2]:offs[3]], w_in[:, offs[3]:offs[4]],
               w_in[:, offs[4]:offs[5]], w_in[:, offs[5]:offs[6]], _pad_cols(w_in[:, offs[6]:])]
    aq, ak, av, bq, bk, bv, bf = _project(x2, weights, [BF16] * 6 + [F32])
    o_a = _swa_attention(aq, ak, av, sinks, swa_bias, bsz, t)

    f_t = bf[:, :B_HEADS].reshape(bsz, t, B_HEADS).transpose(0, 2, 1)
    c = _forget_cumsum(f_t, b_forget)
    c_row = c.reshape(bsz, B_HEADS // 2, 2, t // FOX_TK, FOX_TK)
    c_col = c.reshape(bsz, B_HEADS // 2, 2, t).transpose(0, 1, 3, 2)
    o_b = _fox_attention(bq, bk, bv, c_col, c_row, bsz, t)

    w_oa = w_o[:na].reshape(A_HEADS, hd, D_MODEL)[jnp.asarray(pair_order)].reshape(na, D_MODEL)
    return [o_a, o_b], [w_oa, w_o[na:]]


def _odd_mixer(x2, bsz, t, w_in, kv_norm, w_uk, w_uv, w_o, dsa_tb):
    hd = HEAD_DIM
    nq, nl, ni = C_HEADS * hd, C_LATENT, IDX_HEADS * IDX_DIM
    w_ki = w_in[:, nq + nl + ni:nq + nl + ni + IDX_DIM]
    weights = [w_in[:, :nq], w_in[:, nq:nq + nl], w_in[:, nq + nl:nq + nl + ni],
               jnp.concatenate([w_ki, w_ki], axis=1), _pad_cols(w_in[:, nq + nl + ni + IDX_DIM:])]
    cq, ckv, qi, ki2, wi = _project(x2, weights, [BF16, BF16, BF16, BF16, F32],
                                    rms_gain=kv_norm.reshape(1, nl).astype(F32), rms_index=1)
    uk = w_uk.transpose(1, 2, 0)
    uv = w_uv.transpose(1, 0, 2)
    zk = jnp.zeros_like(uk)
    zv = jnp.zeros_like(uv)
    odd = (jnp.arange(C_HEADS) % 2 == 1)[:, None, None]
    wuk_wide = jnp.where(odd, jnp.concatenate([zk, uk], axis=1), jnp.concatenate([uk, zk], axis=1))
    wuv_wide = jnp.where(odd, jnp.concatenate([zv, uv], axis=2), jnp.concatenate([uv, zv], axis=2))
    o_c = _dsa_attention(cq, ckv, qi, ki2, wi, wuk_wide.astype(BF16), wuv_wide.astype(BF16),
                         dsa_tb, bsz, t)
    return [o_c], [w_o]


def _dispatch_plan(top_idx, rank, counts, n_tok):
    counts = counts.reshape(N_EXPERTS)
    padded = (counts + TM_MOE - 1) // TM_MOE * TM_MOE
    pad_ends = jnp.cumsum(padded)
    pad_starts = pad_ends - padded
    is_e = top_idx[..., None] == jnp.arange(N_EXPERTS, dtype=jnp.int32)
    dest = jnp.sum(jnp.where(is_e, pad_starts.astype(jnp.int32), 0), axis=-1) + rank
    n_blocks = (n_tok * TOP_K) // TM_MOE + N_EXPERTS
    block_start = jnp.arange(n_blocks, dtype=jnp.int32) * TM_MOE
    block_e = jnp.minimum(jnp.sum(pad_ends[None, :] <= block_start[:, None], axis=1),
                          N_EXPERTS - 1).astype(jnp.int32)
    n_used = (pad_ends[-1] // TM_MOE).astype(jnp.int32).reshape(1)
    return dest, block_e, n_used, n_blocks


def kernel(x, p, pos_bias, w_in_even, b_forget, sinks, w_o_even, w_in_odd, kv_norm, w_uk, w_uv,
           w_o_odd, ln_g, ln_b, w_router, b_router, w_gate_up, b_gate_up, w_down, b_down,
           w_ple_proj, w_ple_gate):
    bsz, t, d = x.shape
    depth = ln_g.shape[0]
    alpha = (2 * depth) ** 0.25
    n_tok = bsz * t
    x2 = x.reshape(n_tok, d)
    swa_bias = _swa_bias_table(pos_bias)
    dsa_tb = _dsa_bias_tables(pos_bias)
    p3 = p.reshape(depth, n_tok, PLE_DIM)
    tok_ids = jnp.tile(jnp.arange(n_tok, dtype=jnp.int32), (TOP_K,))
    for i in range(depth):
        j = i // 2
        if i % 2 == 0:
            mixes, w_os = _even_mixer(x2, bsz, t, w_in_even[j], b_forget[j], sinks[j], w_o_even[j],
                                      swa_bias)
        else:
            mixes, w_os = _odd_mixer(x2, bsz, t, w_in_odd[j], kv_norm[j], w_uk[j], w_uv[j],
                                     w_o_odd[j], dsa_tb)
        x1, top_idx, gates, rank, counts = _post_attn(x2, mixes, w_os, ln_g[i, 0], ln_b[i, 0],
                                                      w_router[i], b_router[i], alpha)
        dest, block_e, n_used, n_blocks = _dispatch_plan(top_idx, rank, counts, n_tok)
        dest_flat = dest.reshape(-1)
        src_tok = jnp.zeros((n_blocks * TM_MOE,), jnp.int32).at[dest_flat].set(
            tok_ids, unique_indices=True, mode="promise_in_bounds")
        y_pad = _moe_ffn(x1, src_tok, block_e, n_used, i, w_gate_up, b_gate_up, w_down, b_down)
        x2 = _post_moe(x1, y_pad, dest_flat, gates.T, ln_g[i, 1], ln_b[i, 1], i, p3,
                       w_ple_gate, w_ple_proj, alpha)
    return x2.reshape(bsz, t, d)
```

```python
import functools
import math

import jax
import jax.numpy as jnp
from jax import lax
from jax.experimental import pallas as pl
from jax.experimental.pallas import tpu as pltpu

D_MODEL = 1024
HEAD_DIM = 64
A_HEADS, A_KV_HEADS, WINDOW = 8, 2, 128
B_HEADS = 8
C_HEADS, C_LATENT = 16, 128
IDX_HEADS, IDX_DIM = 8, 64
TOPK_MAX = 256
NUM_BUCKETS, MAX_EXACT, MAX_DISTANCE = 32, 16, 128
N_EXPERTS, TOP_K, D_FF = 32, 4, 1024
SWIGLU_LIMIT, SWIGLU_ALPHA = 7.0, 1.702
PLE_DIM = 256
Q_BLOCK = 128
LN_EPS = 1e-5
NEG = -1e30

LANE = 128
VMEM_LIMIT = 56 * 1024 * 1024
TM_PROJ = 512
TM_MOE = 256
MOE_SLOTS = 4
TM_POST = 256
FOX_TQ = 512
FOX_TK = 2 * LANE
DSA_CK = 256
INT_MIN = -2 ** 31

BF16 = jnp.bfloat16
F32 = jnp.float32


def _cparams(n_axes):
    return pltpu.CompilerParams(dimension_semantics=("arbitrary",) * n_axes,
                                vmem_limit_bytes=VMEM_LIMIT)


def _const_spec(shape):
    nd = len(shape)
    return pl.BlockSpec(shape, lambda *_: (0,) * nd)


def _dot(a, b):
    return jnp.dot(a, b, preferred_element_type=F32)


def _dot_nt(a, b):
    return lax.dot_general(a, b, (((1,), (1,)), ((), ())), preferred_element_type=F32)


def _layer_norm(v, g, b):
    mu = jnp.mean(v, axis=-1, keepdims=True)
    d = v - mu
    var = jnp.mean(d * d, axis=-1, keepdims=True)
    return d * lax.rsqrt(var + LN_EPS) * g + b


def _split3(v):
    hi = v.astype(BF16)
    r1 = v - hi.astype(F32)
    mid = r1.astype(BF16)
    lo = (r1 - mid.astype(F32)).astype(BF16)
    return hi, mid, lo


def _proj_kernel(*refs, n_w, rms_index):
    x_ref = refs[0]
    w_refs = refs[1:1 + n_w]
    g_ref = refs[1 + n_w]
    o_refs = refs[2 + n_w:2 + 2 * n_w]
    wb_refs = refs[2 + 2 * n_w:]

    @pl.when(pl.program_id(0) == 0)
    def _():
        for w_ref, wb_ref in zip(w_refs, wb_refs):
            wb_ref[...] = w_ref[...].astype(BF16)

    xb = x_ref[...].astype(BF16)
    for i in range(n_w):
        h = _dot(xb, wb_refs[i][...])
        if i == rms_index:
            h = h * lax.rsqrt(jnp.mean(h * h, axis=-1, keepdims=True) + LN_EPS) * g_ref[...]
        o_refs[i][...] = h.astype(o_refs[i].dtype)


def _project(x2, weights, out_dtypes, rms_gain=None, rms_index=-1):
    n, d = x2.shape
    n_w = len(weights)
    if rms_gain is None:
        rms_gain = jnp.ones((1, LANE), F32)
    in_specs = [pl.BlockSpec((TM_PROJ, d), lambda i: (i, 0))]
    in_specs += [_const_spec(w.shape) for w in weights]
    in_specs += [_const_spec(rms_gain.shape)]
    out_specs = [pl.BlockSpec((TM_PROJ, w.shape[1]), lambda i: (i, 0)) for w in weights]
    out_shape = [jax.ShapeDtypeStruct((n, w.shape[1]), dt) for w, dt in zip(weights, out_dtypes)]
    scratch = [pltpu.VMEM(w.shape, BF16) for w in weights]
    return pl.pallas_call(
        functools.partial(_proj_kernel, n_w=n_w, rms_index=rms_index),
        grid=(n // TM_PROJ,),
        in_specs=in_specs, out_specs=out_specs, out_shape=out_shape,
        scratch_shapes=scratch, compiler_params=_cparams(1), name="in_proj",
    )(x2, *weights, rms_gain)


def _cumsum_kernel(f_ref, b_ref, o_ref):
    z = f_ref[...] + b_ref[...]
    ls = jnp.minimum(z, 0.0) - jnp.log1p(jnp.exp(-jnp.abs(z)))
    t = ls.shape[1]
    r = lax.broadcasted_iota(jnp.int32, (LANE, LANE), 0)
    c = lax.broadcasted_iota(jnp.int32, (LANE, LANE), 1)
    tri = jnp.where(r <= c, 1.0, 0.0).astype(BF16)
    carry = jnp.zeros((ls.shape[0], 1), F32)
    for blk in range(t // LANE):
        seg = ls[:, blk * LANE:(blk + 1) * LANE]
        hi, mid, lo = _split3(seg)
        pre = _dot(hi, tri) + _dot(mid, tri) + _dot(lo, tri) + carry
        o_ref[:, blk * LANE:(blk + 1) * LANE] = pre
        carry = pre[:, LANE - 1:LANE]


def _forget_cumsum(f_t, b_forget):
    bsz, h, t = f_t.shape
    return pl.pallas_call(
        _cumsum_kernel,
        grid=(bsz,),
        in_specs=[pl.BlockSpec((None, h, t), lambda b: (b, 0, 0)), _const_spec((h, 1))],
        out_specs=pl.BlockSpec((None, h, t), lambda b: (b, 0, 0)),
        out_shape=jax.ShapeDtypeStruct((bsz, h, t), F32),
        compiler_params=_cparams(1), name="forget_cumsum",
    )(f_t, b_forget.reshape(h, 1).astype(F32))


def _fox_kernel(q_ref, k_ref, v_ref, cq_ref, ck_ref, o_ref, s_all, m_s, l_s, acc_s):
    i = pl.program_id(2)
    tq = q_ref.shape[0]
    tk = FOX_TK
    per_q = tq // tk
    q = q_ref[...]
    lane = lax.broadcasted_iota(jnp.int32, (tq, LANE), 1)
    row_minus_col = (lax.broadcasted_iota(jnp.int32, (tq, tk), 0)
                     - lax.broadcasted_iota(jnp.int32, (tq, tk), 1))
    heads = range(2)
    qh, cq = [], []
    for hh in heads:
        in_half = (lane >= hh * HEAD_DIM) & (lane < (hh + 1) * HEAD_DIM)
        qh.append(jnp.where(in_half, q, jnp.zeros_like(q)) * jnp.asarray(HEAD_DIM ** -0.5, BF16))
        cq.append(cq_ref[:, hh:hh + 1])

    def logits(j, hh):
        start = pl.multiple_of(j * tk, tk)
        return _dot_nt(qh[hh], k_ref[pl.ds(start, tk), :]) + cq[hh] - ck_ref[hh, pl.ds(j, 1), :]

    def keep(j, hh, s):
        s_all[hh, j] = s
        m_s[hh] = jnp.maximum(m_s[hh], jnp.maximum(s[:, :LANE], s[:, LANE:]))

    def logit_body(j, carry):
        for hh in heads:
            keep(j, hh, logits(j, hh))
        return carry

    m_s[...] = jnp.full(m_s.shape, NEG, F32)
    lax.fori_loop(0, i * per_q, logit_body, 0)
    for dd in range(per_q):
        j = i * per_q + dd
        causal = row_minus_col >= dd * tk
        for hh in heads:
            keep(j, hh, jnp.where(causal, logits(j, hh), NEG))
    m_row = [jnp.broadcast_to(jnp.max(m_s[hh], axis=-1, keepdims=True), (tq, LANE)) for hh in heads]
    l_s[...] = jnp.zeros(l_s.shape, F32)
    acc_s[...] = jnp.zeros(acc_s.shape, F32)

    def prob_body(j, carry):
        start = pl.multiple_of(j * tk, tk)
        vs = v_ref[pl.ds(start, tk), :]
        for hh in heads:
            p_lo = jnp.exp(s_all[hh, j, :, :LANE] - m_row[hh])
            p_hi = jnp.exp(s_all[hh, j, :, LANE:] - m_row[hh])
            l_s[hh] = l_s[hh] + p_lo + p_hi
            p = jnp.concatenate([p_lo, p_hi], axis=1).astype(BF16)
            acc_s[hh] = acc_s[hh] + _dot(p, vs)
        return carry

    lax.fori_loop(0, (i + 1) * per_q, prob_body, 0)
    outs = [acc_s[hh] / jnp.sum(l_s[hh], axis=-1, keepdims=True) for hh in heads]
    o_ref[...] = jnp.where(lane < HEAD_DIM, outs[0], outs[1]).astype(o_ref.dtype)


def _fox_attention(q, k, v, c_col, c_row, bsz, t):
    n, hd = q.shape
    n_pair = hd // LANE
    nq = t // FOX_TQ
    nk = t // FOX_TK
    return pl.pallas_call(
        _fox_kernel,
        grid=(bsz, n_pair, nq),
        in_specs=[
            pl.BlockSpec((FOX_TQ, LANE), lambda b, j, i: (b * nq + i, j)),
            pl.BlockSpec((t, LANE), lambda b, j, i: (b, j)),
            pl.BlockSpec((t, LANE), lambda b, j, i: (b, j)),
            pl.BlockSpec((None, None, FOX_TQ, 2), lambda b, j, i: (b, j, i, 0)),
            pl.BlockSpec((None, None, 2, nk, FOX_TK), lambda b, j, i: (b, j, 0, 0, 0)),
        ],
        out_specs=pl.BlockSpec((FOX_TQ, LANE), lambda b, j, i: (b * nq + i, j)),
        out_shape=jax.ShapeDtypeStruct((n, hd), BF16),
        scratch_shapes=[pltpu.VMEM((2, nk, FOX_TQ, FOX_TK), F32), pltpu.VMEM((2, FOX_TQ, LANE), F32),
                        pltpu.VMEM((2, FOX_TQ, LANE), F32), pltpu.VMEM((2, FOX_TQ, LANE), F32)],
        compiler_params=_cparams(3), name="fox_attention",
    )(q, k, v, c_col, c_row)


def _swa_kernel(sink_ref, q_ref, kp_ref, kc_ref, vp_ref, vc_ref, bias_ref, o_ref):
    nblk = pl.program_id(1)
    w = q_ref.shape[0]
    kk = jnp.concatenate([kp_ref[...], kc_ref[...]], axis=0)
    vv = jnp.concatenate([vp_ref[...], vc_ref[...]], axis=0)
    lane = lax.broadcasted_iota(jnp.int32, (w, LANE), 1)
    col = lax.broadcasted_iota(jnp.int32, (w, 2 * w), 1)
    no_prev = (col < w) & (nblk == 0)
    grp = A_HEADS // A_KV_HEADS
    for g in range(grp):
        qg = q_ref[:, g * LANE:(g + 1) * LANE]
        outs = []
        for hh in range(A_KV_HEADS):
            head = g + grp * hh
            in_half = (lane >= hh * HEAD_DIM) & (lane < (hh + 1) * HEAD_DIM)
            qh = jnp.where(in_half, qg, jnp.zeros_like(qg)) * jnp.asarray(HEAD_DIM ** -0.5, BF16)
            s = _dot_nt(qh, kk) + bias_ref[head]
            s = jnp.where(no_prev, NEG, s)
            sink = sink_ref[head]
            m = jnp.maximum(jnp.max(s, axis=-1, keepdims=True), sink)
            e = jnp.exp(s - m)
            denom = jnp.sum(e, axis=-1, keepdims=True) + jnp.exp(sink - m)
            p = e / denom
            outs.append(_dot(p.astype(BF16), vv))
        o_ref[:, g * LANE:(g + 1) * LANE] = jnp.where(lane < HEAD_DIM, outs[0], outs[1]).astype(o_ref.dtype)


def _swa_attention(q, k, v, sinks, bias, bsz, t):
    n = q.shape[0]
    nb = t // WINDOW
    cur = lambda b, i: (b * nb + i, 0)
    prev = lambda b, i: (b * nb + jnp.maximum(i - 1, 0), 0)
    return pl.pallas_call(
        _swa_kernel,
        grid=(bsz, nb),
        in_specs=[
            pl.BlockSpec(memory_space=pltpu.SMEM),
            pl.BlockSpec((WINDOW, A_HEADS * HEAD_DIM), cur),
            pl.BlockSpec((WINDOW, LANE), prev), pl.BlockSpec((WINDOW, LANE), cur),
            pl.BlockSpec((WINDOW, LANE), prev), pl.BlockSpec((WINDOW, LANE), cur),
            _const_spec(bias.shape),
        ],
        out_specs=pl.BlockSpec((WINDOW, A_HEADS * HEAD_DIM), cur),
        out_shape=jax.ShapeDtypeStruct((n, A_HEADS * HEAD_DIM), BF16),
        compiler_params=_cparams(2), name="swa_attention",
    )(sinks.astype(F32), q, k, k, v, v, bias)


def _float_order_key(s):
    bits = pltpu.bitcast(s, jnp.int32)
    return bits ^ ((bits >> 31) & jnp.int32(0x7FFFFFFF))


def _dsa_kernel(q_ref, ckv_ref, qi_ref, ki_ref, wi_ref, wuk_ref, wuv_ref, tb_ref, o_ref,
                key_s, sel_s, qlat_s, qis_s, wib_s, s_all, p_buf, acc_s, m_s, l_s):
    n = pl.program_id(1)
    qb = Q_BLOCK
    ck = DSA_CK
    n_sel = TOPK_MAX
    n_chunks = n // (ck // qb) + 1
    lane = lax.broadcasted_iota(jnp.int32, (qb, LANE), 1)
    qpos = n * qb + lax.broadcasted_iota(jnp.int32, (qb, ck), 0)
    kcol = lax.broadcasted_iota(jnp.int32, (qb, ck), 1)

    for h in range(IDX_HEADS):
        pair = qi_ref[:, (h // 2) * LANE:(h // 2 + 1) * LANE]
        in_half = (lane >= (h % 2) * IDX_DIM) & (lane < (h % 2 + 1) * IDX_DIM)
        qis_s[h * qb:(h + 1) * qb, :] = jnp.where(in_half, pair, jnp.zeros_like(pair))

    for h in range(C_HEADS):
        pair = q_ref[:, (h // 2) * LANE:(h // 2 + 1) * LANE]
        ql = _dot(pair, wuk_ref[h]) * (HEAD_DIM ** -0.5)
        qlat_s[h * qb:(h + 1) * qb, :] = ql.astype(BF16)

    wi = wi_ref[...] * (IDX_HEADS ** -0.5)
    for h in range(IDX_HEADS):
        wib_s[h] = jnp.broadcast_to(wi[:, h:h + 1] * (IDX_DIM ** -0.5), (qb, ck))

    def score_body(c, _):
        start = pl.multiple_of(c * ck, ck)
        kc = ki_ref[pl.ds(start, ck), :]
        raw = _dot_nt(qis_s[...], kc)
        score = jnp.zeros((qb, ck), F32)
        for h in range(IDX_HEADS):
            idx_s = jnp.maximum(raw[h * qb:(h + 1) * qb, :], 0.0)
            score = score + idx_s * wib_s[h]
        causal = (start + kcol) <= qpos
        key_s[c] = _float_order_key(jnp.where(causal, score, NEG)).T
        sel_s[c] = jnp.where(causal, 0.0, NEG)
        return 0

    lax.fori_loop(0, n_chunks, score_body, 0)

    @pl.when((n + 1) * qb > n_sel)
    def _():
        def count_where(pred_fn):
            def body(c, acc):
                hit = jnp.where(pred_fn(key_s[c]), 1.0, 0.0)
                return acc + jnp.sum(hit.reshape(ck // 32, 32, qb), axis=0)
            acc = lax.fori_loop(0, n_chunks, body, jnp.zeros((32, qb), F32))
            return jnp.sum(acc, axis=0, keepdims=True)

        cnt0 = count_where(lambda kv: kv >= 0)
        prefix0 = jnp.where(cnt0 >= n_sel, jnp.int32(0), jnp.int32(INT_MIN))

        def bit_body(t, prefix):
            cand = prefix | jnp.left_shift(jnp.int32(1), 30 - t)
            cnt = count_where(lambda kv: kv >= cand)
            return jnp.where(cnt >= n_sel, cand, prefix)

        thr = lax.fori_loop(0, 31, bit_body, prefix0)
        need = n_sel - count_where(lambda kv: kv > thr)
        r = lax.broadcasted_iota(jnp.int32, (ck, ck), 0)
        cc = lax.broadcasted_iota(jnp.int32, (ck, ck), 1)
        earlier_key = jnp.where(cc < r, 1.0, 0.0).astype(BF16)

        def tie_body(c, seen):
            kv = key_s[c]
            tie = kv == thr
            tie_f = jnp.where(tie, 1.0, 0.0)
            earlier = _dot(earlier_key, tie_f.astype(BF16)) + seen
            sel = (kv > thr) | (tie & (earlier < need))
            sel_s[c] = jnp.minimum(sel_s[c], jnp.where(sel, 0.0, NEG).T)
            return seen + jnp.sum(tie_f, axis=0, keepdims=True)

        lax.fori_loop(0, n_chunks, tie_body, jnp.zeros((1, qb), F32))

    m_s[...] = jnp.full(m_s.shape, NEG, F32)

    def logit_body(c, _):
        start = pl.multiple_of(c * ck, ck)
        raw = _dot_nt(qlat_s[...], ckv_ref[pl.ds(start, ck), :])
        rel_blk = jnp.minimum(n - c * (ck // qb), 3)
        mask_add = sel_s[c]
        for h in range(C_HEADS):
            rows = slice(h * qb, (h + 1) * qb)
            s = raw[rows, :] + tb_ref[rel_blk, h] + mask_add
            s_all[c, rows, :] = s
            m_s[rows, :] = jnp.maximum(m_s[rows, :], jnp.maximum(s[:, :LANE], s[:, LANE:]))
        return 0

    lax.fori_loop(0, n_chunks, logit_body, 0)

    for h in range(C_HEADS):
        rows = slice(h * qb, (h + 1) * qb)
        m_s[rows, :] = jnp.broadcast_to(jnp.max(m_s[rows, :], axis=-1, keepdims=True), (qb, LANE))
    l_s[...] = jnp.zeros(l_s.shape, F32)
    acc_s[...] = jnp.zeros(acc_s.shape, F32)

    def prob_body(c, _):
        start = pl.multiple_of(c * ck, ck)
        for h in range(C_HEADS):
            rows = slice(h * qb, (h + 1) * qb)
            m_row = m_s[rows, :]
            p_lo = jnp.exp(s_all[c, rows, :LANE] - m_row)
            p_hi = jnp.exp(s_all[c, rows, LANE:] - m_row)
            l_s[rows, :] = l_s[rows, :] + p_lo + p_hi
            p_buf[rows, :LANE] = p_lo.astype(BF16)
            p_buf[rows, LANE:] = p_hi.astype(BF16)
        acc_s[...] = acc_s[...] + _dot(p_buf[...], ckv_ref[pl.ds(start, ck), :])
        return 0

    lax.fori_loop(0, n_chunks, prob_body, 0)

    for g in range(C_HEADS // 2):
        out = jnp.zeros((qb, LANE), F32)
        for hh in range(2):
            h = 2 * g + hh
            rows = slice(h * qb, (h + 1) * qb)
            denom = jnp.sum(l_s[rows, :], axis=-1, keepdims=True)
            o_lat = (acc_s[rows, :] / denom).astype(BF16)
            out = out + _dot(o_lat, wuv_ref[h])
        o_ref[:, g * LANE:(g + 1) * LANE] = out.astype(o_ref.dtype)


def _dsa_attention(q, ckv, qi, ki2, wi, wuk_wide, wuv_wide, tb, bsz, t):
    n = q.shape[0]
    nq = t // Q_BLOCK
    nck = t // DSA_CK
    rows = C_HEADS * Q_BLOCK
    blk = lambda b, i: (b * nq + i, 0)
    per_b = lambda b, i: (b, 0)
    return pl.pallas_call(
        _dsa_kernel,
        grid=(bsz, nq),
        in_specs=[
            pl.BlockSpec((Q_BLOCK, C_HEADS * HEAD_DIM), blk),
            pl.BlockSpec((t, C_LATENT), per_b),
            pl.BlockSpec((Q_BLOCK, IDX_HEADS * IDX_DIM), blk),
            pl.BlockSpec((t, LANE), per_b),
            pl.BlockSpec((Q_BLOCK, LANE), blk),
            _const_spec(wuk_wide.shape), _const_spec(wuv_wide.shape), _const_spec(tb.shape),
        ],
        out_specs=pl.BlockSpec((Q_BLOCK, C_HEADS * HEAD_DIM), blk),
        out_shape=jax.ShapeDtypeStruct((n, C_HEADS * HEAD_DIM), BF16),
        scratch_shapes=[
            pltpu.VMEM((nck, DSA_CK, Q_BLOCK), jnp.int32),
            pltpu.VMEM((nck, Q_BLOCK, DSA_CK), F32),
            pltpu.VMEM((rows, C_LATENT), BF16),
            pltpu.VMEM((IDX_HEADS * Q_BLOCK, LANE), BF16),
            pltpu.VMEM((IDX_HEADS, Q_BLOCK, DSA_CK), F32),
            pltpu.VMEM((nck, rows, DSA_CK), F32),
            pltpu.VMEM((rows, DSA_CK), BF16),
            pltpu.VMEM((rows, C_LATENT), F32),
            pltpu.VMEM((rows, LANE), F32),
            pltpu.VMEM((rows, LANE), F32),
        ],
        compiler_params=_cparams(2), name="dsa_attention",
    )(q, ckv, qi, ki2, wi, wuk_wide, wuv_wide, tb)


def _post_attn_kernel(*refs, n_mix, alpha):
    x_ref = refs[0]
    mix_refs = refs[1:1 + n_mix]
    w_refs = refs[1 + n_mix:1 + 2 * n_mix]
    g_ref, b_ref, wr_ref, br_ref = refs[1 + 2 * n_mix:5 + 2 * n_mix]
    xo_ref, idx_ref, gate_ref, rank_ref, cnt_ref = refs[5 + 2 * n_mix:10 + 2 * n_mix]
    scr = refs[10 + 2 * n_mix:]
    wb_refs = scr[:n_mix]
    wr3_ref, before_ref, carry_ref = scr[n_mix:]
    tm = x_ref.shape[0]

    @pl.when(pl.program_id(0) == 0)
    def _():
        for w_ref, wb_ref in zip(w_refs, wb_refs):
            wb_ref[...] = w_ref[...].astype(BF16)
        hi, mid, lo = _split3(wr_ref[...])
        wr3_ref[0] = hi
        wr3_ref[1] = mid
        wr3_ref[2] = lo
        r = lax.broadcasted_iota(jnp.int32, (tm, tm), 0)
        c = lax.broadcasted_iota(jnp.int32, (tm, tm), 1)
        before_ref[...] = jnp.where(r < c, 1.0, 0.0).astype(BF16)
        carry_ref[...] = jnp.zeros(carry_ref.shape, F32)

    mix = _dot(mix_refs[0][...], wb_refs[0][...])
    for i in range(1, n_mix):
        mix = mix + _dot(mix_refs[i][...], wb_refs[i][...])
    xn = _layer_norm(alpha * x_ref[...] + mix, g_ref[...], b_ref[...])
    xo_ref[...] = xn

    xh, xm, xl = _split3(xn)
    logits = (_dot_nt(wr3_ref[0], xh) + _dot_nt(wr3_ref[0], xm) + _dot_nt(wr3_ref[1], xh)
              + _dot_nt(wr3_ref[0], xl) + _dot_nt(wr3_ref[1], xm) + _dot_nt(wr3_ref[2], xh)
              + br_ref[...])
    eidx = lax.broadcasted_iota(jnp.int32, logits.shape, 0)
    cur = logits
    vals, idxs = [], []
    for _ in range(TOP_K):
        mx = jnp.max(cur, axis=0, keepdims=True)
        first = jnp.min(jnp.where(cur == mx, eidx, N_EXPERTS), axis=0, keepdims=True)
        vals.append(mx)
        idxs.append(first)
        cur = jnp.where(eidx == first, -jnp.inf, cur)
    es = [jnp.exp(v - vals[0]) for v in vals]
    tot = es[0] + es[1] + es[2] + es[3]
    member = jnp.zeros(logits.shape, F32)
    for k in range(TOP_K):
        idx_ref[k:k + 1, :] = idxs[k]
        gate_ref[k:k + 1, :] = es[k] / tot
        member = member + jnp.where(eidx == idxs[k], 1.0, 0.0)
    earlier = _dot(member.astype(BF16), before_ref[...]) + carry_ref[...]
    for k in range(TOP_K):
        rk = jnp.sum(jnp.where(eidx == idxs[k], earlier, 0.0), axis=0, keepdims=True)
        rank_ref[k:k + 1, :] = rk.astype(jnp.int32)
    carry_ref[...] = carry_ref[...] + jnp.sum(member, axis=1, keepdims=True)
    cnt_ref[...] = carry_ref[...].astype(jnp.int32)


def _post_attn(x2, mixes, w_os, ln_g, ln_b, w_router, b_router, alpha):
    n, d = x2.shape
    n_mix = len(mixes)
    tm = TM_PROJ
    row = lambda i: (i, 0)
    colblk = lambda i: (0, i)
    in_specs = [pl.BlockSpec((tm, d), row)]
    in_specs += [pl.BlockSpec((tm, m.shape[1]), row) for m in mixes]
    in_specs += [_const_spec(w.shape) for w in w_os]
    in_specs += [_const_spec((1, d)), _const_spec((1, d)), _const_spec((N_EXPERTS, d)),
                 _const_spec((N_EXPERTS, 1))]
    out_specs = [pl.BlockSpec((tm, d), row), pl.BlockSpec((TOP_K, tm), colblk),
                 pl.BlockSpec((TOP_K, tm), colblk), pl.BlockSpec((TOP_K, tm), colblk),
                 _const_spec((N_EXPERTS, 1))]
    out_shape = [jax.ShapeDtypeStruct((n, d), F32), jax.ShapeDtypeStruct((TOP_K, n), jnp.int32),
                 jax.ShapeDtypeStruct((TOP_K, n), F32), jax.ShapeDtypeStruct((TOP_K, n), jnp.int32),
                 jax.ShapeDtypeStruct((N_EXPERTS, 1), jnp.int32)]
    scratch = [pltpu.VMEM(w.shape, BF16) for w in w_os]
    scratch += [pltpu.VMEM((3, N_EXPERTS, d), BF16), pltpu.VMEM((tm, tm), BF16),
                pltpu.VMEM((N_EXPERTS, 1), F32)]
    return pl.pallas_call(
        functools.partial(_post_attn_kernel, n_mix=n_mix, alpha=alpha),
        grid=(n // tm,),
        in_specs=in_specs, out_specs=out_specs, out_shape=out_shape, scratch_shapes=scratch,
        compiler_params=_cparams(1), name="post_attn",
    )(x2, *mixes, *w_os, ln_g.reshape(1, d), ln_b.reshape(1, d), w_router.T,
      b_router.reshape(N_EXPERTS, 1))


def _moe_kernel(be_ref, nu_ref, src_ref, x_hbm, wgu_ref, bgu_ref, wd_ref, bd_ref, y_ref,
                wgu_s, wd_s, xbuf, sem):
    i = pl.program_id(0)
    n_steps = pl.num_programs(0)
    tm = y_ref.shape[0]
    prev = be_ref[jnp.maximum(i - 1, 0)]
    n_used = nu_ref[0]
    live = i < n_used
    n_slots = xbuf.shape[0]
    ahead = n_slots - 1
    slot = i % n_slots

    def row_copy(block, s, r):
        tok = src_ref[block * tm + r]
        return pltpu.make_async_copy(x_hbm.at[pl.ds(tok, 1), :], xbuf.at[s, pl.ds(r, 1), :],
                                     sem.at[s])

    def wait_rows(s):
        pltpu.make_async_copy(xbuf.at[s], xbuf.at[s], sem.at[s]).wait()

    @pl.when(i == 0)
    def _():
        for b in range(ahead):
            def body(r, carry, b=b):
                row_copy(b, b, r).start(priority=1)
                return carry
            lax.fori_loop(0, tm, body, 0, unroll=8)

    @pl.when(live & ((i == 0) | (be_ref[i] != prev)))
    def _():
        wgu_s[...] = wgu_ref[...].astype(BF16)
        wd_s[...] = wd_ref[...].astype(BF16)

    @pl.when(live)
    def _():
        wait_rows(slot)
        xb = xbuf[slot].astype(BF16)
        nxt = jnp.minimum(i + ahead, n_steps - 1)
        for r in range(tm):
            row_copy(nxt, (i + ahead) % n_slots, r).start(priority=1)
        ch = 1024
        acc = jnp.zeros(y_ref.shape, F32)
        for j in range(D_FF // ch):
            hg = _dot(xb, wgu_s[:, j * ch:(j + 1) * ch]) + bgu_ref[:, j * ch:(j + 1) * ch]
            hl = (_dot(xb, wgu_s[:, D_FF + j * ch:D_FF + (j + 1) * ch])
                  + bgu_ref[:, D_FF + j * ch:D_FF + (j + 1) * ch])
            glu = jnp.minimum(hg, SWIGLU_LIMIT)
            lin = jnp.clip(hl, -SWIGLU_LIMIT, SWIGLU_LIMIT)
            act = glu * jax.nn.sigmoid(SWIGLU_ALPHA * glu) * (lin + 1.0)
            acc = acc + _dot(act.astype(BF16), wd_s[j * ch:(j + 1) * ch, :])
        y_ref[...] = (acc + bd_ref[...]).astype(y_ref.dtype)

    @pl.when(i == n_used)
    def _():
        for b in range(ahead):
            wait_rows((i + b) % n_slots)

    @pl.when(jnp.logical_not(live))
    def _():
        y_ref[...] = jnp.zeros(y_ref.shape, y_ref.dtype)


def _moe_ffn(x2, src_tok, block_e, n_used, layer, w_gate_up, b_gate_up, w_down, b_down):
    d = x2.shape[1]
    rows = src_tok.shape[0]
    depth = w_gate_up.shape[0]
    n_blocks = rows // TM_MOE
    expert = lambda i, be, nu, src: (layer, be[i], 0, 0)
    grid_spec = pltpu.PrefetchScalarGridSpec(
        num_scalar_prefetch=3,
        grid=(n_blocks,),
        in_specs=[
            pl.BlockSpec(memory_space=pl.ANY),
            pl.BlockSpec((None, None, d, 2 * D_FF), expert),
            pl.BlockSpec((None, None, 1, 2 * D_FF), expert),
            pl.BlockSpec((None, None, D_FF, d), expert),
            pl.BlockSpec((None, None, 1, d), expert),
        ],
        out_specs=pl.BlockSpec((TM_MOE, d), lambda i, be, nu, src: (i, 0)),
        scratch_shapes=[pltpu.VMEM((d, 2 * D_FF), BF16), pltpu.VMEM((D_FF, d), BF16),
                        pltpu.VMEM((MOE_SLOTS, TM_MOE, d), F32),
                        pltpu.SemaphoreType.DMA((MOE_SLOTS,))],
    )
    return pl.pallas_call(
        _moe_kernel, grid_spec=grid_spec,
        out_shape=jax.ShapeDtypeStruct((rows, d), F32),
        compiler_params=_cparams(1), name="moe_ffn",
    )(block_e, n_used, src_tok, x2, w_gate_up, b_gate_up.reshape(depth, N_EXPERTS, 1, 2 * D_FF),
      w_down, b_down.reshape(depth, N_EXPERTS, 1, d))


def _post_moe_kernel(dest_ref, x_ref, y_hbm, gt_ref, g_ref, b_ref, p_ref, wg_ref, wp_ref, o_ref,
                     wg_s, wp_s, ybuf, sem, *, alpha, n_tok):
    i = pl.program_id(0)
    n_steps = pl.num_programs(0)
    tm, d = x_ref.shape

    def row_copy(tile, slot, k, j):
        r = dest_ref[k * n_tok + tile * tm + j]
        return pltpu.make_async_copy(y_hbm.at[pl.ds(r, 1), :], ybuf.at[slot, k, pl.ds(j, 1), :],
                                     sem.at[slot])

    @pl.when(i == 0)
    def _():
        wg_s[...] = wg_ref[...].astype(BF16)
        wp_s[...] = wp_ref[...].astype(BF16)

        def body(j, carry):
            for k in range(TOP_K):
                row_copy(0, 0, k, j).start()
            return carry
        lax.fori_loop(0, tm, body, 0, unroll=8)

    @pl.when(i + 1 < n_steps)
    def _():
        for j in range(tm):
            for k in range(TOP_K):
                row_copy(i + 1, (i + 1) % 2, k, j).start(priority=k % 2)

    slot = i % 2
    pltpu.make_async_copy(ybuf.at[slot], ybuf.at[slot], sem.at[slot]).wait()
    gates = gt_ref[...]
    ffn = gates[:, 0:1] * ybuf[slot, 0]
    for k in range(1, TOP_K):
        ffn = ffn + gates[:, k:k + 1] * ybuf[slot, k]
    xn = _layer_norm(alpha * x_ref[...] + ffn, g_ref[...], b_ref[...])
    gate = jax.nn.sigmoid(_dot(xn.astype(BF16), wg_s[...]))
    emb = _dot(p_ref[...].astype(BF16), wp_s[...])
    o_ref[...] = xn + gate * emb


def _post_moe(x2, y_pad, dest_flat, gates_t, ln_g, ln_b, layer, p3, w_ple_gate, w_ple_proj, alpha):
    n, d = x2.shape
    tm = TM_POST
    row = lambda i, dest: (i, 0)
    const2 = lambda i, dest: (0, 0)
    grid_spec = pltpu.PrefetchScalarGridSpec(
        num_scalar_prefetch=1,
        grid=(n // tm,),
        in_specs=[pl.BlockSpec((tm, d), row), pl.BlockSpec(memory_space=pl.ANY),
                  pl.BlockSpec((tm, TOP_K), row), pl.BlockSpec((1, d), const2),
                  pl.BlockSpec((1, d), const2),
                  pl.BlockSpec((None, tm, PLE_DIM), lambda i, dest: (layer, i, 0)),
                  pl.BlockSpec((None, d, d), lambda i, dest: (layer, 0, 0)),
                  pl.BlockSpec((None, PLE_DIM, d), lambda i, dest: (layer, 0, 0))],
        out_specs=pl.BlockSpec((tm, d), row),
        scratch_shapes=[pltpu.VMEM((d, d), BF16), pltpu.VMEM((PLE_DIM, d), BF16),
                        pltpu.VMEM((2, TOP_K, tm, d), F32), pltpu.SemaphoreType.DMA((2,))],
    )
    return pl.pallas_call(
        functools.partial(_post_moe_kernel, alpha=alpha, n_tok=n),
        grid_spec=grid_spec,
        out_shape=jax.ShapeDtypeStruct((n, d), F32),
        compiler_params=_cparams(1), name="post_moe",
    )(dest_flat, x2, y_pad, gates_t, ln_g.reshape(1, d), ln_b.reshape(1, d), p3, w_ple_gate,
      w_ple_proj)


def _t5_bucket(dist):
    d = jnp.maximum(dist, 0)
    ratio = jnp.log(jnp.maximum(d, 1).astype(F32) / MAX_EXACT) / math.log(MAX_DISTANCE / MAX_EXACT)
    large = MAX_EXACT + (ratio * (NUM_BUCKETS - MAX_EXACT)).astype(jnp.int32)
    large = jnp.minimum(large, NUM_BUCKETS - 1)
    return jnp.where(d < MAX_EXACT, d, large)


def _swa_bias_table(pos_bias):
    qi = jnp.arange(WINDOW)[:, None]
    kj = jnp.arange(2 * WINDOW)[None, :]
    dist = qi + WINDOW - kj
    valid = (dist >= 0) & (dist < WINDOW)
    bias = pos_bias[_t5_bucket(dist)][..., :A_HEADS].astype(F32).transpose(2, 0, 1)
    return jnp.where(valid[None], bias, NEG)


def _dsa_bias_tables(pos_bias):
    qi = jnp.arange(Q_BLOCK)[:, None]
    kj = jnp.arange(DSA_CK)[None, :]
    tabs = []
    for r in range(4):
        dist = r * Q_BLOCK + qi - kj
        tabs.append(pos_bias[_t5_bucket(dist)].astype(F32).transpose(2, 0, 1))
    return jnp.stack(tabs)


def _pad_cols(w, width=LANE):
    return jnp.pad(w, ((0, 0), (0, width - w.shape[1])))


def _even_mixer(x2, bsz, t, w_in, b_forget, sinks, w_o, swa_bias):
    hd = HEAD_DIM
    grp = A_HEADS // A_KV_HEADS
    pair_order = [g + grp * hh for g in range(grp) for hh in range(A_KV_HEADS)]
    na, nkv, nb = A_HEADS * hd, A_KV_HEADS * hd, B_HEADS * hd
    offs = [0, na, na + nkv, na + 2 * nkv, na + 2 * nkv + nb, na + 2 * nkv + 2 * nb,
            na + 2 * nkv + 3 * nb]
    w_aq = w_in[:, offs[0]:offs[1]].reshape(D_MODEL, A_HEADS, hd)[:, pair_order].reshape(D_MODEL, na)
    weights = [w_aq, w_in[:, offs[1]:offs[2]], w_in[:, offs[2]:offs[3]], w_in[:, offs[3]:offs[4]],
               w_in[:, offs[4]:offs[5]], w_in[:, offs[5]:offs[6]], _pad_cols(w_in[:, offs[6]:])]
    aq, ak, av, bq, bk, bv, bf = _project(x2, weights, [BF16] * 6 + [F32])
    o_a = _swa_attention(aq, ak, av, sinks, swa_bias, bsz, t)

    f_t = bf[:, :B_HEADS].reshape(bsz, t, B_HEADS).transpose(0, 2, 1)
    c = _forget_cumsum(f_t, b_forget)
    c_row = c.reshape(bsz, B_HEADS // 2, 2, t // FOX_TK, FOX_TK)
    c_col = c.reshape(bsz, B_HEADS // 2, 2, t).transpose(0, 1, 3, 2)
    o_b = _fox_attention(bq, bk, bv, c_col, c_row, bsz, t)

    w_oa = w_o[:na].reshape(A_HEADS, hd, D_MODEL)[jnp.asarray(pair_order)].reshape(na, D_MODEL)
    return [o_a, o_b], [w_oa, w_o[na:]]


def _odd_mixer(x2, bsz, t, w_in, kv_norm, w_uk, w_uv, w_o, dsa_tb):
    hd = HEAD_DIM
    nq, nl, ni = C_HEADS * hd, C_LATENT, IDX_HEADS * IDX_DIM
    w_ki = w_in[:, nq + nl + ni:nq + nl + ni + IDX_DIM]
    weights = [w_in[:, :nq], w_in[:, nq:nq + nl], w_in[:, nq + nl:nq + nl + ni],
               jnp.concatenate([w_ki, w_ki], axis=1), _pad_cols(w_in[:, nq + nl + ni + IDX_DIM:])]
    cq, ckv, qi, ki2, wi = _project(x2, weights, [BF16, BF16, BF16, BF16, F32],
                                    rms_gain=kv_norm.reshape(1, nl).astype(F32), rms_index=1)
    uk = w_uk.transpose(1, 2, 0)
    uv = w_uv.transpose(1, 0, 2)
    zk = jnp.zeros_like(uk)
    zv = jnp.zeros_like(uv)
    odd = (jnp.arange(C_HEADS) % 2 == 1)[:, None, None]
    wuk_wide = jnp.where(odd, jnp.concatenate([zk, uk], axis=1), jnp.concatenate([uk, zk], axis=1))
    wuv_wide = jnp.where(odd, jnp.concatenate([zv, uv], axis=2), jnp.concatenate([uv, zv], axis=2))
    o_c = _dsa_attention(cq, ckv, qi, ki2, wi, wuk_wide.astype(BF16), wuv_wide.astype(BF16),
                         dsa_tb, bsz, t)
    return [o_c], [w_o]


def _dispatch_plan(top_idx, rank, counts, n_tok):
    counts = counts.reshape(N_EXPERTS)
    padded = (counts + TM_MOE - 1) // TM_MOE * TM_MOE
    pad_ends = jnp.cumsum(padded)
    pad_starts = pad_ends - padded
    is_e = top_idx[..., None] == jnp.arange(N_EXPERTS, dtype=jnp.int32)
    dest = jnp.sum(jnp.where(is_e, pad_starts.astype(jnp.int32), 0), axis=-1) + rank
    n_blocks = (n_tok * TOP_K) // TM_MOE + N_EXPERTS
    block_start = jnp.arange(n_blocks, dtype=jnp.int32) * TM_MOE
    block_e = jnp.minimum(jnp.sum(pad_ends[None, :] <= block_start[:, None], axis=1),
                          N_EXPERTS - 1).astype(jnp.int32)
    n_used = (pad_ends[-1] // TM_MOE).astype(jnp.int32).reshape(1)
    return dest, block_e, n_used, n_blocks


def kernel(x, p, pos_bias, w_in_even, b_forget, sinks, w_o_even, w_in_odd, kv_norm, w_uk, w_uv,
           w_o_odd, ln_g, ln_b, w_router, b_router, w_gate_up, b_gate_up, w_down, b_down,
           w_ple_proj, w_ple_gate):
    bsz, t, d = x.shape
    depth = ln_g.shape[0]
    alpha = (2 * depth) ** 0.25
    n_tok = bsz * t
    x2 = x.reshape(n_tok, d)
    swa_bias = _swa_bias_table(pos_bias)
    dsa_tb = _dsa_bias_tables(pos_bias)
    p3 = p.reshape(depth, n_tok, PLE_DIM)
    tok_ids = jnp.tile(jnp.arange(n_tok, dtype=jnp.int32), (TOP_K,))
    for i in range(depth):
        j = i // 2
        if i % 2 == 0:
            mixes, w_os = _even_mixer(x2, bsz, t, w_in_even[j], b_forget[j], sinks[j], w_o_even[j],
                                      swa_bias)
        else:
            mixes, w_os = _odd_mixer(x2, bsz, t, w_in_odd[j], kv_norm[j], w_uk[j], w_uv[j],
                                     w_o_odd[j], dsa_tb)
        x1, top_idx, gates, rank, counts = _post_attn(x2, mixes, w_os, ln_g[i, 0], ln_b[i, 0],
                                                      w_router[i], b_router[i], alpha)
        dest, block_e, n_used, n_blocks = _dispatch_plan(top_idx, rank, counts, n_tok)
        dest_flat = dest.reshape(-1)
        src_tok = jnp.zeros((n_blocks * TM_MOE,), jnp.int32).at[dest_flat].set(
            tok_ids, unique_indices=True, mode="promise_in_bounds")
        y_pad = _moe_ffn(x1, src_tok, block_e, n_used, i, w_gate_up, b_gate_up, w_down, b_down)
        x2 = _post_moe(x1, y_pad, dest_flat, gates.T, ln_g[i, 1], ln_b[i, 1], i, p3,
                       w_ple_gate, w_ple_proj, alpha)
    return x2.reshape(bsz, t, d)
```

```python
import functools
import math

import jax
import jax.numpy as jnp
from jax import lax
from jax.experimental import pallas as pl
from jax.experimental.pallas import tpu as pltpu

D_MODEL = 1024
HEAD_DIM = 64
A_HEADS, A_KV_HEADS, WINDOW = 8, 2, 128
B_HEADS = 8
C_HEADS, C_LATENT = 16, 128
IDX_HEADS, IDX_DIM = 8, 64
TOPK_MAX = 256
NUM_BUCKETS, MAX_EXACT, MAX_DISTANCE = 32, 16, 128
N_EXPERTS, TOP_K, D_FF = 32, 4, 1024
SWIGLU_LIMIT, SWIGLU_ALPHA = 7.0, 1.702
PLE_DIM = 256
Q_BLOCK = 128
LN_EPS = 1e-5
NEG = -1e30

LANE = 128
VMEM_LIMIT = 56 * 1024 * 1024
TM_PROJ = 512
TM_MOE = 256
MOE_SLOTS = 6
TM_POST = 256
FOX_TQ = 512
FOX_TK = 2 * LANE
DSA_CK = 256
INT_MIN = -2 ** 31

BF16 = jnp.bfloat16
F32 = jnp.float32


def _cparams(n_axes):
    return pltpu.CompilerParams(dimension_semantics=("arbitrary",) * n_axes,
                                vmem_limit_bytes=VMEM_LIMIT)


def _const_spec(shape):
    nd = len(shape)
    return pl.BlockSpec(shape, lambda *_: (0,) * nd)


def _dot(a, b):
    return jnp.dot(a, b, preferred_element_type=F32)


def _dot_nt(a, b):
    return lax.dot_general(a, b, (((1,), (1,)), ((), ())), preferred_element_type=F32)


def _layer_norm(v, g, b):
    mu = jnp.mean(v, axis=-1, keepdims=True)
    d = v - mu
    var = jnp.mean(d * d, axis=-1, keepdims=True)
    return d * lax.rsqrt(var + LN_EPS) * g + b


def _split3(v):
    hi = v.astype(BF16)
    r1 = v - hi.astype(F32)
    mid = r1.astype(BF16)
    lo = (r1 - mid.astype(F32)).astype(BF16)
    return hi, mid, lo


def _proj_kernel(*refs, n_w, rms_index):
    x_ref = refs[0]
    w_refs = refs[1:1 + n_w]
    g_ref = refs[1 + n_w]
    o_refs = refs[2 + n_w:2 + 2 * n_w]
    wb_refs = refs[2 + 2 * n_w:]

    @pl.when(pl.program_id(0) == 0)
    def _():
        for w_ref, wb_ref in zip(w_refs, wb_refs):
            wb_ref[...] = w_ref[...].astype(BF16)

    xb = x_ref[...].astype(BF16)
    for i in range(n_w):
        h = _dot(xb, wb_refs[i][...])
        if i == rms_index:
            h = h * lax.rsqrt(jnp.mean(h * h, axis=-1, keepdims=True) + LN_EPS) * g_ref[...]
        o_refs[i][...] = h.astype(o_refs[i].dtype)


def _project(x2, weights, out_dtypes, rms_gain=None, rms_index=-1):
    n, d = x2.shape
    n_w = len(weights)
    if rms_gain is None:
        rms_gain = jnp.ones((1, LANE), F32)
    in_specs = [pl.BlockSpec((TM_PROJ, d), lambda i: (i, 0))]
    in_specs += [_const_spec(w.shape) for w in weights]
    in_specs += [_const_spec(rms_gain.shape)]
    out_specs = [pl.BlockSpec((TM_PROJ, w.shape[1]), lambda i: (i, 0)) for w in weights]
    out_shape = [jax.ShapeDtypeStruct((n, w.shape[1]), dt) for w, dt in zip(weights, out_dtypes)]
    scratch = [pltpu.VMEM(w.shape, BF16) for w in weights]
    return pl.pallas_call(
        functools.partial(_proj_kernel, n_w=n_w, rms_index=rms_index),
        grid=(n // TM_PROJ,),
        in_specs=in_specs, out_specs=out_specs, out_shape=out_shape,
        scratch_shapes=scratch, compiler_params=_cparams(1), name="in_proj",
    )(x2, *weights, rms_gain)


def _cumsum_kernel(f_ref, b_ref, o_ref):
    z = f_ref[...] + b_ref[...]
    ls = jnp.minimum(z, 0.0) - jnp.log1p(jnp.exp(-jnp.abs(z)))
    t = ls.shape[1]
    r = lax.broadcasted_iota(jnp.int32, (LANE, LANE), 0)
    c = lax.broadcasted_iota(jnp.int32, (LANE, LANE), 1)
    tri = jnp.where(r <= c, 1.0, 0.0).astype(BF16)
    carry = jnp.zeros((ls.shape[0], 1), F32)
    for blk in range(t // LANE):
        seg = ls[:, blk * LANE:(blk + 1) * LANE]
        hi, mid, lo = _split3(seg)
        pre = _dot(hi, tri) + _dot(mid, tri) + _dot(lo, tri) + carry
        o_ref[:, blk * LANE:(blk + 1) * LANE] = pre
        carry = pre[:, LANE - 1:LANE]


def _forget_cumsum(f_t, b_forget):
    bsz, h, t = f_t.shape
    return pl.pallas_call(
        _cumsum_kernel,
        grid=(bsz,),
        in_specs=[pl.BlockSpec((None, h, t), lambda b: (b, 0, 0)), _const_spec((h, 1))],
        out_specs=pl.BlockSpec((None, h, t), lambda b: (b, 0, 0)),
        out_shape=jax.ShapeDtypeStruct((bsz, h, t), F32),
        compiler_params=_cparams(1), name="forget_cumsum",
    )(f_t, b_forget.reshape(h, 1).astype(F32))


def _fox_kernel(q_ref, k_ref, v_ref, cq_ref, ck_ref, o_ref, s_all, m_s, l_s, acc_s):
    i = pl.program_id(2)
    tq = q_ref.shape[0]
    tk = FOX_TK
    per_q = tq // tk
    q = q_ref[...]
    lane = lax.broadcasted_iota(jnp.int32, (tq, LANE), 1)
    row_minus_col = (lax.broadcasted_iota(jnp.int32, (tq, tk), 0)
                     - lax.broadcasted_iota(jnp.int32, (tq, tk), 1))
    heads = range(2)
    qh, cq = [], []
    for hh in heads:
        in_half = (lane >= hh * HEAD_DIM) & (lane < (hh + 1) * HEAD_DIM)
        qh.append(jnp.where(in_half, q, jnp.zeros_like(q)) * jnp.asarray(HEAD_DIM ** -0.5, BF16))
        cq.append(cq_ref[:, hh:hh + 1])

    def logits(j, hh):
        start = pl.multiple_of(j * tk, tk)
        return _dot_nt(qh[hh], k_ref[pl.ds(start, tk), :]) + cq[hh] - ck_ref[hh, pl.ds(j, 1), :]

    def keep(j, hh, s):
        s_all[hh, j] = s
        m_s[hh] = jnp.maximum(m_s[hh], jnp.maximum(s[:, :LANE], s[:, LANE:]))

    def logit_body(j, carry):
        for hh in heads:
            keep(j, hh, logits(j, hh))
        return carry

    m_s[...] = jnp.full(m_s.shape, NEG, F32)
    lax.fori_loop(0, i * per_q, logit_body, 0)
    for dd in range(per_q):
        j = i * per_q + dd
        causal = row_minus_col >= dd * tk
        for hh in heads:
            keep(j, hh, jnp.where(causal, logits(j, hh), NEG))
    m_row = [jnp.broadcast_to(jnp.max(m_s[hh], axis=-1, keepdims=True), (tq, LANE)) for hh in heads]
    l_s[...] = jnp.zeros(l_s.shape, F32)
    acc_s[...] = jnp.zeros(acc_s.shape, F32)

    def prob_body(j, carry):
        start = pl.multiple_of(j * tk, tk)
        vs = v_ref[pl.ds(start, tk), :]
        for hh in heads:
            p_lo = jnp.exp(s_all[hh, j, :, :LANE] - m_row[hh])
            p_hi = jnp.exp(s_all[hh, j, :, LANE:] - m_row[hh])
            l_s[hh] = l_s[hh] + p_lo + p_hi
            p = jnp.concatenate([p_lo, p_hi], axis=1).astype(BF16)
            acc_s[hh] = acc_s[hh] + _dot(p, vs)
        return carry

    lax.fori_loop(0, (i + 1) * per_q, prob_body, 0)
    outs = [acc_s[hh] / jnp.sum(l_s[hh], axis=-1, keepdims=True) for hh in heads]
    o_ref[...] = jnp.where(lane < HEAD_DIM, outs[0], outs[1]).astype(o_ref.dtype)


def _fox_attention(q, k, v, c_col, c_row, bsz, t):
    n, hd = q.shape
    n_pair = hd // LANE
    nq = t // FOX_TQ
    nk = t // FOX_TK
    return pl.pallas_call(
        _fox_kernel,
        grid=(bsz, n_pair, nq),
        in_specs=[
            pl.BlockSpec((FOX_TQ, LANE), lambda b, j, i: (b * nq + i, j)),
            pl.BlockSpec((t, LANE), lambda b, j, i: (b, j)),
            pl.BlockSpec((t, LANE), lambda b, j, i: (b, j)),
            pl.BlockSpec((None, None, FOX_TQ, 2), lambda b, j, i: (b, j, i, 0)),
            pl.BlockSpec((None, None, 2, nk, FOX_TK), lambda b, j, i: (b, j, 0, 0, 0)),
        ],
        out_specs=pl.BlockSpec((FOX_TQ, LANE), lambda b, j, i: (b * nq + i, j)),
        out_shape=jax.ShapeDtypeStruct((n, hd), BF16),
        scratch_shapes=[pltpu.VMEM((2, nk, FOX_TQ, FOX_TK), F32), pltpu.VMEM((2, FOX_TQ, LANE), F32),
                        pltpu.VMEM((2, FOX_TQ, LANE), F32), pltpu.VMEM((2, FOX_TQ, LANE), F32)],
        compiler_params=_cparams(3), name="fox_attention",
    )(q, k, v, c_col, c_row)


def _swa_kernel(sink_ref, q_ref, kp_ref, kc_ref, vp_ref, vc_ref, bias_ref, o_ref):
    nblk = pl.program_id(1)
    w = q_ref.shape[0]
    kk = jnp.concatenate([kp_ref[...], kc_ref[...]], axis=0)
    vv = jnp.concatenate([vp_ref[...], vc_ref[...]], axis=0)
    lane = lax.broadcasted_iota(jnp.int32, (w, LANE), 1)
    col = lax.broadcasted_iota(jnp.int32, (w, 2 * w), 1)
    no_prev = (col < w) & (nblk == 0)
    grp = A_HEADS // A_KV_HEADS
    for g in range(grp):
        qg = q_ref[:, g * LANE:(g + 1) * LANE]
        outs = []
        for hh in range(A_KV_HEADS):
            head = g + grp * hh
            in_half = (lane >= hh * HEAD_DIM) & (lane < (hh + 1) * HEAD_DIM)
            qh = jnp.where(in_half, qg, jnp.zeros_like(qg)) * jnp.asarray(HEAD_DIM ** -0.5, BF16)
            s = _dot_nt(qh, kk) + bias_ref[head]
            s = jnp.where(no_prev, NEG, s)
            sink = sink_ref[head]
            m = jnp.maximum(jnp.max(s, axis=-1, keepdims=True), sink)
            e = jnp.exp(s - m)
            denom = jnp.sum(e, axis=-1, keepdims=True) + jnp.exp(sink - m)
            p = e / denom
            outs.append(_dot(p.astype(BF16), vv))
        o_ref[:, g * LANE:(g + 1) * LANE] = jnp.where(lane < HEAD_DIM, outs[0], outs[1]).astype(o_ref.dtype)


def _swa_attention(q, k, v, sinks, bias, bsz, t):
    n = q.shape[0]
    nb = t // WINDOW
    cur = lambda b, i: (b * nb + i, 0)
    prev = lambda b, i: (b * nb + jnp.maximum(i - 1, 0), 0)
    return pl.pallas_call(
        _swa_kernel,
        grid=(bsz, nb),
        in_specs=[
            pl.BlockSpec(memory_space=pltpu.SMEM),
            pl.BlockSpec((WINDOW, A_HEADS * HEAD_DIM), cur),
            pl.BlockSpec((WINDOW, LANE), prev), pl.BlockSpec((WINDOW, LANE), cur),
            pl.BlockSpec((WINDOW, LANE), prev), pl.BlockSpec((WINDOW, LANE), cur),
            _const_spec(bias.shape),
        ],
        out_specs=pl.BlockSpec((WINDOW, A_HEADS * HEAD_DIM), cur),
        out_shape=jax.ShapeDtypeStruct((n, A_HEADS * HEAD_DIM), BF16),
        compiler_params=_cparams(2), name="swa_attention",
    )(sinks.astype(F32), q, k, k, v, v, bias)


def _float_order_key(s):
    bits = pltpu.bitcast(s, jnp.int32)
    return bits ^ ((bits >> 31) & jnp.int32(0x7FFFFFFF))


def _dsa_kernel(q_ref, ckv_ref, qi_ref, ki_ref, wi_ref, wuk_ref, wuv_ref, tb_ref, o_ref,
                key_s, sel_s, qlat_s, qis_s, wib_s, s_all, p_buf, acc_s, m_s, l_s):
    n = pl.program_id(1)
    qb = Q_BLOCK
    ck = DSA_CK
    n_sel = TOPK_MAX
    n_chunks = n // (ck // qb) + 1
    lane = lax.broadcasted_iota(jnp.int32, (qb, LANE), 1)
    qpos = n * qb + lax.broadcasted_iota(jnp.int32, (qb, ck), 0)
    kcol = lax.broadcasted_iota(jnp.int32, (qb, ck), 1)

    for h in range(IDX_HEADS):
        pair = qi_ref[:, (h // 2) * LANE:(h // 2 + 1) * LANE]
        in_half = (lane >= (h % 2) * IDX_DIM) & (lane < (h % 2 + 1) * IDX_DIM)
        qis_s[h * qb:(h + 1) * qb, :] = jnp.where(in_half, pair, jnp.zeros_like(pair))

    for h in range(C_HEADS):
        pair = q_ref[:, (h // 2) * LANE:(h // 2 + 1) * LANE]
        ql = _dot(pair, wuk_ref[h]) * (HEAD_DIM ** -0.5)
        qlat_s[h * qb:(h + 1) * qb, :] = ql.astype(BF16)

    wi = wi_ref[...] * (IDX_HEADS ** -0.5)
    for h in range(IDX_HEADS):
        wib_s[h] = jnp.broadcast_to(wi[:, h:h + 1] * (IDX_DIM ** -0.5), (qb, ck))

    def score_body(c, _):
        start = pl.multiple_of(c * ck, ck)
        kc = ki_ref[pl.ds(start, ck), :]
        raw = _dot_nt(qis_s[...], kc)
        score = jnp.zeros((qb, ck), F32)
        for h in range(IDX_HEADS):
            idx_s = jnp.maximum(raw[h * qb:(h + 1) * qb, :], 0.0)
            score = score + idx_s * wib_s[h]
        causal = (start + kcol) <= qpos
        key_s[c] = _float_order_key(jnp.where(causal, score, NEG)).T
        sel_s[c] = jnp.where(causal, 0.0, NEG)
        return 0

    lax.fori_loop(0, n_chunks, score_body, 0)

    @pl.when((n + 1) * qb > n_sel)
    def _():
        def count_where(pred_fn):
            def body(c, acc):
                hit = jnp.where(pred_fn(key_s[c]), 1.0, 0.0)
                return acc + jnp.sum(hit.reshape(ck // 32, 32, qb), axis=0)
            acc = lax.fori_loop(0, n_chunks, body, jnp.zeros((32, qb), F32))
            return jnp.sum(acc, axis=0, keepdims=True)

        cnt0 = count_where(lambda kv: kv >= 0)
        prefix0 = jnp.where(cnt0 >= n_sel, jnp.int32(0), jnp.int32(INT_MIN))

        def bit_body(t, prefix):
            cand = prefix | jnp.left_shift(jnp.int32(1), 30 - t)
            cnt = count_where(lambda kv: kv >= cand)
            return jnp.where(cnt >= n_sel, cand, prefix)

        thr = lax.fori_loop(0, 31, bit_body, prefix0)
        need = n_sel - count_where(lambda kv: kv > thr)
        r = lax.broadcasted_iota(jnp.int32, (ck, ck), 0)
        cc = lax.broadcasted_iota(jnp.int32, (ck, ck), 1)
        earlier_key = jnp.where(cc < r, 1.0, 0.0).astype(BF16)

        def tie_body(c, seen):
            kv = key_s[c]
            tie = kv == thr
            tie_f = jnp.where(tie, 1.0, 0.0)
            earlier = _dot(earlier_key, tie_f.astype(BF16)) + seen
            sel = (kv > thr) | (tie & (earlier < need))
            sel_s[c] = jnp.minimum(sel_s[c], jnp.where(sel, 0.0, NEG).T)
            return seen + jnp.sum(tie_f, axis=0, keepdims=True)

        lax.fori_loop(0, n_chunks, tie_body, jnp.zeros((1, qb), F32))

    m_s[...] = jnp.full(m_s.shape, NEG, F32)

    def logit_body(c, _):
        start = pl.multiple_of(c * ck, ck)
        raw = _dot_nt(qlat_s[...], ckv_ref[pl.ds(start, ck), :])
        rel_blk = jnp.minimum(n - c * (ck // qb), 3)
        mask_add = sel_s[c]
        for h in range(C_HEADS):
            rows = slice(h * qb, (h + 1) * qb)
            s = raw[rows, :] + tb_ref[rel_blk, h] + mask_add
            s_all[c, rows, :] = s
            m_s[rows, :] = jnp.maximum(m_s[rows, :], jnp.maximum(s[:, :LANE], s[:, LANE:]))
        return 0

    lax.fori_loop(0, n_chunks, logit_body, 0)

    for h in range(C_HEADS):
        rows = slice(h * qb, (h + 1) * qb)
        m_s[rows, :] = jnp.broadcast_to(jnp.max(m_s[rows, :], axis=-1, keepdims=True), (qb, LANE))
    l_s[...] = jnp.zeros(l_s.shape, F32)
    acc_s[...] = jnp.zeros(acc_s.shape, F32)

    def prob_body(c, _):
        start = pl.multiple_of(c * ck, ck)
        for h in range(C_HEADS):
            rows = slice(h * qb, (h + 1) * qb)
            m_row = m_s[rows, :]
            p_lo = jnp.exp(s_all[c, rows, :LANE] - m_row)
            p_hi = jnp.exp(s_all[c, rows, LANE:] - m_row)
            l_s[rows, :] = l_s[rows, :] + p_lo + p_hi
            p_buf[rows, :LANE] = p_lo.astype(BF16)
            p_buf[rows, LANE:] = p_hi.astype(BF16)
        acc_s[...] = acc_s[...] + _dot(p_buf[...], ckv_ref[pl.ds(start, ck), :])
        return 0

    lax.fori_loop(0, n_chunks, prob_body, 0)

    for g in range(C_HEADS // 2):
        out = jnp.zeros((qb, LANE), F32)
        for hh in range(2):
            h = 2 * g + hh
            rows = slice(h * qb, (h + 1) * qb)
            denom = jnp.sum(l_s[rows, :], axis=-1, keepdims=True)
            o_lat = (acc_s[rows, :] / denom).astype(BF16)
            out = out + _dot(o_lat, wuv_ref[h])
        o_ref[:, g * LANE:(g + 1) * LANE] = out.astype(o_ref.dtype)


def _dsa_attention(q, ckv, qi, ki2, wi, wuk_wide, wuv_wide, tb, bsz, t):
    n = q.shape[0]
    nq = t // Q_BLOCK
    nck = t // DSA_CK
    rows = C_HEADS * Q_BLOCK
    blk = lambda b, i: (b * nq + i, 0)
    per_b = lambda b, i: (b, 0)
    return pl.pallas_call(
        _dsa_kernel,
        grid=(bsz, nq),
        in_specs=[
            pl.BlockSpec((Q_BLOCK, C_HEADS * HEAD_DIM), blk),
            pl.BlockSpec((t, C_LATENT), per_b),
            pl.BlockSpec((Q_BLOCK, IDX_HEADS * IDX_DIM), blk),
            pl.BlockSpec((t, LANE), per_b),
            pl.BlockSpec((Q_BLOCK, LANE), blk),
            _const_spec(wuk_wide.shape), _const_spec(wuv_wide.shape), _const_spec(tb.shape),
        ],
        out_specs=pl.BlockSpec((Q_BLOCK, C_HEADS * HEAD_DIM), blk),
        out_shape=jax.ShapeDtypeStruct((n, C_HEADS * HEAD_DIM), BF16),
        scratch_shapes=[
            pltpu.VMEM((nck, DSA_CK, Q_BLOCK), jnp.int32),
            pltpu.VMEM((nck, Q_BLOCK, DSA_CK), F32),
            pltpu.VMEM((rows, C_LATENT), BF16),
            pltpu.VMEM((IDX_HEADS * Q_BLOCK, LANE), BF16),
            pltpu.VMEM((IDX_HEADS, Q_BLOCK, DSA_CK), F32),
            pltpu.VMEM((nck, rows, DSA_CK), F32),
            pltpu.VMEM((rows, DSA_CK), BF16),
            pltpu.VMEM((rows, C_LATENT), F32),
            pltpu.VMEM((rows, LANE), F32),
            pltpu.VMEM((rows, LANE), F32),
        ],
        compiler_params=_cparams(2), name="dsa_attention",
    )(q, ckv, qi, ki2, wi, wuk_wide, wuv_wide, tb)


def _post_attn_kernel(*refs, n_mix, alpha):
    x_ref = refs[0]
    mix_refs = refs[1:1 + n_mix]
    w_refs = refs[1 + n_mix:1 + 2 * n_mix]
    g_ref, b_ref, wr_ref, br_ref = refs[1 + 2 * n_mix:5 + 2 * n_mix]
    xo_ref, idx_ref, gate_ref, rank_ref, cnt_ref = refs[5 + 2 * n_mix:10 + 2 * n_mix]
    scr = refs[10 + 2 * n_mix:]
    wb_refs = scr[:n_mix]
    wr3_ref, before_ref, carry_ref = scr[n_mix:]
    tm = x_ref.shape[0]

    @pl.when(pl.program_id(0) == 0)
    def _():
        for w_ref, wb_ref in zip(w_refs, wb_refs):
            wb_ref[...] = w_ref[...].astype(BF16)
        hi, mid, lo = _split3(wr_ref[...])
        wr3_ref[0] = hi
        wr3_ref[1] = mid
        wr3_ref[2] = lo
        r = lax.broadcasted_iota(jnp.int32, (tm, tm), 0)
        c = lax.broadcasted_iota(jnp.int32, (tm, tm), 1)
        before_ref[...] = jnp.where(r < c, 1.0, 0.0).astype(BF16)
        carry_ref[...] = jnp.zeros(carry_ref.shape, F32)

    mix = _dot(mix_refs[0][...], wb_refs[0][...])
    for i in range(1, n_mix):
        mix = mix + _dot(mix_refs[i][...], wb_refs[i][...])
    xn = _layer_norm(alpha * x_ref[...] + mix, g_ref[...], b_ref[...])
    xo_ref[...] = xn

    xh, xm, xl = _split3(xn)
    logits = (_dot_nt(wr3_ref[0], xh) + _dot_nt(wr3_ref[0], xm) + _dot_nt(wr3_ref[1], xh)
              + _dot_nt(wr3_ref[0], xl) + _dot_nt(wr3_ref[1], xm) + _dot_nt(wr3_ref[2], xh)
              + br_ref[...])
    eidx = lax.broadcasted_iota(jnp.int32, logits.shape, 0)
    cur = logits
    vals, idxs = [], []
    for _ in range(TOP_K):
        mx = jnp.max(cur, axis=0, keepdims=True)
        first = jnp.min(jnp.where(cur == mx, eidx, N_EXPERTS), axis=0, keepdims=True)
        vals.append(mx)
        idxs.append(first)
        cur = jnp.where(eidx == first, -jnp.inf, cur)
    es = [jnp.exp(v - vals[0]) for v in vals]
    tot = es[0] + es[1] + es[2] + es[3]
    member = jnp.zeros(logits.shape, F32)
    for k in range(TOP_K):
        idx_ref[k:k + 1, :] = idxs[k]
        gate_ref[k:k + 1, :] = es[k] / tot
        member = member + jnp.where(eidx == idxs[k], 1.0, 0.0)
    earlier = _dot(member.astype(BF16), before_ref[...]) + carry_ref[...]
    for k in range(TOP_K):
        rk = jnp.sum(jnp.where(eidx == idxs[k], earlier, 0.0), axis=0, keepdims=True)
        rank_ref[k:k + 1, :] = rk.astype(jnp.int32)
    carry_ref[...] = carry_ref[...] + jnp.sum(member, axis=1, keepdims=True)
    cnt_ref[...] = carry_ref[...].astype(jnp.int32)


def _post_attn(x2, mixes, w_os, ln_g, ln_b, w_router, b_router, alpha):
    n, d = x2.shape
    n_mix = len(mixes)
    tm = TM_PROJ
    row = lambda i: (i, 0)
    colblk = lambda i: (0, i)
    in_specs = [pl.BlockSpec((tm, d), row)]
    in_specs += [pl.BlockSpec((tm, m.shape[1]), row) for m in mixes]
    in_specs += [_const_spec(w.shape) for w in w_os]
    in_specs += [_const_spec((1, d)), _const_spec((1, d)), _const_spec((N_EXPERTS, d)),
                 _const_spec((N_EXPERTS, 1))]
    out_specs = [pl.BlockSpec((tm, d), row), pl.BlockSpec((TOP_K, tm), colblk),
                 pl.BlockSpec((TOP_K, tm), colblk), pl.BlockSpec((TOP_K, tm), colblk),
                 _const_spec((N_EXPERTS, 1))]
    out_shape = [jax.ShapeDtypeStruct((n, d), F32), jax.ShapeDtypeStruct((TOP_K, n), jnp.int32),
                 jax.ShapeDtypeStruct((TOP_K, n), F32), jax.ShapeDtypeStruct((TOP_K, n), jnp.int32),
                 jax.ShapeDtypeStruct((N_EXPERTS, 1), jnp.int32)]
    scratch = [pltpu.VMEM(w.shape, BF16) for w in w_os]
    scratch += [pltpu.VMEM((3, N_EXPERTS, d), BF16), pltpu.VMEM((tm, tm), BF16),
                pltpu.VMEM((N_EXPERTS, 1), F32)]
    return pl.pallas_call(
        functools.partial(_post_attn_kernel, n_mix=n_mix, alpha=alpha),
        grid=(n // tm,),
        in_specs=in_specs, out_specs=out_specs, out_shape=out_shape, scratch_shapes=scratch,
        compiler_params=_cparams(1), name="post_attn",
    )(x2, *mixes, *w_os, ln_g.reshape(1, d), ln_b.reshape(1, d), w_router.T,
      b_router.reshape(N_EXPERTS, 1))


def _moe_kernel(be_ref, nu_ref, src_ref, x_hbm, wgu_ref, bgu_ref, wd_ref, bd_ref, y_ref,
                wgu_s, wd_s, xbuf, sem):
    i = pl.program_id(0)
    n_steps = pl.num_programs(0)
    tm = y_ref.shape[0]
    prev = be_ref[jnp.maximum(i - 1, 0)]
    n_used = nu_ref[0]
    live = i < n_used
    n_slots = xbuf.shape[0]
    ahead = n_slots - 1
    slot = i % n_slots

    def row_copy(block, s, r):
        tok = src_ref[block * tm + r]
        return pltpu.make_async_copy(x_hbm.at[pl.ds(tok, 1), :], xbuf.at[s, pl.ds(r, 1), :],
                                     sem.at[s])

    def wait_rows(s):
        pltpu.make_async_copy(xbuf.at[s], xbuf.at[s], sem.at[s]).wait()

    @pl.when(i == 0)
    def _():
        for b in range(ahead):
            def body(r, carry, b=b):
                row_copy(b, b, r).start(priority=1)
                return carry
            lax.fori_loop(0, tm, body, 0, unroll=8)

    @pl.when(live & ((i == 0) | (be_ref[i] != prev)))
    def _():
        wgu_s[...] = wgu_ref[...].astype(BF16)
        wd_s[...] = wd_ref[...].astype(BF16)

    @pl.when(live)
    def _():
        wait_rows(slot)
        xb = xbuf[slot].astype(BF16)
        nxt = jnp.minimum(i + ahead, n_steps - 1)
        for r in range(tm):
            row_copy(nxt, (i + ahead) % n_slots, r).start(priority=1)
        ch = 1024
        acc = jnp.zeros(y_ref.shape, F32)
        for j in range(D_FF // ch):
            hg = _dot(xb, wgu_s[:, j * ch:(j + 1) * ch]) + bgu_ref[:, j * ch:(j + 1) * ch]
            hl = (_dot(xb, wgu_s[:, D_FF + j * ch:D_FF + (j + 1) * ch])
                  + bgu_ref[:, D_FF + j * ch:D_FF + (j + 1) * ch])
            glu = jnp.minimum(hg, SWIGLU_LIMIT)
            lin = jnp.clip(hl, -SWIGLU_LIMIT, SWIGLU_LIMIT)
            act = glu * jax.nn.sigmoid(SWIGLU_ALPHA * glu) * (lin + 1.0)
            acc = acc + _dot(act.astype(BF16), wd_s[j * ch:(j + 1) * ch, :])
        y_ref[...] = (acc + bd_ref[...]).astype(y_ref.dtype)

    @pl.when(i == n_used)
    def _():
        for b in range(ahead):
            wait_rows((i + b) % n_slots)

    @pl.when(jnp.logical_not(live))
    def _():
        y_ref[...] = jnp.zeros(y_ref.shape, y_ref.dtype)


def _moe_ffn(x2, src_tok, block_e, n_used, layer, w_gate_up, b_gate_up, w_down, b_down):
    d = x2.shape[1]
    rows = src_tok.shape[0]
    depth = w_gate_up.shape[0]
    n_blocks = rows // TM_MOE
    expert = lambda i, be, nu, src: (layer, be[i], 0, 0)
    grid_spec = pltpu.PrefetchScalarGridSpec(
        num_scalar_prefetch=3,
        grid=(n_blocks,),
        in_specs=[
            pl.BlockSpec(memory_space=pl.ANY),
            pl.BlockSpec((None, None, d, 2 * D_FF), expert),
            pl.BlockSpec((None, None, 1, 2 * D_FF), expert),
            pl.BlockSpec((None, None, D_FF, d), expert),
            pl.BlockSpec((None, None, 1, d), expert),
        ],
        out_specs=pl.BlockSpec((TM_MOE, d), lambda i, be, nu, src: (i, 0)),
        scratch_shapes=[pltpu.VMEM((d, 2 * D_FF), BF16), pltpu.VMEM((D_FF, d), BF16),
                        pltpu.VMEM((MOE_SLOTS, TM_MOE, d), F32),
                        pltpu.SemaphoreType.DMA((MOE_SLOTS,))],
    )
    return pl.pallas_call(
        _moe_kernel, grid_spec=grid_spec,
        out_shape=jax.ShapeDtypeStruct((rows, d), F32),
        compiler_params=_cparams(1), name="moe_ffn",
    )(block_e, n_used, src_tok, x2, w_gate_up, b_gate_up.reshape(depth, N_EXPERTS, 1, 2 * D_FF),
      w_down, b_down.reshape(depth, N_EXPERTS, 1, d))


def _post_moe_kernel(dest_ref, x_ref, y_hbm, gt_ref, g_ref, b_ref, p_ref, wg_ref, wp_ref, o_ref,
                     wg_s, wp_s, ybuf, sem, *, alpha, n_tok):
    i = pl.program_id(0)
    n_steps = pl.num_programs(0)
    tm, d = x_ref.shape

    def row_copy(tile, slot, k, j):
        r = dest_ref[k * n_tok + tile * tm + j]
        return pltpu.make_async_copy(y_hbm.at[pl.ds(r, 1), :], ybuf.at[slot, k, pl.ds(j, 1), :],
                                     sem.at[slot])

    @pl.when(i == 0)
    def _():
        wg_s[...] = wg_ref[...].astype(BF16)
        wp_s[...] = wp_ref[...].astype(BF16)

        def body(j, carry):
            for k in range(TOP_K):
                row_copy(0, 0, k, j).start()
            return carry
        lax.fori_loop(0, tm, body, 0, unroll=8)

    @pl.when(i + 1 < n_steps)
    def _():
        for j in range(tm):
            for k in range(TOP_K):
                row_copy(i + 1, (i + 1) % 2, k, j).start(priority=k % 2)

    slot = i % 2
    pltpu.make_async_copy(ybuf.at[slot], ybuf.at[slot], sem.at[slot]).wait()
    gates = gt_ref[...]
    ffn = gates[:, 0:1] * ybuf[slot, 0]
    for k in range(1, TOP_K):
        ffn = ffn + gates[:, k:k + 1] * ybuf[slot, k]
    xn = _layer_norm(alpha * x_ref[...] + ffn, g_ref[...], b_ref[...])
    gate = jax.nn.sigmoid(_dot(xn.astype(BF16), wg_s[...]))
    emb = _dot(p_ref[...].astype(BF16), wp_s[...])
    o_ref[...] = xn + gate * emb


def _post_moe(x2, y_pad, dest_flat, gates_t, ln_g, ln_b, layer, p3, w_ple_gate, w_ple_proj, alpha):
    n, d = x2.shape
    tm = TM_POST
    row = lambda i, dest: (i, 0)
    const2 = lambda i, dest: (0, 0)
    grid_spec = pltpu.PrefetchScalarGridSpec(
        num_scalar_prefetch=1,
        grid=(n // tm,),
        in_specs=[pl.BlockSpec((tm, d), row), pl.BlockSpec(memory_space=pl.ANY),
                  pl.BlockSpec((tm, TOP_K), row), pl.BlockSpec((1, d), const2),
                  pl.BlockSpec((1, d), const2),
                  pl.BlockSpec((None, tm, PLE_DIM), lambda i, dest: (layer, i, 0)),
                  pl.BlockSpec((None, d, d), lambda i, dest: (layer, 0, 0)),
                  pl.BlockSpec((None, PLE_DIM, d), lambda i, dest: (layer, 0, 0))],
        out_specs=pl.BlockSpec((tm, d), row),
        scratch_shapes=[pltpu.VMEM((d, d), BF16), pltpu.VMEM((PLE_DIM, d), BF16),
                        pltpu.VMEM((2, TOP_K, tm, d), F32), pltpu.SemaphoreType.DMA((2,))],
    )
    return pl.pallas_call(
        functools.partial(_post_moe_kernel, alpha=alpha, n_tok=n),
        grid_spec=grid_spec,
        out_shape=jax.ShapeDtypeStruct((n, d), F32),
        compiler_params=_cparams(1), name="post_moe",
    )(dest_flat, x2, y_pad, gates_t, ln_g.reshape(1, d), ln_b.reshape(1, d), p3, w_ple_gate,
      w_ple_proj)


def _t5_bucket(dist):
    d = jnp.maximum(dist, 0)
    ratio = jnp.log(jnp.maximum(d, 1).astype(F32) / MAX_EXACT) / math.log(MAX_DISTANCE / MAX_EXACT)
    large = MAX_EXACT + (ratio * (NUM_BUCKETS - MAX_EXACT)).astype(jnp.int32)
    large = jnp.minimum(large, NUM_BUCKETS - 1)
    return jnp.where(d < MAX_EXACT, d, large)


def _swa_bias_table(pos_bias):
    qi = jnp.arange(WINDOW)[:, None]
    kj = jnp.arange(2 * WINDOW)[None, :]
    dist = qi + WINDOW - kj
    valid = (dist >= 0) & (dist < WINDOW)
    bias = pos_bias[_t5_bucket(dist)][..., :A_HEADS].astype(F32).transpose(2, 0, 1)
    return jnp.where(valid[None], bias, NEG)


def _dsa_bias_tables(pos_bias):
    qi = jnp.arange(Q_BLOCK)[:, None]
    kj = jnp.arange(DSA_CK)[None, :]
    tabs = []
    for r in range(4):
        dist = r * Q_BLOCK + qi - kj
        tabs.append(pos_bias[_t5_bucket(dist)].astype(F32).transpose(2, 0, 1))
    return jnp.stack(tabs)


def _pad_cols(w, width=LANE):
    return jnp.pad(w, ((0, 0), (0, width - w.shape[1])))


def _even_mixer(x2, bsz, t, w_in, b_forget, sinks, w_o, swa_bias):
    hd = HEAD_DIM
    grp = A_HEADS // A_KV_HEADS
    pair_order = [g + grp * hh for g in range(grp) for hh in range(A_KV_HEADS)]
    na, nkv, nb = A_HEADS * hd, A_KV_HEADS * hd, B_HEADS * hd
    offs = [0, na, na + nkv, na + 2 * nkv, na + 2 * nkv + nb, na + 2 * nkv + 2 * nb,
            na + 2 * nkv + 3 * nb]
    w_aq = w_in[:, offs[0]:offs[1]].reshape(D_MODEL, A_HEADS, hd)[:, pair_order].reshape(D_MODEL, na)
    weights = [w_aq, w_in[:, offs[1]:offs[2]], w_in[:, offs[2]:offs[3]], w_in[:, offs[3]:offs[4]],
               w_in[:, offs[4]:offs[5]], w_in[:, offs[5]:offs[6]], _pad_cols(w_in[:, offs[6]:])]
    aq, ak, av, bq, bk, bv, bf = _project(x2, weights, [BF16] * 6 + [F32])
    o_a = _swa_attention(aq, ak, av, sinks, swa_bias, bsz, t)

    f_t = bf[:, :B_HEADS].reshape(bsz, t, B_HEADS).transpose(0, 2, 1)
    c = _forget_cumsum(f_t, b_forget)
    c_row = c.reshape(bsz, B_HEADS // 2, 2, t // FOX_TK, FOX_TK)
    c_col = c.reshape(bsz, B_HEADS // 2, 2, t).transpose(0, 1, 3, 2)
    o_b = _fox_attention(bq, bk, bv, c_col, c_row, bsz, t)

    w_oa = w_o[:na].reshape(A_HEADS, hd, D_MODEL)[jnp.asarray(pair_order)].reshape(na, D_MODEL)
    return [o_a, o_b], [w_oa, w_o[na:]]


def _odd_mixer(x2, bsz, t, w_in, kv_norm, w_uk, w_uv, w_o, dsa_tb):
    hd = HEAD_DIM
    nq, nl, ni = C_HEADS * hd, C_LATENT, IDX_HEADS * IDX_DIM
    w_ki = w_in[:, nq + nl + ni:nq + nl + ni + IDX_DIM]
    weights = [w_in[:, :nq], w_in[:, nq:nq + nl], w_in[:, nq + nl:nq + nl + ni],
               jnp.concatenate([w_ki, w_ki], axis=1), _pad_cols(w_in[:, nq + nl + ni + IDX_DIM:])]
    cq, ckv, qi, ki2, wi = _project(x2, weights, [BF16, BF16, BF16, BF16, F32],
                                    rms_gain=kv_norm.reshape(1, nl).astype(F32), rms_index=1)
    uk = w_uk.transpose(1, 2, 0)
    uv = w_uv.transpose(1, 0, 2)
    zk = jnp.zeros_like(uk)
    zv = jnp.zeros_like(uv)
    odd = (jnp.arange(C_HEADS) % 2 == 1)[:, None, None]
    wuk_wide = jnp.where(odd, jnp.concatenate([zk, uk], axis=1), jnp.concatenate([uk, zk], axis=1))
    wuv_wide = jnp.where(odd, jnp.concatenate([zv, uv], axis=2), jnp.concatenate([uv, zv], axis=2))
    o_c = _dsa_attention(cq, ckv, qi, ki2, wi, wuk_wide.astype(BF16), wuv_wide.astype(BF16),
                         dsa_tb, bsz, t)
    return [o_c], [w_o]


def _dispatch_plan(top_idx, rank, counts, n_tok):
    counts = counts.reshape(N_EXPERTS)
    padded = (counts + TM_MOE - 1) // TM_MOE * TM_MOE
    pad_ends = jnp.cumsum(padded)
    pad_starts = pad_ends - padded
    is_e = top_idx[..., None] == jnp.arange(N_EXPERTS, dtype=jnp.int32)
    dest = jnp.sum(jnp.where(is_e, pad_starts.astype(jnp.int32), 0), axis=-1) + rank
    n_blocks = (n_tok * TOP_K) // TM_MOE + N_EXPERTS
    block_start = jnp.arange(n_blocks, dtype=jnp.int32) * TM_MOE
    block_e = jnp.minimum(jnp.sum(pad_ends[None, :] <= block_start[:, None], axis=1),
                          N_EXPERTS - 1).astype(jnp.int32)
    n_used = (pad_ends[-1] // TM_MOE).astype(jnp.int32).reshape(1)
    return dest, block_e, n_used, n_blocks


def kernel(x, p, pos_bias, w_in_even, b_forget, sinks, w_o_even, w_in_odd, kv_norm, w_uk, w_uv,
           w_o_odd, ln_g, ln_b, w_router, b_router, w_gate_up, b_gate_up, w_down, b_down,
           w_ple_proj, w_ple_gate):
    bsz, t, d = x.shape
    depth = ln_g.shape[0]
    alpha = (2 * depth) ** 0.25
    n_tok = bsz * t
    x2 = x.reshape(n_tok, d)
    swa_bias = _swa_bias_table(pos_bias)
    dsa_tb = _dsa_bias_tables(pos_bias)
    p3 = p.reshape(depth, n_tok, PLE_DIM)
    tok_ids = jnp.tile(jnp.arange(n_tok, dtype=jnp.int32), (TOP_K,))
    for i in range(depth):
        j = i // 2
        if i % 2 == 0:
            mixes, w_os = _even_mixer(x2, bsz, t, w_in_even[j], b_forget[j], sinks[j], w_o_even[j],
                                      swa_bias)
        else:
            mixes, w_os = _odd_mixer(x2, bsz, t, w_in_odd[j], kv_norm[j], w_uk[j], w_uv[j],
                                     w_o_odd[j], dsa_tb)
        x1, top_idx, gates, rank, counts = _post_attn(x2, mixes, w_os, ln_g[i, 0], ln_b[i, 0],
                                                      w_router[i], b_router[i], alpha)
        dest, block_e, n_used, n_blocks = _dispatch_plan(top_idx, rank, counts, n_tok)
        dest_flat = dest.reshape(-1)
        src_tok = jnp.zeros((n_blocks * TM_MOE,), jnp.int32).at[dest_flat].set(
            tok_ids, unique_indices=True, mode="promise_in_bounds")
        y_pad = _moe_ffn(x1, src_tok, block_e, n_used, i, w_gate_up, b_gate_up, w_down, b_down)
        x2 = _post_moe(x1, y_pad, dest_flat, gates.T, ln_g[i, 1], ln_b[i, 1], i, p3,
                       w_ple_gate, w_ple_proj, alpha)
    return x2.reshape(bsz, t, d)
```

```python
import functools
import math

import jax
import jax.numpy as jnp
from jax import lax
from jax.experimental import pallas as pl
from jax.experimental.pallas import tpu as pltpu

D_MODEL = 1024
HEAD_DIM = 64
A_HEADS, A_KV_HEADS, WINDOW = 8, 2, 128
B_HEADS = 8
C_HEADS, C_LATENT = 16, 128
IDX_HEADS, IDX_DIM = 8, 64
TOPK_MAX = 256
NUM_BUCKETS, MAX_EXACT, MAX_DISTANCE = 32, 16, 128
N_EXPERTS, TOP_K, D_FF = 32, 4, 1024
SWIGLU_LIMIT, SWIGLU_ALPHA = 7.0, 1.702
PLE_DIM = 256
Q_BLOCK = 128
LN_EPS = 1e-5
NEG = -1e30

LANE = 128
VMEM_LIMIT = 56 * 1024 * 1024
TM_PROJ = 512
TM_MOE = 256
MOE_SLOTS = 10
TM_POST = 256
FOX_TQ = 512
FOX_TK = 2 * LANE
DSA_CK = 256
INT_MIN = -2 ** 31

BF16 = jnp.bfloat16
F32 = jnp.float32


def _cparams(n_axes):
    return pltpu.CompilerParams(dimension_semantics=("arbitrary",) * n_axes,
                                vmem_limit_bytes=VMEM_LIMIT)


def _const_spec(shape):
    nd = len(shape)
    return pl.BlockSpec(shape, lambda *_: (0,) * nd)


def _dot(a, b):
    return jnp.dot(a, b, preferred_element_type=F32)


def _dot_nt(a, b):
    return lax.dot_general(a, b, (((1,), (1,)), ((), ())), preferred_element_type=F32)


def _layer_norm(v, g, b):
    mu = jnp.mean(v, axis=-1, keepdims=True)
    d = v - mu
    var = jnp.mean(d * d, axis=-1, keepdims=True)
    return d * lax.rsqrt(var + LN_EPS) * g + b


def _split3(v):
    hi = v.astype(BF16)
    r1 = v - hi.astype(F32)
    mid = r1.astype(BF16)
    lo = (r1 - mid.astype(F32)).astype(BF16)
    return hi, mid, lo


def _proj_kernel(*refs, n_w, rms_index):
    x_ref = refs[0]
    w_refs = refs[1:1 + n_w]
    g_ref = refs[1 + n_w]
    o_refs = refs[2 + n_w:2 + 2 * n_w]
    wb_refs = refs[2 + 2 * n_w:]

    @pl.when(pl.program_id(0) == 0)
    def _():
        for w_ref, wb_ref in zip(w_refs, wb_refs):
            wb_ref[...] = w_ref[...].astype(BF16)

    xb = x_ref[...].astype(BF16)
    for i in range(n_w):
        h = _dot(xb, wb_refs[i][...])
        if i == rms_index:
            h = h * lax.rsqrt(jnp.mean(h * h, axis=-1, keepdims=True) + LN_EPS) * g_ref[...]
        o_refs[i][...] = h.astype(o_refs[i].dtype)


def _project(x2, weights, out_dtypes, rms_gain=None, rms_index=-1):
    n, d = x2.shape
    n_w = len(weights)
    if rms_gain is None:
        rms_gain = jnp.ones((1, LANE), F32)
    in_specs = [pl.BlockSpec((TM_PROJ, d), lambda i: (i, 0))]
    in_specs += [_const_spec(w.shape) for w in weights]
    in_specs += [_const_spec(rms_gain.shape)]
    out_specs = [pl.BlockSpec((TM_PROJ, w.shape[1]), lambda i: (i, 0)) for w in weights]
    out_shape = [jax.ShapeDtypeStruct((n, w.shape[1]), dt) for w, dt in zip(weights, out_dtypes)]
    scratch = [pltpu.VMEM(w.shape, BF16) for w in weights]
    return pl.pallas_call(
        functools.partial(_proj_kernel, n_w=n_w, rms_index=rms_index),
        grid=(n // TM_PROJ,),
        in_specs=in_specs, out_specs=out_specs, out_shape=out_shape,
        scratch_shapes=scratch, compiler_params=_cparams(1), name="in_proj",
    )(x2, *weights, rms_gain)


def _cumsum_kernel(f_ref, b_ref, o_ref):
    z = f_ref[...] + b_ref[...]
    ls = jnp.minimum(z, 0.0) - jnp.log1p(jnp.exp(-jnp.abs(z)))
    t = ls.shape[1]
    r = lax.broadcasted_iota(jnp.int32, (LANE, LANE), 0)
    c = lax.broadcasted_iota(jnp.int32, (LANE, LANE), 1)
    tri = jnp.where(r <= c, 1.0, 0.0).astype(BF16)
    carry = jnp.zeros((ls.shape[0], 1), F32)
    for blk in range(t // LANE):
        seg = ls[:, blk * LANE:(blk + 1) * LANE]
        hi, mid, lo = _split3(seg)
        pre = _dot(hi, tri) + _dot(mid, tri) + _dot(lo, tri) + carry
        o_ref[:, blk * LANE:(blk + 1) * LANE] = pre
        carry = pre[:, LANE - 1:LANE]


def _forget_cumsum(f_t, b_forget):
    bsz, h, t = f_t.shape
    return pl.pallas_call(
        _cumsum_kernel,
        grid=(bsz,),
        in_specs=[pl.BlockSpec((None, h, t), lambda b: (b, 0, 0)), _const_spec((h, 1))],
        out_specs=pl.BlockSpec((None, h, t), lambda b: (b, 0, 0)),
        out_shape=jax.ShapeDtypeStruct((bsz, h, t), F32),
        compiler_params=_cparams(1), name="forget_cumsum",
    )(f_t, b_forget.reshape(h, 1).astype(F32))


def _fox_kernel(q_ref, k_ref, v_ref, cq_ref, ck_ref, o_ref, s_all, m_s, l_s, acc_s):
    i = pl.program_id(2)
    tq = q_ref.shape[0]
    tk = FOX_TK
    per_q = tq // tk
    q = q_ref[...]
    lane = lax.broadcasted_iota(jnp.int32, (tq, LANE), 1)
    row_minus_col = (lax.broadcasted_iota(jnp.int32, (tq, tk), 0)
                     - lax.broadcasted_iota(jnp.int32, (tq, tk), 1))
    heads = range(2)
    qh, cq = [], []
    for hh in heads:
        in_half = (lane >= hh * HEAD_DIM) & (lane < (hh + 1) * HEAD_DIM)
        qh.append(jnp.where(in_half, q, jnp.zeros_like(q)) * jnp.asarray(HEAD_DIM ** -0.5, BF16))
        cq.append(cq_ref[:, hh:hh + 1])

    def logits(j, hh):
        start = pl.multiple_of(j * tk, tk)
        return _dot_nt(qh[hh], k_ref[pl.ds(start, tk), :]) + cq[hh] - ck_ref[hh, pl.ds(j, 1), :]

    def keep(j, hh, s):
        s_all[hh, j] = s
        m_s[hh] = jnp.maximum(m_s[hh], jnp.maximum(s[:, :LANE], s[:, LANE:]))

    def logit_body(j, carry):
        for hh in heads:
            keep(j, hh, logits(j, hh))
        return carry

    m_s[...] = jnp.full(m_s.shape, NEG, F32)
    lax.fori_loop(0, i * per_q, logit_body, 0)
    for dd in range(per_q):
        j = i * per_q + dd
        causal = row_minus_col >= dd * tk
        for hh in heads:
            keep(j, hh, jnp.where(causal, logits(j, hh), NEG))
    m_row = [jnp.broadcast_to(jnp.max(m_s[hh], axis=-1, keepdims=True), (tq, LANE)) for hh in heads]
    l_s[...] = jnp.zeros(l_s.shape, F32)
    acc_s[...] = jnp.zeros(acc_s.shape, F32)

    def prob_body(j, carry):
        start = pl.multiple_of(j * tk, tk)
        vs = v_ref[pl.ds(start, tk), :]
        for hh in heads:
            p_lo = jnp.exp(s_all[hh, j, :, :LANE] - m_row[hh])
            p_hi = jnp.exp(s_all[hh, j, :, LANE:] - m_row[hh])
            l_s[hh] = l_s[hh] + p_lo + p_hi
            p = jnp.concatenate([p_lo, p_hi], axis=1).astype(BF16)
            acc_s[hh] = acc_s[hh] + _dot(p, vs)
        return carry

    lax.fori_loop(0, (i + 1) * per_q, prob_body, 0)
    outs = [acc_s[hh] / jnp.sum(l_s[hh], axis=-1, keepdims=True) for hh in heads]
    o_ref[...] = jnp.where(lane < HEAD_DIM, outs[0], outs[1]).astype(o_ref.dtype)


def _fox_attention(q, k, v, c_col, c_row, bsz, t):
    n, hd = q.shape
    n_pair = hd // LANE
    nq = t // FOX_TQ
    nk = t // FOX_TK
    return pl.pallas_call(
        _fox_kernel,
        grid=(bsz, n_pair, nq),
        in_specs=[
            pl.BlockSpec((FOX_TQ, LANE), lambda b, j, i: (b * nq + i, j)),
            pl.BlockSpec((t, LANE), lambda b, j, i: (b, j)),
            pl.BlockSpec((t, LANE), lambda b, j, i: (b, j)),
            pl.BlockSpec((None, None, FOX_TQ, 2), lambda b, j, i: (b, j, i, 0)),
            pl.BlockSpec((None, None, 2, nk, FOX_TK), lambda b, j, i: (b, j, 0, 0, 0)),
        ],
        out_specs=pl.BlockSpec((FOX_TQ, LANE), lambda b, j, i: (b * nq + i, j)),
        out_shape=jax.ShapeDtypeStruct((n, hd), BF16),
        scratch_shapes=[pltpu.VMEM((2, nk, FOX_TQ, FOX_TK), F32), pltpu.VMEM((2, FOX_TQ, LANE), F32),
                        pltpu.VMEM((2, FOX_TQ, LANE), F32), pltpu.VMEM((2, FOX_TQ, LANE), F32)],
        compiler_params=_cparams(3), name="fox_attention",
    )(q, k, v, c_col, c_row)


def _swa_kernel(sink_ref, q_ref, kp_ref, kc_ref, vp_ref, vc_ref, bias_ref, o_ref):
    nblk = pl.program_id(1)
    w = q_ref.shape[0]
    kk = jnp.concatenate([kp_ref[...], kc_ref[...]], axis=0)
    vv = jnp.concatenate([vp_ref[...], vc_ref[...]], axis=0)
    lane = lax.broadcasted_iota(jnp.int32, (w, LANE), 1)
    col = lax.broadcasted_iota(jnp.int32, (w, 2 * w), 1)
    no_prev = (col < w) & (nblk == 0)
    grp = A_HEADS // A_KV_HEADS
    for g in range(grp):
        qg = q_ref[:, g * LANE:(g + 1) * LANE]
        outs = []
        for hh in range(A_KV_HEADS):
            head = g + grp * hh
            in_half = (lane >= hh * HEAD_DIM) & (lane < (hh + 1) * HEAD_DIM)
            qh = jnp.where(in_half, qg, jnp.zeros_like(qg)) * jnp.asarray(HEAD_DIM ** -0.5, BF16)
            s = _dot_nt(qh, kk) + bias_ref[head]
            s = jnp.where(no_prev, NEG, s)
            sink = sink_ref[head]
            m = jnp.maximum(jnp.max(s, axis=-1, keepdims=True), sink)
            e = jnp.exp(s - m)
            denom = jnp.sum(e, axis=-1, keepdims=True) + jnp.exp(sink - m)
            p = e / denom
            outs.append(_dot(p.astype(BF16), vv))
        o_ref[:, g * LANE:(g + 1) * LANE] = jnp.where(lane < HEAD_DIM, outs[0], outs[1]).astype(o_ref.dtype)


def _swa_attention(q, k, v, sinks, bias, bsz, t):
    n = q.shape[0]
    nb = t // WINDOW
    cur = lambda b, i: (b * nb + i, 0)
    prev = lambda b, i: (b * nb + jnp.maximum(i - 1, 0), 0)
    return pl.pallas_call(
        _swa_kernel,
        grid=(bsz, nb),
        in_specs=[
            pl.BlockSpec(memory_space=pltpu.SMEM),
            pl.BlockSpec((WINDOW, A_HEADS * HEAD_DIM), cur),
            pl.BlockSpec((WINDOW, LANE), prev), pl.BlockSpec((WINDOW, LANE), cur),
            pl.BlockSpec((WINDOW, LANE), prev), pl.BlockSpec((WINDOW, LANE), cur),
            _const_spec(bias.shape),
        ],
        out_specs=pl.BlockSpec((WINDOW, A_HEADS * HEAD_DIM), cur),
        out_shape=jax.ShapeDtypeStruct((n, A_HEADS * HEAD_DIM), BF16),
        compiler_params=_cparams(2), name="swa_attention",
    )(sinks.astype(F32), q, k, k, v, v, bias)


def _float_order_key(s):
    bits = pltpu.bitcast(s, jnp.int32)
    return bits ^ ((bits >> 31) & jnp.int32(0x7FFFFFFF))


def _dsa_kernel(q_ref, ckv_ref, qi_ref, ki_ref, wi_ref, wuk_ref, wuv_ref, tb_ref, o_ref,
                key_s, sel_s, qlat_s, qis_s, wib_s, s_all, p_buf, acc_s, m_s, l_s):
    n = pl.program_id(1)
    qb = Q_BLOCK
    ck = DSA_CK
    n_sel = TOPK_MAX
    n_chunks = n // (ck // qb) + 1
    lane = lax.broadcasted_iota(jnp.int32, (qb, LANE), 1)
    qpos = n * qb + lax.broadcasted_iota(jnp.int32, (qb, ck), 0)
    kcol = lax.broadcasted_iota(jnp.int32, (qb, ck), 1)

    for h in range(IDX_HEADS):
        pair = qi_ref[:, (h // 2) * LANE:(h // 2 + 1) * LANE]
        in_half = (lane >= (h % 2) * IDX_DIM) & (lane < (h % 2 + 1) * IDX_DIM)
        qis_s[h * qb:(h + 1) * qb, :] = jnp.where(in_half, pair, jnp.zeros_like(pair))

    for h in range(C_HEADS):
        pair = q_ref[:, (h // 2) * LANE:(h // 2 + 1) * LANE]
        ql = _dot(pair, wuk_ref[h]) * (HEAD_DIM ** -0.5)
        qlat_s[h * qb:(h + 1) * qb, :] = ql.astype(BF16)

    wi = wi_ref[...] * (IDX_HEADS ** -0.5)
    for h in range(IDX_HEADS):
        wib_s[h] = jnp.broadcast_to(wi[:, h:h + 1] * (IDX_DIM ** -0.5), (qb, ck))

    def score_body(c, _):
        start = pl.multiple_of(c * ck, ck)
        kc = ki_ref[pl.ds(start, ck), :]
        raw = _dot_nt(qis_s[...], kc)
        score = jnp.zeros((qb, ck), F32)
        for h in range(IDX_HEADS):
            idx_s = jnp.maximum(raw[h * qb:(h + 1) * qb, :], 0.0)
            score = score + idx_s * wib_s[h]
        causal = (start + kcol) <= qpos
        key_s[c] = _float_order_key(jnp.where(causal, score, NEG)).T
        sel_s[c] = jnp.where(causal, 0.0, NEG)
        return 0

    lax.fori_loop(0, n_chunks, score_body, 0)

    @pl.when((n + 1) * qb > n_sel)
    def _():
        def count_where(pred_fn):
            def body(c, acc):
                hit = jnp.where(pred_fn(key_s[c]), 1.0, 0.0)
                return acc + jnp.sum(hit.reshape(ck // 32, 32, qb), axis=0)
            acc = lax.fori_loop(0, n_chunks, body, jnp.zeros((32, qb), F32))
            return jnp.sum(acc, axis=0, keepdims=True)

        cnt0 = count_where(lambda kv: kv >= 0)
        prefix0 = jnp.where(cnt0 >= n_sel, jnp.int32(0), jnp.int32(INT_MIN))

        def bit_body(t, prefix):
            cand = prefix | jnp.left_shift(jnp.int32(1), 30 - t)
            cnt = count_where(lambda kv: kv >= cand)
            return jnp.where(cnt >= n_sel, cand, prefix)

        thr = lax.fori_loop(0, 31, bit_body, prefix0)
        need = n_sel - count_where(lambda kv: kv > thr)
        r = lax.broadcasted_iota(jnp.int32, (ck, ck), 0)
        cc = lax.broadcasted_iota(jnp.int32, (ck, ck), 1)
        earlier_key = jnp.where(cc < r, 1.0, 0.0).astype(BF16)

        def tie_body(c, seen):
            kv = key_s[c]
            tie = kv == thr
            tie_f = jnp.where(tie, 1.0, 0.0)
            earlier = _dot(earlier_key, tie_f.astype(BF16)) + seen
            sel = (kv > thr) | (tie & (earlier < need))
            sel_s[c] = jnp.minimum(sel_s[c], jnp.where(sel, 0.0, NEG).T)
            return seen + jnp.sum(tie_f, axis=0, keepdims=True)

        lax.fori_loop(0, n_chunks, tie_body, jnp.zeros((1, qb), F32))

    m_s[...] = jnp.full(m_s.shape, NEG, F32)

    def logit_body(c, _):
        start = pl.multiple_of(c * ck, ck)
        raw = _dot_nt(qlat_s[...], ckv_ref[pl.ds(start, ck), :])
        rel_blk = jnp.minimum(n - c * (ck // qb), 3)
        mask_add = sel_s[c]
        for h in range(C_HEADS):
            rows = slice(h * qb, (h + 1) * qb)
            s = raw[rows, :] + tb_ref[rel_blk, h] + mask_add
            s_all[c, rows, :] = s
            m_s[rows, :] = jnp.maximum(m_s[rows, :], jnp.maximum(s[:, :LANE], s[:, LANE:]))
        return 0

    lax.fori_loop(0, n_chunks, logit_body, 0)

    for h in range(C_HEADS):
        rows = slice(h * qb, (h + 1) * qb)
        m_s[rows, :] = jnp.broadcast_to(jnp.max(m_s[rows, :], axis=-1, keepdims=True), (qb, LANE))
    l_s[...] = jnp.zeros(l_s.shape, F32)
    acc_s[...] = jnp.zeros(acc_s.shape, F32)

    def prob_body(c, _):
        start = pl.multiple_of(c * ck, ck)
        for h in range(C_HEADS):
            rows = slice(h * qb, (h + 1) * qb)
            m_row = m_s[rows, :]
            p_lo = jnp.exp(s_all[c, rows, :LANE] - m_row)
            p_hi = jnp.exp(s_all[c, rows, LANE:] - m_row)
            l_s[rows, :] = l_s[rows, :] + p_lo + p_hi
            p_buf[rows, :LANE] = p_lo.astype(BF16)
            p_buf[rows, LANE:] = p_hi.astype(BF16)
        acc_s[...] = acc_s[...] + _dot(p_buf[...], ckv_ref[pl.ds(start, ck), :])
        return 0

    lax.fori_loop(0, n_chunks, prob_body, 0)

    for g in range(C_HEADS // 2):
        out = jnp.zeros((qb, LANE), F32)
        for hh in range(2):
            h = 2 * g + hh
            rows = slice(h * qb, (h + 1) * qb)
            denom = jnp.sum(l_s[rows, :], axis=-1, keepdims=True)
            o_lat = (acc_s[rows, :] / denom).astype(BF16)
            out = out + _dot(o_lat, wuv_ref[h])
        o_ref[:, g * LANE:(g + 1) * LANE] = out.astype(o_ref.dtype)


def _dsa_attention(q, ckv, qi, ki2, wi, wuk_wide, wuv_wide, tb, bsz, t):
    n = q.shape[0]
    nq = t // Q_BLOCK
    nck = t // DSA_CK
    rows = C_HEADS * Q_BLOCK
    blk = lambda b, i: (b * nq + i, 0)
    per_b = lambda b, i: (b, 0)
    return pl.pallas_call(
        _dsa_kernel,
        grid=(bsz, nq),
        in_specs=[
            pl.BlockSpec((Q_BLOCK, C_HEADS * HEAD_DIM), blk),
            pl.BlockSpec((t, C_LATENT), per_b),
            pl.BlockSpec((Q_BLOCK, IDX_HEADS * IDX_DIM), blk),
            pl.BlockSpec((t, LANE), per_b),
            pl.BlockSpec((Q_BLOCK, LANE), blk),
            _const_spec(wuk_wide.shape), _const_spec(wuv_wide.shape), _const_spec(tb.shape),
        ],
        out_specs=pl.BlockSpec((Q_BLOCK, C_HEADS * HEAD_DIM), blk),
        out_shape=jax.ShapeDtypeStruct((n, C_HEADS * HEAD_DIM), BF16),
        scratch_shapes=[
            pltpu.VMEM((nck, DSA_CK, Q_BLOCK), jnp.int32),
            pltpu.VMEM((nck, Q_BLOCK, DSA_CK), F32),
            pltpu.VMEM((rows, C_LATENT), BF16),
            pltpu.VMEM((IDX_HEADS * Q_BLOCK, LANE), BF16),
            pltpu.VMEM((IDX_HEADS, Q_BLOCK, DSA_CK), F32),
            pltpu.VMEM((nck, rows, DSA_CK), F32),
            pltpu.VMEM((rows, DSA_CK), BF16),
            pltpu.VMEM((rows, C_LATENT), F32),
            pltpu.VMEM((rows, LANE), F32),
            pltpu.VMEM((rows, LANE), F32),
        ],
        compiler_params=_cparams(2), name="dsa_attention",
    )(q, ckv, qi, ki2, wi, wuk_wide, wuv_wide, tb)


def _post_attn_kernel(*refs, n_mix, alpha):
    x_ref = refs[0]
    mix_refs = refs[1:1 + n_mix]
    w_refs = refs[1 + n_mix:1 + 2 * n_mix]
    g_ref, b_ref, wr_ref, br_ref = refs[1 + 2 * n_mix:5 + 2 * n_mix]
    xo_ref, idx_ref, gate_ref, rank_ref, cnt_ref = refs[5 + 2 * n_mix:10 + 2 * n_mix]
    scr = refs[10 + 2 * n_mix:]
    wb_refs = scr[:n_mix]
    wr3_ref, before_ref, carry_ref = scr[n_mix:]
    tm = x_ref.shape[0]

    @pl.when(pl.program_id(0) == 0)
    def _():
        for w_ref, wb_ref in zip(w_refs, wb_refs):
            wb_ref[...] = w_ref[...].astype(BF16)
        hi, mid, lo = _split3(wr_ref[...])
        wr3_ref[0] = hi
        wr3_ref[1] = mid
        wr3_ref[2] = lo
        r = lax.broadcasted_iota(jnp.int32, (tm, tm), 0)
        c = lax.broadcasted_iota(jnp.int32, (tm, tm), 1)
        before_ref[...] = jnp.where(r < c, 1.0, 0.0).astype(BF16)
        carry_ref[...] = jnp.zeros(carry_ref.shape, F32)

    mix = _dot(mix_refs[0][...], wb_refs[0][...])
    for i in range(1, n_mix):
        mix = mix + _dot(mix_refs[i][...], wb_refs[i][...])
    xn = _layer_norm(alpha * x_ref[...] + mix, g_ref[...], b_ref[...])
    xo_ref[...] = xn

    xh, xm, xl = _split3(xn)
    logits = (_dot_nt(wr3_ref[0], xh) + _dot_nt(wr3_ref[0], xm) + _dot_nt(wr3_ref[1], xh)
              + _dot_nt(wr3_ref[0], xl) + _dot_nt(wr3_ref[1], xm) + _dot_nt(wr3_ref[2], xh)
              + br_ref[...])
    eidx = lax.broadcasted_iota(jnp.int32, logits.shape, 0)
    cur = logits
    vals, idxs = [], []
    for _ in range(TOP_K):
        mx = jnp.max(cur, axis=0, keepdims=True)
        first = jnp.min(jnp.where(cur == mx, eidx, N_EXPERTS), axis=0, keepdims=True)
        vals.append(mx)
        idxs.append(first)
        cur = jnp.where(eidx == first, -jnp.inf, cur)
    es = [jnp.exp(v - vals[0]) for v in vals]
    tot = es[0] + es[1] + es[2] + es[3]
    member = jnp.zeros(logits.shape, F32)
    for k in range(TOP_K):
        idx_ref[k:k + 1, :] = idxs[k]
        gate_ref[k:k + 1, :] = es[k] / tot
        member = member + jnp.where(eidx == idxs[k], 1.0, 0.0)
    earlier = _dot(member.astype(BF16), before_ref[...]) + carry_ref[...]
    for k in range(TOP_K):
        rk = jnp.sum(jnp.where(eidx == idxs[k], earlier, 0.0), axis=0, keepdims=True)
        rank_ref[k:k + 1, :] = rk.astype(jnp.int32)
    carry_ref[...] = carry_ref[...] + jnp.sum(member, axis=1, keepdims=True)
    cnt_ref[...] = carry_ref[...].astype(jnp.int32)


def _post_attn(x2, mixes, w_os, ln_g, ln_b, w_router, b_router, alpha):
    n, d = x2.shape
    n_mix = len(mixes)
    tm = TM_PROJ
    row = lambda i: (i, 0)
    colblk = lambda i: (0, i)
    in_specs = [pl.BlockSpec((tm, d), row)]
    in_specs += [pl.BlockSpec((tm, m.shape[1]), row) for m in mixes]
    in_specs += [_const_spec(w.shape) for w in w_os]
    in_specs += [_const_spec((1, d)), _const_spec((1, d)), _const_spec((N_EXPERTS, d)),
                 _const_spec((N_EXPERTS, 1))]
    out_specs = [pl.BlockSpec((tm, d), row), pl.BlockSpec((TOP_K, tm), colblk),
                 pl.BlockSpec((TOP_K, tm), colblk), pl.BlockSpec((TOP_K, tm), colblk),
                 _const_spec((N_EXPERTS, 1))]
    out_shape = [jax.ShapeDtypeStruct((n, d), F32), jax.ShapeDtypeStruct((TOP_K, n), jnp.int32),
                 jax.ShapeDtypeStruct((TOP_K, n), F32), jax.ShapeDtypeStruct((TOP_K, n), jnp.int32),
                 jax.ShapeDtypeStruct((N_EXPERTS, 1), jnp.int32)]
    scratch = [pltpu.VMEM(w.shape, BF16) for w in w_os]
    scratch += [pltpu.VMEM((3, N_EXPERTS, d), BF16), pltpu.VMEM((tm, tm), BF16),
                pltpu.VMEM((N_EXPERTS, 1), F32)]
    return pl.pallas_call(
        functools.partial(_post_attn_kernel, n_mix=n_mix, alpha=alpha),
        grid=(n // tm,),
        in_specs=in_specs, out_specs=out_specs, out_shape=out_shape, scratch_shapes=scratch,
        compiler_params=_cparams(1), name="post_attn",
    )(x2, *mixes, *w_os, ln_g.reshape(1, d), ln_b.reshape(1, d), w_router.T,
      b_router.reshape(N_EXPERTS, 1))


def _moe_kernel(be_ref, nu_ref, src_ref, x_hbm, wgu_ref, bgu_ref, wd_ref, bd_ref, y_ref,
                wgu_s, wd_s, xbuf, sem):
    i = pl.program_id(0)
    n_steps = pl.num_programs(0)
    tm = y_ref.shape[0]
    prev = be_ref[jnp.maximum(i - 1, 0)]
    n_used = nu_ref[0]
    live = i < n_used
    n_slots = xbuf.shape[0]
    ahead = n_slots - 1
    slot = i % n_slots

    def row_copy(block, s, r):
        tok = src_ref[block * tm + r]
        return pltpu.make_async_copy(x_hbm.at[pl.ds(tok, 1), :], xbuf.at[s, pl.ds(r, 1), :],
                                     sem.at[s])

    def wait_rows(s):
        pltpu.make_async_copy(xbuf.at[s], xbuf.at[s], sem.at[s]).wait()

    @pl.when(i == 0)
    def _():
        for b in range(ahead):
            def body(r, carry, b=b):
                row_copy(b, b, r).start(priority=1)
                return carry
            lax.fori_loop(0, tm, body, 0, unroll=8)

    @pl.when(live & ((i == 0) | (be_ref[i] != prev)))
    def _():
        wgu_s[...] = wgu_ref[...].astype(BF16)
        wd_s[...] = wd_ref[...].astype(BF16)

    @pl.when(live)
    def _():
        wait_rows(slot)
        xb = xbuf[slot].astype(BF16)
        nxt = jnp.minimum(i + ahead, n_steps - 1)
        for r in range(tm):
            row_copy(nxt, (i + ahead) % n_slots, r).start(priority=1)
        ch = 1024
        acc = jnp.zeros(y_ref.shape, F32)
        for j in range(D_FF // ch):
            hg = _dot(xb, wgu_s[:, j * ch:(j + 1) * ch]) + bgu_ref[:, j * ch:(j + 1) * ch]
            hl = (_dot(xb, wgu_s[:, D_FF + j * ch:D_FF + (j + 1) * ch])
                  + bgu_ref[:, D_FF + j * ch:D_FF + (j + 1) * ch])
            glu = jnp.minimum(hg, SWIGLU_LIMIT)
            lin = jnp.clip(hl, -SWIGLU_LIMIT, SWIGLU_LIMIT)
            act = glu * jax.nn.sigmoid(SWIGLU_ALPHA * glu) * (lin + 1.0)
            acc = acc + _dot(act.astype(BF16), wd_s[j * ch:(j + 1) * ch, :])
        y_ref[...] = (acc + bd_ref[...]).astype(y_ref.dtype)

    @pl.when(i == n_used)
    def _():
        for b in range(ahead):
            wait_rows((i + b) % n_slots)

    @pl.when(jnp.logical_not(live))
    def _():
        y_ref[...] = jnp.zeros(y_ref.shape, y_ref.dtype)


def _moe_ffn(x2, src_tok, block_e, n_used, layer, w_gate_up, b_gate_up, w_down, b_down):
    d = x2.shape[1]
    rows = src_tok.shape[0]
    depth = w_gate_up.shape[0]
    n_blocks = rows // TM_MOE
    expert = lambda i, be, nu, src: (layer, be[i], 0, 0)
    grid_spec = pltpu.PrefetchScalarGridSpec(
        num_scalar_prefetch=3,
        grid=(n_blocks,),
        in_specs=[
            pl.BlockSpec(memory_space=pl.ANY),
            pl.BlockSpec((None, None, d, 2 * D_FF), expert),
            pl.BlockSpec((None, None, 1, 2 * D_FF), expert),
            pl.BlockSpec((None, None, D_FF, d), expert),
            pl.BlockSpec((None, None, 1, d), expert),
        ],
        out_specs=pl.BlockSpec((TM_MOE, d), lambda i, be, nu, src: (i, 0)),
        scratch_shapes=[pltpu.VMEM((d, 2 * D_FF), BF16), pltpu.VMEM((D_FF, d), BF16),
                        pltpu.VMEM((MOE_SLOTS, TM_MOE, d), F32),
                        pltpu.SemaphoreType.DMA((MOE_SLOTS,))],
    )
    return pl.pallas_call(
        _moe_kernel, grid_spec=grid_spec,
        out_shape=jax.ShapeDtypeStruct((rows, d), F32),
        compiler_params=_cparams(1), name="moe_ffn",
    )(block_e, n_used, src_tok, x2, w_gate_up, b_gate_up.reshape(depth, N_EXPERTS, 1, 2 * D_FF),
      w_down, b_down.reshape(depth, N_EXPERTS, 1, d))


def _post_moe_kernel(dest_ref, x_ref, y_hbm, gt_ref, g_ref, b_ref, p_ref, wg_ref, wp_ref, o_ref,
                     wg_s, wp_s, ybuf, sem, *, alpha, n_tok):
    i = pl.program_id(0)
    n_steps = pl.num_programs(0)
    tm, d = x_ref.shape

    def row_copy(tile, slot, k, j):
        r = dest_ref[k * n_tok + tile * tm + j]
        return pltpu.make_async_copy(y_hbm.at[pl.ds(r, 1), :], ybuf.at[slot, k, pl.ds(j, 1), :],
                                     sem.at[slot])

    @pl.when(i == 0)
    def _():
        wg_s[...] = wg_ref[...].astype(BF16)
        wp_s[...] = wp_ref[...].astype(BF16)

        def body(j, carry):
            for k in range(TOP_K):
                row_copy(0, 0, k, j).start()
            return carry
        lax.fori_loop(0, tm, body, 0, unroll=8)

    @pl.when(i + 1 < n_steps)
    def _():
        for j in range(tm):
            for k in range(TOP_K):
                row_copy(i + 1, (i + 1) % 2, k, j).start(priority=k % 2)

    slot = i % 2
    pltpu.make_async_copy(ybuf.at[slot], ybuf.at[slot], sem.at[slot]).wait()
    gates = gt_ref[...]
    ffn = gates[:, 0:1] * ybuf[slot, 0]
    for k in range(1, TOP_K):
        ffn = ffn + gates[:, k:k + 1] * ybuf[slot, k]
    xn = _layer_norm(alpha * x_ref[...] + ffn, g_ref[...], b_ref[...])
    gate = jax.nn.sigmoid(_dot(xn.astype(BF16), wg_s[...]))
    emb = _dot(p_ref[...].astype(BF16), wp_s[...])
    o_ref[...] = xn + gate * emb


def _post_moe(x2, y_pad, dest_flat, gates_t, ln_g, ln_b, layer, p3, w_ple_gate, w_ple_proj, alpha):
    n, d = x2.shape
    tm = TM_POST
    row = lambda i, dest: (i, 0)
    const2 = lambda i, dest: (0, 0)
    grid_spec = pltpu.PrefetchScalarGridSpec(
        num_scalar_prefetch=1,
        grid=(n // tm,),
        in_specs=[pl.BlockSpec((tm, d), row), pl.BlockSpec(memory_space=pl.ANY),
                  pl.BlockSpec((tm, TOP_K), row), pl.BlockSpec((1, d), const2),
                  pl.BlockSpec((1, d), const2),
                  pl.BlockSpec((None, tm, PLE_DIM), lambda i, dest: (layer, i, 0)),
                  pl.BlockSpec((None, d, d), lambda i, dest: (layer, 0, 0)),
                  pl.BlockSpec((None, PLE_DIM, d), lambda i, dest: (layer, 0, 0))],
        out_specs=pl.BlockSpec((tm, d), row),
        scratch_shapes=[pltpu.VMEM((d, d), BF16), pltpu.VMEM((PLE_DIM, d), BF16),
                        pltpu.VMEM((2, TOP_K, tm, d), F32), pltpu.SemaphoreType.DMA((2,))],
    )
    return pl.pallas_call(
        functools.partial(_post_moe_kernel, alpha=alpha, n_tok=n),
        grid_spec=grid_spec,
        out_shape=jax.ShapeDtypeStruct((n, d), F32),
        compiler_params=_cparams(1), name="post_moe",
    )(dest_flat, x2, y_pad, gates_t, ln_g.reshape(1, d), ln_b.reshape(1, d), p3, w_ple_gate,
      w_ple_proj)


def _t5_bucket(dist):
    d = jnp.maximum(dist, 0)
    ratio = jnp.log(jnp.maximum(d, 1).astype(F32) / MAX_EXACT) / math.log(MAX_DISTANCE / MAX_EXACT)
    large = MAX_EXACT + (ratio * (NUM_BUCKETS - MAX_EXACT)).astype(jnp.int32)
    large = jnp.minimum(large, NUM_BUCKETS - 1)
    return jnp.where(d < MAX_EXACT, d, large)


def _swa_bias_table(pos_bias):
    qi = jnp.arange(WINDOW)[:, None]
    kj = jnp.arange(2 * WINDOW)[None, :]
    dist = qi + WINDOW - kj
    valid = (dist >= 0) & (dist < WINDOW)
    bias = pos_bias[_t5_bucket(dist)][..., :A_HEADS].astype(F32).transpose(2, 0, 1)
    return jnp.where(valid[None], bias, NEG)


def _dsa_bias_tables(pos_bias):
    qi = jnp.arange(Q_BLOCK)[:, None]
    kj = jnp.arange(DSA_CK)[None, :]
    tabs = []
    for r in range(4):
        dist = r * Q_BLOCK + qi - kj
        tabs.append(pos_bias[_t5_bucket(dist)].astype(F32).transpose(2, 0, 1))
    return jnp.stack(tabs)


def _pad_cols(w, width=LANE):
    return jnp.pad(w, ((0, 0), (0, width - w.shape[1])))


def _even_mixer(x2, bsz, t, w_in, b_forget, sinks, w_o, swa_bias):
    hd = HEAD_DIM
    grp = A_HEADS // A_KV_HEADS
    pair_order = [g + grp * hh for g in range(grp) for hh in range(A_KV_HEADS)]
    na, nkv, nb = A_HEADS * hd, A_KV_HEADS * hd, B_HEADS * hd
    offs = [0, na, na + nkv, na + 2 * nkv, na + 2 * nkv + nb, na + 2 * nkv + 2 * nb,
            na + 2 * nkv + 3 * nb]
    w_aq = w_in[:, offs[0]:offs[1]].reshape(D_MODEL, A_HEADS, hd)[:, pair_order].reshape(D_MODEL, na)
    weights = [w_aq, w_in[:, offs[1]:offs[2]], w_in[:, offs[2]:offs[3]], w_in[:, offs[3]:offs[4]],
               w_in[:, offs[4]:offs[5]], w_in[:, offs[5]:offs[6]], _pad_cols(w_in[:, offs[6]:])]
    aq, ak, av, bq, bk, bv, bf = _project(x2, weights, [BF16] * 6 + [F32])
    o_a = _swa_attention(aq, ak, av, sinks, swa_bias, bsz, t)

    f_t = bf[:, :B_HEADS].reshape(bsz, t, B_HEADS).transpose(0, 2, 1)
    c = _forget_cumsum(f_t, b_forget)
    c_row = c.reshape(bsz, B_HEADS // 2, 2, t // FOX_TK, FOX_TK)
    c_col = c.reshape(bsz, B_HEADS // 2, 2, t).transpose(0, 1, 3, 2)
    o_b = _fox_attention(bq, bk, bv, c_col, c_row, bsz, t)

    w_oa = w_o[:na].reshape(A_HEADS, hd, D_MODEL)[jnp.asarray(pair_order)].reshape(na, D_MODEL)
    return [o_a, o_b], [w_oa, w_o[na:]]


def _odd_mixer(x2, bsz, t, w_in, kv_norm, w_uk, w_uv, w_o, dsa_tb):
    hd = HEAD_DIM
    nq, nl, ni = C_HEADS * hd, C_LATENT, IDX_HEADS * IDX_DIM
    w_ki = w_in[:, nq + nl + ni:nq + nl + ni + IDX_DIM]
    weights = [w_in[:, :nq], w_in[:, nq:nq + nl], w_in[:, nq + nl:nq + nl + ni],
               jnp.concatenate([w_ki, w_ki], axis=1), _pad_cols(w_in[:, nq + nl + ni + IDX_DIM:])]
    cq, ckv, qi, ki2, wi = _project(x2, weights, [BF16, BF16, BF16, BF16, F32],
                                    rms_gain=kv_norm.reshape(1, nl).astype(F32), rms_index=1)
    uk = w_uk.transpose(1, 2, 0)
    uv = w_uv.transpose(1, 0, 2)
    zk = jnp.zeros_like(uk)
    zv = jnp.zeros_like(uv)
    odd = (jnp.arange(C_HEADS) % 2 == 1)[:, None, None]
    wuk_wide = jnp.where(odd, jnp.concatenate([zk, uk], axis=1), jnp.concatenate([uk, zk], axis=1))
    wuv_wide = jnp.where(odd, jnp.concatenate([zv, uv], axis=2), jnp.concatenate([uv, zv], axis=2))
    o_c = _dsa_attention(cq, ckv, qi, ki2, wi, wuk_wide.astype(BF16), wuv_wide.astype(BF16),
                         dsa_tb, bsz, t)
    return [o_c], [w_o]


def _dispatch_plan(top_idx, rank, counts, n_tok):
    counts = counts.reshape(N_EXPERTS)
    padded = (counts + TM_MOE - 1) // TM_MOE * TM_MOE
    pad_ends = jnp.cumsum(padded)
    pad_starts = pad_ends - padded
    is_e = top_idx[..., None] == jnp.arange(N_EXPERTS, dtype=jnp.int32)
    dest = jnp.sum(jnp.where(is_e, pad_starts.astype(jnp.int32), 0), axis=-1) + rank
    n_blocks = (n_tok * TOP_K) // TM_MOE + N_EXPERTS
    block_start = jnp.arange(n_blocks, dtype=jnp.int32) * TM_MOE
    block_e = jnp.minimum(jnp.sum(pad_ends[None, :] <= block_start[:, None], axis=1),
                          N_EXPERTS - 1).astype(jnp.int32)
    n_used = (pad_ends[-1] // TM_MOE).astype(jnp.int32).reshape(1)
    return dest, block_e, n_used, n_blocks


def kernel(x, p, pos_bias, w_in_even, b_forget, sinks, w_o_even, w_in_odd, kv_norm, w_uk, w_uv,
           w_o_odd, ln_g, ln_b, w_router, b_router, w_gate_up, b_gate_up, w_down, b_down,
           w_ple_proj, w_ple_gate):
    bsz, t, d = x.shape
    depth = ln_g.shape[0]
    alpha = (2 * depth) ** 0.25
    n_tok = bsz * t
    x2 = x.reshape(n_tok, d)
    swa_bias = _swa_bias_table(pos_bias)
    dsa_tb = _dsa_bias_tables(pos_bias)
    p3 = p.reshape(depth, n_tok, PLE_DIM)
    tok_ids = jnp.tile(jnp.arange(n_tok, dtype=jnp.int32), (TOP_K,))
    for i in range(depth):
        j = i // 2
        if i % 2 == 0:
            mixes, w_os = _even_mixer(x2, bsz, t, w_in_even[j], b_forget[j], sinks[j], w_o_even[j],
                                      swa_bias)
        else:
            mixes, w_os = _odd_mixer(x2, bsz, t, w_in_odd[j], kv_norm[j], w_uk[j], w_uv[j],
                                     w_o_odd[j], dsa_tb)
        x1, top_idx, gates, rank, counts = _post_attn(x2, mixes, w_os, ln_g[i, 0], ln_b[i, 0],
                                                      w_router[i], b_router[i], alpha)
        dest, block_e, n_used, n_blocks = _dispatch_plan(top_idx, rank, counts, n_tok)
        dest_flat = dest.reshape(-1)
        src_tok = jnp.zeros((n_blocks * TM_MOE,), jnp.int32).at[dest_flat].set(
            tok_ids, unique_indices=True, mode="promise_in_bounds")
        y_pad = _moe_ffn(x1, src_tok, block_e, n_used, i, w_gate_up, b_gate_up, w_down, b_down)
        x2 = _post_moe(x1, y_pad, dest_flat, gates.T, ln_g[i, 1], ln_b[i, 1], i, p3,
                       w_ple_gate, w_ple_proj, alpha)
    return x2.reshape(bsz, t, d)
```
